```python
import jax, jax.numpy as jnp
from jax import lax
import numpy as np

D_MODEL = 1024
BATCH = 8
SEQ = 8192
DEPTH = 1

D_FF = 2816
D_CONV = D_MODEL // 2
D_POOL = D_MODEL - D_CONV
CONV_WIDTH = 31
POOL_WINDOWS = (2, 4, 8, 16)
N_POOL_GROUPS = len(POOL_WINDOWS)
POOL_GROUP = D_POOL // N_POOL_GROUPS
D_IN = 2 * D_CONV + D_POOL
RMS_EPS = 1e-6
LN_EPS = 1e-5
FFN_RES_WEIGHT = 0.5

kernel_name = "macaron_conv_pool_hybrid_layer"


def _rmsnorm(x, g):
    xf = x.astype(jnp.float32)
    r = lax.rsqrt(jnp.mean(xf * xf, axis=-1, keepdims=True) + RMS_EPS)
    return (xf * r).astype(x.dtype) * g


def _layernorm(x, g, b):
    xf = x.astype(jnp.float32)
    mu = jnp.mean(xf, axis=-1, keepdims=True)
    var = jnp.mean(jnp.square(xf - mu), axis=-1, keepdims=True)
    return ((xf - mu) * lax.rsqrt(var + LN_EPS)).astype(x.dtype) * g + b


def _swiglu(h, w_gate, w_up, w_down):
    return (jax.nn.silu(h @ w_gate) * (h @ w_up)) @ w_down


def _causal_depthwise_conv(u, w, b):
    c = u.shape[-1]
    rhs = w.reshape(CONV_WIDTH, 1, c).astype(u.dtype)
    out = lax.conv_general_dilated(
        u, rhs, window_strides=(1,), padding=[(CONV_WIDTH - 1, 0)],
        dimension_numbers=("NWC", "WIO", "NWC"), feature_group_count=c)
    return out + b


def _multiscale_pool(p, pool_w, pool_scale):
    bsz, t, _ = p.shape
    pf = p.reshape(bsz, t, N_POOL_GROUPS, POOL_GROUP).astype(jnp.float32)
    cs = jnp.cumsum(pf, axis=1)
    pos = jnp.arange(t, dtype=jnp.float32)[None, :, None] + 1.0
    pooled = []
    for gi, w in enumerate(POOL_WINDOWS):
        csg = cs[:, :, gi]
        lag = jnp.pad(csg, ((0, 0), (w, 0), (0, 0)))[:, :t]
        cnt = jnp.minimum(pos, float(w))
        pooled.append((csg - lag) / cnt)
    mixed = (jnp.stack(pooled, axis=2) - pf).astype(p.dtype)
    out = jnp.einsum("btgc,gcd->btgd", mixed, pool_w)
    return out.reshape(bsz, t, D_POOL) * pool_scale


def _fwd_setup_inputs(seed: int = 0) -> dict:
    key = jax.random.key(seed)
    ks = jax.random.split(key, 24)
    f32 = jnp.float32

    def nrm(k, shape, fan_in):
        return jax.random.normal(k, shape, f32) * (fan_in ** -0.5)

    def gain(k, n):
        return 1.0 + 0.02 * jax.random.normal(k, (n,), f32)

    def bias(k, n):
        return 0.02 * jax.random.normal(k, (n,), f32)

    return {
        "x": jax.random.normal(ks[0], (BATCH, SEQ, D_MODEL), f32),
        "ffn1_norm": gain(ks[1], D_MODEL),
        "ffn1_w_gate": nrm(ks[2], (D_MODEL, D_FF), D_MODEL),
        "ffn1_w_up": nrm(ks[3], (D_MODEL, D_FF), D_MODEL),
        "ffn1_w_down": nrm(ks[4], (D_FF, D_MODEL), D_FF),
        "mix_norm": gain(ks[5], D_MODEL),
        "w_in": nrm(ks[6], (D_MODEL, D_IN), D_MODEL),
        "conv_dw": nrm(ks[7], (CONV_WIDTH, D_CONV), CONV_WIDTH),
        "conv_dw_b": bias(ks[8], D_CONV),
        "conv_ln_g": gain(ks[9], D_CONV),
        "conv_ln_b": bias(ks[10], D_CONV),
        "conv_pw": nrm(ks[11], (D_CONV, D_CONV), D_CONV),
        "pool_w": nrm(ks[12], (N_POOL_GROUPS, POOL_GROUP, POOL_GROUP), POOL_GROUP),
        "pool_scale": gain(ks[13], D_POOL),
        "w_out": nrm(ks[14], (D_CONV + D_POOL, D_MODEL), D_CONV + D_POOL),
        "ffn2_norm": gain(ks[15], D_MODEL),
        "ffn2_w_gate": nrm(ks[16], (D_MODEL, D_FF), D_MODEL),
        "ffn2_w_up": nrm(ks[17], (D_MODEL, D_FF), D_MODEL),
        "ffn2_w_down": nrm(ks[18], (D_FF, D_MODEL), D_FF),
        "final_norm": gain(ks[19], D_MODEL),
    }


def _fwd_reference(x, ffn1_norm, ffn1_w_gate, ffn1_w_up, ffn1_w_down, mix_norm, w_in,
              conv_dw, conv_dw_b, conv_ln_g, conv_ln_b, conv_pw, pool_w, pool_scale,
              w_out, ffn2_norm, ffn2_w_gate, ffn2_w_up, ffn2_w_down, final_norm):
    for _ in range(DEPTH):
        x = x + FFN_RES_WEIGHT * _swiglu(_rmsnorm(x, ffn1_norm), ffn1_w_gate, ffn1_w_up, ffn1_w_down)

        h = _rmsnorm(x, mix_norm)
        proj = h @ w_in
        a = proj[..., :D_CONV]
        g = proj[..., D_CONV:2 * D_CONV]
        p = proj[..., 2 * D_CONV:]

        u = a * jax.nn.sigmoid(g)
        u = _causal_depthwise_conv(u, conv_dw, conv_dw_b)
        u = jax.nn.silu(_layernorm(u, conv_ln_g, conv_ln_b))
        conv_out = u @ conv_pw

        pool_out = _multiscale_pool(p, pool_w, pool_scale)

        x = x + jnp.concatenate([conv_out, pool_out], axis=-1) @ w_out

        x = x + FFN_RES_WEIGHT * _swiglu(_rmsnorm(x, ffn2_norm), ffn2_w_gate, ffn2_w_up, ffn2_w_down)
    return _rmsnorm(x, final_norm)


import jax as _jax
import jax.numpy as _jnp

TWIN_FORMAT = 'train_step'
FWD_PARAMS = ['x', 'ffn1_norm', 'ffn1_w_gate', 'ffn1_w_up', 'ffn1_w_down', 'mix_norm', 'w_in', 'conv_dw', 'conv_dw_b', 'conv_ln_g', 'conv_ln_b', 'conv_pw', 'pool_w', 'pool_scale', 'w_out', 'ffn2_norm', 'ffn2_w_gate', 'ffn2_w_up', 'ffn2_w_down', 'final_norm']
TWIN_WEIGHTS = ['ffn1_norm', 'ffn1_w_gate', 'ffn1_w_up', 'ffn1_w_down', 'mix_norm', 'w_in', 'conv_dw', 'conv_dw_b', 'conv_ln_g', 'conv_ln_b', 'conv_pw', 'pool_w', 'pool_scale', 'w_out', 'ffn2_norm', 'ffn2_w_gate', 'ffn2_w_up', 'ffn2_w_down', 'final_norm']
TWIN_DIFF_INPUT = 'x'
TWIN_INPUTS = ['x', 'ffn1_norm', 'ffn1_w_gate', 'ffn1_w_up', 'ffn1_w_down', 'mix_norm', 'w_in', 'conv_dw', 'conv_dw_b', 'conv_ln_g', 'conv_ln_b', 'conv_pw', 'pool_w', 'pool_scale', 'w_out', 'ffn2_norm', 'ffn2_w_gate', 'ffn2_w_up', 'ffn2_w_down', 'final_norm', 'loss_target', 'm_ffn1_norm', 'm_ffn1_w_gate', 'm_ffn1_w_up', 'm_ffn1_w_down', 'm_mix_norm', 'm_w_in', 'm_conv_dw', 'm_conv_dw_b', 'm_conv_ln_g', 'm_conv_ln_b', 'm_conv_pw', 'm_pool_w', 'm_pool_scale', 'm_w_out', 'm_ffn2_norm', 'm_ffn2_w_gate', 'm_ffn2_w_up', 'm_ffn2_w_down', 'm_final_norm', 'v_ffn1_norm', 'v_ffn1_w_gate', 'v_ffn1_w_up', 'v_ffn1_w_down', 'v_mix_norm', 'v_w_in', 'v_conv_dw', 'v_conv_dw_b', 'v_conv_ln_g', 'v_conv_ln_b', 'v_conv_pw', 'v_pool_w', 'v_pool_scale', 'v_w_out', 'v_ffn2_norm', 'v_ffn2_w_gate', 'v_ffn2_w_up', 'v_ffn2_w_down', 'v_final_norm']
TWIN_OUTPUTS = ['loss', 'grad_x', 'grad_ffn1_norm', 'grad_ffn1_w_gate', 'grad_ffn1_w_up', 'grad_ffn1_w_down', 'grad_mix_norm', 'grad_w_in', 'grad_conv_dw', 'grad_conv_dw_b', 'grad_conv_ln_g', 'grad_conv_ln_b', 'grad_conv_pw', 'grad_pool_w', 'grad_pool_scale', 'grad_w_out', 'grad_ffn2_norm', 'grad_ffn2_w_gate', 'grad_ffn2_w_up', 'grad_ffn2_w_down', 'grad_final_norm', 'delta_ffn1_norm', 'delta_ffn1_w_gate', 'delta_ffn1_w_up', 'delta_ffn1_w_down', 'delta_mix_norm', 'delta_w_in', 'delta_conv_dw', 'delta_conv_dw_b', 'delta_conv_ln_g', 'delta_conv_ln_b', 'delta_conv_pw', 'delta_pool_w', 'delta_pool_scale', 'delta_w_out', 'delta_ffn2_norm', 'delta_ffn2_w_gate', 'delta_ffn2_w_up', 'delta_ffn2_w_down', 'delta_final_norm', 'new_m_ffn1_norm', 'new_m_ffn1_w_gate', 'new_m_ffn1_w_up', 'new_m_ffn1_w_down', 'new_m_mix_norm', 'new_m_w_in', 'new_m_conv_dw', 'new_m_conv_dw_b', 'new_m_conv_ln_g', 'new_m_conv_ln_b', 'new_m_conv_pw', 'new_m_pool_w', 'new_m_pool_scale', 'new_m_w_out', 'new_m_ffn2_norm', 'new_m_ffn2_w_gate', 'new_m_ffn2_w_up', 'new_m_ffn2_w_down', 'new_m_final_norm', 'new_v_ffn1_norm', 'new_v_ffn1_w_gate', 'new_v_ffn1_w_up', 'new_v_ffn1_w_down', 'new_v_mix_norm', 'new_v_w_in', 'new_v_conv_dw', 'new_v_conv_dw_b', 'new_v_conv_ln_g', 'new_v_conv_ln_b', 'new_v_conv_pw', 'new_v_pool_w', 'new_v_pool_scale', 'new_v_w_out', 'new_v_ffn2_norm', 'new_v_ffn2_w_gate', 'new_v_ffn2_w_up', 'new_v_ffn2_w_down', 'new_v_final_norm']
TWIN_LEAF_KINDS = {'loss': 'loss', 'grad_x': 'grad_x', 'grad_ffn1_norm': 'grad_w', 'grad_ffn1_w_gate': 'grad_w', 'grad_ffn1_w_up': 'grad_w', 'grad_ffn1_w_down': 'grad_w', 'grad_mix_norm': 'grad_w', 'grad_w_in': 'grad_w', 'grad_conv_dw': 'grad_w', 'grad_conv_dw_b': 'grad_w', 'grad_conv_ln_g': 'grad_w', 'grad_conv_ln_b': 'grad_w', 'grad_conv_pw': 'grad_w', 'grad_pool_w': 'grad_w', 'grad_pool_scale': 'grad_w', 'grad_w_out': 'grad_w', 'grad_ffn2_norm': 'grad_w', 'grad_ffn2_w_gate': 'grad_w', 'grad_ffn2_w_up': 'grad_w', 'grad_ffn2_w_down': 'grad_w', 'grad_final_norm': 'grad_w', 'delta_ffn1_norm': 'delta_w', 'delta_ffn1_w_gate': 'delta_w', 'delta_ffn1_w_up': 'delta_w', 'delta_ffn1_w_down': 'delta_w', 'delta_mix_norm': 'delta_w', 'delta_w_in': 'delta_w', 'delta_conv_dw': 'delta_w', 'delta_conv_dw_b': 'delta_w', 'delta_conv_ln_g': 'delta_w', 'delta_conv_ln_b': 'delta_w', 'delta_conv_pw': 'delta_w', 'delta_pool_w': 'delta_w', 'delta_pool_scale': 'delta_w', 'delta_w_out': 'delta_w', 'delta_ffn2_norm': 'delta_w', 'delta_ffn2_w_gate': 'delta_w', 'delta_ffn2_w_up': 'delta_w', 'delta_ffn2_w_down': 'delta_w', 'delta_final_norm': 'delta_w', 'new_m_ffn1_norm': 'new_m', 'new_m_ffn1_w_gate': 'new_m', 'new_m_ffn1_w_up': 'new_m', 'new_m_ffn1_w_down': 'new_m', 'new_m_mix_norm': 'new_m', 'new_m_w_in': 'new_m', 'new_m_conv_dw': 'new_m', 'new_m_conv_dw_b': 'new_m', 'new_m_conv_ln_g': 'new_m', 'new_m_conv_ln_b': 'new_m', 'new_m_conv_pw': 'new_m', 'new_m_pool_w': 'new_m', 'new_m_pool_scale': 'new_m', 'new_m_w_out': 'new_m', 'new_m_ffn2_norm': 'new_m', 'new_m_ffn2_w_gate': 'new_m', 'new_m_ffn2_w_up': 'new_m', 'new_m_ffn2_w_down': 'new_m', 'new_m_final_norm': 'new_m', 'new_v_ffn1_norm': 'new_v', 'new_v_ffn1_w_gate': 'new_v', 'new_v_ffn1_w_up': 'new_v', 'new_v_ffn1_w_down': 'new_v', 'new_v_mix_norm': 'new_v', 'new_v_w_in': 'new_v', 'new_v_conv_dw': 'new_v', 'new_v_conv_dw_b': 'new_v', 'new_v_conv_ln_g': 'new_v', 'new_v_conv_ln_b': 'new_v', 'new_v_conv_pw': 'new_v', 'new_v_pool_w': 'new_v', 'new_v_pool_scale': 'new_v', 'new_v_w_out': 'new_v', 'new_v_ffn2_norm': 'new_v', 'new_v_ffn2_w_gate': 'new_v', 'new_v_ffn2_w_up': 'new_v', 'new_v_ffn2_w_down': 'new_v', 'new_v_final_norm': 'new_v'}


def _forward(args):
    return _fwd_reference(*[args[k] for k in FWD_PARAMS])


def _output_shape():
    def fwd():
        inp = _fwd_setup_inputs(0)
        return _fwd_reference(*[inp[k] for k in FWD_PARAMS])
    out = _jax.eval_shape(fwd)
    return out.shape, out.dtype

N_MICROBATCH = 1
ADAM_LR = 0.001
ADAM_B1 = 0.9
ADAM_B2 = 0.999
ADAM_EPS = 1e-08
ADAM_WD = 0.01
ADAM_STEP = 10
PER_EXAMPLE_BATCH_AXIS = {'x': 0, 'loss_target': 0}
SHARED_INPUTS = []
_WEIGHT_DTYPES = {'ffn1_norm': _jnp.float32, 'ffn1_w_gate': _jnp.float32, 'ffn1_w_up': _jnp.float32, 'ffn1_w_down': _jnp.float32, 'mix_norm': _jnp.float32, 'w_in': _jnp.float32, 'conv_dw': _jnp.float32, 'conv_dw_b': _jnp.float32, 'conv_ln_g': _jnp.float32, 'conv_ln_b': _jnp.float32, 'conv_pw': _jnp.float32, 'pool_w': _jnp.float32, 'pool_scale': _jnp.float32, 'w_out': _jnp.float32, 'ffn2_norm': _jnp.float32, 'ffn2_w_gate': _jnp.float32, 'ffn2_w_up': _jnp.float32, 'ffn2_w_down': _jnp.float32, 'final_norm': _jnp.float32}
MOMENT_SCALE = {'ffn1_norm': 1.089035e-01, 'ffn1_w_gate': 4.633851e-02, 'ffn1_w_up': 4.486214e-02, 'ffn1_w_down': 7.455720e-02, 'mix_norm': 1.649507e-01, 'w_in': 1.286758e-01, 'conv_dw': 1.270567e-01, 'conv_dw_b': 2.759836e-01, 'conv_ln_g': 1.518145e-01, 'conv_ln_b': 1.266067e-01, 'conv_pw': 1.219101e-01, 'pool_w': 1.776380e-01, 'pool_scale': 1.773451e-01, 'w_out': 1.527769e-01, 'ffn2_norm': 8.424857e-02, 'ffn2_w_gate': 3.529299e-02, 'ffn2_w_up': 3.437218e-02, 'ffn2_w_down': 5.682175e-02, 'final_norm': 6.401790e+01}


def _to_microbatches(a, axis):
    t = _jnp.moveaxis(a, axis, 0)
    t = t.reshape((N_MICROBATCH, t.shape[0] // N_MICROBATCH) + t.shape[1:])
    return _jnp.moveaxis(t, 1, axis + 1)


def setup_inputs(seed: int = 0) -> dict:
    inp = _fwd_setup_inputs(seed)
    key = _jax.random.fold_in(_jax.random.key(seed), 7919)
    shape, _ = _output_shape()
    out = dict(inp)
    out["loss_target"] = _jax.random.normal(_jax.random.fold_in(key, 0), shape, _jnp.float32)
    for i, name in enumerate(TWIN_WEIGHTS):
        w = inp[name].astype(_jnp.float32)
        if MOMENT_SCALE is None:
            s = _jnp.sqrt(_jnp.mean(_jnp.square(w)) + 1e-30)
        else:
            s = MOMENT_SCALE[name]
        km, kv = _jax.random.split(_jax.random.fold_in(key, i + 1))
        out[name] = w
        out["m_" + name] = s * _jax.random.normal(km, w.shape, _jnp.float32)
        out["v_" + name] = (s * s) * _jax.random.uniform(kv, w.shape, _jnp.float32, 0.5, 1.5)
    if N_MICROBATCH > 1:
        for name, axis in PER_EXAMPLE_BATCH_AXIS.items():
            out[name] = _to_microbatches(out[name], axis)
    return {'x': out['x'], 'ffn1_norm': out['ffn1_norm'], 'ffn1_w_gate': out['ffn1_w_gate'], 'ffn1_w_up': out['ffn1_w_up'], 'ffn1_w_down': out['ffn1_w_down'], 'mix_norm': out['mix_norm'], 'w_in': out['w_in'], 'conv_dw': out['conv_dw'], 'conv_dw_b': out['conv_dw_b'], 'conv_ln_g': out['conv_ln_g'], 'conv_ln_b': out['conv_ln_b'], 'conv_pw': out['conv_pw'], 'pool_w': out['pool_w'], 'pool_scale': out['pool_scale'], 'w_out': out['w_out'], 'ffn2_norm': out['ffn2_norm'], 'ffn2_w_gate': out['ffn2_w_gate'], 'ffn2_w_up': out['ffn2_w_up'], 'ffn2_w_down': out['ffn2_w_down'], 'final_norm': out['final_norm'], 'loss_target': out['loss_target'], 'm_ffn1_norm': out['m_ffn1_norm'], 'm_ffn1_w_gate': out['m_ffn1_w_gate'], 'm_ffn1_w_up': out['m_ffn1_w_up'], 'm_ffn1_w_down': out['m_ffn1_w_down'], 'm_mix_norm': out['m_mix_norm'], 'm_w_in': out['m_w_in'], 'm_conv_dw': out['m_conv_dw'], 'm_conv_dw_b': out['m_conv_dw_b'], 'm_conv_ln_g': out['m_conv_ln_g'], 'm_conv_ln_b': out['m_conv_ln_b'], 'm_conv_pw': out['m_conv_pw'], 'm_pool_w': out['m_pool_w'], 'm_pool_scale': out['m_pool_scale'], 'm_w_out': out['m_w_out'], 'm_ffn2_norm': out['m_ffn2_norm'], 'm_ffn2_w_gate': out['m_ffn2_w_gate'], 'm_ffn2_w_up': out['m_ffn2_w_up'], 'm_ffn2_w_down': out['m_ffn2_w_down'], 'm_final_norm': out['m_final_norm'], 'v_ffn1_norm': out['v_ffn1_norm'], 'v_ffn1_w_gate': out['v_ffn1_w_gate'], 'v_ffn1_w_up': out['v_ffn1_w_up'], 'v_ffn1_w_down': out['v_ffn1_w_down'], 'v_mix_norm': out['v_mix_norm'], 'v_w_in': out['v_w_in'], 'v_conv_dw': out['v_conv_dw'], 'v_conv_dw_b': out['v_conv_dw_b'], 'v_conv_ln_g': out['v_conv_ln_g'], 'v_conv_ln_b': out['v_conv_ln_b'], 'v_conv_pw': out['v_conv_pw'], 'v_pool_w': out['v_pool_w'], 'v_pool_scale': out['v_pool_scale'], 'v_w_out': out['v_w_out'], 'v_ffn2_norm': out['v_ffn2_norm'], 'v_ffn2_w_gate': out['v_ffn2_w_gate'], 'v_ffn2_w_up': out['v_ffn2_w_up'], 'v_ffn2_w_down': out['v_ffn2_w_down'], 'v_final_norm': out['v_final_norm']}


def _loss(weights, diff, rest, loss_target):
    with _jax.named_scope("forward"):
        args = {**rest, TWIN_DIFF_INPUT: diff, **{k: w.astype(_WEIGHT_DTYPES[k]) for k, w in weights.items()}}
        y = _forward(args)
    with _jax.named_scope("loss_head"):
        err = _jnp.square(y.astype(_jnp.float32) - loss_target)
        return 0.5 * _jnp.sum(_jnp.mean(err, axis=-1)) if err.ndim else 0.5 * err


def _adamw(w, g, m, v):
    m = ADAM_B1 * m + (1.0 - ADAM_B1) * g
    v = ADAM_B2 * v + (1.0 - ADAM_B2) * _jnp.square(g)
    m_hat = m / (1.0 - ADAM_B1 ** ADAM_STEP)
    v_hat = v / (1.0 - ADAM_B2 ** ADAM_STEP)
    delta = -ADAM_LR * (m_hat / (_jnp.sqrt(v_hat) + ADAM_EPS) + ADAM_WD * w)
    return delta, m, v


def reference(x, ffn1_norm, ffn1_w_gate, ffn1_w_up, ffn1_w_down, mix_norm, w_in, conv_dw, conv_dw_b, conv_ln_g, conv_ln_b, conv_pw, pool_w, pool_scale, w_out, ffn2_norm, ffn2_w_gate, ffn2_w_up, ffn2_w_down, final_norm, loss_target, m_ffn1_norm, m_ffn1_w_gate, m_ffn1_w_up, m_ffn1_w_down, m_mix_norm, m_w_in, m_conv_dw, m_conv_dw_b, m_conv_ln_g, m_conv_ln_b, m_conv_pw, m_pool_w, m_pool_scale, m_w_out, m_ffn2_norm, m_ffn2_w_gate, m_ffn2_w_up, m_ffn2_w_down, m_final_norm, v_ffn1_norm, v_ffn1_w_gate, v_ffn1_w_up, v_ffn1_w_down, v_mix_norm, v_w_in, v_conv_dw, v_conv_dw_b, v_conv_ln_g, v_conv_ln_b, v_conv_pw, v_pool_w, v_pool_scale, v_w_out, v_ffn2_norm, v_ffn2_w_gate, v_ffn2_w_up, v_ffn2_w_down, v_final_norm):
    given = dict(x=x, ffn1_norm=ffn1_norm, ffn1_w_gate=ffn1_w_gate, ffn1_w_up=ffn1_w_up, ffn1_w_down=ffn1_w_down, mix_norm=mix_norm, w_in=w_in, conv_dw=conv_dw, conv_dw_b=conv_dw_b, conv_ln_g=conv_ln_g, conv_ln_b=conv_ln_b, conv_pw=conv_pw, pool_w=pool_w, pool_scale=pool_scale, w_out=w_out, ffn2_norm=ffn2_norm, ffn2_w_gate=ffn2_w_gate, ffn2_w_up=ffn2_w_up, ffn2_w_down=ffn2_w_down, final_norm=final_norm, loss_target=loss_target, m_ffn1_norm=m_ffn1_norm, m_ffn1_w_gate=m_ffn1_w_gate, m_ffn1_w_up=m_ffn1_w_up, m_ffn1_w_down=m_ffn1_w_down, m_mix_norm=m_mix_norm, m_w_in=m_w_in, m_conv_dw=m_conv_dw, m_conv_dw_b=m_conv_dw_b, m_conv_ln_g=m_conv_ln_g, m_conv_ln_b=m_conv_ln_b, m_conv_pw=m_conv_pw, m_pool_w=m_pool_w, m_pool_scale=m_pool_scale, m_w_out=m_w_out, m_ffn2_norm=m_ffn2_norm, m_ffn2_w_gate=m_ffn2_w_gate, m_ffn2_w_up=m_ffn2_w_up, m_ffn2_w_down=m_ffn2_w_down, m_final_norm=m_final_norm, v_ffn1_norm=v_ffn1_norm, v_ffn1_w_gate=v_ffn1_w_gate, v_ffn1_w_up=v_ffn1_w_up, v_ffn1_w_down=v_ffn1_w_down, v_mix_norm=v_mix_norm, v_w_in=v_w_in, v_conv_dw=v_conv_dw, v_conv_dw_b=v_conv_dw_b, v_conv_ln_g=v_conv_ln_g, v_conv_ln_b=v_conv_ln_b, v_conv_pw=v_conv_pw, v_pool_w=v_pool_w, v_pool_scale=v_pool_scale, v_w_out=v_w_out, v_ffn2_norm=v_ffn2_norm, v_ffn2_w_gate=v_ffn2_w_gate, v_ffn2_w_up=v_ffn2_w_up, v_ffn2_w_down=v_ffn2_w_down, v_final_norm=v_final_norm)
    weights = {n: given[n] for n in TWIN_WEIGHTS}
    shared = {n: given[n] for n in SHARED_INPUTS}
    per_example = {n: given[n] for n in ['x']}
    grad_fn = _jax.value_and_grad(_loss, argnums=(0, 1))

    def one_microbatch(ex, loss_target):
        ex = dict(ex)
        diff = ex.pop(TWIN_DIFF_INPUT)
        return grad_fn(weights, diff, {**shared, **ex}, loss_target)

    if N_MICROBATCH == 1:
        loss, (grad_w, grad_x) = one_microbatch(per_example, given["loss_target"])
    else:
        def body(carry, xs):
            loss_sum, grad_sum = carry
            l_k, (gw_k, gx_k) = one_microbatch(xs[0], xs[1])
            with _jax.named_scope("update"):
                return (loss_sum + l_k, _jax.tree.map(_jnp.add, grad_sum, gw_k)), gx_k

        init = (_jnp.zeros((), _jnp.float32), _jax.tree.map(_jnp.zeros_like, weights))
        (loss, grad_w), grad_x = _jax.lax.scan(body, init, (per_example, given["loss_target"]))
    with _jax.named_scope("update"):
        delta_w, new_m, new_v = {}, {}, {}
        for n in TWIN_WEIGHTS:
            delta_w[n], new_m[n], new_v[n] = _adamw(weights[n], grad_w[n], given["m_" + n], given["v_" + n])
    return (loss, grad_x, *[grad_w[n] for n in TWIN_WEIGHTS], *[delta_w[n] for n in TWIN_WEIGHTS],
            *[new_m[n] for n in TWIN_WEIGHTS], *[new_v[n] for n in TWIN_WEIGHTS])
```

```python
import functools

import jax
import jax.numpy as jnp
from jax import lax
from jax.experimental import pallas as pl
from jax.experimental.pallas import tpu as pltpu

F32 = jnp.float32
BF16 = jnp.bfloat16

D_MODEL = 1024
D_CONV = 512
D_POOL = 512
D_IN = 2 * D_CONV + D_POOL
POOL_WINDOWS = (2, 4, 8, 16)
POOL_GROUP = D_POOL // len(POOL_WINDOWS)
CONV_WIDTH = 31
RMS_EPS = 1e-6
LN_EPS = 1e-5
FFN_RES_WEIGHT = 0.5

ADAM_LR = 0.001
ADAM_B1 = 0.9
ADAM_B2 = 0.999
ADAM_EPS = 1e-08
ADAM_WD = 0.01
ADAM_STEP = 10

N_DEV = 8
MESH_AXES = ("x", "y", "c")
MESH_ID = pl.DeviceIdType.MESH

SUBLANES = 8
TOKEN_TILE = 512
FFN_BWD_TILE = 256
FF_CHUNK = 256
HALO = 32
V7X_VMEM_LIMIT = 56 * 1024 * 1024
CDW_ROWS = 16
REP_ROWS = 16

WEIGHTS = ("ffn1_norm", "ffn1_w_gate", "ffn1_w_up", "ffn1_w_down", "mix_norm", "w_in", "conv_dw", "conv_dw_b",
           "conv_ln_g", "conv_ln_b", "conv_pw", "pool_w", "pool_scale", "w_out", "ffn2_norm", "ffn2_w_gate",
           "ffn2_w_up", "ffn2_w_down", "final_norm")


def _dot_nn(a, b):
    return lax.dot_general(a, b, (((1,), (0,)), ((), ())), preferred_element_type=F32)


def _dot_nt(a, b):
    return lax.dot_general(a, b, (((1,), (1,)), ((), ())), preferred_element_type=F32)


def _dot_tn(a, b):
    return lax.dot_general(a, b, (((0,), (0,)), ((), ())), preferred_element_type=F32)


def _rowsum8(v):
    r, c = v.shape
    return jnp.sum(v.reshape(r // SUBLANES, SUBLANES, c), axis=0)


def _fold8(ref):
    ref[0:1, :] = jnp.sum(ref[...], axis=0, keepdims=True)


def _row_tile(n, cap, mult):
    best = None
    for t in range(mult, min(n, cap) + 1, mult):
        if n % t == 0:
            best = t
    return n if best is None else best


def _params(n_grid):
    return pltpu.CompilerParams(dimension_semantics=("arbitrary",) * n_grid, vmem_limit_bytes=V7X_VMEM_LIMIT)


def _full(shape):
    return pl.BlockSpec(shape, lambda *_: (0,) * len(shape))


def _slab_layout(f8):
    offs, r = {}, 0
    for name, rows in (("g1", f8), ("u1", f8), ("d1", f8), ("g2", f8), ("u2", f8), ("d2", f8),
                       ("win", D_IN // N_DEV), ("wout", D_MODEL // N_DEV), ("pw", D_CONV // N_DEV // 2),
                       ("cdw", CDW_ROWS), ("rep", REP_ROWS)):
        offs[name] = (r, rows)
        r += rows
    return offs, r


def _mesh_pos():
    return lax.axis_index("x"), lax.axis_index("y"), lax.axis_index("c")


def _all_gather(shard, name):
    rows, cols = shard.shape

    def body(x_ref, out_ref, send_sems, recv_sems, local_sem):
        x, y, c = _mesh_pos()
        me, sibling = (x, y, c), (x, y, 1 - c)
        chips = [(1 - x, y), (x, 1 - y), (1 - x, 1 - y)]

        def block(px, py, pc):
            return out_ref.at[4 * px + 2 * py + pc]

        def copy(k, blk, to, src=None):
            return pltpu.make_async_remote_copy(
                src_ref=block(*blk) if src is None else src, dst_ref=block(*blk),
                send_sem=send_sems.at[k], recv_sem=recv_sems.at[k], device_id=to, device_id_type=MESH_ID)

        mine = pltpu.make_async_copy(x_ref, block(*me), local_sem)
        mine.start()
        first = [copy(0, me, sibling, src=x_ref)]
        first += [copy(1 + j, me, (*chip, c), src=x_ref) for j, chip in enumerate(chips)]
        for cp in first:
            cp.start()
        passed = [copy(4 + j, (*chip, c), sibling) for j, chip in enumerate(chips)]
        for j, chip in enumerate(chips):
            copy(1 + j, (*chip, c), me).wait_recv()
            passed[j].start()
        copy(0, sibling, me).wait_recv()
        for j, chip in enumerate(chips):
            copy(4 + j, (*chip, 1 - c), me).wait_recv()
        for cp in first + passed:
            cp.wait_send()
        mine.wait()

    return pl.pallas_call(
        body, name=name,
        out_shape=jax.ShapeDtypeStruct((N_DEV, rows, cols), shard.dtype),
        in_specs=[pl.BlockSpec(memory_space=pl.ANY)],
        out_specs=pl.BlockSpec(memory_space=pl.ANY),
        scratch_shapes=[pltpu.SemaphoreType.DMA((7,)), pltpu.SemaphoreType.DMA((7,)), pltpu.SemaphoreType.DMA],
    )(shard)


def _exchange_sibling(gslab):
    _, rows, cols = gslab.shape

    def body(g_ref, recv_ref, send_sems, recv_sems):
        x, y, c = _mesh_pos()
        cps = []
        for k in range(4):
            cp = pltpu.make_async_remote_copy(
                src_ref=g_ref.at[2 * k + (1 - c)], dst_ref=recv_ref.at[k],
                send_sem=send_sems.at[k], recv_sem=recv_sems.at[k], device_id=(x, y, 1 - c), device_id_type=MESH_ID)
            cp.start()
            cps.append(cp)
        for cp in cps:
            cp.wait()

    return pl.pallas_call(
        body, name="rs_sibling",
        out_shape=jax.ShapeDtypeStruct((4, rows, cols), gslab.dtype),
        in_specs=[pl.BlockSpec(memory_space=pl.ANY)],
        out_specs=pl.BlockSpec(memory_space=pl.ANY),
        scratch_shapes=[pltpu.SemaphoreType.DMA((4,)), pltpu.SemaphoreType.DMA((4,))],
    )(gslab)


def _exchange_chips(part):
    _, rows, cols = part.shape

    def body(p_ref, recv_ref, send_sems, recv_sems):
        x, y, c = _mesh_pos()
        peers = [(1 - x, y, c), (x, 1 - y, c), (1 - x, 1 - y, c)]
        cps = []
        for k, peer in enumerate(peers):
            cp = pltpu.make_async_remote_copy(
                src_ref=p_ref.at[k], dst_ref=recv_ref.at[k],
                send_sem=send_sems.at[k], recv_sem=recv_sems.at[k], device_id=peer, device_id_type=MESH_ID)
            cp.start()
            cps.append(cp)
        for cp in cps:
            cp.wait()

    return pl.pallas_call(
        body, name="rs_chips",
        out_shape=jax.ShapeDtypeStruct((3, rows, cols), part.dtype),
        in_specs=[pl.BlockSpec(memory_space=pl.ANY)],
        out_specs=pl.BlockSpec(memory_space=pl.ANY),
        scratch_shapes=[pltpu.SemaphoreType.DMA((3,)), pltpu.SemaphoreType.DMA((3,))],
    )(part)


def _add_chunks(gslab, recv, gid, rid, out_dtype, name):
    n = gid.shape[0]
    _, rows, cols = gslab.shape
    tr = _row_tile(rows, 1024, 16)

    def body(gid_ref, rid_ref, a_ref, b_ref, o_ref):
        o_ref[...] = (a_ref[...] + b_ref[...]).astype(out_dtype)

    grid_spec = pltpu.PrefetchScalarGridSpec(
        num_scalar_prefetch=2, grid=(n, rows // tr),
        in_specs=[pl.BlockSpec((1, tr, cols), lambda k, i, g, r: (g[k], i, 0)),
                  pl.BlockSpec((1, tr, cols), lambda k, i, g, r: (r[k], i, 0))],
        out_specs=pl.BlockSpec((1, tr, cols), lambda k, i, g, r: (k, i, 0)))
    return pl.pallas_call(
        body, name=name, grid_spec=grid_spec,
        out_shape=jax.ShapeDtypeStruct((n, rows, cols), out_dtype),
        compiler_params=_params(2),
    )(gid, rid, gslab, recv)


def _sum_partials(own, recv):
    _, rows, cols = own.shape
    tr = _row_tile(rows, 1024, 16)

    def body(o_ref, r_ref, out_ref):
        acc = o_ref[0]
        for k in range(3):
            acc = acc + r_ref[k].astype(F32)
        out_ref[...] = acc

    return pl.pallas_call(
        body, name="rs_sum", grid=(rows // tr,),
        in_specs=[pl.BlockSpec((1, tr, cols), lambda i: (0, i, 0)), pl.BlockSpec((3, tr, cols), lambda i: (0, i, 0))],
        out_specs=pl.BlockSpec((tr, cols), lambda i: (i, 0)),
        out_shape=jax.ShapeDtypeStruct((rows, cols), F32),
        compiler_params=_params(1),
    )(own, recv)


def _reduce_scatter(gslab):
    x, y, c = _mesh_pos()
    chips = [(x, y), (1 - x, y), (x, 1 - y), (1 - x, 1 - y)]
    gid = jnp.stack([4 * px + 2 * py + c for px, py in chips]).astype(jnp.int32)
    rid = jnp.stack([2 * px + py for px, py in chips]).astype(jnp.int32)
    recv = _exchange_sibling(gslab)
    own = _add_chunks(gslab, recv, gid[:1], rid[:1], F32, "rs_add_own")
    part = _add_chunks(gslab, recv, gid[1:], rid[1:], BF16, "rs_add_send")
    return _sum_partials(own, _exchange_chips(part))


def _load_weights(pairs, sems):
    cps = [pltpu.make_async_copy(src, dst, sems.at[k]) for k, (src, dst) in enumerate(pairs)]
    for cp in cps:
        cp.start()
    for cp in cps:
        cp.wait()


def _chunk_rows(c):
    return pl.ds(pl.multiple_of(c * FF_CHUNK, FF_CHUNK), FF_CHUNK)


def _rms(xv):
    return lax.rsqrt(jnp.mean(xv * xv, axis=-1, keepdims=True) + RMS_EPS)


def _ffn_fwd(x, gain, wg, wu, wd, name):
    t, d = x.shape
    f = wg.shape[0]
    nc, tm = f // FF_CHUNK, TOKEN_TILE

    def body(x_ref, gain_ref, wg_hbm, wu_hbm, wd_hbm, xo_ref, g_ref, u_ref, n_ref, wg_v, wu_v, wd_v, acc_ref, sems):
        @pl.when(pl.program_id(0) == 0)
        def _():
            _load_weights(((wg_hbm, wg_v), (wu_hbm, wu_v), (wd_hbm, wd_v)), sems)

        xv = x_ref[...]
        n_ref[...] = ((xv * _rms(xv)) * gain_ref[...]).astype(BF16)
        acc_ref[...] = jnp.zeros_like(acc_ref)

        def chunk(c, carry):
            rows = _chunk_rows(c)
            nb = n_ref[...]
            g = _dot_nt(nb, wg_v[rows, :])
            u = _dot_nt(nb, wu_v[rows, :])
            g_ref[c] = g.astype(BF16)
            u_ref[c] = u.astype(BF16)
            h = (g * jax.nn.sigmoid(g)) * u
            acc_ref[...] += _dot_nn(h.astype(BF16), wd_v[rows, :])
            return carry

        lax.fori_loop(0, nc, chunk, 0)
        xo_ref[...] = xv + FFN_RES_WEIGHT * acc_ref[...]

    hbm = pl.BlockSpec(memory_space=pl.ANY)
    tile = pl.BlockSpec((tm, d), lambda i: (i, 0))
    act = pl.BlockSpec((nc, tm, FF_CHUNK), lambda i: (0, i, 0))
    return pl.pallas_call(
        body, name=name, grid=(t // tm,),
        in_specs=[tile, _full((1, d)), hbm, hbm, hbm],
        out_specs=[tile, act, act, tile],
        out_shape=[jax.ShapeDtypeStruct((t, d), F32), jax.ShapeDtypeStruct((nc, t, FF_CHUNK), BF16),
                   jax.ShapeDtypeStruct((nc, t, FF_CHUNK), BF16), jax.ShapeDtypeStruct((t, d), BF16)],
        scratch_shapes=[pltpu.VMEM((f, d), BF16), pltpu.VMEM((f, d), BF16), pltpu.VMEM((f, d), BF16),
                        pltpu.VMEM((tm, d), F32), pltpu.SemaphoreType.DMA((3,))],
        compiler_params=_params(1),
    )(x, gain, wg, wu, wd)


def _ffn_bwd(dxo, x, gain, gs, us, wg, wu, wd, name):
    t, d = x.shape
    f = wg.shape[0]
    nc, tm = f // FF_CHUNK, FFN_BWD_TILE
    nt = t // tm

    def body(dxo_ref, x_ref, gain_ref, g_ref, u_ref, wg_hbm, wu_hbm, wd_hbm,
             dx_ref, h_ref, dg_ref, du_ref, df_ref, dgain_ref, wg_v, wu_v, wd_v, dn_ref, sems):
        i = pl.program_id(0)

        @pl.when(i == 0)
        def _():
            _load_weights(((wg_hbm, wg_v), (wu_hbm, wu_v), (wd_hbm, wd_v)), sems)
            dgain_ref[...] = jnp.zeros_like(dgain_ref)

        df_ref[...] = (FFN_RES_WEIGHT * dxo_ref[...]).astype(BF16)
        dn_ref[...] = jnp.zeros_like(dn_ref)

        def chunk(c, carry):
            rows = _chunk_rows(c)
            g = g_ref[c].astype(F32)
            u = u_ref[c].astype(F32)
            sg = jax.nn.sigmoid(g)
            sil = g * sg
            dh = _dot_nt(df_ref[...], wd_v[rows, :])
            h_ref[c] = (sil * u).astype(BF16)
            du_ref[c] = (dh * sil).astype(BF16)
            dg_ref[c] = (dh * u * (sg * (1.0 + g * (1.0 - sg)))).astype(BF16)
            dn_ref[...] += _dot_nn(dg_ref[c], wg_v[rows, :]) + _dot_nn(du_ref[c], wu_v[rows, :])
            return carry

        lax.fori_loop(0, nc, chunk, 0)
        xv = x_ref[...]
        r = _rms(xv)
        xh = xv * r
        dn = dn_ref[...]
        dgain_ref[...] += _rowsum8(dn * xh)
        dxh = dn * gain_ref[...]
        dx_ref[...] = dxo_ref[...] + r * (dxh - xh * jnp.mean(dxh * xh, axis=-1, keepdims=True))

        @pl.when(i == nt - 1)
        def _():
            _fold8(dgain_ref)

    hbm = pl.BlockSpec(memory_space=pl.ANY)
    tile = pl.BlockSpec((tm, d), lambda i: (i, 0))
    act = pl.BlockSpec((nc, tm, FF_CHUNK), lambda i: (0, i, 0))
    act_shape = jax.ShapeDtypeStruct((nc, t, FF_CHUNK), BF16)
    return pl.pallas_call(
        body, name=name, grid=(nt,),
        in_specs=[tile, tile, _full((1, d)), act, act, hbm, hbm, hbm],
        out_specs=[tile, act, act, act, tile, _full((SUBLANES, d))],
        out_shape=[jax.ShapeDtypeStruct((t, d), F32), act_shape, act_shape, act_shape,
                   jax.ShapeDtypeStruct((t, d), BF16), jax.ShapeDtypeStruct((SUBLANES, d), F32)],
        scratch_shapes=[pltpu.VMEM((f, d), BF16), pltpu.VMEM((f, d), BF16), pltpu.VMEM((f, d), BF16),
                        pltpu.VMEM((tm, d), F32), pltpu.SemaphoreType.DMA((3,))],
        compiler_params=_params(1),
    )(dxo, x, gain, gs, us, wg, wu, wd)


def _tn_chunked(a, b, name):
    nc, t, _ = a.shape
    n = b.shape[1]
    tb = _row_tile(t, 1024, TOKEN_TILE)

    def body(a_ref, b_ref, o_ref):
        @pl.when(pl.program_id(0) == 0)
        def _():
            o_ref[...] = jnp.zeros_like(o_ref)

        def chunk(c, carry):
            rows = _chunk_rows(c)
            o_ref[rows, :] += _dot_tn(a_ref[c], b_ref[...])
            return carry

        lax.fori_loop(0, nc, chunk, 0)

    return pl.pallas_call(
        body, name=name, grid=(t // tb,),
        in_specs=[pl.BlockSpec((nc, tb, FF_CHUNK), lambda i: (0, i, 0)), pl.BlockSpec((tb, n), lambda i: (i, 0))],
        out_specs=_full((nc * FF_CHUNK, n)),
        out_shape=jax.ShapeDtypeStruct((nc * FF_CHUNK, n), F32),
        compiler_params=_params(1),
    )(a, b)


def _tn(a, b, name):
    t, k = a.shape
    n = b.shape[1]
    tb = _row_tile(t, 1024, TOKEN_TILE)

    def body(a_ref, b_ref, o_ref):
        @pl.when(pl.program_id(0) == 0)
        def _():
            o_ref[...] = jnp.zeros_like(o_ref)

        o_ref[...] += _dot_tn(a_ref[...].astype(BF16), b_ref[...].astype(BF16))

    return pl.pallas_call(
        body, name=name, grid=(t // tb,),
        in_specs=[pl.BlockSpec((tb, k), lambda i: (i, 0)), pl.BlockSpec((tb, n), lambda i: (i, 0))],
        out_specs=_full((k, n)),
        out_shape=jax.ShapeDtypeStruct((k, n), F32),
        compiler_params=_params(1),
    )(a, b)


def _loss_head(x, target, gain):
    t, d = x.shape
    tm = TOKEN_TILE
    nt = t // tm

    def body(x_ref, tgt_ref, gain_ref, dx_ref, dgain_ref, loss_ref):
        i = pl.program_id(0)

        @pl.when(i == 0)
        def _():
            dgain_ref[...] = jnp.zeros_like(dgain_ref)
            loss_ref[...] = jnp.zeros_like(loss_ref)

        xv = x_ref[...]
        r = _rms(xv)
        xh = xv * r
        err = xh * gain_ref[...] - tgt_ref[...]
        loss_ref[...] += 0.5 * jnp.sum(jnp.mean(err * err, axis=-1, keepdims=True))
        dy = err * (1.0 / d)
        dgain_ref[...] += _rowsum8(dy * xh)
        dxh = dy * gain_ref[...]
        dx_ref[...] = r * (dxh - xh * jnp.mean(dxh * xh, axis=-1, keepdims=True))

        @pl.when(i == nt - 1)
        def _():
            _fold8(dgain_ref)

    tile = pl.BlockSpec((tm, d), lambda i: (i, 0))
    return pl.pallas_call(
        body, name="loss_head", grid=(nt,),
        in_specs=[tile, tile, _full((1, d))],
        out_specs=[tile, _full((SUBLANES, d)), _full((SUBLANES, 128))],
        out_shape=[jax.ShapeDtypeStruct((t, d), F32), jax.ShapeDtypeStruct((SUBLANES, d), F32),
                   jax.ShapeDtypeStruct((SUBLANES, 128), F32)],
        compiler_params=_params(1),
    )(x, target, gain)


def _layernorm_stats(u1):
    mu = jnp.mean(u1, axis=-1, keepdims=True)
    xc = u1 - mu
    rstd = lax.rsqrt(jnp.mean(xc * xc, axis=-1, keepdims=True) + LN_EPS)
    return xc * rstd, rstd


def _positions(tile_index, tm):
    return (tile_index * tm + lax.broadcasted_iota(jnp.int32, (tm, 1), 0)).astype(F32)


def _mix_fwd(x, gm, win_t, cdw, cb, lg, lb, pw, poolw, ps, wout):
    t, d = x.shape
    tm = TOKEN_TILE

    def body(x_ref, gm_ref, win_ref, cdw_ref, cb_ref, lg_ref, lb_ref, pw_ref, poolw_ref, ps_ref, wout_ref,
             xo_ref, h_ref, ag_ref, u0_ref, u1_ref, u2_ref, mixed_ref, cat_ref, eu_ref, ep_ref):
        i = pl.program_id(0)

        @pl.when(i == 0)
        def _():
            eu_ref[0:HALO, :] = jnp.zeros((HALO, D_CONV), F32)
            ep_ref[0:HALO, :] = jnp.zeros((HALO, D_POOL), F32)

        @pl.when(i > 0)
        def _():
            eu_ref[0:HALO, :] = eu_ref[tm:tm + HALO, :]
            ep_ref[0:HALO, :] = ep_ref[tm:tm + HALO, :]

        xv = x_ref[...]
        hb = ((xv * _rms(xv)) * gm_ref[...]).astype(BF16)
        h_ref[...] = hb
        proj = _dot_nt(hb, win_ref[...])
        a = proj[:, :D_CONV]
        g = proj[:, D_CONV:2 * D_CONV]
        ag_ref[...] = proj[:, :2 * D_CONV]
        u0 = a * jax.nn.sigmoid(g)
        u0_ref[...] = u0
        eu_ref[HALO:HALO + tm, :] = u0
        ep_ref[HALO:HALO + tm, :] = proj[:, 2 * D_CONV:]

        u1 = jnp.broadcast_to(cb_ref[...], (tm, D_CONV))
        for k in range(CONV_WIDTH):
            off = HALO - (CONV_WIDTH - 1) + k
            u1 = u1 + cdw_ref[k:k + 1, :] * eu_ref[off:off + tm, :]
        u1_ref[...] = u1
        lnh, _ = _layernorm_stats(u1)
        ln = lnh * lg_ref[...] + lb_ref[...]
        u2 = (ln * jax.nn.sigmoid(ln)).astype(BF16)
        u2_ref[...] = u2
        conv_out = _dot_nn(u2, pw_ref[...])

        pos = _positions(i, tm)
        outs = []
        for gi, w in enumerate(POOL_WINDOWS):
            lo = gi * POOL_GROUP
            p = ep_ref[HALO:HALO + tm, lo:lo + POOL_GROUP]
            s = p
            for j in range(1, w):
                s = s + ep_ref[HALO - j:HALO - j + tm, lo:lo + POOL_GROUP]
            mixed = (s / jnp.minimum(pos + 1.0, float(w)) - p).astype(BF16)
            mixed_ref[:, lo:lo + POOL_GROUP] = mixed
            outs.append(_dot_nn(mixed, poolw_ref[gi]))
        pool_out = jnp.concatenate(outs, axis=-1) * ps_ref[...]
        cat = jnp.concatenate([conv_out, pool_out], axis=-1).astype(BF16)
        cat_ref[...] = cat
        xo_ref[...] = xv + _dot_nn(cat, wout_ref[...])

    def tile(c):
        return pl.BlockSpec((tm, c), lambda i: (i, 0))

    def out(c, dt):
        return jax.ShapeDtypeStruct((t, c), dt)

    return pl.pallas_call(
        body, name="mix_fwd", grid=(t // tm,),
        in_specs=[tile(d), _full((1, d)), _full((D_IN, d)), _full((HALO, D_CONV)), _full((1, D_CONV)),
                  _full((1, D_CONV)), _full((1, D_CONV)), _full((D_CONV, D_CONV)),
                  _full((len(POOL_WINDOWS), POOL_GROUP, POOL_GROUP)), _full((1, D_POOL)), _full((d, d))],
        out_specs=[tile(d), tile(d), tile(2 * D_CONV), tile(D_CONV), tile(D_CONV), tile(D_CONV), tile(D_POOL), tile(d)],
        out_shape=[out(d, F32), out(d, BF16), out(2 * D_CONV, F32), out(D_CONV, F32), out(D_CONV, F32),
                   out(D_CONV, BF16), out(D_POOL, BF16), out(d, BF16)],
        scratch_shapes=[pltpu.VMEM((HALO + tm, D_CONV), F32), pltpu.VMEM((HALO + tm, D_POOL), F32)],
        compiler_params=_params(1),
    )(x, gm, win_t, cdw, cb, lg, lb, pw, poolw, ps, wout)


def _mix_bwd(dxo, x, gm, ag, u0, u1, mixed, win_t, cdw, lg, lb, pw, poolw, ps, wout):
    t, d = x.shape
    tm = TOKEN_TILE
    nt = t // tm
    halo_blocks = tm // HALO

    def body(dxo_ref, x_ref, gm_ref, ag_ref, u0_ref, u0h_ref, u1_ref, mixed_ref,
             win_ref, cdw_ref, lg_ref, lb_ref, pw_ref, poolw_ref, ps_ref, wout_ref,
             dx_ref, dproj_ref, dco_ref, dgm_ref, dcdw_ref, dcb_ref, dlg_ref, dlb_ref, dpoolw_ref, dps_ref,
             eu_ref, ed_ref, eq_ref):
        i = pl.program_id(0)
        ti = nt - 1 - i

        @pl.when(i == 0)
        def _():
            for ref in (dgm_ref, dcdw_ref, dcb_ref, dlg_ref, dlb_ref, dpoolw_ref, dps_ref):
                ref[...] = jnp.zeros_like(ref)
            ed_ref[tm:tm + HALO, :] = jnp.zeros((HALO, D_CONV), F32)
            eq_ref[tm:tm + HALO, :] = jnp.zeros((HALO, D_POOL), F32)

        @pl.when(i > 0)
        def _():
            ed_ref[tm:tm + HALO, :] = ed_ref[0:HALO, :]
            eq_ref[tm:tm + HALO, :] = eq_ref[0:HALO, :]

        @pl.when(ti == 0)
        def _():
            eu_ref[0:HALO, :] = jnp.zeros((HALO, D_CONV), F32)

        @pl.when(ti > 0)
        def _():
            eu_ref[0:HALO, :] = u0h_ref[...]

        eu_ref[HALO:HALO + tm, :] = u0_ref[...]

        dxo = dxo_ref[...]
        dcat = _dot_nt(dxo.astype(BF16), wout_ref[...])
        dco = dcat[:, :D_CONV].astype(BF16)
        dco_ref[...] = dco
        dpo = dcat[:, D_CONV:]

        lnh, rstd = _layernorm_stats(u1_ref[...])
        ln = lnh * lg_ref[...] + lb_ref[...]
        sl = jax.nn.sigmoid(ln)
        dln = _dot_nt(dco, pw_ref[...]) * (sl * (1.0 + ln * (1.0 - sl)))
        dlg_ref[...] += _rowsum8(dln * lnh)
        dlb_ref[...] += _rowsum8(dln)
        dlnh = dln * lg_ref[...]
        du1 = rstd * (dlnh - jnp.mean(dlnh, axis=-1, keepdims=True)
                      - lnh * jnp.mean(dlnh * lnh, axis=-1, keepdims=True))
        dcb_ref[...] += _rowsum8(du1)
        ed_ref[0:tm, :] = du1

        du0 = jnp.zeros((tm, D_CONV), F32)
        for k in range(CONV_WIDTH):
            off = CONV_WIDTH - 1 - k
            du0 = du0 + cdw_ref[k:k + 1, :] * ed_ref[off:off + tm, :]
            back = HALO - (CONV_WIDTH - 1) + k
            dcdw_ref[SUBLANES * k:SUBLANES * (k + 1), :] += _rowsum8(du1 * eu_ref[back:back + tm, :])
        a = ag_ref[:, :D_CONV]
        sg = jax.nn.sigmoid(ag_ref[:, D_CONV:])
        pieces = [du0 * sg, du0 * a * (sg * (1.0 - sg))]

        pos = _positions(ti, tm)
        for gi, w in enumerate(POOL_WINDOWS):
            lo = gi * POOL_GROUP
            mg = mixed_ref[:, lo:lo + POOL_GROUP]
            dpo_g = dpo[:, lo:lo + POOL_GROUP]
            dps_ref[:, lo:lo + POOL_GROUP] += _rowsum8(dpo_g * _dot_nn(mg, poolw_ref[gi]))
            dout = (dpo_g * ps_ref[:, lo:lo + POOL_GROUP]).astype(BF16)
            dpoolw_ref[gi] += _dot_tn(mg, dout)
            dmx = _dot_nt(dout, poolw_ref[gi])
            q = dmx / jnp.minimum(pos + 1.0, float(w))
            eq_ref[0:tm, lo:lo + POOL_GROUP] = q
            s = q
            for j in range(1, w):
                s = s + eq_ref[j:j + tm, lo:lo + POOL_GROUP]
            pieces.append(s - dmx)
        dproj = jnp.concatenate(pieces, axis=-1).astype(BF16)
        dproj_ref[...] = dproj

        dh = _dot_nn(dproj, win_ref[...])
        xv = x_ref[...]
        r = _rms(xv)
        xh = xv * r
        dgm_ref[...] += _rowsum8(dh * xh)
        dxh = dh * gm_ref[...]
        dx_ref[...] = dxo + r * (dxh - xh * jnp.mean(dxh * xh, axis=-1, keepdims=True))

        @pl.when(i == nt - 1)
        def _():
            for ref in (dgm_ref, dcb_ref, dlg_ref, dlb_ref, dps_ref):
                _fold8(ref)
            for k in range(CONV_WIDTH):
                dcdw_ref[SUBLANES * k:SUBLANES * k + 1, :] = jnp.sum(
                    dcdw_ref[SUBLANES * k:SUBLANES * (k + 1), :], axis=0, keepdims=True)

    def tile(c):
        return pl.BlockSpec((tm, c), lambda i: (nt - 1 - i, 0))

    halo = pl.BlockSpec((HALO, D_CONV), lambda i: (jnp.maximum((nt - 1 - i) * halo_blocks - 1, 0), 0))
    n_groups = len(POOL_WINDOWS)
    return pl.pallas_call(
        body, name="mix_bwd", grid=(nt,),
        in_specs=[tile(d), tile(d), _full((1, d)), tile(2 * D_CONV), tile(D_CONV), halo, tile(D_CONV), tile(D_POOL),
                  _full((D_IN, d)), _full((HALO, D_CONV)), _full((1, D_CONV)), _full((1, D_CONV)),
                  _full((D_CONV, D_CONV)), _full((n_groups, POOL_GROUP, POOL_GROUP)), _full((1, D_POOL)), _full((d, d))],
        out_specs=[tile(d), tile(D_IN), tile(D_CONV), _full((SUBLANES, d)), _full((HALO * SUBLANES, D_CONV)),
                   _full((SUBLANES, D_CONV)), _full((SUBLANES, D_CONV)), _full((SUBLANES, D_CONV)),
                   _full((n_groups, POOL_GROUP, POOL_GROUP)), _full((SUBLANES, D_POOL))],
        out_shape=[jax.ShapeDtypeStruct((t, d), F32), jax.ShapeDtypeStruct((t, D_IN), BF16),
                   jax.ShapeDtypeStruct((t, D_CONV), BF16), jax.ShapeDtypeStruct((SUBLANES, d), F32),
                   jax.ShapeDtypeStruct((HALO * SUBLANES, D_CONV), F32), jax.ShapeDtypeStruct((SUBLANES, D_CONV), F32),
                   jax.ShapeDtypeStruct((SUBLANES, D_CONV), F32), jax.ShapeDtypeStruct((SUBLANES, D_CONV), F32),
                   jax.ShapeDtypeStruct((n_groups, POOL_GROUP, POOL_GROUP), F32),
                   jax.ShapeDtypeStruct((SUBLANES, D_POOL), F32)],
        scratch_shapes=[pltpu.VMEM((HALO + tm, D_CONV), F32), pltpu.VMEM((tm + HALO, D_CONV), F32),
                        pltpu.VMEM((tm + HALO, D_POOL), F32)],
        compiler_params=_params(1),
    )(dxo, x, gm, ag, u0, u0, u1, mixed, win_t, cdw, lg, lb, pw, poolw, ps, wout)


def _adamw(w, g, m, v, name):
    rows, cols = w.shape
    tr = _row_tile(rows, max(SUBLANES, (256 * 1024) // cols // SUBLANES * SUBLANES), SUBLANES)

    def body(w_ref, g_ref, m_ref, v_ref, d_ref, nm_ref, nv_ref):
        gv = g_ref[...]
        nm = ADAM_B1 * m_ref[...] + (1.0 - ADAM_B1) * gv
        nv = ADAM_B2 * v_ref[...] + (1.0 - ADAM_B2) * (gv * gv)
        m_hat = nm / (1.0 - ADAM_B1 ** ADAM_STEP)
        v_hat = nv / (1.0 - ADAM_B2 ** ADAM_STEP)
        d_ref[...] = -ADAM_LR * (m_hat / (jnp.sqrt(v_hat) + ADAM_EPS) + ADAM_WD * w_ref[...])
        nm_ref[...] = nm
        nv_ref[...] = nv

    blk = pl.BlockSpec((tr, cols), lambda i: (i, 0))
    shape = jax.ShapeDtypeStruct((rows, cols), F32)
    return pl.pallas_call(
        body, name=name, grid=(rows // tr,),
        in_specs=[blk] * 4, out_specs=[blk] * 3, out_shape=[shape] * 3,
        compiler_params=_params(1),
    )(w, g, m, v)


def _as_2d(a):
    if a.ndim == 1:
        return a.reshape(a.shape[0] // 128, 128)
    if a.ndim == 3:
        return a.reshape(a.shape[0] * a.shape[1], a.shape[2])
    return a


def _pack_weight_slab(p):
    cdw_bits = lax.bitcast_convert_type(p["conv_dw"], BF16).reshape(CONV_WIDTH, 2 * D_CONV // N_DEV)
    cdw_bits = jnp.pad(cdw_bits, ((0, 1), (0, 0))).reshape(4, D_MODEL)
    cdw_bits = jnp.pad(cdw_bits, ((0, CDW_ROWS - 4), (0, 0)))
    parts = [p["ffn1_w_gate"].T, p["ffn1_w_up"].T, p["ffn1_w_down"], p["ffn2_w_gate"].T, p["ffn2_w_up"].T,
             p["ffn2_w_down"], p["w_in"].T, p["w_out"], p["conv_pw"].reshape(D_CONV // N_DEV // 2, D_MODEL)]
    return jnp.concatenate([a.astype(BF16) for a in parts] + [cdw_bits], axis=0)


def _unpack_weight_slab(slab, offs):
    def rows(name):
        o, n = offs[name]
        return slab[:, o:o + n, :].reshape(N_DEV * n, D_MODEL)

    w = {k: rows(k) for k in ("g1", "u1", "d1", "g2", "u2", "d2", "win", "wout")}
    w["pw"] = rows("pw").reshape(D_CONV, D_CONV)
    o, _ = offs["cdw"]
    bits = slab[:, o:o + 4, :].reshape(N_DEV, CONV_WIDTH + 1, D_CONV // N_DEV, 2)[:, :CONV_WIDTH]
    cdw = lax.bitcast_convert_type(bits, F32)
    w["cdw"] = jnp.transpose(cdw, (1, 0, 2)).reshape(CONV_WIDTH, D_CONV)
    return w


def kernel(x, ffn1_norm, ffn1_w_gate, ffn1_w_up, ffn1_w_down, mix_norm, w_in, conv_dw, conv_dw_b, conv_ln_g, conv_ln_b, conv_pw, pool_w, pool_scale, w_out, ffn2_norm, ffn2_w_gate, ffn2_w_up, ffn2_w_down, final_norm, loss_target, m_ffn1_norm, m_ffn1_w_gate, m_ffn1_w_up, m_ffn1_w_down, m_mix_norm, m_w_in, m_conv_dw, m_conv_dw_b, m_conv_ln_g, m_conv_ln_b, m_conv_pw, m_pool_w, m_pool_scale, m_w_out, m_ffn2_norm, m_ffn2_w_gate, m_ffn2_w_up, m_ffn2_w_down, m_final_norm, v_ffn1_norm, v_ffn1_w_gate, v_ffn1_w_up, v_ffn1_w_down, v_mix_norm, v_w_in, v_conv_dw, v_conv_dw_b, v_conv_ln_g, v_conv_ln_b, v_conv_pw, v_pool_w, v_pool_scale, v_w_out, v_ffn2_norm, v_ffn2_w_gate, v_ffn2_w_up, v_ffn2_w_down, v_final_norm):
    given = dict(locals())
    p = {n: given[n] for n in WEIGHTS}
    f8 = ffn1_w_gate.shape[1]
    offs, slab_rows = _slab_layout(f8)
    x0 = x[0]
    target = loss_target[0]

    def row(vec):
        return vec.reshape(1, vec.shape[0])

    w = _unpack_weight_slab(_all_gather(_pack_weight_slab(p), "gather_weights"), offs)
    cdw = jnp.pad(w["cdw"], ((0, HALO - CONV_WIDTH), (0, 0)))
    poolw = pool_w.astype(BF16)

    x1, g1s, u1s, n1 = _ffn_fwd(x0, row(ffn1_norm), w["g1"], w["u1"], w["d1"], "ffn1_fwd")
    x2, h, ag, u0, u1, u2, mixed, cat = _mix_fwd(
        x1, row(mix_norm), w["win"], cdw, row(conv_dw_b), row(conv_ln_g), row(conv_ln_b), w["pw"], poolw,
        row(pool_scale), w["wout"])
    x3, g2s, u2s, n2 = _ffn_fwd(x2, row(ffn2_norm), w["g2"], w["u2"], w["d2"], "ffn2_fwd")
    dx3, d_final_norm, loss_part = _loss_head(x3, target, row(final_norm))

    dx2, h2, dg2, du2, df2, d_ffn2_norm = _ffn_bwd(dx3, x2, row(ffn2_norm), g2s, u2s, w["g2"], w["u2"], w["d2"], "ffn2_bwd")
    grads = {"g2": _tn_chunked(dg2, n2, "ffn2_dgate"), "u2": _tn_chunked(du2, n2, "ffn2_dup"),
             "d2": _tn_chunked(h2, df2, "ffn2_ddown")}
    dx1, dproj, dco, d_mix_norm, d_cdw, d_cb, d_lg, d_lb, d_poolw, d_ps = _mix_bwd(
        dx2, x1, row(mix_norm), ag, u0, u1, mixed, w["win"], cdw, row(conv_ln_g), row(conv_ln_b), w["pw"], poolw,
        row(pool_scale), w["wout"])
    grads["win"] = _tn(dproj, h, "mix_dwin")
    grads["wout"] = _tn(cat, dx2, "mix_dwout")
    d_pw = _tn(u2, dco, "mix_dpw")
    dx0, h1, dg1, du1, df1, d_ffn1_norm = _ffn_bwd(dx1, x0, row(ffn1_norm), g1s, u1s, w["g1"], w["u1"], w["d1"], "ffn1_bwd")
    grads["g1"] = _tn_chunked(dg1, n1, "ffn1_dgate")
    grads["u1"] = _tn_chunked(du1, n1, "ffn1_dup")
    grads["d1"] = _tn_chunked(h1, df1, "ffn1_ddown")

    d_cdw = d_cdw.reshape(HALO, SUBLANES, D_CONV)[:CONV_WIDTH, 0]
    d_cdw = jnp.transpose(d_cdw.reshape(CONV_WIDTH, N_DEV, D_CONV // N_DEV), (1, 0, 2)).reshape(N_DEV, -1)
    d_cdw = jnp.pad(d_cdw, ((0, 0), (0, CDW_ROWS * D_MODEL - d_cdw.shape[1]))).reshape(N_DEV, CDW_ROWS, D_MODEL)
    rep = jnp.concatenate([
        d_ffn1_norm[0:1], d_mix_norm[0:1], d_ffn2_norm[0:1], d_final_norm[0:1],
        jnp.concatenate([d_cb[0:1], d_lg[0:1]], axis=1), jnp.concatenate([d_lb[0:1], d_ps[0:1]], axis=1),
        jnp.zeros((2, D_MODEL), F32), d_poolw.reshape(-1, D_MODEL)], axis=0)
    rep = jnp.pad(rep, ((0, N_DEV * REP_ROWS - rep.shape[0]), (0, 0))).reshape(N_DEV, REP_ROWS, D_MODEL)
    parts = [grads[k].reshape(N_DEV, -1, D_MODEL) for k in ("g1", "u1", "d1", "g2", "u2", "d2", "win", "wout")]
    parts += [d_pw.reshape(N_DEV, -1, D_MODEL), d_cdw, rep]
    gslab = jnp.concatenate(parts, axis=1)

    mine = _reduce_scatter(gslab)
    o_rep, _ = offs["rep"]
    rep_all = _all_gather(mine[o_rep:o_rep + REP_ROWS], "gather_replicated").reshape(N_DEV * REP_ROWS, D_MODEL)

    def shard(name):
        o, n = offs[name]
        return mine[o:o + n]

    o_cdw, _ = offs["cdw"]
    g = {
        "ffn1_norm": rep_all[0], "mix_norm": rep_all[1], "ffn2_norm": rep_all[2], "final_norm": rep_all[3],
        "conv_dw_b": rep_all[4, :D_CONV], "conv_ln_g": rep_all[4, D_CONV:],
        "conv_ln_b": rep_all[5, :D_CONV], "pool_scale": rep_all[5, D_CONV:],
        "pool_w": rep_all[8:8 + pool_w.size // D_MODEL].reshape(pool_w.shape),
        "ffn1_w_gate": shard("g1").T, "ffn1_w_up": shard("u1").T, "ffn1_w_down": shard("d1"),
        "ffn2_w_gate": shard("g2").T, "ffn2_w_up": shard("u2").T, "ffn2_w_down": shard("d2"),
        "w_in": shard("win").T, "w_out": shard("wout"), "conv_pw": shard("pw").reshape(conv_pw.shape),
        "conv_dw": shard("cdw").reshape(-1)[:conv_dw.size].reshape(conv_dw.shape),
    }

    delta, new_m, new_v = {}, {}, {}
    for n in WEIGHTS:
        dl, nm, nv = _adamw(_as_2d(p[n]), _as_2d(g[n]), _as_2d(given["m_" + n]), _as_2d(given["v_" + n]), "adamw_" + n)
        delta[n], new_m[n], new_v[n] = (a.reshape(p[n].shape) for a in (dl, nm, nv))

    loss = lax.psum(loss_part[0, 0], MESH_AXES)
    return (loss, dx0[None], *[g[n] for n in WEIGHTS], *[delta[n] for n in WEIGHTS],
            *[new_m[n] for n in WEIGHTS], *[new_v[n] for n in WEIGHTS])
```

```python
import functools

import jax
import jax.numpy as jnp
from jax import lax
from jax.experimental import pallas as pl
from jax.experimental.pallas import tpu as pltpu

F32 = jnp.float32
BF16 = jnp.bfloat16

D_MODEL = 1024
D_CONV = 512
D_POOL = 512
D_IN = 2 * D_CONV + D_POOL
POOL_WINDOWS = (2, 4, 8, 16)
POOL_GROUP = D_POOL // len(POOL_WINDOWS)
CONV_WIDTH = 31
RMS_EPS = 1e-6
LN_EPS = 1e-5
FFN_RES_WEIGHT = 0.5

ADAM_LR = 0.001
ADAM_B1 = 0.9
ADAM_B2 = 0.999
ADAM_EPS = 1e-08
ADAM_WD = 0.01
ADAM_STEP = 10

N_DEV = 8
MESH_AXES = ("x", "y", "c")
MESH_ID = pl.DeviceIdType.MESH

SUBLANES = 8
TOKEN_TILE = 512
FFN_BWD_TILE = 256
FF_CHUNK = 256
HALO = 32
V7X_VMEM_LIMIT = 56 * 1024 * 1024
CDW_ROWS = 16
REP_ROWS = 16

WEIGHTS = ("ffn1_norm", "ffn1_w_gate", "ffn1_w_up", "ffn1_w_down", "mix_norm", "w_in", "conv_dw", "conv_dw_b",
           "conv_ln_g", "conv_ln_b", "conv_pw", "pool_w", "pool_scale", "w_out", "ffn2_norm", "ffn2_w_gate",
           "ffn2_w_up", "ffn2_w_down", "final_norm")


def _dot_nn(a, b):
    return lax.dot_general(a, b, (((1,), (0,)), ((), ())), preferred_element_type=F32)


def _dot_nt(a, b):
    return lax.dot_general(a, b, (((1,), (1,)), ((), ())), preferred_element_type=F32)


def _dot_tn(a, b):
    return lax.dot_general(a, b, (((0,), (0,)), ((), ())), preferred_element_type=F32)


def _rowsum8(v):
    r, c = v.shape
    return jnp.sum(v.reshape(r // SUBLANES, SUBLANES, c), axis=0)


def _fold8(ref):
    ref[0:1, :] = jnp.sum(ref[...], axis=0, keepdims=True)


def _row_tile(n, cap, mult):
    best = None
    for t in range(mult, min(n, cap) + 1, mult):
        if n % t == 0:
            best = t
    return n if best is None else best


def _params(n_grid):
    return pltpu.CompilerParams(dimension_semantics=("arbitrary",) * n_grid, vmem_limit_bytes=V7X_VMEM_LIMIT)


def _full(shape):
    return pl.BlockSpec(shape, lambda *_: (0,) * len(shape))


def _slab_layout(f8):
    offs, r = {}, 0
    for name, rows in (("g1", f8), ("u1", f8), ("d1", f8), ("g2", f8), ("u2", f8), ("d2", f8),
                       ("win", D_IN // N_DEV), ("wout", D_MODEL // N_DEV), ("pw", D_CONV // N_DEV // 2),
                       ("cdw", CDW_ROWS), ("rep", REP_ROWS)):
        offs[name] = (r, rows)
        r += rows
    return offs, r


def _mesh_pos():
    return lax.axis_index("x"), lax.axis_index("y"), lax.axis_index("c")


def _all_gather(shard, name):
    rows, cols = shard.shape

    def body(x_ref, out_ref, send_sems, recv_sems, local_sem):
        x, y, c = _mesh_pos()
        me, sibling = (x, y, c), (x, y, 1 - c)
        chips = [(1 - x, y), (x, 1 - y), (1 - x, 1 - y)]

        def block(px, py, pc):
            return out_ref.at[4 * px + 2 * py + pc]

        def copy(k, blk, to, src=None):
            return pltpu.make_async_remote_copy(
                src_ref=block(*blk) if src is None else src, dst_ref=block(*blk),
                send_sem=send_sems.at[k], recv_sem=recv_sems.at[k], device_id=to, device_id_type=MESH_ID)

        mine = pltpu.make_async_copy(x_ref, block(*me), local_sem)
        mine.start()
        first = [copy(0, me, sibling, src=x_ref)]
        first += [copy(1 + j, me, (*chip, c), src=x_ref) for j, chip in enumerate(chips)]
        for cp in first:
            cp.start()
        passed = [copy(4 + j, (*chip, c), sibling) for j, chip in enumerate(chips)]
        for j, chip in enumerate(chips):
            copy(1 + j, (*chip, c), me).wait_recv()
            passed[j].start()
        copy(0, sibling, me).wait_recv()
        for j, chip in enumerate(chips):
            copy(4 + j, (*chip, 1 - c), me).wait_recv()
        for cp in first + passed:
            cp.wait_send()
        mine.wait()

    return pl.pallas_call(
        body, name=name,
        out_shape=jax.ShapeDtypeStruct((N_DEV, rows, cols), shard.dtype),
        in_specs=[pl.BlockSpec(memory_space=pl.ANY)],
        out_specs=pl.BlockSpec(memory_space=pl.ANY),
        scratch_shapes=[pltpu.SemaphoreType.DMA((7,)), pltpu.SemaphoreType.DMA((7,)), pltpu.SemaphoreType.DMA],
    )(shard)


def _exchange_sibling(gslab):
    _, rows, cols = gslab.shape

    def body(g_ref, recv_ref, send_sems, recv_sems):
        x, y, c = _mesh_pos()
        cps = []
        for k in range(4):
            cp = pltpu.make_async_remote_copy(
                src_ref=g_ref.at[2 * k + (1 - c)], dst_ref=recv_ref.at[k],
                send_sem=send_sems.at[k], recv_sem=recv_sems.at[k], device_id=(x, y, 1 - c), device_id_type=MESH_ID)
            cp.start()
            cps.append(cp)
        for cp in cps:
            cp.wait()

    return pl.pallas_call(
        body, name="rs_sibling",
        out_shape=jax.ShapeDtypeStruct((4, rows, cols), gslab.dtype),
        in_specs=[pl.BlockSpec(memory_space=pl.ANY)],
        out_specs=pl.BlockSpec(memory_space=pl.ANY),
        scratch_shapes=[pltpu.SemaphoreType.DMA((4,)), pltpu.SemaphoreType.DMA((4,))],
    )(gslab)


def _exchange_chips(part):
    _, rows, cols = part.shape

    def body(p_ref, recv_ref, send_sems, recv_sems):
        x, y, c = _mesh_pos()
        peers = [(1 - x, y, c), (x, 1 - y, c), (1 - x, 1 - y, c)]
        cps = []
        for k, peer in enumerate(peers):
            cp = pltpu.make_async_remote_copy(
                src_ref=p_ref.at[k], dst_ref=recv_ref.at[k],
                send_sem=send_sems.at[k], recv_sem=recv_sems.at[k], device_id=peer, device_id_type=MESH_ID)
            cp.start()
            cps.append(cp)
        for cp in cps:
            cp.wait()

    return pl.pallas_call(
        body, name="rs_chips",
        out_shape=jax.ShapeDtypeStruct((3, rows, cols), part.dtype),
        in_specs=[pl.BlockSpec(memory_space=pl.ANY)],
        out_specs=pl.BlockSpec(memory_space=pl.ANY),
        scratch_shapes=[pltpu.SemaphoreType.DMA((3,)), pltpu.SemaphoreType.DMA((3,))],
    )(part)


def _add_chunks(gslab, recv, gid, rid, out_dtype, name):
    n = gid.shape[0]
    _, rows, cols = gslab.shape
    tr = _row_tile(rows, 1024, 16)

    def body(gid_ref, rid_ref, a_ref, b_ref, o_ref):
        o_ref[...] = (a_ref[...] + b_ref[...]).astype(out_dtype)

    grid_spec = pltpu.PrefetchScalarGridSpec(
        num_scalar_prefetch=2, grid=(n, rows // tr),
        in_specs=[pl.BlockSpec((1, tr, cols), lambda k, i, g, r: (g[k], i, 0)),
                  pl.BlockSpec((1, tr, cols), lambda k, i, g, r: (r[k], i, 0))],
        out_specs=pl.BlockSpec((1, tr, cols), lambda k, i, g, r: (k, i, 0)))
    return pl.pallas_call(
        body, name=name, grid_spec=grid_spec,
        out_shape=jax.ShapeDtypeStruct((n, rows, cols), out_dtype),
        compiler_params=_params(2),
    )(gid, rid, gslab, recv)


def _sum_partials(own, recv):
    _, rows, cols = own.shape
    tr = _row_tile(rows, 1024, 16)

    def body(o_ref, r_ref, out_ref):
        acc = o_ref[0]
        for k in range(3):
            acc = acc + r_ref[k].astype(F32)
        out_ref[...] = acc

    return pl.pallas_call(
        body, name="rs_sum", grid=(rows // tr,),
        in_specs=[pl.BlockSpec((1, tr, cols), lambda i: (0, i, 0)), pl.BlockSpec((3, tr, cols), lambda i: (0, i, 0))],
        out_specs=pl.BlockSpec((tr, cols), lambda i: (i, 0)),
        out_shape=jax.ShapeDtypeStruct((rows, cols), F32),
        compiler_params=_params(1),
    )(own, recv)


def _reduce_scatter(gslab):
    x, y, c = _mesh_pos()
    chips = [(x, y), (1 - x, y), (x, 1 - y), (1 - x, 1 - y)]
    gid = jnp.stack([4 * px + 2 * py + c for px, py in chips]).astype(jnp.int32)
    rid = jnp.stack([2 * px + py for px, py in chips]).astype(jnp.int32)
    recv = _exchange_sibling(gslab)
    own = _add_chunks(gslab, recv, gid[:1], rid[:1], F32, "rs_add_own")
    part = _add_chunks(gslab, recv, gid[1:], rid[1:], BF16, "rs_add_send")
    return _sum_partials(own, _exchange_chips(part))


def _load_weights(pairs, sems):
    cps = [pltpu.make_async_copy(src, dst, sems.at[k]) for k, (src, dst) in enumerate(pairs)]
    for cp in cps:
        cp.start()
    for cp in cps:
        cp.wait()


def _chunk_rows(c):
    return pl.ds(pl.multiple_of(c * FF_CHUNK, FF_CHUNK), FF_CHUNK)


def _rms(xv):
    return lax.rsqrt(jnp.mean(xv * xv, axis=-1, keepdims=True) + RMS_EPS)


def _ffn_fwd(x, gain, wg, wu, wd, name):
    t, d = x.shape
    f = wg.shape[0]
    nc, tm = f // FF_CHUNK, TOKEN_TILE

    def body(x_ref, gain_ref, wg_hbm, wu_hbm, wd_hbm, xo_ref, g_ref, u_ref, n_ref, wg_v, wu_v, wd_v, acc_ref, sems):
        @pl.when(pl.program_id(0) == 0)
        def _():
            _load_weights(((wg_hbm, wg_v), (wu_hbm, wu_v), (wd_hbm, wd_v)), sems)

        xv = x_ref[...]
        n_ref[...] = ((xv * _rms(xv)) * gain_ref[...]).astype(BF16)
        acc_ref[...] = jnp.zeros_like(acc_ref)

        def chunk(c, carry):
            rows = _chunk_rows(c)
            nb = n_ref[...]
            g = _dot_nt(nb, wg_v[rows, :])
            u = _dot_nt(nb, wu_v[rows, :])
            g_ref[c] = g.astype(BF16)
            u_ref[c] = u.astype(BF16)
            h = (g * jax.nn.sigmoid(g)) * u
            acc_ref[...] += _dot_nn(h.astype(BF16), wd_v[rows, :])
            return carry

        lax.fori_loop(0, nc, chunk, 0, unroll=True)
        xo_ref[...] = xv + FFN_RES_WEIGHT * acc_ref[...]

    hbm = pl.BlockSpec(memory_space=pl.ANY)
    tile = pl.BlockSpec((tm, d), lambda i: (i, 0))
    act = pl.BlockSpec((nc, tm, FF_CHUNK), lambda i: (0, i, 0))
    return pl.pallas_call(
        body, name=name, grid=(t // tm,),
        in_specs=[tile, _full((1, d)), hbm, hbm, hbm],
        out_specs=[tile, act, act, tile],
        out_shape=[jax.ShapeDtypeStruct((t, d), F32), jax.ShapeDtypeStruct((nc, t, FF_CHUNK), BF16),
                   jax.ShapeDtypeStruct((nc, t, FF_CHUNK), BF16), jax.ShapeDtypeStruct((t, d), BF16)],
        scratch_shapes=[pltpu.VMEM((f, d), BF16), pltpu.VMEM((f, d), BF16), pltpu.VMEM((f, d), BF16),
                        pltpu.VMEM((tm, d), F32), pltpu.SemaphoreType.DMA((3,))],
        compiler_params=_params(1),
    )(x, gain, wg, wu, wd)


def _ffn_bwd(dxo, x, gain, gs, us, wg, wu, wd, name):
    t, d = x.shape
    f = wg.shape[0]
    nc, tm = f // FF_CHUNK, FFN_BWD_TILE
    nt = t // tm

    def body(dxo_ref, x_ref, gain_ref, g_ref, u_ref, wg_hbm, wu_hbm, wd_hbm,
             dx_ref, h_ref, dg_ref, du_ref, df_ref, dgain_ref, wg_v, wu_v, wd_v, dn_ref, sems):
        i = pl.program_id(0)

        @pl.when(i == 0)
        def _():
            _load_weights(((wg_hbm, wg_v), (wu_hbm, wu_v), (wd_hbm, wd_v)), sems)
            dgain_ref[...] = jnp.zeros_like(dgain_ref)

        df_ref[...] = (FFN_RES_WEIGHT * dxo_ref[...]).astype(BF16)
        dn_ref[...] = jnp.zeros_like(dn_ref)

        def chunk(c, carry):
            rows = _chunk_rows(c)
            g = g_ref[c].astype(F32)
            u = u_ref[c].astype(F32)
            sg = jax.nn.sigmoid(g)
            sil = g * sg
            dh = _dot_nt(df_ref[...], wd_v[rows, :])
            h_ref[c] = (sil * u).astype(BF16)
            du_ref[c] = (dh * sil).astype(BF16)
            dg_ref[c] = (dh * u * (sg * (1.0 + g * (1.0 - sg)))).astype(BF16)
            dn_ref[...] += _dot_nn(dg_ref[c], wg_v[rows, :]) + _dot_nn(du_ref[c], wu_v[rows, :])
            return carry

        lax.fori_loop(0, nc, chunk, 0, unroll=True)
        xv = x_ref[...]
        r = _rms(xv)
        xh = xv * r
        dn = dn_ref[...]
        dgain_ref[...] += _rowsum8(dn * xh)
        dxh = dn * gain_ref[...]
        dx_ref[...] = dxo_ref[...] + r * (dxh - xh * jnp.mean(dxh * xh, axis=-1, keepdims=True))

        @pl.when(i == nt - 1)
        def _():
            _fold8(dgain_ref)

    hbm = pl.BlockSpec(memory_space=pl.ANY)
    tile = pl.BlockSpec((tm, d), lambda i: (i, 0))
    act = pl.BlockSpec((nc, tm, FF_CHUNK), lambda i: (0, i, 0))
    act_shape = jax.ShapeDtypeStruct((nc, t, FF_CHUNK), BF16)
    return pl.pallas_call(
        body, name=name, grid=(nt,),
        in_specs=[tile, tile, _full((1, d)), act, act, hbm, hbm, hbm],
        out_specs=[tile, act, act, act, tile, _full((SUBLANES, d))],
        out_shape=[jax.ShapeDtypeStruct((t, d), F32), act_shape, act_shape, act_shape,
                   jax.ShapeDtypeStruct((t, d), BF16), jax.ShapeDtypeStruct((SUBLANES, d), F32)],
        scratch_shapes=[pltpu.VMEM((f, d), BF16), pltpu.VMEM((f, d), BF16), pltpu.VMEM((f, d), BF16),
                        pltpu.VMEM((tm, d), F32), pltpu.SemaphoreType.DMA((3,))],
        compiler_params=_params(1),
    )(dxo, x, gain, gs, us, wg, wu, wd)


def _tn_chunked(a, b, name):
    nc, t, _ = a.shape
    n = b.shape[1]
    tb = _row_tile(t, 1024, TOKEN_TILE)

    def body(a_ref, b_ref, o_ref):
        @pl.when(pl.program_id(0) == 0)
        def _():
            o_ref[...] = jnp.zeros_like(o_ref)

        def chunk(c, carry):
            rows = _chunk_rows(c)
            o_ref[rows, :] += _dot_tn(a_ref[c], b_ref[...])
            return carry

        lax.fori_loop(0, nc, chunk, 0, unroll=True)

    return pl.pallas_call(
        body, name=name, grid=(t // tb,),
        in_specs=[pl.BlockSpec((nc, tb, FF_CHUNK), lambda i: (0, i, 0)), pl.BlockSpec((tb, n), lambda i: (i, 0))],
        out_specs=_full((nc * FF_CHUNK, n)),
        out_shape=jax.ShapeDtypeStruct((nc * FF_CHUNK, n), F32),
        compiler_params=_params(1),
    )(a, b)


def _tn(a, b, name):
    t, k = a.shape
    n = b.shape[1]
    tb = _row_tile(t, 1024, TOKEN_TILE)

    def body(a_ref, b_ref, o_ref):
        @pl.when(pl.program_id(0) == 0)
        def _():
            o_ref[...] = jnp.zeros_like(o_ref)

        o_ref[...] += _dot_tn(a_ref[...].astype(BF16), b_ref[...].astype(BF16))

    return pl.pallas_call(
        body, name=name, grid=(t // tb,),
        in_specs=[pl.BlockSpec((tb, k), lambda i: (i, 0)), pl.BlockSpec((tb, n), lambda i: (i, 0))],
        out_specs=_full((k, n)),
        out_shape=jax.ShapeDtypeStruct((k, n), F32),
        compiler_params=_params(1),
    )(a, b)


def _loss_head(x, target, gain):
    t, d = x.shape
    tm = TOKEN_TILE
    nt = t // tm

    def body(x_ref, tgt_ref, gain_ref, dx_ref, dgain_ref, loss_ref):
        i = pl.program_id(0)

        @pl.when(i == 0)
        def _():
            dgain_ref[...] = jnp.zeros_like(dgain_ref)
            loss_ref[...] = jnp.zeros_like(loss_ref)

        xv = x_ref[...]
        r = _rms(xv)
        xh = xv * r
        err = xh * gain_ref[...] - tgt_ref[...]
        loss_ref[...] += 0.5 * jnp.sum(jnp.mean(err * err, axis=-1, keepdims=True))
        dy = err * (1.0 / d)
        dgain_ref[...] += _rowsum8(dy * xh)
        dxh = dy * gain_ref[...]
        dx_ref[...] = r * (dxh - xh * jnp.mean(dxh * xh, axis=-1, keepdims=True))

        @pl.when(i == nt - 1)
        def _():
            _fold8(dgain_ref)

    tile = pl.BlockSpec((tm, d), lambda i: (i, 0))
    return pl.pallas_call(
        body, name="loss_head", grid=(nt,),
        in_specs=[tile, tile, _full((1, d))],
        out_specs=[tile, _full((SUBLANES, d)), _full((SUBLANES, 128))],
        out_shape=[jax.ShapeDtypeStruct((t, d), F32), jax.ShapeDtypeStruct((SUBLANES, d), F32),
                   jax.ShapeDtypeStruct((SUBLANES, 128), F32)],
        compiler_params=_params(1),
    )(x, target, gain)


def _layernorm_stats(u1):
    mu = jnp.mean(u1, axis=-1, keepdims=True)
    xc = u1 - mu
    rstd = lax.rsqrt(jnp.mean(xc * xc, axis=-1, keepdims=True) + LN_EPS)
    return xc * rstd, rstd


def _positions(tile_index, tm):
    return (tile_index * tm + lax.broadcasted_iota(jnp.int32, (tm, 1), 0)).astype(F32)


def _mix_fwd(x, gm, win_t, cdw, cb, lg, lb, pw, poolw, ps, wout):
    t, d = x.shape
    tm = TOKEN_TILE

    def body(x_ref, gm_ref, win_ref, cdw_ref, cb_ref, lg_ref, lb_ref, pw_ref, poolw_ref, ps_ref, wout_ref,
             xo_ref, h_ref, ag_ref, u0_ref, u1_ref, u2_ref, mixed_ref, cat_ref, eu_ref, ep_ref):
        i = pl.program_id(0)

        @pl.when(i == 0)
        def _():
            eu_ref[0:HALO, :] = jnp.zeros((HALO, D_CONV), F32)
            ep_ref[0:HALO, :] = jnp.zeros((HALO, D_POOL), F32)

        @pl.when(i > 0)
        def _():
            eu_ref[0:HALO, :] = eu_ref[tm:tm + HALO, :]
            ep_ref[0:HALO, :] = ep_ref[tm:tm + HALO, :]

        xv = x_ref[...]
        hb = ((xv * _rms(xv)) * gm_ref[...]).astype(BF16)
        h_ref[...] = hb
        proj = _dot_nt(hb, win_ref[...])
        a = proj[:, :D_CONV]
        g = proj[:, D_CONV:2 * D_CONV]
        ag_ref[...] = proj[:, :2 * D_CONV]
        u0 = a * jax.nn.sigmoid(g)
        u0_ref[...] = u0
        eu_ref[HALO:HALO + tm, :] = u0
        ep_ref[HALO:HALO + tm, :] = proj[:, 2 * D_CONV:]

        u1 = jnp.broadcast_to(cb_ref[...], (tm, D_CONV))
        for k in range(CONV_WIDTH):
            off = HALO - (CONV_WIDTH - 1) + k
            u1 = u1 + cdw_ref[k:k + 1, :] * eu_ref[off:off + tm, :]
        u1_ref[...] = u1
        lnh, _ = _layernorm_stats(u1)
        ln = lnh * lg_ref[...] + lb_ref[...]
        u2 = (ln * jax.nn.sigmoid(ln)).astype(BF16)
        u2_ref[...] = u2
        conv_out = _dot_nn(u2, pw_ref[...])

        pos = _positions(i, tm)
        outs = []
        for gi, w in enumerate(POOL_WINDOWS):
            lo = gi * POOL_GROUP
            p = ep_ref[HALO:HALO + tm, lo:lo + POOL_GROUP]
            s = p
            for j in range(1, w):
                s = s + ep_ref[HALO - j:HALO - j + tm, lo:lo + POOL_GROUP]
            mixed = (s / jnp.minimum(pos + 1.0, float(w)) - p).astype(BF16)
            mixed_ref[:, lo:lo + POOL_GROUP] = mixed
            outs.append(_dot_nn(mixed, poolw_ref[gi]))
        pool_out = jnp.concatenate(outs, axis=-1) * ps_ref[...]
        cat = jnp.concatenate([conv_out, pool_out], axis=-1).astype(BF16)
        cat_ref[...] = cat
        xo_ref[...] = xv + _dot_nn(cat, wout_ref[...])

    def tile(c):
        return pl.BlockSpec((tm, c), lambda i: (i, 0))

    def out(c, dt):
        return jax.ShapeDtypeStruct((t, c), dt)

    return pl.pallas_call(
        body, name="mix_fwd", grid=(t // tm,),
        in_specs=[tile(d), _full((1, d)), _full((D_IN, d)), _full((HALO, D_CONV)), _full((1, D_CONV)),
                  _full((1, D_CONV)), _full((1, D_CONV)), _full((D_CONV, D_CONV)),
                  _full((len(POOL_WINDOWS), POOL_GROUP, POOL_GROUP)), _full((1, D_POOL)), _full((d, d))],
        out_specs=[tile(d), tile(d), tile(2 * D_CONV), tile(D_CONV), tile(D_CONV), tile(D_CONV), tile(D_POOL), tile(d)],
        out_shape=[out(d, F32), out(d, BF16), out(2 * D_CONV, F32), out(D_CONV, F32), out(D_CONV, F32),
                   out(D_CONV, BF16), out(D_POOL, BF16), out(d, BF16)],
        scratch_shapes=[pltpu.VMEM((HALO + tm, D_CONV), F32), pltpu.VMEM((HALO + tm, D_POOL), F32)],
        compiler_params=_params(1),
    )(x, gm, win_t, cdw, cb, lg, lb, pw, poolw, ps, wout)


def _mix_bwd(dxo, x, gm, ag, u0, u1, mixed, win_t, cdw, lg, lb, pw, poolw, ps, wout):
    t, d = x.shape
    tm = TOKEN_TILE
    nt = t // tm
    halo_blocks = tm // HALO

    def body(dxo_ref, x_ref, gm_ref, ag_ref, u0_ref, u0h_ref, u1_ref, mixed_ref,
             win_ref, cdw_ref, lg_ref, lb_ref, pw_ref, poolw_ref, ps_ref, wout_ref,
             dx_ref, dproj_ref, dco_ref, dgm_ref, dcdw_ref, dcb_ref, dlg_ref, dlb_ref, dpoolw_ref, dps_ref,
             eu_ref, ed_ref, eq_ref):
        i = pl.program_id(0)
        ti = nt - 1 - i

        @pl.when(i == 0)
        def _():
            for ref in (dgm_ref, dcdw_ref, dcb_ref, dlg_ref, dlb_ref, dpoolw_ref, dps_ref):
                ref[...] = jnp.zeros_like(ref)
            ed_ref[tm:tm + HALO, :] = jnp.zeros((HALO, D_CONV), F32)
            eq_ref[tm:tm + HALO, :] = jnp.zeros((HALO, D_POOL), F32)

        @pl.when(i > 0)
        def _():
            ed_ref[tm:tm + HALO, :] = ed_ref[0:HALO, :]
            eq_ref[tm:tm + HALO, :] = eq_ref[0:HALO, :]

        @pl.when(ti == 0)
        def _():
            eu_ref[0:HALO, :] = jnp.zeros((HALO, D_CONV), F32)

        @pl.when(ti > 0)
        def _():
            eu_ref[0:HALO, :] = u0h_ref[...]

        eu_ref[HALO:HALO + tm, :] = u0_ref[...]

        dxo = dxo_ref[...]
        dcat = _dot_nt(dxo.astype(BF16), wout_ref[...])
        dco = dcat[:, :D_CONV].astype(BF16)
        dco_ref[...] = dco
        dpo = dcat[:, D_CONV:]

        lnh, rstd = _layernorm_stats(u1_ref[...])
        ln = lnh * lg_ref[...] + lb_ref[...]
        sl = jax.nn.sigmoid(ln)
        dln = _dot_nt(dco, pw_ref[...]) * (sl * (1.0 + ln * (1.0 - sl)))
        dlg_ref[...] += _rowsum8(dln * lnh)
        dlb_ref[...] += _rowsum8(dln)
        dlnh = dln * lg_ref[...]
        du1 = rstd * (dlnh - jnp.mean(dlnh, axis=-1, keepdims=True)
                      - lnh * jnp.mean(dlnh * lnh, axis=-1, keepdims=True))
        dcb_ref[...] += _rowsum8(du1)
        ed_ref[0:tm, :] = du1

        du0 = jnp.zeros((tm, D_CONV), F32)
        for k in range(CONV_WIDTH):
            off = CONV_WIDTH - 1 - k
            du0 = du0 + cdw_ref[k:k + 1, :] * ed_ref[off:off + tm, :]
            back = HALO - (CONV_WIDTH - 1) + k
            dcdw_ref[SUBLANES * k:SUBLANES * (k + 1), :] += _rowsum8(du1 * eu_ref[back:back + tm, :])
        a = ag_ref[:, :D_CONV]
        sg = jax.nn.sigmoid(ag_ref[:, D_CONV:])
        pieces = [du0 * sg, du0 * a * (sg * (1.0 - sg))]

        pos = _positions(ti, tm)
        for gi, w in enumerate(POOL_WINDOWS):
            lo = gi * POOL_GROUP
            mg = mixed_ref[:, lo:lo + POOL_GROUP]
            dpo_g = dpo[:, lo:lo + POOL_GROUP]
            dps_ref[:, lo:lo + POOL_GROUP] += _rowsum8(dpo_g * _dot_nn(mg, poolw_ref[gi]))
            dout = (dpo_g * ps_ref[:, lo:lo + POOL_GROUP]).astype(BF16)
            dpoolw_ref[gi] += _dot_tn(mg, dout)
            dmx = _dot_nt(dout, poolw_ref[gi])
            q = dmx / jnp.minimum(pos + 1.0, float(w))
            eq_ref[0:tm, lo:lo + POOL_GROUP] = q
            s = q
            for j in range(1, w):
                s = s + eq_ref[j:j + tm, lo:lo + POOL_GROUP]
            pieces.append(s - dmx)
        dproj = jnp.concatenate(pieces, axis=-1).astype(BF16)
        dproj_ref[...] = dproj

        dh = _dot_nn(dproj, win_ref[...])
        xv = x_ref[...]
        r = _rms(xv)
        xh = xv * r
        dgm_ref[...] += _rowsum8(dh * xh)
        dxh = dh * gm_ref[...]
        dx_ref[...] = dxo + r * (dxh - xh * jnp.mean(dxh * xh, axis=-1, keepdims=True))

        @pl.when(i == nt - 1)
        def _():
            for ref in (dgm_ref, dcb_ref, dlg_ref, dlb_ref, dps_ref):
                _fold8(ref)
            for k in range(CONV_WIDTH):
                dcdw_ref[SUBLANES * k:SUBLANES * k + 1, :] = jnp.sum(
                    dcdw_ref[SUBLANES * k:SUBLANES * (k + 1), :], axis=0, keepdims=True)

    def tile(c):
        return pl.BlockSpec((tm, c), lambda i: (nt - 1 - i, 0))

    halo = pl.BlockSpec((HALO, D_CONV), lambda i: (jnp.maximum((nt - 1 - i) * halo_blocks - 1, 0), 0))
    n_groups = len(POOL_WINDOWS)
    return pl.pallas_call(
        body, name="mix_bwd", grid=(nt,),
        in_specs=[tile(d), tile(d), _full((1, d)), tile(2 * D_CONV), tile(D_CONV), halo, tile(D_CONV), tile(D_POOL),
                  _full((D_IN, d)), _full((HALO, D_CONV)), _full((1, D_CONV)), _full((1, D_CONV)),
                  _full((D_CONV, D_CONV)), _full((n_groups, POOL_GROUP, POOL_GROUP)), _full((1, D_POOL)), _full((d, d))],
        out_specs=[tile(d), tile(D_IN), tile(D_CONV), _full((SUBLANES, d)), _full((HALO * SUBLANES, D_CONV)),
                   _full((SUBLANES, D_CONV)), _full((SUBLANES, D_CONV)), _full((SUBLANES, D_CONV)),
                   _full((n_groups, POOL_GROUP, POOL_GROUP)), _full((SUBLANES, D_POOL))],
        out_shape=[jax.ShapeDtypeStruct((t, d), F32), jax.ShapeDtypeStruct((t, D_IN), BF16),
                   jax.ShapeDtypeStruct((t, D_CONV), BF16), jax.ShapeDtypeStruct((SUBLANES, d), F32),
                   jax.ShapeDtypeStruct((HALO * SUBLANES, D_CONV), F32), jax.ShapeDtypeStruct((SUBLANES, D_CONV), F32),
                   jax.ShapeDtypeStruct((SUBLANES, D_CONV), F32), jax.ShapeDtypeStruct((SUBLANES, D_CONV), F32),
                   jax.ShapeDtypeStruct((n_groups, POOL_GROUP, POOL_GROUP), F32),
                   jax.ShapeDtypeStruct((SUBLANES, D_POOL), F32)],
        scratch_shapes=[pltpu.VMEM((HALO + tm, D_CONV), F32), pltpu.VMEM((tm + HALO, D_CONV), F32),
                        pltpu.VMEM((tm + HALO, D_POOL), F32)],
        compiler_params=_params(1),
    )(dxo, x, gm, ag, u0, u0, u1, mixed, win_t, cdw, lg, lb, pw, poolw, ps, wout)


def _adamw(w, g, m, v, name):
    rows, cols = w.shape
    tr = _row_tile(rows, max(SUBLANES, (256 * 1024) // cols // SUBLANES * SUBLANES), SUBLANES)

    def body(w_ref, g_ref, m_ref, v_ref, d_ref, nm_ref, nv_ref):
        gv = g_ref[...]
        nm = ADAM_B1 * m_ref[...] + (1.0 - ADAM_B1) * gv
        nv = ADAM_B2 * v_ref[...] + (1.0 - ADAM_B2) * (gv * gv)
        m_hat = nm / (1.0 - ADAM_B1 ** ADAM_STEP)
        v_hat = nv / (1.0 - ADAM_B2 ** ADAM_STEP)
        d_ref[...] = -ADAM_LR * (m_hat / (jnp.sqrt(v_hat) + ADAM_EPS) + ADAM_WD * w_ref[...])
        nm_ref[...] = nm
        nv_ref[...] = nv

    blk = pl.BlockSpec((tr, cols), lambda i: (i, 0))
    shape = jax.ShapeDtypeStruct((rows, cols), F32)
    return pl.pallas_call(
        body, name=name, grid=(rows // tr,),
        in_specs=[blk] * 4, out_specs=[blk] * 3, out_shape=[shape] * 3,
        compiler_params=_params(1),
    )(w, g, m, v)


def _as_2d(a):
    if a.ndim == 1:
        return a.reshape(a.shape[0] // 128, 128)
    if a.ndim == 3:
        return a.reshape(a.shape[0] * a.shape[1], a.shape[2])
    return a


def _pack_weight_slab(p):
    cdw_bits = lax.bitcast_convert_type(p["conv_dw"], BF16).reshape(CONV_WIDTH, 2 * D_CONV // N_DEV)
    cdw_bits = jnp.pad(cdw_bits, ((0, 1), (0, 0))).reshape(4, D_MODEL)
    cdw_bits = jnp.pad(cdw_bits, ((0, CDW_ROWS - 4), (0, 0)))
    parts = [p["ffn1_w_gate"].T, p["ffn1_w_up"].T, p["ffn1_w_down"], p["ffn2_w_gate"].T, p["ffn2_w_up"].T,
             p["ffn2_w_down"], p["w_in"].T, p["w_out"], p["conv_pw"].reshape(D_CONV // N_DEV // 2, D_MODEL)]
    return jnp.concatenate([a.astype(BF16) for a in parts] + [cdw_bits], axis=0)


def _unpack_weight_slab(slab, offs):
    def rows(name):
        o, n = offs[name]
        return slab[:, o:o + n, :].reshape(N_DEV * n, D_MODEL)

    w = {k: rows(k) for k in ("g1", "u1", "d1", "g2", "u2", "d2", "win", "wout")}
    w["pw"] = rows("pw").reshape(D_CONV, D_CONV)
    o, _ = offs["cdw"]
    bits = slab[:, o:o + 4, :].reshape(N_DEV, CONV_WIDTH + 1, D_CONV // N_DEV, 2)[:, :CONV_WIDTH]
    cdw = lax.bitcast_convert_type(bits, F32)
    w["cdw"] = jnp.transpose(cdw, (1, 0, 2)).reshape(CONV_WIDTH, D_CONV)
    return w


def kernel(x, ffn1_norm, ffn1_w_gate, ffn1_w_up, ffn1_w_down, mix_norm, w_in, conv_dw, conv_dw_b, conv_ln_g, conv_ln_b, conv_pw, pool_w, pool_scale, w_out, ffn2_norm, ffn2_w_gate, ffn2_w_up, ffn2_w_down, final_norm, loss_target, m_ffn1_norm, m_ffn1_w_gate, m_ffn1_w_up, m_ffn1_w_down, m_mix_norm, m_w_in, m_conv_dw, m_conv_dw_b, m_conv_ln_g, m_conv_ln_b, m_conv_pw, m_pool_w, m_pool_scale, m_w_out, m_ffn2_norm, m_ffn2_w_gate, m_ffn2_w_up, m_ffn2_w_down, m_final_norm, v_ffn1_norm, v_ffn1_w_gate, v_ffn1_w_up, v_ffn1_w_down, v_mix_norm, v_w_in, v_conv_dw, v_conv_dw_b, v_conv_ln_g, v_conv_ln_b, v_conv_pw, v_pool_w, v_pool_scale, v_w_out, v_ffn2_norm, v_ffn2_w_gate, v_ffn2_w_up, v_ffn2_w_down, v_final_norm):
    given = dict(locals())
    p = {n: given[n] for n in WEIGHTS}
    f8 = ffn1_w_gate.shape[1]
    offs, slab_rows = _slab_layout(f8)
    x0 = x[0]
    target = loss_target[0]

    def row(vec):
        return vec.reshape(1, vec.shape[0])

    w = _unpack_weight_slab(_all_gather(_pack_weight_slab(p), "gather_weights"), offs)
    cdw = jnp.pad(w["cdw"], ((0, HALO - CONV_WIDTH), (0, 0)))
    poolw = pool_w.astype(BF16)

    x1, g1s, u1s, n1 = _ffn_fwd(x0, row(ffn1_norm), w["g1"], w["u1"], w["d1"], "ffn1_fwd")
    x2, h, ag, u0, u1, u2, mixed, cat = _mix_fwd(
        x1, row(mix_norm), w["win"], cdw, row(conv_dw_b), row(conv_ln_g), row(conv_ln_b), w["pw"], poolw,
        row(pool_scale), w["wout"])
    x3, g2s, u2s, n2 = _ffn_fwd(x2, row(ffn2_norm), w["g2"], w["u2"], w["d2"], "ffn2_fwd")
    dx3, d_final_norm, loss_part = _loss_head(x3, target, row(final_norm))

    dx2, h2, dg2, du2, df2, d_ffn2_norm = _ffn_bwd(dx3, x2, row(ffn2_norm), g2s, u2s, w["g2"], w["u2"], w["d2"], "ffn2_bwd")
    grads = {"g2": _tn_chunked(dg2, n2, "ffn2_dgate"), "u2": _tn_chunked(du2, n2, "ffn2_dup"),
             "d2": _tn_chunked(h2, df2, "ffn2_ddown")}
    dx1, dproj, dco, d_mix_norm, d_cdw, d_cb, d_lg, d_lb, d_poolw, d_ps = _mix_bwd(
        dx2, x1, row(mix_norm), ag, u0, u1, mixed, w["win"], cdw, row(conv_ln_g), row(conv_ln_b), w["pw"], poolw,
        row(pool_scale), w["wout"])
    grads["win"] = _tn(dproj, h, "mix_dwin")
    grads["wout"] = _tn(cat, dx2, "mix_dwout")
    d_pw = _tn(u2, dco, "mix_dpw")
    dx0, h1, dg1, du1, df1, d_ffn1_norm = _ffn_bwd(dx1, x0, row(ffn1_norm), g1s, u1s, w["g1"], w["u1"], w["d1"], "ffn1_bwd")
    grads["g1"] = _tn_chunked(dg1, n1, "ffn1_dgate")
    grads["u1"] = _tn_chunked(du1, n1, "ffn1_dup")
    grads["d1"] = _tn_chunked(h1, df1, "ffn1_ddown")

    d_cdw = d_cdw.reshape(HALO, SUBLANES, D_CONV)[:CONV_WIDTH, 0]
    d_cdw = jnp.transpose(d_cdw.reshape(CONV_WIDTH, N_DEV, D_CONV // N_DEV), (1, 0, 2)).reshape(N_DEV, -1)
    d_cdw = jnp.pad(d_cdw, ((0, 0), (0, CDW_ROWS * D_MODEL - d_cdw.shape[1]))).reshape(N_DEV, CDW_ROWS, D_MODEL)
    rep = jnp.concatenate([
        d_ffn1_norm[0:1], d_mix_norm[0:1], d_ffn2_norm[0:1], d_final_norm[0:1],
        jnp.concatenate([d_cb[0:1], d_lg[0:1]], axis=1), jnp.concatenate([d_lb[0:1], d_ps[0:1]], axis=1),
        jnp.zeros((2, D_MODEL), F32), d_poolw.reshape(-1, D_MODEL)], axis=0)
    rep = jnp.pad(rep, ((0, N_DEV * REP_ROWS - rep.shape[0]), (0, 0))).reshape(N_DEV, REP_ROWS, D_MODEL)
    parts = [grads[k].reshape(N_DEV, -1, D_MODEL) for k in ("g1", "u1", "d1", "g2", "u2", "d2", "win", "wout")]
    parts += [d_pw.reshape(N_DEV, -1, D_MODEL), d_cdw, rep]
    gslab = jnp.concatenate(parts, axis=1)

    mine = _reduce_scatter(gslab)
    o_rep, _ = offs["rep"]
    loss_rows = jnp.pad(loss_part, ((0, 0), (0, D_MODEL - loss_part.shape[1])))
    shared = _all_gather(jnp.concatenate([mine[o_rep:o_rep + REP_ROWS], loss_rows], axis=0), "gather_replicated")
    rep_all = shared[:, :REP_ROWS].reshape(N_DEV * REP_ROWS, D_MODEL)
    loss = jnp.sum(shared[:, REP_ROWS, 0])

    def shard(name):
        o, n = offs[name]
        return mine[o:o + n]

    o_cdw, _ = offs["cdw"]
    g = {
        "ffn1_norm": rep_all[0], "mix_norm": rep_all[1], "ffn2_norm": rep_all[2], "final_norm": rep_all[3],
        "conv_dw_b": rep_all[4, :D_CONV], "conv_ln_g": rep_all[4, D_CONV:],
        "conv_ln_b": rep_all[5, :D_CONV], "pool_scale": rep_all[5, D_CONV:],
        "pool_w": rep_all[8:8 + pool_w.size // D_MODEL].reshape(pool_w.shape),
        "ffn1_w_gate": shard("g1").T, "ffn1_w_up": shard("u1").T, "ffn1_w_down": shard("d1"),
        "ffn2_w_gate": shard("g2").T, "ffn2_w_up": shard("u2").T, "ffn2_w_down": shard("d2"),
        "w_in": shard("win").T, "w_out": shard("wout"), "conv_pw": shard("pw").reshape(conv_pw.shape),
        "conv_dw": shard("cdw").reshape(-1)[:conv_dw.size].reshape(conv_dw.shape),
    }

    delta, new_m, new_v = {}, {}, {}
    for n in WEIGHTS:
        dl, nm, nv = _adamw(_as_2d(p[n]), _as_2d(g[n]), _as_2d(given["m_" + n]), _as_2d(given["v_" + n]), "adamw_" + n)
        delta[n], new_m[n], new_v[n] = (a.reshape(p[n].shape) for a in (dl, nm, nv))

    return (loss, dx0[None], *[g[n] for n in WEIGHTS], *[delta[n] for n in WEIGHTS],
            *[new_m[n] for n in WEIGHTS], *[new_v[n] for n in WEIGHTS])
```

```python
import functools

import jax
import jax.numpy as jnp
from jax import lax
from jax.experimental import pallas as pl
from jax.experimental.pallas import tpu as pltpu

F32 = jnp.float32
BF16 = jnp.bfloat16

D_MODEL = 1024
D_CONV = 512
D_POOL = 512
D_IN = 2 * D_CONV + D_POOL
POOL_WINDOWS = (2, 4, 8, 16)
POOL_GROUP = D_POOL // len(POOL_WINDOWS)
CONV_WIDTH = 31
RMS_EPS = 1e-6
LN_EPS = 1e-5
FFN_RES_WEIGHT = 0.5

ADAM_LR = 0.001
ADAM_B1 = 0.9
ADAM_B2 = 0.999
ADAM_EPS = 1e-08
ADAM_WD = 0.01
ADAM_STEP = 10

N_DEV = 8
MESH_ID = pl.DeviceIdType.MESH

SUBLANES = 8
TOKEN_TILE = 512
FFN_BWD_TILE = 256
FF_CHUNK = 256
HALO = 32
V7X_VMEM_LIMIT = 56 * 1024 * 1024
CDW_ROWS = 16
REP_ROWS = 16

WEIGHTS = ("ffn1_norm", "ffn1_w_gate", "ffn1_w_up", "ffn1_w_down", "mix_norm", "w_in", "conv_dw", "conv_dw_b",
           "conv_ln_g", "conv_ln_b", "conv_pw", "pool_w", "pool_scale", "w_out", "ffn2_norm", "ffn2_w_gate",
           "ffn2_w_up", "ffn2_w_down", "final_norm")


def _dot_nn(a, b):
    return lax.dot_general(a, b, (((1,), (0,)), ((), ())), preferred_element_type=F32)


def _dot_nt(a, b):
    return lax.dot_general(a, b, (((1,), (1,)), ((), ())), preferred_element_type=F32)


def _dot_tn(a, b):
    return lax.dot_general(a, b, (((0,), (0,)), ((), ())), preferred_element_type=F32)


def _rowsum8(v):
    r, c = v.shape
    return jnp.sum(v.reshape(r // SUBLANES, SUBLANES, c), axis=0)


def _fold8(ref):
    ref[0:1, :] = jnp.sum(ref[...], axis=0, keepdims=True)


def _row_tile(n, cap, mult):
    best = None
    for t in range(mult, min(n, cap) + 1, mult):
        if n % t == 0:
            best = t
    return n if best is None else best


def _params(n_grid):
    return pltpu.CompilerParams(dimension_semantics=("arbitrary",) * n_grid, vmem_limit_bytes=V7X_VMEM_LIMIT)


def _full(shape):
    return pl.BlockSpec(shape, lambda *_: (0,) * len(shape))


def _layout(pieces):
    offs, r = {}, 0
    for name, rows in pieces:
        offs[name] = (r, rows)
        r += rows
    return offs, r


HBM = pl.BlockSpec(memory_space=pl.ANY)


def _mesh_pos():
    return lax.axis_index("x"), lax.axis_index("y"), lax.axis_index("c")


def _remote(src, dst, send_sems, recv_sems, k, to):
    return pltpu.make_async_remote_copy(src_ref=src, dst_ref=dst, send_sem=send_sems.at[k], recv_sem=recv_sems.at[k],
                                        device_id=to, device_id_type=MESH_ID)


class _Gather:
    def __init__(self, shard):
        self.inputs = (shard,)
        self.out_shape = (jax.ShapeDtypeStruct((N_DEV, *shard.shape), shard.dtype),)
        self.scratch = (pltpu.SemaphoreType.DMA((7,)), pltpu.SemaphoreType.DMA((7,)), pltpu.SemaphoreType.DMA)

    def phases(self, ins, outs, scr):
        (x_ref,), (out_ref,), (send_sems, recv_sems, local_sem) = ins, outs, scr
        x, y, c = _mesh_pos()
        me, sibling = (x, y, c), (x, y, 1 - c)
        chips = [(1 - x, y), (x, 1 - y), (1 - x, 1 - y)]

        def block(px, py, pc):
            return out_ref.at[4 * px + 2 * py + pc]

        def copy(k, blk, to, src=None):
            return _remote(block(*blk) if src is None else src, block(*blk), send_sems, recv_sems, k, to)

        def mine():
            return pltpu.make_async_copy(x_ref, block(*me), local_sem)

        def first():
            return [copy(0, me, sibling, src=x_ref)] + [copy(1 + j, me, (*chip, c), src=x_ref) for j, chip in enumerate(chips)]

        def passed():
            return [copy(4 + j, (*chip, c), sibling) for j, chip in enumerate(chips)]

        def start():
            mine().start()
            for cp in first():
                cp.start()

        def forward():
            for j, chip in enumerate(chips):
                copy(1 + j, (*chip, c), me).wait_recv()
                passed()[j].start()

        def finish():
            copy(0, sibling, me).wait_recv()
            for j, chip in enumerate(chips):
                copy(4 + j, (*chip, 1 - c), me).wait_recv()
            for cp in first() + passed():
                cp.wait_send()
            mine().wait()

        return [start, forward, finish]


class _SiblingExchange:
    def __init__(self, src):
        self.inputs = (src,)
        self.out_shape = (jax.ShapeDtypeStruct((4, *src.shape[1:]), src.dtype),)
        self.scratch = (pltpu.SemaphoreType.DMA((4,)), pltpu.SemaphoreType.DMA((4,)))

    def phases(self, ins, outs, scr):
        (g_ref,), (recv_ref,), (send_sems, recv_sems) = ins, outs, scr
        x, y, c = _mesh_pos()

        def copies():
            return [_remote(g_ref.at[2 * k + (1 - c)], recv_ref.at[k], send_sems, recv_sems, k, (x, y, 1 - c))
                    for k in range(4)]

        def start():
            for cp in copies():
                cp.start()

        def finish():
            for cp in copies():
                cp.wait()

        return [start, finish]


class _ChipsExchange:
    def __init__(self, src):
        self.inputs = (src,)
        self.out_shape = (jax.ShapeDtypeStruct(src.shape, src.dtype),)
        self.scratch = (pltpu.SemaphoreType.DMA((3,)), pltpu.SemaphoreType.DMA((3,)))

    def phases(self, ins, outs, scr):
        (p_ref,), (recv_ref,), (send_sems, recv_sems) = ins, outs, scr
        x, y, c = _mesh_pos()
        peers = [(1 - x, y, c), (x, 1 - y, c), (1 - x, 1 - y, c)]

        def copies():
            return [_remote(p_ref.at[k], recv_ref.at[k], send_sems, recv_sems, k, peer) for k, peer in enumerate(peers)]

        def start():
            for cp in copies():
                cp.start()

        def finish():
            for cp in copies():
                cp.wait()

        return [start, finish]


def _run_comm(plan, name):
    n_in, n_out = len(plan.inputs), len(plan.out_shape)

    def body(*refs):
        for phase in plan.phases(refs[:n_in], refs[n_in:n_in + n_out], refs[n_in + n_out:]):
            phase()

    return pl.pallas_call(
        body, name=name, out_shape=list(plan.out_shape), in_specs=[HBM] * n_in, out_specs=[HBM] * n_out,
        scratch_shapes=list(plan.scratch))(*plan.inputs)


def _grid_call(body, *, name, nt, in_specs, out_specs, out_shape, scratch_shapes, args, plan=None):
    if plan is None:
        return pl.pallas_call(body, name=name, grid=(nt,), in_specs=in_specs, out_specs=out_specs, out_shape=out_shape,
                              scratch_shapes=scratch_shapes, compiler_params=_params(1))(*args)
    n_in, n_out, n_scr = len(in_specs), len(out_specs), len(scratch_shapes)
    p_in, p_out = len(plan.inputs), len(plan.out_shape)

    def with_plan(*refs):
        ins, refs = refs[:n_in], refs[n_in:]
        p_ins, refs = refs[:p_in], refs[p_in:]
        outs, refs = refs[:n_out], refs[n_out:]
        p_outs, refs = refs[:p_out], refs[p_out:]
        scr, p_scr = refs[:n_scr], refs[n_scr:]
        phases = plan.phases(p_ins, p_outs, p_scr)
        i = pl.program_id(0)
        pl.when(i == 0)(phases[0])
        for phase in phases[1:-1]:
            pl.when(i == max(nt - 3, 1))(phase)
        body(*ins, *outs, *scr)
        pl.when(i == nt - 1)(phases[-1])

    return pl.pallas_call(
        with_plan, name=name, grid=(nt,), in_specs=[*in_specs, *[HBM] * p_in], out_specs=[*out_specs, *[HBM] * p_out],
        out_shape=[*out_shape, *plan.out_shape], scratch_shapes=[*scratch_shapes, *plan.scratch],
        compiler_params=_params(1))(*args, *plan.inputs)


def _add_chunks(gslab, recv, gid, rid, out_dtype, name):
    n = gid.shape[0]
    _, rows, cols = gslab.shape
    tr = _row_tile(rows, 1024, 16)

    def body(gid_ref, rid_ref, a_ref, b_ref, o_ref):
        o_ref[...] = (a_ref[...] + b_ref[...]).astype(out_dtype)

    grid_spec = pltpu.PrefetchScalarGridSpec(
        num_scalar_prefetch=2, grid=(n, rows // tr),
        in_specs=[pl.BlockSpec((1, tr, cols), lambda k, i, g, r: (g[k], i, 0)),
                  pl.BlockSpec((1, tr, cols), lambda k, i, g, r: (r[k], i, 0))],
        out_specs=pl.BlockSpec((1, tr, cols), lambda k, i, g, r: (k, i, 0)))
    return pl.pallas_call(
        body, name=name, grid_spec=grid_spec,
        out_shape=jax.ShapeDtypeStruct((n, rows, cols), out_dtype),
        compiler_params=_params(2),
    )(gid, rid, gslab, recv)


def _sum_partials(own, recv, name):
    _, rows, cols = own.shape
    tr = _row_tile(rows, 1024, 16)

    def body(o_ref, r_ref, out_ref):
        acc = o_ref[0]
        for k in range(3):
            acc = acc + r_ref[k].astype(F32)
        out_ref[...] = acc

    return pl.pallas_call(
        body, name=name, grid=(rows // tr,),
        in_specs=[pl.BlockSpec((1, tr, cols), lambda i: (0, i, 0)), pl.BlockSpec((3, tr, cols), lambda i: (0, i, 0))],
        out_specs=pl.BlockSpec((tr, cols), lambda i: (i, 0)),
        out_shape=jax.ShapeDtypeStruct((rows, cols), F32),
        compiler_params=_params(1),
    )(own, recv)


def _chunk_ids():
    x, y, c = _mesh_pos()
    chips = [(x, y), (1 - x, y), (x, 1 - y), (1 - x, 1 - y)]
    gid = jnp.stack([4 * px + 2 * py + c for px, py in chips]).astype(jnp.int32)
    rid = jnp.stack([2 * px + py for px, py in chips]).astype(jnp.int32)
    return gid, rid


def _sibling_sums(slab, recv, tag):
    gid, rid = _chunk_ids()
    own = _add_chunks(slab, recv, gid[:1], rid[:1], F32, "rs_add_own_" + tag)
    part = _add_chunks(slab, recv, gid[1:], rid[1:], BF16, "rs_add_send_" + tag)
    return own, part


def _write_rows(piece, slab, off, slab_rows, name):
    n, r, cols = piece.shape

    def body(*refs):
        p_ref, o_ref, sem = refs[0], refs[-2], refs[-1]
        cp = pltpu.make_async_copy(p_ref, o_ref.at[:, pl.ds(off, r), :], sem)
        cp.start()
        cp.wait()

    args = (piece,) if slab is None else (piece, slab)
    return pl.pallas_call(
        body, name=name, out_shape=jax.ShapeDtypeStruct((n, slab_rows, cols), piece.dtype),
        in_specs=[HBM] * len(args), out_specs=HBM, scratch_shapes=[pltpu.SemaphoreType.DMA],
        input_output_aliases={} if slab is None else {1: 0})(*args)


def _load_weights(pairs, sems):
    cps = [pltpu.make_async_copy(src, dst, sems.at[k]) for k, (src, dst) in enumerate(pairs)]
    for cp in cps:
        cp.start()
    for cp in cps:
        cp.wait()


def _chunk_rows(c):
    return pl.ds(pl.multiple_of(c * FF_CHUNK, FF_CHUNK), FF_CHUNK)


def _rms(xv):
    return lax.rsqrt(jnp.mean(xv * xv, axis=-1, keepdims=True) + RMS_EPS)


def _ffn_fwd(x, gain, wg, wu, wd, name, plan=None):
    t, d = x.shape
    f = wg.shape[0]
    nc, tm = f // FF_CHUNK, TOKEN_TILE

    def body(x_ref, gain_ref, wg_hbm, wu_hbm, wd_hbm, xo_ref, g_ref, u_ref, n_ref, wg_v, wu_v, wd_v, acc_ref, sems):
        @pl.when(pl.program_id(0) == 0)
        def _():
            _load_weights(((wg_hbm, wg_v), (wu_hbm, wu_v), (wd_hbm, wd_v)), sems)

        xv = x_ref[...]
        n_ref[...] = ((xv * _rms(xv)) * gain_ref[...]).astype(BF16)
        acc_ref[...] = jnp.zeros_like(acc_ref)

        def chunk(c, carry):
            rows = _chunk_rows(c)
            nb = n_ref[...]
            g = _dot_nt(nb, wg_v[rows, :])
            u = _dot_nt(nb, wu_v[rows, :])
            g_ref[c] = g.astype(BF16)
            u_ref[c] = u.astype(BF16)
            h = (g * jax.nn.sigmoid(g)) * u
            acc_ref[...] += _dot_nn(h.astype(BF16), wd_v[rows, :])
            return carry

        lax.fori_loop(0, nc, chunk, 0, unroll=True)
        xo_ref[...] = xv + FFN_RES_WEIGHT * acc_ref[...]

    tile = pl.BlockSpec((tm, d), lambda i: (i, 0))
    act = pl.BlockSpec((nc, tm, FF_CHUNK), lambda i: (0, i, 0))
    return _grid_call(
        body, name=name, nt=t // tm, plan=plan,
        in_specs=[tile, _full((1, d)), HBM, HBM, HBM],
        out_specs=[tile, act, act, tile],
        out_shape=[jax.ShapeDtypeStruct((t, d), F32), jax.ShapeDtypeStruct((nc, t, FF_CHUNK), BF16),
                   jax.ShapeDtypeStruct((nc, t, FF_CHUNK), BF16), jax.ShapeDtypeStruct((t, d), BF16)],
        scratch_shapes=[pltpu.VMEM((f, d), BF16), pltpu.VMEM((f, d), BF16), pltpu.VMEM((f, d), BF16),
                        pltpu.VMEM((tm, d), F32), pltpu.SemaphoreType.DMA((3,))],
        args=(x, gain, wg, wu, wd))


def _ffn_bwd(dxo, x, gain, gs, us, wg, wu, wd, name, plan=None):
    t, d = x.shape
    f = wg.shape[0]
    nc, tm = f // FF_CHUNK, FFN_BWD_TILE
    nt = t // tm

    def body(dxo_ref, x_ref, gain_ref, g_ref, u_ref, wg_hbm, wu_hbm, wd_hbm,
             dx_ref, h_ref, dg_ref, du_ref, df_ref, dgain_ref, wg_v, wu_v, wd_v, dn_ref, sems):
        i = pl.program_id(0)

        @pl.when(i == 0)
        def _():
            _load_weights(((wg_hbm, wg_v), (wu_hbm, wu_v), (wd_hbm, wd_v)), sems)
            dgain_ref[...] = jnp.zeros_like(dgain_ref)

        df_ref[...] = (FFN_RES_WEIGHT * dxo_ref[...]).astype(BF16)
        dn_ref[...] = jnp.zeros_like(dn_ref)

        def chunk(c, carry):
            rows = _chunk_rows(c)
            g = g_ref[c].astype(F32)
            u = u_ref[c].astype(F32)
            sg = jax.nn.sigmoid(g)
            sil = g * sg
            dh = _dot_nt(df_ref[...], wd_v[rows, :])
            h_ref[c] = (sil * u).astype(BF16)
            du_ref[c] = (dh * sil).astype(BF16)
            dg_ref[c] = (dh * u * (sg * (1.0 + g * (1.0 - sg)))).astype(BF16)
            dn_ref[...] += _dot_nn(dg_ref[c], wg_v[rows, :]) + _dot_nn(du_ref[c], wu_v[rows, :])
            return carry

        lax.fori_loop(0, nc, chunk, 0, unroll=True)
        xv = x_ref[...]
        r = _rms(xv)
        xh = xv * r
        dn = dn_ref[...]
        dgain_ref[...] += _rowsum8(dn * xh)
        dxh = dn * gain_ref[...]
        dx_ref[...] = dxo_ref[...] + r * (dxh - xh * jnp.mean(dxh * xh, axis=-1, keepdims=True))

        @pl.when(i == nt - 1)
        def _():
            _fold8(dgain_ref)

    tile = pl.BlockSpec((tm, d), lambda i: (i, 0))
    act = pl.BlockSpec((nc, tm, FF_CHUNK), lambda i: (0, i, 0))
    act_shape = jax.ShapeDtypeStruct((nc, t, FF_CHUNK), BF16)
    return _grid_call(
        body, name=name, nt=nt, plan=plan,
        in_specs=[tile, tile, _full((1, d)), act, act, HBM, HBM, HBM],
        out_specs=[tile, act, act, act, tile, _full((SUBLANES, d))],
        out_shape=[jax.ShapeDtypeStruct((t, d), F32), act_shape, act_shape, act_shape,
                   jax.ShapeDtypeStruct((t, d), BF16), jax.ShapeDtypeStruct((SUBLANES, d), F32)],
        scratch_shapes=[pltpu.VMEM((f, d), BF16), pltpu.VMEM((f, d), BF16), pltpu.VMEM((f, d), BF16),
                        pltpu.VMEM((tm, d), F32), pltpu.SemaphoreType.DMA((3,))],
        args=(dxo, x, gain, gs, us, wg, wu, wd))


def _tn_chunked(a, b, name, slab, slot, slab_rows):
    nc, t, _ = a.shape
    n = b.shape[1]
    f8 = nc * FF_CHUNK // N_DEV
    tb = _row_tile(t, 1024, TOKEN_TILE)
    nt = t // tb

    def body(*refs):
        a_ref, b_ref, slab_ref, acc_ref, sems = refs[0], refs[1], refs[-3], refs[-2], refs[-1]
        i = pl.program_id(0)

        @pl.when(i == 0)
        def _():
            acc_ref[...] = jnp.zeros_like(acc_ref)

        def chunk(c, carry):
            rows = _chunk_rows(c)
            acc_ref[rows, :] += _dot_tn(a_ref[c], b_ref[...])
            return carry

        lax.fori_loop(0, nc, chunk, 0, unroll=True)

        @pl.when(i == nt - 1)
        def _():
            cps = [pltpu.make_async_copy(acc_ref.at[pl.ds(j * f8, f8), :], slab_ref.at[j, pl.ds(slot * f8, f8), :],
                                         sems.at[j]) for j in range(N_DEV)]
            for cp in cps:
                cp.start()
            for cp in cps:
                cp.wait()

    args = (a, b) if slab is None else (a, b, slab)
    return pl.pallas_call(
        body, name=name, grid=(nt,),
        in_specs=[pl.BlockSpec((nc, tb, FF_CHUNK), lambda i: (0, i, 0)), pl.BlockSpec((tb, n), lambda i: (i, 0)),
                  *([] if slab is None else [HBM])],
        out_specs=HBM,
        out_shape=jax.ShapeDtypeStruct((N_DEV, slab_rows, n), F32),
        scratch_shapes=[pltpu.VMEM((nc * FF_CHUNK, n), F32), pltpu.SemaphoreType.DMA((N_DEV,))],
        input_output_aliases={} if slab is None else {2: 0},
        compiler_params=_params(1),
    )(*args)


def _tn(a, b, name):
    t, k = a.shape
    n = b.shape[1]
    tb = _row_tile(t, 1024, TOKEN_TILE)

    def body(a_ref, b_ref, o_ref):
        @pl.when(pl.program_id(0) == 0)
        def _():
            o_ref[...] = jnp.zeros_like(o_ref)

        o_ref[...] += _dot_tn(a_ref[...].astype(BF16), b_ref[...].astype(BF16))

    return pl.pallas_call(
        body, name=name, grid=(t // tb,),
        in_specs=[pl.BlockSpec((tb, k), lambda i: (i, 0)), pl.BlockSpec((tb, n), lambda i: (i, 0))],
        out_specs=_full((k, n)),
        out_shape=jax.ShapeDtypeStruct((k, n), F32),
        compiler_params=_params(1),
    )(a, b)


def _loss_head(x, target, gain):
    t, d = x.shape
    tm = TOKEN_TILE
    nt = t // tm

    def body(x_ref, tgt_ref, gain_ref, dx_ref, dgain_ref, loss_ref):
        i = pl.program_id(0)

        @pl.when(i == 0)
        def _():
            dgain_ref[...] = jnp.zeros_like(dgain_ref)
            loss_ref[...] = jnp.zeros_like(loss_ref)

        xv = x_ref[...]
        r = _rms(xv)
        xh = xv * r
        err = xh * gain_ref[...] - tgt_ref[...]
        loss_ref[...] += 0.5 * jnp.sum(jnp.mean(err * err, axis=-1, keepdims=True))
        dy = err * (1.0 / d)
        dgain_ref[...] += _rowsum8(dy * xh)
        dxh = dy * gain_ref[...]
        dx_ref[...] = r * (dxh - xh * jnp.mean(dxh * xh, axis=-1, keepdims=True))

        @pl.when(i == nt - 1)
        def _():
            _fold8(dgain_ref)

    tile = pl.BlockSpec((tm, d), lambda i: (i, 0))
    return pl.pallas_call(
        body, name="loss_head", grid=(nt,),
        in_specs=[tile, tile, _full((1, d))],
        out_specs=[tile, _full((SUBLANES, d)), _full((SUBLANES, 128))],
        out_shape=[jax.ShapeDtypeStruct((t, d), F32), jax.ShapeDtypeStruct((SUBLANES, d), F32),
                   jax.ShapeDtypeStruct((SUBLANES, 128), F32)],
        compiler_params=_params(1),
    )(x, target, gain)


def _layernorm_stats(u1):
    mu = jnp.mean(u1, axis=-1, keepdims=True)
    xc = u1 - mu
    rstd = lax.rsqrt(jnp.mean(xc * xc, axis=-1, keepdims=True) + LN_EPS)
    return xc * rstd, rstd


def _positions(tile_index, tm):
    return (tile_index * tm + lax.broadcasted_iota(jnp.int32, (tm, 1), 0)).astype(F32)


def _mix_fwd(x, gm, win_t, cdw, cb, lg, lb, pw, poolw, ps, wout):
    t, d = x.shape
    tm = TOKEN_TILE

    def body(x_ref, gm_ref, win_ref, cdw_ref, cb_ref, lg_ref, lb_ref, pw_ref, poolw_ref, ps_ref, wout_ref,
             xo_ref, h_ref, ag_ref, u0_ref, u1_ref, u2_ref, mixed_ref, cat_ref, eu_ref, ep_ref):
        i = pl.program_id(0)

        @pl.when(i == 0)
        def _():
            eu_ref[0:HALO, :] = jnp.zeros((HALO, D_CONV), F32)
            ep_ref[0:HALO, :] = jnp.zeros((HALO, D_POOL), F32)

        @pl.when(i > 0)
        def _():
            eu_ref[0:HALO, :] = eu_ref[tm:tm + HALO, :]
            ep_ref[0:HALO, :] = ep_ref[tm:tm + HALO, :]

        xv = x_ref[...]
        hb = ((xv * _rms(xv)) * gm_ref[...]).astype(BF16)
        h_ref[...] = hb
        proj = _dot_nt(hb, win_ref[...])
        a = proj[:, :D_CONV]
        g = proj[:, D_CONV:2 * D_CONV]
        ag_ref[...] = proj[:, :2 * D_CONV]
        u0 = a * jax.nn.sigmoid(g)
        u0_ref[...] = u0
        eu_ref[HALO:HALO + tm, :] = u0
        ep_ref[HALO:HALO + tm, :] = proj[:, 2 * D_CONV:]

        u1 = jnp.broadcast_to(cb_ref[...], (tm, D_CONV))
        for k in range(CONV_WIDTH):
            off = HALO - (CONV_WIDTH - 1) + k
            u1 = u1 + cdw_ref[k:k + 1, :] * eu_ref[off:off + tm, :]
        u1_ref[...] = u1
        lnh, _ = _layernorm_stats(u1)
        ln = lnh * lg_ref[...] + lb_ref[...]
        u2 = (ln * jax.nn.sigmoid(ln)).astype(BF16)
        u2_ref[...] = u2
        conv_out = _dot_nn(u2, pw_ref[...])

        pos = _positions(i, tm)
        outs = []
        for gi, w in enumerate(POOL_WINDOWS):
            lo = gi * POOL_GROUP
            p = ep_ref[HALO:HALO + tm, lo:lo + POOL_GROUP]
            s = p
            for j in range(1, w):
                s = s + ep_ref[HALO - j:HALO - j + tm, lo:lo + POOL_GROUP]
            mixed = (s / jnp.minimum(pos + 1.0, float(w)) - p).astype(BF16)
            mixed_ref[:, lo:lo + POOL_GROUP] = mixed
            outs.append(_dot_nn(mixed, poolw_ref[gi]))
        pool_out = jnp.concatenate(outs, axis=-1) * ps_ref[...]
        cat = jnp.concatenate([conv_out, pool_out], axis=-1).astype(BF16)
        cat_ref[...] = cat
        xo_ref[...] = xv + _dot_nn(cat, wout_ref[...])

    def tile(c):
        return pl.BlockSpec((tm, c), lambda i: (i, 0))

    def out(c, dt):
        return jax.ShapeDtypeStruct((t, c), dt)

    return pl.pallas_call(
        body, name="mix_fwd", grid=(t // tm,),
        in_specs=[tile(d), _full((1, d)), _full((D_IN, d)), _full((HALO, D_CONV)), _full((1, D_CONV)),
                  _full((1, D_CONV)), _full((1, D_CONV)), _full((D_CONV, D_CONV)),
                  _full((len(POOL_WINDOWS), POOL_GROUP, POOL_GROUP)), _full((1, D_POOL)), _full((d, d))],
        out_specs=[tile(d), tile(d), tile(2 * D_CONV), tile(D_CONV), tile(D_CONV), tile(D_CONV), tile(D_POOL), tile(d)],
        out_shape=[out(d, F32), out(d, BF16), out(2 * D_CONV, F32), out(D_CONV, F32), out(D_CONV, F32),
                   out(D_CONV, BF16), out(D_POOL, BF16), out(d, BF16)],
        scratch_shapes=[pltpu.VMEM((HALO + tm, D_CONV), F32), pltpu.VMEM((HALO + tm, D_POOL), F32)],
        compiler_params=_params(1),
    )(x, gm, win_t, cdw, cb, lg, lb, pw, poolw, ps, wout)


def _mix_bwd(dxo, x, gm, ag, u0, u1, mixed, win_t, cdw, lg, lb, pw, poolw, ps, wout, plan=None):
    t, d = x.shape
    tm = TOKEN_TILE
    nt = t // tm
    halo_blocks = tm // HALO

    def body(dxo_ref, x_ref, gm_ref, ag_ref, u0_ref, u0h_ref, u1_ref, mixed_ref,
             win_ref, cdw_ref, lg_ref, lb_ref, pw_ref, poolw_ref, ps_ref, wout_ref,
             dx_ref, dproj_ref, dco_ref, dgm_ref, dcdw_ref, dcb_ref, dlg_ref, dlb_ref, dpoolw_ref, dps_ref,
             eu_ref, ed_ref, eq_ref):
        i = pl.program_id(0)
        ti = nt - 1 - i

        @pl.when(i == 0)
        def _():
            for ref in (dgm_ref, dcdw_ref, dcb_ref, dlg_ref, dlb_ref, dpoolw_ref, dps_ref):
                ref[...] = jnp.zeros_like(ref)
            ed_ref[tm:tm + HALO, :] = jnp.zeros((HALO, D_CONV), F32)
            eq_ref[tm:tm + HALO, :] = jnp.zeros((HALO, D_POOL), F32)

        @pl.when(i > 0)
        def _():
            ed_ref[tm:tm + HALO, :] = ed_ref[0:HALO, :]
            eq_ref[tm:tm + HALO, :] = eq_ref[0:HALO, :]

        @pl.when(ti == 0)
        def _():
            eu_ref[0:HALO, :] = jnp.zeros((HALO, D_CONV), F32)

        @pl.when(ti > 0)
        def _():
            eu_ref[0:HALO, :] = u0h_ref[...]

        eu_ref[HALO:HALO + tm, :] = u0_ref[...]

        dxo = dxo_ref[...]
        dcat = _dot_nt(dxo.astype(BF16), wout_ref[...])
        dco = dcat[:, :D_CONV].astype(BF16)
        dco_ref[...] = dco
        dpo = dcat[:, D_CONV:]

        lnh, rstd = _layernorm_stats(u1_ref[...])
        ln = lnh * lg_ref[...] + lb_ref[...]
        sl = jax.nn.sigmoid(ln)
        dln = _dot_nt(dco, pw_ref[...]) * (sl * (1.0 + ln * (1.0 - sl)))
        dlg_ref[...] += _rowsum8(dln * lnh)
        dlb_ref[...] += _rowsum8(dln)
        dlnh = dln * lg_ref[...]
        du1 = rstd * (dlnh - jnp.mean(dlnh, axis=-1, keepdims=True)
                      - lnh * jnp.mean(dlnh * lnh, axis=-1, keepdims=True))
        dcb_ref[...] += _rowsum8(du1)
        ed_ref[0:tm, :] = du1

        du0 = jnp.zeros((tm, D_CONV), F32)
        for k in range(CONV_WIDTH):
            off = CONV_WIDTH - 1 - k
            du0 = du0 + cdw_ref[k:k + 1, :] * ed_ref[off:off + tm, :]
            back = HALO - (CONV_WIDTH - 1) + k
            dcdw_ref[SUBLANES * k:SUBLANES * (k + 1), :] += _rowsum8(du1 * eu_ref[back:back + tm, :])
        a = ag_ref[:, :D_CONV]
        sg = jax.nn.sigmoid(ag_ref[:, D_CONV:])
        pieces = [du0 * sg, du0 * a * (sg * (1.0 - sg))]

        pos = _positions(ti, tm)
        for gi, w in enumerate(POOL_WINDOWS):
            lo = gi * POOL_GROUP
            mg = mixed_ref[:, lo:lo + POOL_GROUP]
            dpo_g = dpo[:, lo:lo + POOL_GROUP]
            dps_ref[:, lo:lo + POOL_GROUP] += _rowsum8(dpo_g * _dot_nn(mg, poolw_ref[gi]))
            dout = (dpo_g * ps_ref[:, lo:lo + POOL_GROUP]).astype(BF16)
            dpoolw_ref[gi] += _dot_tn(mg, dout)
            dmx = _dot_nt(dout, poolw_ref[gi])
            q = dmx / jnp.minimum(pos + 1.0, float(w))
            eq_ref[0:tm, lo:lo + POOL_GROUP] = q
            s = q
            for j in range(1, w):
                s = s + eq_ref[j:j + tm, lo:lo + POOL_GROUP]
            pieces.append(s - dmx)
        dproj = jnp.concatenate(pieces, axis=-1).astype(BF16)
        dproj_ref[...] = dproj

        dh = _dot_nn(dproj, win_ref[...])
        xv = x_ref[...]
        r = _rms(xv)
        xh = xv * r
        dgm_ref[...] += _rowsum8(dh * xh)
        dxh = dh * gm_ref[...]
        dx_ref[...] = dxo + r * (dxh - xh * jnp.mean(dxh * xh, axis=-1, keepdims=True))

        @pl.when(i == nt - 1)
        def _():
            for ref in (dgm_ref, dcb_ref, dlg_ref, dlb_ref, dps_ref):
                _fold8(ref)
            for k in range(CONV_WIDTH):
                dcdw_ref[SUBLANES * k:SUBLANES * k + 1, :] = jnp.sum(
                    dcdw_ref[SUBLANES * k:SUBLANES * (k + 1), :], axis=0, keepdims=True)

    def tile(c):
        return pl.BlockSpec((tm, c), lambda i: (nt - 1 - i, 0))

    halo = pl.BlockSpec((HALO, D_CONV), lambda i: (jnp.maximum((nt - 1 - i) * halo_blocks - 1, 0), 0))
    n_groups = len(POOL_WINDOWS)
    return _grid_call(
        body, name="mix_bwd", nt=nt, plan=plan,
        in_specs=[tile(d), tile(d), _full((1, d)), tile(2 * D_CONV), tile(D_CONV), halo, tile(D_CONV), tile(D_POOL),
                  _full((D_IN, d)), _full((HALO, D_CONV)), _full((1, D_CONV)), _full((1, D_CONV)),
                  _full((D_CONV, D_CONV)), _full((n_groups, POOL_GROUP, POOL_GROUP)), _full((1, D_POOL)), _full((d, d))],
        out_specs=[tile(d), tile(D_IN), tile(D_CONV), _full((SUBLANES, d)), _full((HALO * SUBLANES, D_CONV)),
                   _full((SUBLANES, D_CONV)), _full((SUBLANES, D_CONV)), _full((SUBLANES, D_CONV)),
                   _full((n_groups, POOL_GROUP, POOL_GROUP)), _full((SUBLANES, D_POOL))],
        out_shape=[jax.ShapeDtypeStruct((t, d), F32), jax.ShapeDtypeStruct((t, D_IN), BF16),
                   jax.ShapeDtypeStruct((t, D_CONV), BF16), jax.ShapeDtypeStruct((SUBLANES, d), F32),
                   jax.ShapeDtypeStruct((HALO * SUBLANES, D_CONV), F32), jax.ShapeDtypeStruct((SUBLANES, D_CONV), F32),
                   jax.ShapeDtypeStruct((SUBLANES, D_CONV), F32), jax.ShapeDtypeStruct((SUBLANES, D_CONV), F32),
                   jax.ShapeDtypeStruct((n_groups, POOL_GROUP, POOL_GROUP), F32),
                   jax.ShapeDtypeStruct((SUBLANES, D_POOL), F32)],
        scratch_shapes=[pltpu.VMEM((HALO + tm, D_CONV), F32), pltpu.VMEM((tm + HALO, D_CONV), F32),
                        pltpu.VMEM((tm + HALO, D_POOL), F32)],
        args=(dxo, x, gm, ag, u0, u0, u1, mixed, win_t, cdw, lg, lb, pw, poolw, ps, wout))


def _adamw(w, g, m, v, name):
    rows, cols = w.shape
    tr = _row_tile(rows, max(SUBLANES, (256 * 1024) // cols // SUBLANES * SUBLANES), SUBLANES)

    def body(w_ref, g_ref, m_ref, v_ref, d_ref, nm_ref, nv_ref):
        gv = g_ref[...]
        nm = ADAM_B1 * m_ref[...] + (1.0 - ADAM_B1) * gv
        nv = ADAM_B2 * v_ref[...] + (1.0 - ADAM_B2) * (gv * gv)
        m_hat = nm / (1.0 - ADAM_B1 ** ADAM_STEP)
        v_hat = nv / (1.0 - ADAM_B2 ** ADAM_STEP)
        d_ref[...] = -ADAM_LR * (m_hat / (jnp.sqrt(v_hat) + ADAM_EPS) + ADAM_WD * w_ref[...])
        nm_ref[...] = nm
        nv_ref[...] = nv

    blk = pl.BlockSpec((tr, cols), lambda i: (i, 0))
    shape = jax.ShapeDtypeStruct((rows, cols), F32)
    return pl.pallas_call(
        body, name=name, grid=(rows // tr,),
        in_specs=[blk] * 4, out_specs=[blk] * 3, out_shape=[shape] * 3,
        compiler_params=_params(1),
    )(w, g, m, v)


def _as_2d(a):
    if a.ndim == 1:
        return a.reshape(a.shape[0] // 128, 128)
    if a.ndim == 3:
        return a.reshape(a.shape[0] * a.shape[1], a.shape[2])
    return a


def _pack_weight_slabs(p):
    def bf(parts):
        return [a.astype(BF16) for a in parts]

    cdw_bits = lax.bitcast_convert_type(p["conv_dw"], BF16).reshape(CONV_WIDTH, 2 * D_CONV // N_DEV)
    cdw_bits = jnp.pad(cdw_bits, ((0, 1), (0, 0))).reshape(4, D_MODEL)
    cdw_bits = jnp.pad(cdw_bits, ((0, CDW_ROWS - 4), (0, 0)))
    first = bf([p["ffn1_w_gate"].T, p["ffn1_w_up"].T, p["ffn1_w_down"]])
    rest = bf([p["ffn2_w_gate"].T, p["ffn2_w_up"].T, p["ffn2_w_down"], p["w_in"].T, p["w_out"],
               p["conv_pw"].reshape(D_CONV // N_DEV // 2, D_MODEL)]) + [cdw_bits]
    return jnp.concatenate(first, axis=0), jnp.concatenate(rest, axis=0)


def _unpack_rows(slab, offs, names):
    out = {}
    for name in names:
        o, n = offs[name]
        out[name] = slab[:, o:o + n, :].reshape(N_DEV * n, D_MODEL)
    return out


def _unpack_conv_taps(slab, offs):
    o, _ = offs["cdw"]
    bits = slab[:, o:o + 4, :].reshape(N_DEV, CONV_WIDTH + 1, D_CONV // N_DEV, 2)[:, :CONV_WIDTH]
    cdw = lax.bitcast_convert_type(bits, F32)
    return jnp.transpose(cdw, (1, 0, 2)).reshape(CONV_WIDTH, D_CONV)


def kernel(x, ffn1_norm, ffn1_w_gate, ffn1_w_up, ffn1_w_down, mix_norm, w_in, conv_dw, conv_dw_b, conv_ln_g, conv_ln_b, conv_pw, pool_w, pool_scale, w_out, ffn2_norm, ffn2_w_gate, ffn2_w_up, ffn2_w_down, final_norm, loss_target, m_ffn1_norm, m_ffn1_w_gate, m_ffn1_w_up, m_ffn1_w_down, m_mix_norm, m_w_in, m_conv_dw, m_conv_dw_b, m_conv_ln_g, m_conv_ln_b, m_conv_pw, m_pool_w, m_pool_scale, m_w_out, m_ffn2_norm, m_ffn2_w_gate, m_ffn2_w_up, m_ffn2_w_down, m_final_norm, v_ffn1_norm, v_ffn1_w_gate, v_ffn1_w_up, v_ffn1_w_down, v_mix_norm, v_w_in, v_conv_dw, v_conv_dw_b, v_conv_ln_g, v_conv_ln_b, v_conv_pw, v_pool_w, v_pool_scale, v_w_out, v_ffn2_norm, v_ffn2_w_gate, v_ffn2_w_up, v_ffn2_w_down, v_final_norm):
    given = dict(locals())
    p = {n: given[n] for n in WEIGHTS}
    f8 = ffn1_w_gate.shape[1]
    small = (("win", D_IN // N_DEV), ("wout", D_MODEL // N_DEV), ("pw", D_CONV // N_DEV // 2), ("cdw", CDW_ROWS))
    w1_offs, _ = _layout((("g1", f8), ("u1", f8), ("d1", f8)))
    w2_offs, _ = _layout((("g2", f8), ("u2", f8), ("d2", f8)) + small)
    g2_offs, g2_rows = _layout((("g2", f8), ("u2", f8), ("d2", f8)))
    g1_offs, g1_rows = _layout((("g1", f8), ("u1", f8), ("d1", f8)) + small + (("rep", REP_ROWS),))
    x0 = x[0]
    target = loss_target[0]

    def row(vec):
        return vec.reshape(1, vec.shape[0])

    slab_first, slab_rest = _pack_weight_slabs(p)
    w = _unpack_rows(_run_comm(_Gather(slab_first), "gather_ffn1")[0], w1_offs, ("g1", "u1", "d1"))
    x1, g1s, u1s, n1, w_rest = _ffn_fwd(x0, row(ffn1_norm), w["g1"], w["u1"], w["d1"], "ffn1_fwd", _Gather(slab_rest))
    w.update(_unpack_rows(w_rest, w2_offs, ("g2", "u2", "d2", "win", "wout", "pw")))
    w["pw"] = w["pw"].reshape(D_CONV, D_CONV)
    cdw = jnp.pad(_unpack_conv_taps(w_rest, w2_offs), ((0, HALO - CONV_WIDTH), (0, 0)))
    poolw = pool_w.astype(BF16)

    x2, h, ag, u0, u1, u2, mixed, cat = _mix_fwd(
        x1, row(mix_norm), w["win"], cdw, row(conv_dw_b), row(conv_ln_g), row(conv_ln_b), w["pw"], poolw,
        row(pool_scale), w["wout"])
    x3, g2s, u2s, n2 = _ffn_fwd(x2, row(ffn2_norm), w["g2"], w["u2"], w["d2"], "ffn2_fwd")
    dx3, d_final_norm, loss_part = _loss_head(x3, target, row(final_norm))

    dx2, h2, dg2, du2, df2, d_ffn2_norm = _ffn_bwd(dx3, x2, row(ffn2_norm), g2s, u2s, w["g2"], w["u2"], w["d2"], "ffn2_bwd")
    slab2 = _tn_chunked(dg2, n2, "ffn2_dgate", None, 0, g2_rows)
    slab2 = _tn_chunked(du2, n2, "ffn2_dup", slab2, 1, g2_rows)
    slab2 = _tn_chunked(h2, df2, "ffn2_ddown", slab2, 2, g2_rows)
    dx1, dproj, dco, d_mix_norm, d_cdw, d_cb, d_lg, d_lb, d_poolw, d_ps, recv2 = _mix_bwd(
        dx2, x1, row(mix_norm), ag, u0, u1, mixed, w["win"], cdw, row(conv_ln_g), row(conv_ln_b), w["pw"], poolw,
        row(pool_scale), w["wout"], _SiblingExchange(slab2))
    own2, part2 = _sibling_sums(slab2, recv2, "ffn2")
    d_win = _tn(dproj, h, "mix_dwin")
    d_wout = _tn(cat, dx2, "mix_dwout")
    d_pw = _tn(u2, dco, "mix_dpw")
    dx0, h1, dg1, du1, df1, d_ffn1_norm, recv2 = _ffn_bwd(
        dx1, x0, row(ffn1_norm), g1s, u1s, w["g1"], w["u1"], w["d1"], "ffn1_bwd", _ChipsExchange(part2))
    mine2 = _sum_partials(own2, recv2, "rs_sum_ffn2")

    d_cdw = d_cdw.reshape(HALO, SUBLANES, D_CONV)[:CONV_WIDTH, 0]
    d_cdw = jnp.transpose(d_cdw.reshape(CONV_WIDTH, N_DEV, D_CONV // N_DEV), (1, 0, 2)).reshape(N_DEV, -1)
    d_cdw = jnp.pad(d_cdw, ((0, 0), (0, CDW_ROWS * D_MODEL - d_cdw.shape[1]))).reshape(N_DEV, CDW_ROWS, D_MODEL)
    rep = jnp.concatenate([
        d_ffn1_norm[0:1], d_mix_norm[0:1], d_ffn2_norm[0:1], d_final_norm[0:1],
        jnp.concatenate([d_cb[0:1], d_lg[0:1]], axis=1), jnp.concatenate([d_lb[0:1], d_ps[0:1]], axis=1),
        jnp.zeros((2, D_MODEL), F32), d_poolw.reshape(-1, D_MODEL)], axis=0)
    rep = jnp.pad(rep, ((0, N_DEV * REP_ROWS - rep.shape[0]), (0, 0))).reshape(N_DEV, REP_ROWS, D_MODEL)
    small_grads = jnp.concatenate([d_win.reshape(N_DEV, -1, D_MODEL), d_wout.reshape(N_DEV, -1, D_MODEL),
                                   d_pw.reshape(N_DEV, -1, D_MODEL), d_cdw, rep], axis=1)
    slab1 = _write_rows(small_grads, None, g1_offs["win"][0], g1_rows, "pack_small_grads")
    slab1 = _tn_chunked(dg1, n1, "ffn1_dgate", slab1, 0, g1_rows)
    slab1 = _tn_chunked(du1, n1, "ffn1_dup", slab1, 1, g1_rows)
    slab1 = _tn_chunked(h1, df1, "ffn1_ddown", slab1, 2, g1_rows)
    own1, part1 = _sibling_sums(slab1, _run_comm(_SiblingExchange(slab1), "rs_sibling")[0], "ffn1")
    mine1 = _sum_partials(own1, _run_comm(_ChipsExchange(part1), "rs_chips")[0], "rs_sum_ffn1")

    o_rep, _ = g1_offs["rep"]
    loss_rows = jnp.pad(loss_part, ((0, 0), (0, D_MODEL - loss_part.shape[1])))
    shared = _run_comm(_Gather(jnp.concatenate([mine1[o_rep:o_rep + REP_ROWS], loss_rows], axis=0)), "gather_replicated")[0]
    rep_all = shared[:, :REP_ROWS].reshape(N_DEV * REP_ROWS, D_MODEL)
    loss = jnp.sum(shared[:, REP_ROWS, 0])

    def shard(name):
        mine, offs = (mine2, g2_offs) if name in g2_offs else (mine1, g1_offs)
        o, n = offs[name]
        return mine[o:o + n]

    g = {
        "ffn1_norm": rep_all[0], "mix_norm": rep_all[1], "ffn2_norm": rep_all[2], "final_norm": rep_all[3],
        "conv_dw_b": rep_all[4, :D_CONV], "conv_ln_g": rep_all[4, D_CONV:],
        "conv_ln_b": rep_all[5, :D_CONV], "pool_scale": rep_all[5, D_CONV:],
        "pool_w": rep_all[8:8 + pool_w.size // D_MODEL].reshape(pool_w.shape),
        "ffn1_w_gate": shard("g1").T, "ffn1_w_up": shard("u1").T, "ffn1_w_down": shard("d1"),
        "ffn2_w_gate": shard("g2").T, "ffn2_w_up": shard("u2").T, "ffn2_w_down": shard("d2"),
        "w_in": shard("win").T, "w_out": shard("wout"), "conv_pw": shard("pw").reshape(conv_pw.shape),
        "conv_dw": shard("cdw").reshape(-1)[:conv_dw.size].reshape(conv_dw.shape),
    }

    delta, new_m, new_v = {}, {}, {}
    for n in WEIGHTS:
        dl, nm, nv = _adamw(_as_2d(p[n]), _as_2d(g[n]), _as_2d(given["m_" + n]), _as_2d(given["v_" + n]), "adamw_" + n)
        delta[n], new_m[n], new_v[n] = (a.reshape(p[n].shape) for a in (dl, nm, nv))

    return (loss, dx0[None], *[g[n] for n in WEIGHTS], *[delta[n] for n in WEIGHTS],
            *[new_m[n] for n in WEIGHTS], *[new_v[n] for n in WEIGHTS])
```

```python
import functools

import jax
import jax.numpy as jnp
from jax import lax
from jax.experimental import pallas as pl
from jax.experimental.pallas import tpu as pltpu

F32 = jnp.float32
BF16 = jnp.bfloat16

D_MODEL = 1024
D_CONV = 512
D_POOL = 512
D_IN = 2 * D_CONV + D_POOL
POOL_WINDOWS = (2, 4, 8, 16)
POOL_GROUP = D_POOL // len(POOL_WINDOWS)
CONV_WIDTH = 31
RMS_EPS = 1e-6
LN_EPS = 1e-5
FFN_RES_WEIGHT = 0.5

ADAM_LR = 0.001
ADAM_B1 = 0.9
ADAM_B2 = 0.999
ADAM_EPS = 1e-08
ADAM_WD = 0.01
ADAM_STEP = 10

N_DEV = 8
MESH_ID = pl.DeviceIdType.MESH

SUBLANES = 8
TOKEN_TILE = 512
FFN_BWD_TILE = 256
FF_CHUNK = 256
HALO = 32
V7X_VMEM_LIMIT = 56 * 1024 * 1024
CDW_ROWS = 16
REP_ROWS = 16

WEIGHTS = ("ffn1_norm", "ffn1_w_gate", "ffn1_w_up", "ffn1_w_down", "mix_norm", "w_in", "conv_dw", "conv_dw_b",
           "conv_ln_g", "conv_ln_b", "conv_pw", "pool_w", "pool_scale", "w_out", "ffn2_norm", "ffn2_w_gate",
           "ffn2_w_up", "ffn2_w_down", "final_norm")


def _dot_nn(a, b):
    return lax.dot_general(a, b, (((1,), (0,)), ((), ())), preferred_element_type=F32)


def _dot_nt(a, b):
    return lax.dot_general(a, b, (((1,), (1,)), ((), ())), preferred_element_type=F32)


def _dot_tn(a, b):
    return lax.dot_general(a, b, (((0,), (0,)), ((), ())), preferred_element_type=F32)


def _rowsum8(v):
    r, c = v.shape
    return jnp.sum(v.reshape(r // SUBLANES, SUBLANES, c), axis=0)


def _fold8(ref):
    ref[0:1, :] = jnp.sum(ref[...], axis=0, keepdims=True)


def _row_tile(n, cap, mult):
    best = None
    for t in range(mult, min(n, cap) + 1, mult):
        if n % t == 0:
            best = t
    return n if best is None else best


def _params(n_grid):
    return pltpu.CompilerParams(dimension_semantics=("arbitrary",) * n_grid, vmem_limit_bytes=V7X_VMEM_LIMIT)


def _full(shape):
    return pl.BlockSpec(shape, lambda *_: (0,) * len(shape))


def _layout(pieces):
    offs, r = {}, 0
    for name, rows in pieces:
        offs[name] = (r, rows)
        r += rows
    return offs, r


HBM = pl.BlockSpec(memory_space=pl.ANY)


def _mesh_pos():
    return lax.axis_index("x"), lax.axis_index("y"), lax.axis_index("c")


def _remote(src, dst, send_sems, recv_sems, k, to):
    return pltpu.make_async_remote_copy(src_ref=src, dst_ref=dst, send_sem=send_sems.at[k], recv_sem=recv_sems.at[k],
                                        device_id=to, device_id_type=MESH_ID)


class _Gather:
    def __init__(self, shard):
        self.inputs = (shard,)
        self.out_shape = (jax.ShapeDtypeStruct((N_DEV, *shard.shape), shard.dtype),)
        self.scratch = (pltpu.SemaphoreType.DMA((7,)), pltpu.SemaphoreType.DMA((7,)), pltpu.SemaphoreType.DMA)

    def phases(self, ins, outs, scr):
        (x_ref,), (out_ref,), (send_sems, recv_sems, local_sem) = ins, outs, scr
        x, y, c = _mesh_pos()
        me, sibling = (x, y, c), (x, y, 1 - c)
        chips = [(1 - x, y), (x, 1 - y), (1 - x, 1 - y)]

        def block(px, py, pc):
            return out_ref.at[4 * px + 2 * py + pc]

        def copy(k, blk, to, src=None):
            return _remote(block(*blk) if src is None else src, block(*blk), send_sems, recv_sems, k, to)

        def mine():
            return pltpu.make_async_copy(x_ref, block(*me), local_sem)

        def first():
            return [copy(0, me, sibling, src=x_ref)] + [copy(1 + j, me, (*chip, c), src=x_ref) for j, chip in enumerate(chips)]

        def passed():
            return [copy(4 + j, (*chip, c), sibling) for j, chip in enumerate(chips)]

        def start():
            mine().start()
            for cp in first():
                cp.start()

        def forward():
            for j, chip in enumerate(chips):
                copy(1 + j, (*chip, c), me).wait_recv()
                passed()[j].start()

        def finish():
            copy(0, sibling, me).wait_recv()
            for j, chip in enumerate(chips):
                copy(4 + j, (*chip, 1 - c), me).wait_recv()
            for cp in first() + passed():
                cp.wait_send()
            mine().wait()

        return [start, forward, finish]


class _SiblingExchange:
    def __init__(self, src):
        self.inputs = (src,)
        self.out_shape = (jax.ShapeDtypeStruct((4, *src.shape[1:]), src.dtype),)
        self.scratch = (pltpu.SemaphoreType.DMA((4,)), pltpu.SemaphoreType.DMA((4,)))

    def phases(self, ins, outs, scr):
        (g_ref,), (recv_ref,), (send_sems, recv_sems) = ins, outs, scr
        x, y, c = _mesh_pos()

        def copies():
            return [_remote(g_ref.at[2 * k + (1 - c)], recv_ref.at[k], send_sems, recv_sems, k, (x, y, 1 - c))
                    for k in range(4)]

        def start():
            for cp in copies():
                cp.start()

        def finish():
            for cp in copies():
                cp.wait()

        return [start, finish]


class _ChipsExchange:
    def __init__(self, src):
        self.inputs = (src,)
        self.out_shape = (jax.ShapeDtypeStruct(src.shape, src.dtype),)
        self.scratch = (pltpu.SemaphoreType.DMA((3,)), pltpu.SemaphoreType.DMA((3,)))

    def phases(self, ins, outs, scr):
        (p_ref,), (recv_ref,), (send_sems, recv_sems) = ins, outs, scr
        x, y, c = _mesh_pos()
        peers = [(1 - x, y, c), (x, 1 - y, c), (1 - x, 1 - y, c)]

        def copies():
            return [_remote(p_ref.at[k], recv_ref.at[k], send_sems, recv_sems, k, peer) for k, peer in enumerate(peers)]

        def start():
            for cp in copies():
                cp.start()

        def finish():
            for cp in copies():
                cp.wait()

        return [start, finish]


def _run_comm(plan, name):
    n_in, n_out = len(plan.inputs), len(plan.out_shape)

    def body(*refs):
        for phase in plan.phases(refs[:n_in], refs[n_in:n_in + n_out], refs[n_in + n_out:]):
            phase()

    return pl.pallas_call(
        body, name=name, out_shape=list(plan.out_shape), in_specs=[HBM] * n_in, out_specs=[HBM] * n_out,
        scratch_shapes=list(plan.scratch))(*plan.inputs)


def _grid_call(body, *, name, nt, in_specs, out_specs, out_shape, scratch_shapes, args, plan=None):
    if plan is None:
        return pl.pallas_call(body, name=name, grid=(nt,), in_specs=in_specs, out_specs=out_specs, out_shape=out_shape,
                              scratch_shapes=scratch_shapes, compiler_params=_params(1))(*args)
    n_in, n_out, n_scr = len(in_specs), len(out_specs), len(scratch_shapes)
    p_in, p_out = len(plan.inputs), len(plan.out_shape)

    def with_plan(*refs):
        ins, refs = refs[:n_in], refs[n_in:]
        p_ins, refs = refs[:p_in], refs[p_in:]
        outs, refs = refs[:n_out], refs[n_out:]
        p_outs, refs = refs[:p_out], refs[p_out:]
        scr, p_scr = refs[:n_scr], refs[n_scr:]
        phases = plan.phases(p_ins, p_outs, p_scr)
        i = pl.program_id(0)
        pl.when(i == 0)(phases[0])
        for phase in phases[1:-1]:
            pl.when(i == max(nt - 3, 1))(phase)
        body(*ins, *outs, *scr)
        pl.when(i == nt - 1)(phases[-1])

    return pl.pallas_call(
        with_plan, name=name, grid=(nt,), in_specs=[*in_specs, *[HBM] * p_in], out_specs=[*out_specs, *[HBM] * p_out],
        out_shape=[*out_shape, *plan.out_shape], scratch_shapes=[*scratch_shapes, *plan.scratch],
        compiler_params=_params(1))(*args, *plan.inputs)


def _add_chunks(gslab, recv, gid, rid, out_dtype, name):
    n = gid.shape[0]
    _, rows, cols = gslab.shape
    tr = _row_tile(rows, 1024, 16)

    def body(gid_ref, rid_ref, a_ref, b_ref, o_ref):
        o_ref[...] = (a_ref[...] + b_ref[...]).astype(out_dtype)

    grid_spec = pltpu.PrefetchScalarGridSpec(
        num_scalar_prefetch=2, grid=(n, rows // tr),
        in_specs=[pl.BlockSpec((1, tr, cols), lambda k, i, g, r: (g[k], i, 0)),
                  pl.BlockSpec((1, tr, cols), lambda k, i, g, r: (r[k], i, 0))],
        out_specs=pl.BlockSpec((1, tr, cols), lambda k, i, g, r: (k, i, 0)))
    return pl.pallas_call(
        body, name=name, grid_spec=grid_spec,
        out_shape=jax.ShapeDtypeStruct((n, rows, cols), out_dtype),
        compiler_params=_params(2),
    )(gid, rid, gslab, recv)


def _sum_partials(own, recv, name):
    _, rows, cols = own.shape
    tr = _row_tile(rows, 1024, 16)

    def body(o_ref, r_ref, out_ref):
        acc = o_ref[0]
        for k in range(3):
            acc = acc + r_ref[k].astype(F32)
        out_ref[...] = acc

    return pl.pallas_call(
        body, name=name, grid=(rows // tr,),
        in_specs=[pl.BlockSpec((1, tr, cols), lambda i: (0, i, 0)), pl.BlockSpec((3, tr, cols), lambda i: (0, i, 0))],
        out_specs=pl.BlockSpec((tr, cols), lambda i: (i, 0)),
        out_shape=jax.ShapeDtypeStruct((rows, cols), F32),
        compiler_params=_params(1),
    )(own, recv)


def _chunk_ids():
    x, y, c = _mesh_pos()
    chips = [(x, y), (1 - x, y), (x, 1 - y), (1 - x, 1 - y)]
    gid = jnp.stack([4 * px + 2 * py + c for px, py in chips]).astype(jnp.int32)
    rid = jnp.stack([2 * px + py for px, py in chips]).astype(jnp.int32)
    return gid, rid


def _sibling_sums(slab, recv, tag):
    gid, rid = _chunk_ids()
    own = _add_chunks(slab, recv, gid[:1], rid[:1], F32, "rs_add_own_" + tag)
    part = _add_chunks(slab, recv, gid[1:], rid[1:], BF16, "rs_add_send_" + tag)
    return own, part


def _start_slab(piece, slab_rows, name):
    n, r, cols = piece.shape

    def body(p_ref, o_ref):
        o_ref[...] = p_ref[...]

    blk = pl.BlockSpec((1, r, cols), lambda j: (j, 0, 0))
    return pl.pallas_call(
        body, name=name, grid=(n,), in_specs=[blk], out_specs=blk,
        out_shape=jax.ShapeDtypeStruct((n, slab_rows, cols), piece.dtype), compiler_params=_params(1))(piece)


def _load_weights(pairs, sems):
    cps = [pltpu.make_async_copy(src, dst, sems.at[k]) for k, (src, dst) in enumerate(pairs)]
    for cp in cps:
        cp.start()
    for cp in cps:
        cp.wait()


def _chunk_rows(c):
    return pl.ds(pl.multiple_of(c * FF_CHUNK, FF_CHUNK), FF_CHUNK)


def _rms(xv):
    return lax.rsqrt(jnp.mean(xv * xv, axis=-1, keepdims=True) + RMS_EPS)


def _ffn_fwd(x, gain, wg, wu, wd, name, plan=None):
    t, d = x.shape
    f = wg.shape[0]
    nc, tm = f // FF_CHUNK, TOKEN_TILE

    def body(x_ref, gain_ref, wg_hbm, wu_hbm, wd_hbm, xo_ref, g_ref, u_ref, n_ref, wg_v, wu_v, wd_v, acc_ref, sems):
        @pl.when(pl.program_id(0) == 0)
        def _():
            _load_weights(((wg_hbm, wg_v), (wu_hbm, wu_v), (wd_hbm, wd_v)), sems)

        xv = x_ref[...]
        n_ref[...] = ((xv * _rms(xv)) * gain_ref[...]).astype(BF16)
        acc_ref[...] = jnp.zeros_like(acc_ref)

        def chunk(c, carry):
            rows = _chunk_rows(c)
            nb = n_ref[...]
            g = _dot_nt(nb, wg_v[rows, :])
            u = _dot_nt(nb, wu_v[rows, :])
            g_ref[c] = g.astype(BF16)
            u_ref[c] = u.astype(BF16)
            h = (g * jax.nn.sigmoid(g)) * u
            acc_ref[...] += _dot_nn(h.astype(BF16), wd_v[rows, :])
            return carry

        lax.fori_loop(0, nc, chunk, 0, unroll=True)
        xo_ref[...] = xv + FFN_RES_WEIGHT * acc_ref[...]

    tile = pl.BlockSpec((tm, d), lambda i: (i, 0))
    act = pl.BlockSpec((nc, tm, FF_CHUNK), lambda i: (0, i, 0))
    return _grid_call(
        body, name=name, nt=t // tm, plan=plan,
        in_specs=[tile, _full((1, d)), HBM, HBM, HBM],
        out_specs=[tile, act, act, tile],
        out_shape=[jax.ShapeDtypeStruct((t, d), F32), jax.ShapeDtypeStruct((nc, t, FF_CHUNK), BF16),
                   jax.ShapeDtypeStruct((nc, t, FF_CHUNK), BF16), jax.ShapeDtypeStruct((t, d), BF16)],
        scratch_shapes=[pltpu.VMEM((f, d), BF16), pltpu.VMEM((f, d), BF16), pltpu.VMEM((f, d), BF16),
                        pltpu.VMEM((tm, d), F32), pltpu.SemaphoreType.DMA((3,))],
        args=(x, gain, wg, wu, wd))


def _ffn_bwd(dxo, x, gain, gs, us, wg, wu, wd, name, plan=None):
    t, d = x.shape
    f = wg.shape[0]
    nc, tm = f // FF_CHUNK, FFN_BWD_TILE
    nt = t // tm

    def body(dxo_ref, x_ref, gain_ref, g_ref, u_ref, wg_hbm, wu_hbm, wd_hbm,
             dx_ref, h_ref, dg_ref, du_ref, df_ref, dgain_ref, wg_v, wu_v, wd_v, dn_ref, sems):
        i = pl.program_id(0)

        @pl.when(i == 0)
        def _():
            _load_weights(((wg_hbm, wg_v), (wu_hbm, wu_v), (wd_hbm, wd_v)), sems)
            dgain_ref[...] = jnp.zeros_like(dgain_ref)

        df_ref[...] = (FFN_RES_WEIGHT * dxo_ref[...]).astype(BF16)
        dn_ref[...] = jnp.zeros_like(dn_ref)

        def chunk(c, carry):
            rows = _chunk_rows(c)
            g = g_ref[c].astype(F32)
            u = u_ref[c].astype(F32)
            sg = jax.nn.sigmoid(g)
            sil = g * sg
            dh = _dot_nt(df_ref[...], wd_v[rows, :])
            h_ref[c] = (sil * u).astype(BF16)
            du_ref[c] = (dh * sil).astype(BF16)
            dg_ref[c] = (dh * u * (sg * (1.0 + g * (1.0 - sg)))).astype(BF16)
            dn_ref[...] += _dot_nn(dg_ref[c], wg_v[rows, :]) + _dot_nn(du_ref[c], wu_v[rows, :])
            return carry

        lax.fori_loop(0, nc, chunk, 0, unroll=True)
        xv = x_ref[...]
        r = _rms(xv)
        xh = xv * r
        dn = dn_ref[...]
        dgain_ref[...] += _rowsum8(dn * xh)
        dxh = dn * gain_ref[...]
        dx_ref[...] = dxo_ref[...] + r * (dxh - xh * jnp.mean(dxh * xh, axis=-1, keepdims=True))

        @pl.when(i == nt - 1)
        def _():
            _fold8(dgain_ref)

    tile = pl.BlockSpec((tm, d), lambda i: (i, 0))
    act = pl.BlockSpec((nc, tm, FF_CHUNK), lambda i: (0, i, 0))
    act_shape = jax.ShapeDtypeStruct((nc, t, FF_CHUNK), BF16)
    return _grid_call(
        body, name=name, nt=nt, plan=plan,
        in_specs=[tile, tile, _full((1, d)), act, act, HBM, HBM, HBM],
        out_specs=[tile, act, act, act, tile, _full((SUBLANES, d))],
        out_shape=[jax.ShapeDtypeStruct((t, d), F32), act_shape, act_shape, act_shape,
                   jax.ShapeDtypeStruct((t, d), BF16), jax.ShapeDtypeStruct((SUBLANES, d), F32)],
        scratch_shapes=[pltpu.VMEM((f, d), BF16), pltpu.VMEM((f, d), BF16), pltpu.VMEM((f, d), BF16),
                        pltpu.VMEM((tm, d), F32), pltpu.SemaphoreType.DMA((3,))],
        args=(dxo, x, gain, gs, us, wg, wu, wd))


def _tn_chunked(a, b, name, slab, off, slab_rows):
    nc, t, _ = a.shape
    n = b.shape[1]
    f8 = nc * FF_CHUNK // N_DEV
    tb = _row_tile(t, 1024, TOKEN_TILE)
    nt = t // tb

    def body(*refs):
        a_ref, b_ref, slab_ref, acc_ref, sems = refs[0], refs[1], refs[-3], refs[-2], refs[-1]
        i = pl.program_id(0)

        @pl.when(i == 0)
        def _():
            acc_ref[...] = jnp.zeros_like(acc_ref)

        def chunk(c, carry):
            rows = _chunk_rows(c)
            acc_ref[rows, :] += _dot_tn(a_ref[c], b_ref[...])
            return carry

        lax.fori_loop(0, nc, chunk, 0, unroll=True)

        @pl.when(i == nt - 1)
        def _():
            cps = [pltpu.make_async_copy(acc_ref.at[pl.ds(j * f8, f8), :], slab_ref.at[j, pl.ds(off, f8), :],
                                         sems.at[j]) for j in range(N_DEV)]
            for cp in cps:
                cp.start()
            for cp in cps:
                cp.wait()

    args = (a, b) if slab is None else (a, b, slab)
    return pl.pallas_call(
        body, name=name, grid=(nt,),
        in_specs=[pl.BlockSpec((nc, tb, FF_CHUNK), lambda i: (0, i, 0)), pl.BlockSpec((tb, n), lambda i: (i, 0)),
                  *([] if slab is None else [HBM])],
        out_specs=HBM,
        out_shape=jax.ShapeDtypeStruct((N_DEV, slab_rows, n), F32),
        scratch_shapes=[pltpu.VMEM((nc * FF_CHUNK, n), F32), pltpu.SemaphoreType.DMA((N_DEV,))],
        input_output_aliases={} if slab is None else {2: 0},
        compiler_params=_params(1),
    )(*args)


def _tn(a, b, name):
    t, k = a.shape
    n = b.shape[1]
    tb = _row_tile(t, 1024, TOKEN_TILE)

    def body(a_ref, b_ref, o_ref):
        @pl.when(pl.program_id(0) == 0)
        def _():
            o_ref[...] = jnp.zeros_like(o_ref)

        o_ref[...] += _dot_tn(a_ref[...].astype(BF16), b_ref[...].astype(BF16))

    return pl.pallas_call(
        body, name=name, grid=(t // tb,),
        in_specs=[pl.BlockSpec((tb, k), lambda i: (i, 0)), pl.BlockSpec((tb, n), lambda i: (i, 0))],
        out_specs=_full((k, n)),
        out_shape=jax.ShapeDtypeStruct((k, n), F32),
        compiler_params=_params(1),
    )(a, b)


def _loss_head(x, target, gain):
    t, d = x.shape
    tm = TOKEN_TILE
    nt = t // tm

    def body(x_ref, tgt_ref, gain_ref, dx_ref, dgain_ref, loss_ref):
        i = pl.program_id(0)

        @pl.when(i == 0)
        def _():
            dgain_ref[...] = jnp.zeros_like(dgain_ref)
            loss_ref[...] = jnp.zeros_like(loss_ref)

        xv = x_ref[...]
        r = _rms(xv)
        xh = xv * r
        err = xh * gain_ref[...] - tgt_ref[...]
        loss_ref[...] += 0.5 * jnp.sum(jnp.mean(err * err, axis=-1, keepdims=True))
        dy = err * (1.0 / d)
        dgain_ref[...] += _rowsum8(dy * xh)
        dxh = dy * gain_ref[...]
        dx_ref[...] = r * (dxh - xh * jnp.mean(dxh * xh, axis=-1, keepdims=True))

        @pl.when(i == nt - 1)
        def _():
            _fold8(dgain_ref)

    tile = pl.BlockSpec((tm, d), lambda i: (i, 0))
    return pl.pallas_call(
        body, name="loss_head", grid=(nt,),
        in_specs=[tile, tile, _full((1, d))],
        out_specs=[tile, _full((SUBLANES, d)), _full((SUBLANES, 128))],
        out_shape=[jax.ShapeDtypeStruct((t, d), F32), jax.ShapeDtypeStruct((SUBLANES, d), F32),
                   jax.ShapeDtypeStruct((SUBLANES, 128), F32)],
        compiler_params=_params(1),
    )(x, target, gain)


def _layernorm_stats(u1):
    mu = jnp.mean(u1, axis=-1, keepdims=True)
    xc = u1 - mu
    rstd = lax.rsqrt(jnp.mean(xc * xc, axis=-1, keepdims=True) + LN_EPS)
    return xc * rstd, rstd


def _positions(tile_index, tm):
    return (tile_index * tm + lax.broadcasted_iota(jnp.int32, (tm, 1), 0)).astype(F32)


def _shifted_taps(src_ref, sh_ref, tm, offset_of):
    groups = {}
    for k in range(CONV_WIDTH):
        groups.setdefault(offset_of(k) % SUBLANES, []).append(k)
    span = tm + HALO - SUBLANES
    for rem, taps in sorted(groups.items()):
        if rem:
            sh_ref[0:span, :] = src_ref[rem:rem + span, :]
        ref = sh_ref if rem else src_ref
        for k in taps:
            base = offset_of(k) - rem
            yield k, ref[base:base + tm, :]


def _mix_fwd(x, gm, win_t, cdw, cb, lg, lb, pw, poolw, ps, wout):
    t, d = x.shape
    tm = TOKEN_TILE

    def body(x_ref, gm_ref, win_ref, cdw_ref, cb_ref, lg_ref, lb_ref, pw_ref, poolw_ref, ps_ref, wout_ref,
             xo_ref, h_ref, ag_ref, u0_ref, u1_ref, u2_ref, mixed_ref, cat_ref, eu_ref, ep_ref, sh_ref):
        i = pl.program_id(0)

        @pl.when(i == 0)
        def _():
            eu_ref[0:HALO, :] = jnp.zeros((HALO, D_CONV), F32)
            ep_ref[0:HALO, :] = jnp.zeros((HALO, D_POOL), F32)

        @pl.when(i > 0)
        def _():
            eu_ref[0:HALO, :] = eu_ref[tm:tm + HALO, :]
            ep_ref[0:HALO, :] = ep_ref[tm:tm + HALO, :]

        xv = x_ref[...]
        hb = ((xv * _rms(xv)) * gm_ref[...]).astype(BF16)
        h_ref[...] = hb
        proj = _dot_nt(hb, win_ref[...])
        a = proj[:, :D_CONV]
        g = proj[:, D_CONV:2 * D_CONV]
        ag_ref[...] = proj[:, :2 * D_CONV]
        u0 = a * jax.nn.sigmoid(g)
        u0_ref[...] = u0
        eu_ref[HALO:HALO + tm, :] = u0
        ep_ref[HALO:HALO + tm, :] = proj[:, 2 * D_CONV:]

        u1 = jnp.broadcast_to(cb_ref[...], (tm, D_CONV))
        for k, rows in _shifted_taps(eu_ref, sh_ref, tm, lambda k: HALO - (CONV_WIDTH - 1) + k):
            u1 = u1 + cdw_ref[k:k + 1, :] * rows
        u1_ref[...] = u1
        lnh, _ = _layernorm_stats(u1)
        ln = lnh * lg_ref[...] + lb_ref[...]
        u2 = (ln * jax.nn.sigmoid(ln)).astype(BF16)
        u2_ref[...] = u2
        conv_out = _dot_nn(u2, pw_ref[...])

        pos = _positions(i, tm)
        outs = []
        for gi, w in enumerate(POOL_WINDOWS):
            lo = gi * POOL_GROUP
            p = ep_ref[HALO:HALO + tm, lo:lo + POOL_GROUP]
            s = p
            for j in range(1, w):
                s = s + ep_ref[HALO - j:HALO - j + tm, lo:lo + POOL_GROUP]
            mixed = (s / jnp.minimum(pos + 1.0, float(w)) - p).astype(BF16)
            mixed_ref[:, lo:lo + POOL_GROUP] = mixed
            outs.append(_dot_nn(mixed, poolw_ref[gi]))
        pool_out = jnp.concatenate(outs, axis=-1) * ps_ref[...]
        cat = jnp.concatenate([conv_out, pool_out], axis=-1).astype(BF16)
        cat_ref[...] = cat
        xo_ref[...] = xv + _dot_nn(cat, wout_ref[...])

    def tile(c):
        return pl.BlockSpec((tm, c), lambda i: (i, 0))

    def out(c, dt):
        return jax.ShapeDtypeStruct((t, c), dt)

    return pl.pallas_call(
        body, name="mix_fwd", grid=(t // tm,),
        in_specs=[tile(d), _full((1, d)), _full((D_IN, d)), _full((HALO, D_CONV)), _full((1, D_CONV)),
                  _full((1, D_CONV)), _full((1, D_CONV)), _full((D_CONV, D_CONV)),
                  _full((len(POOL_WINDOWS), POOL_GROUP, POOL_GROUP)), _full((1, D_POOL)), _full((d, d))],
        out_specs=[tile(d), tile(d), tile(2 * D_CONV), tile(D_CONV), tile(D_CONV), tile(D_CONV), tile(D_POOL), tile(d)],
        out_shape=[out(d, F32), out(d, BF16), out(2 * D_CONV, F32), out(D_CONV, F32), out(D_CONV, F32),
                   out(D_CONV, BF16), out(D_POOL, BF16), out(d, BF16)],
        scratch_shapes=[pltpu.VMEM((HALO + tm, D_CONV), F32), pltpu.VMEM((HALO + tm, D_POOL), F32),
                        pltpu.VMEM((HALO + tm, D_CONV), F32)],
        compiler_params=_params(1),
    )(x, gm, win_t, cdw, cb, lg, lb, pw, poolw, ps, wout)


def _mix_bwd(dxo, x, gm, ag, u0, u1, mixed, win_t, cdw, lg, lb, pw, poolw, ps, wout, plan=None):
    t, d = x.shape
    tm = TOKEN_TILE
    nt = t // tm
    halo_blocks = tm // HALO

    def body(dxo_ref, x_ref, gm_ref, ag_ref, u0_ref, u0h_ref, u1_ref, mixed_ref,
             win_ref, cdw_ref, lg_ref, lb_ref, pw_ref, poolw_ref, ps_ref, wout_ref,
             dx_ref, dproj_ref, dco_ref, dgm_ref, dcdw_ref, dcb_ref, dlg_ref, dlb_ref, dpoolw_ref, dps_ref,
             eu_ref, ed_ref, eq_ref, sh_ref):
        i = pl.program_id(0)
        ti = nt - 1 - i

        @pl.when(i == 0)
        def _():
            for ref in (dgm_ref, dcdw_ref, dcb_ref, dlg_ref, dlb_ref, dpoolw_ref, dps_ref):
                ref[...] = jnp.zeros_like(ref)
            ed_ref[tm:tm + HALO, :] = jnp.zeros((HALO, D_CONV), F32)
            eq_ref[tm:tm + HALO, :] = jnp.zeros((HALO, D_POOL), F32)

        @pl.when(i > 0)
        def _():
            ed_ref[tm:tm + HALO, :] = ed_ref[0:HALO, :]
            eq_ref[tm:tm + HALO, :] = eq_ref[0:HALO, :]

        @pl.when(ti == 0)
        def _():
            eu_ref[0:HALO, :] = jnp.zeros((HALO, D_CONV), F32)

        @pl.when(ti > 0)
        def _():
            eu_ref[0:HALO, :] = u0h_ref[...]

        eu_ref[HALO:HALO + tm, :] = u0_ref[...]

        dxo = dxo_ref[...]
        dcat = _dot_nt(dxo.astype(BF16), wout_ref[...])
        dco = dcat[:, :D_CONV].astype(BF16)
        dco_ref[...] = dco
        dpo = dcat[:, D_CONV:]

        lnh, rstd = _layernorm_stats(u1_ref[...])
        ln = lnh * lg_ref[...] + lb_ref[...]
        sl = jax.nn.sigmoid(ln)
        dln = _dot_nt(dco, pw_ref[...]) * (sl * (1.0 + ln * (1.0 - sl)))
        dlg_ref[...] += _rowsum8(dln * lnh)
        dlb_ref[...] += _rowsum8(dln)
        dlnh = dln * lg_ref[...]
        du1 = rstd * (dlnh - jnp.mean(dlnh, axis=-1, keepdims=True)
                      - lnh * jnp.mean(dlnh * lnh, axis=-1, keepdims=True))
        dcb_ref[...] += _rowsum8(du1)
        ed_ref[0:tm, :] = du1

        du0 = jnp.zeros((tm, D_CONV), F32)
        for k, rows in _shifted_taps(ed_ref, sh_ref, tm, lambda k: CONV_WIDTH - 1 - k):
            du0 = du0 + cdw_ref[k:k + 1, :] * rows
        for k, rows in _shifted_taps(eu_ref, sh_ref, tm, lambda k: HALO - (CONV_WIDTH - 1) + k):
            dcdw_ref[SUBLANES * k:SUBLANES * (k + 1), :] += _rowsum8(du1 * rows)
        a = ag_ref[:, :D_CONV]
        sg = jax.nn.sigmoid(ag_ref[:, D_CONV:])
        pieces = [du0 * sg, du0 * a * (sg * (1.0 - sg))]

        pos = _positions(ti, tm)
        for gi, w in enumerate(POOL_WINDOWS):
            lo = gi * POOL_GROUP
            mg = mixed_ref[:, lo:lo + POOL_GROUP]
            dpo_g = dpo[:, lo:lo + POOL_GROUP]
            dps_ref[:, lo:lo + POOL_GROUP] += _rowsum8(dpo_g * _dot_nn(mg, poolw_ref[gi]))
            dout = (dpo_g * ps_ref[:, lo:lo + POOL_GROUP]).astype(BF16)
            dpoolw_ref[gi] += _dot_tn(mg, dout)
            dmx = _dot_nt(dout, poolw_ref[gi])
            q = dmx / jnp.minimum(pos + 1.0, float(w))
            eq_ref[0:tm, lo:lo + POOL_GROUP] = q
            s = q
            for j in range(1, w):
                s = s + eq_ref[j:j + tm, lo:lo + POOL_GROUP]
            pieces.append(s - dmx)
        dproj = jnp.concatenate(pieces, axis=-1).astype(BF16)
        dproj_ref[...] = dproj

        dh = _dot_nn(dproj, win_ref[...])
        xv = x_ref[...]
        r = _rms(xv)
        xh = xv * r
        dgm_ref[...] += _rowsum8(dh * xh)
        dxh = dh * gm_ref[...]
        dx_ref[...] = dxo + r * (dxh - xh * jnp.mean(dxh * xh, axis=-1, keepdims=True))

        @pl.when(i == nt - 1)
        def _():
            for ref in (dgm_ref, dcb_ref, dlg_ref, dlb_ref, dps_ref):
                _fold8(ref)
            for k in range(CONV_WIDTH):
                dcdw_ref[SUBLANES * k:SUBLANES * k + 1, :] = jnp.sum(
                    dcdw_ref[SUBLANES * k:SUBLANES * (k + 1), :], axis=0, keepdims=True)

    def tile(c):
        return pl.BlockSpec((tm, c), lambda i: (nt - 1 - i, 0))

    halo = pl.BlockSpec((HALO, D_CONV), lambda i: (jnp.maximum((nt - 1 - i) * halo_blocks - 1, 0), 0))
    n_groups = len(POOL_WINDOWS)
    return _grid_call(
        body, name="mix_bwd", nt=nt, plan=plan,
        in_specs=[tile(d), tile(d), _full((1, d)), tile(2 * D_CONV), tile(D_CONV), halo, tile(D_CONV), tile(D_POOL),
                  _full((D_IN, d)), _full((HALO, D_CONV)), _full((1, D_CONV)), _full((1, D_CONV)),
                  _full((D_CONV, D_CONV)), _full((n_groups, POOL_GROUP, POOL_GROUP)), _full((1, D_POOL)), _full((d, d))],
        out_specs=[tile(d), tile(D_IN), tile(D_CONV), _full((SUBLANES, d)), _full((HALO * SUBLANES, D_CONV)),
                   _full((SUBLANES, D_CONV)), _full((SUBLANES, D_CONV)), _full((SUBLANES, D_CONV)),
                   _full((n_groups, POOL_GROUP, POOL_GROUP)), _full((SUBLANES, D_POOL))],
        out_shape=[jax.ShapeDtypeStruct((t, d), F32), jax.ShapeDtypeStruct((t, D_IN), BF16),
                   jax.ShapeDtypeStruct((t, D_CONV), BF16), jax.ShapeDtypeStruct((SUBLANES, d), F32),
                   jax.ShapeDtypeStruct((HALO * SUBLANES, D_CONV), F32), jax.ShapeDtypeStruct((SUBLANES, D_CONV), F32),
                   jax.ShapeDtypeStruct((SUBLANES, D_CONV), F32), jax.ShapeDtypeStruct((SUBLANES, D_CONV), F32),
                   jax.ShapeDtypeStruct((n_groups, POOL_GROUP, POOL_GROUP), F32),
                   jax.ShapeDtypeStruct((SUBLANES, D_POOL), F32)],
        scratch_shapes=[pltpu.VMEM((HALO + tm, D_CONV), F32), pltpu.VMEM((tm + HALO, D_CONV), F32),
                        pltpu.VMEM((tm + HALO, D_POOL), F32), pltpu.VMEM((tm + HALO, D_CONV), F32)],
        args=(dxo, x, gm, ag, u0, u0, u1, mixed, win_t, cdw, lg, lb, pw, poolw, ps, wout))


def _adamw(w, g, m, v, name):
    rows, cols = w.shape
    tr = _row_tile(rows, max(SUBLANES, (256 * 1024) // cols // SUBLANES * SUBLANES), SUBLANES)

    def body(w_ref, g_ref, m_ref, v_ref, d_ref, nm_ref, nv_ref):
        gv = g_ref[...]
        nm = ADAM_B1 * m_ref[...] + (1.0 - ADAM_B1) * gv
        nv = ADAM_B2 * v_ref[...] + (1.0 - ADAM_B2) * (gv * gv)
        m_hat = nm / (1.0 - ADAM_B1 ** ADAM_STEP)
        v_hat = nv / (1.0 - ADAM_B2 ** ADAM_STEP)
        d_ref[...] = -ADAM_LR * (m_hat / (jnp.sqrt(v_hat) + ADAM_EPS) + ADAM_WD * w_ref[...])
        nm_ref[...] = nm
        nv_ref[...] = nv

    blk = pl.BlockSpec((tr, cols), lambda i: (i, 0))
    shape = jax.ShapeDtypeStruct((rows, cols), F32)
    return pl.pallas_call(
        body, name=name, grid=(rows // tr,),
        in_specs=[blk] * 4, out_specs=[blk] * 3, out_shape=[shape] * 3,
        compiler_params=_params(1),
    )(w, g, m, v)


def _as_2d(a):
    if a.ndim == 1:
        return a.reshape(a.shape[0] // 128, 128)
    if a.ndim == 3:
        return a.reshape(a.shape[0] * a.shape[1], a.shape[2])
    return a


def _pack_weight_slabs(p):
    def bf(parts):
        return [a.astype(BF16) for a in parts]

    cdw_bits = lax.bitcast_convert_type(p["conv_dw"], BF16).reshape(CONV_WIDTH, 2 * D_CONV // N_DEV)
    cdw_bits = jnp.pad(cdw_bits, ((0, 1), (0, 0))).reshape(4, D_MODEL)
    cdw_bits = jnp.pad(cdw_bits, ((0, CDW_ROWS - 4), (0, 0)))
    first = bf([p["ffn1_w_gate"].T, p["ffn1_w_up"].T, p["ffn1_w_down"]])
    rest = bf([p["ffn2_w_gate"].T, p["ffn2_w_up"].T, p["ffn2_w_down"], p["w_in"].T, p["w_out"],
               p["conv_pw"].reshape(D_CONV // N_DEV // 2, D_MODEL)]) + [cdw_bits]
    return jnp.concatenate(first, axis=0), jnp.concatenate(rest, axis=0)


def _unpack_rows(slab, offs, names):
    out = {}
    for name in names:
        o, n = offs[name]
        out[name] = slab[:, o:o + n, :].reshape(N_DEV * n, D_MODEL)
    return out


def _unpack_conv_taps(slab, offs):
    o, _ = offs["cdw"]
    bits = slab[:, o:o + 4, :].reshape(N_DEV, CONV_WIDTH + 1, D_CONV // N_DEV, 2)[:, :CONV_WIDTH]
    cdw = lax.bitcast_convert_type(bits, F32)
    return jnp.transpose(cdw, (1, 0, 2)).reshape(CONV_WIDTH, D_CONV)


def kernel(x, ffn1_norm, ffn1_w_gate, ffn1_w_up, ffn1_w_down, mix_norm, w_in, conv_dw, conv_dw_b, conv_ln_g, conv_ln_b, conv_pw, pool_w, pool_scale, w_out, ffn2_norm, ffn2_w_gate, ffn2_w_up, ffn2_w_down, final_norm, loss_target, m_ffn1_norm, m_ffn1_w_gate, m_ffn1_w_up, m_ffn1_w_down, m_mix_norm, m_w_in, m_conv_dw, m_conv_dw_b, m_conv_ln_g, m_conv_ln_b, m_conv_pw, m_pool_w, m_pool_scale, m_w_out, m_ffn2_norm, m_ffn2_w_gate, m_ffn2_w_up, m_ffn2_w_down, m_final_norm, v_ffn1_norm, v_ffn1_w_gate, v_ffn1_w_up, v_ffn1_w_down, v_mix_norm, v_w_in, v_conv_dw, v_conv_dw_b, v_conv_ln_g, v_conv_ln_b, v_conv_pw, v_pool_w, v_pool_scale, v_w_out, v_ffn2_norm, v_ffn2_w_gate, v_ffn2_w_up, v_ffn2_w_down, v_final_norm):
    given = dict(locals())
    p = {n: given[n] for n in WEIGHTS}
    f8 = ffn1_w_gate.shape[1]
    small = (("win", D_IN // N_DEV), ("wout", D_MODEL // N_DEV), ("pw", D_CONV // N_DEV // 2), ("cdw", CDW_ROWS))
    w1_offs, _ = _layout((("g1", f8), ("u1", f8), ("d1", f8)))
    w2_offs, _ = _layout((("g2", f8), ("u2", f8), ("d2", f8)) + small)
    g2_offs, g2_rows = _layout((("g2", f8), ("u2", f8), ("d2", f8)))
    g1_offs, g1_rows = _layout(small + (("rep", REP_ROWS), ("g1", f8), ("u1", f8), ("d1", f8)))
    x0 = x[0]
    target = loss_target[0]

    def row(vec):
        return vec.reshape(1, vec.shape[0])

    slab_first, slab_rest = _pack_weight_slabs(p)
    w = _unpack_rows(_run_comm(_Gather(slab_first), "gather_ffn1")[0], w1_offs, ("g1", "u1", "d1"))
    x1, g1s, u1s, n1, w_rest = _ffn_fwd(x0, row(ffn1_norm), w["g1"], w["u1"], w["d1"], "ffn1_fwd", _Gather(slab_rest))
    w.update(_unpack_rows(w_rest, w2_offs, ("g2", "u2", "d2", "win", "wout", "pw")))
    w["pw"] = w["pw"].reshape(D_CONV, D_CONV)
    cdw = jnp.pad(_unpack_conv_taps(w_rest, w2_offs), ((0, HALO - CONV_WIDTH), (0, 0)))
    poolw = pool_w.astype(BF16)

    x2, h, ag, u0, u1, u2, mixed, cat = _mix_fwd(
        x1, row(mix_norm), w["win"], cdw, row(conv_dw_b), row(conv_ln_g), row(conv_ln_b), w["pw"], poolw,
        row(pool_scale), w["wout"])
    x3, g2s, u2s, n2 = _ffn_fwd(x2, row(ffn2_norm), w["g2"], w["u2"], w["d2"], "ffn2_fwd")
    dx3, d_final_norm, loss_part = _loss_head(x3, target, row(final_norm))

    dx2, h2, dg2, du2, df2, d_ffn2_norm = _ffn_bwd(dx3, x2, row(ffn2_norm), g2s, u2s, w["g2"], w["u2"], w["d2"], "ffn2_bwd")
    slab2 = _tn_chunked(dg2, n2, "ffn2_dgate", None, g2_offs["g2"][0], g2_rows)
    slab2 = _tn_chunked(du2, n2, "ffn2_dup", slab2, g2_offs["u2"][0], g2_rows)
    slab2 = _tn_chunked(h2, df2, "ffn2_ddown", slab2, g2_offs["d2"][0], g2_rows)
    dx1, dproj, dco, d_mix_norm, d_cdw, d_cb, d_lg, d_lb, d_poolw, d_ps, recv2 = _mix_bwd(
        dx2, x1, row(mix_norm), ag, u0, u1, mixed, w["win"], cdw, row(conv_ln_g), row(conv_ln_b), w["pw"], poolw,
        row(pool_scale), w["wout"], _SiblingExchange(slab2))
    own2, part2 = _sibling_sums(slab2, recv2, "ffn2")
    d_win = _tn(dproj, h, "mix_dwin")
    d_wout = _tn(cat, dx2, "mix_dwout")
    d_pw = _tn(u2, dco, "mix_dpw")
    dx0, h1, dg1, du1, df1, d_ffn1_norm, recv2 = _ffn_bwd(
        dx1, x0, row(ffn1_norm), g1s, u1s, w["g1"], w["u1"], w["d1"], "ffn1_bwd", _ChipsExchange(part2))
    mine2 = _sum_partials(own2, recv2, "rs_sum_ffn2")

    d_cdw = d_cdw.reshape(HALO, SUBLANES, D_CONV)[:CONV_WIDTH, 0]
    d_cdw = jnp.transpose(d_cdw.reshape(CONV_WIDTH, N_DEV, D_CONV // N_DEV), (1, 0, 2)).reshape(N_DEV, -1)
    d_cdw = jnp.pad(d_cdw, ((0, 0), (0, CDW_ROWS * D_MODEL - d_cdw.shape[1]))).reshape(N_DEV, CDW_ROWS, D_MODEL)
    rep = jnp.concatenate([
        d_ffn1_norm[0:1], d_mix_norm[0:1], d_ffn2_norm[0:1], d_final_norm[0:1],
        jnp.concatenate([d_cb[0:1], d_lg[0:1]], axis=1), jnp.concatenate([d_lb[0:1], d_ps[0:1]], axis=1),
        jnp.zeros((2, D_MODEL), F32), d_poolw.reshape(-1, D_MODEL)], axis=0)
    rep = jnp.pad(rep, ((0, N_DEV * REP_ROWS - rep.shape[0]), (0, 0))).reshape(N_DEV, REP_ROWS, D_MODEL)
    small_grads = jnp.concatenate([d_win.reshape(N_DEV, -1, D_MODEL), d_wout.reshape(N_DEV, -1, D_MODEL),
                                   d_pw.reshape(N_DEV, -1, D_MODEL), d_cdw, rep], axis=1)
    slab1 = _start_slab(small_grads, g1_rows, "pack_small_grads")
    slab1 = _tn_chunked(dg1, n1, "ffn1_dgate", slab1, g1_offs["g1"][0], g1_rows)
    slab1 = _tn_chunked(du1, n1, "ffn1_dup", slab1, g1_offs["u1"][0], g1_rows)
    slab1 = _tn_chunked(h1, df1, "ffn1_ddown", slab1, g1_offs["d1"][0], g1_rows)
    own1, part1 = _sibling_sums(slab1, _run_comm(_SiblingExchange(slab1), "rs_sibling")[0], "ffn1")
    mine1 = _sum_partials(own1, _run_comm(_ChipsExchange(part1), "rs_chips")[0], "rs_sum_ffn1")

    o_rep, _ = g1_offs["rep"]
    loss_rows = jnp.pad(loss_part, ((0, 0), (0, D_MODEL - loss_part.shape[1])))
    shared = _run_comm(_Gather(jnp.concatenate([mine1[o_rep:o_rep + REP_ROWS], loss_rows], axis=0)), "gather_replicated")[0]
    rep_all = shared[:, :REP_ROWS].reshape(N_DEV * REP_ROWS, D_MODEL)
    loss = jnp.sum(shared[:, REP_ROWS, 0])

    def shard(name):
        mine, offs = (mine2, g2_offs) if name in g2_offs else (mine1, g1_offs)
        o, n = offs[name]
        return mine[o:o + n]

    g = {
        "ffn1_norm": rep_all[0], "mix_norm": rep_all[1], "ffn2_norm": rep_all[2], "final_norm": rep_all[3],
        "conv_dw_b": rep_all[4, :D_CONV], "conv_ln_g": rep_all[4, D_CONV:],
        "conv_ln_b": rep_all[5, :D_CONV], "pool_scale": rep_all[5, D_CONV:],
        "pool_w": rep_all[8:8 + pool_w.size // D_MODEL].reshape(pool_w.shape),
        "ffn1_w_gate": shard("g1").T, "ffn1_w_up": shard("u1").T, "ffn1_w_down": shard("d1"),
        "ffn2_w_gate": shard("g2").T, "ffn2_w_up": shard("u2").T, "ffn2_w_down": shard("d2"),
        "w_in": shard("win").T, "w_out": shard("wout"), "conv_pw": shard("pw").reshape(conv_pw.shape),
        "conv_dw": shard("cdw").reshape(-1)[:conv_dw.size].reshape(conv_dw.shape),
    }

    delta, new_m, new_v = {}, {}, {}
    for n in WEIGHTS:
        dl, nm, nv = _adamw(_as_2d(p[n]), _as_2d(g[n]), _as_2d(given["m_" + n]), _as_2d(given["v_" + n]), "adamw_" + n)
        delta[n], new_m[n], new_v[n] = (a.reshape(p[n].shape) for a in (dl, nm, nv))

    return (loss, dx0[None], *[g[n] for n in WEIGHTS], *[delta[n] for n in WEIGHTS],
            *[new_m[n] for n in WEIGHTS], *[new_v[n] for n in WEIGHTS])
```

```python
import functools

import jax
import jax.numpy as jnp
from jax import lax
from jax.experimental import pallas as pl
from jax.experimental.pallas import tpu as pltpu

F32 = jnp.float32
BF16 = jnp.bfloat16

D_MODEL = 1024
D_CONV = 512
D_POOL = 512
D_IN = 2 * D_CONV + D_POOL
POOL_WINDOWS = (2, 4, 8, 16)
POOL_GROUP = D_POOL // len(POOL_WINDOWS)
CONV_WIDTH = 31
RMS_EPS = 1e-6
LN_EPS = 1e-5
FFN_RES_WEIGHT = 0.5

ADAM_LR = 0.001
ADAM_B1 = 0.9
ADAM_B2 = 0.999
ADAM_EPS = 1e-08
ADAM_WD = 0.01
ADAM_STEP = 10

N_DEV = 8
MESH_ID = pl.DeviceIdType.MESH

SUBLANES = 8
TOKEN_TILE = 512
FFN_BWD_TILE = 256
FF_CHUNK = 256
HALO = 32
V7X_VMEM_LIMIT = 56 * 1024 * 1024
CDW_ROWS = 16
REP_ROWS = 16

WEIGHTS = ("ffn1_norm", "ffn1_w_gate", "ffn1_w_up", "ffn1_w_down", "mix_norm", "w_in", "conv_dw", "conv_dw_b",
           "conv_ln_g", "conv_ln_b", "conv_pw", "pool_w", "pool_scale", "w_out", "ffn2_norm", "ffn2_w_gate",
           "ffn2_w_up", "ffn2_w_down", "final_norm")


def _dot_nn(a, b):
    return lax.dot_general(a, b, (((1,), (0,)), ((), ())), preferred_element_type=F32)


def _dot_nt(a, b):
    return lax.dot_general(a, b, (((1,), (1,)), ((), ())), preferred_element_type=F32)


def _dot_tn(a, b):
    return lax.dot_general(a, b, (((0,), (0,)), ((), ())), preferred_element_type=F32)


def _rowsum8(v):
    r, c = v.shape
    return jnp.sum(v.reshape(r // SUBLANES, SUBLANES, c), axis=0)


def _fold8(ref):
    ref[0:1, :] = jnp.sum(ref[...], axis=0, keepdims=True)


def _row_tile(n, cap, mult):
    best = None
    for t in range(mult, min(n, cap) + 1, mult):
        if n % t == 0:
            best = t
    return n if best is None else best


def _params(n_grid):
    return pltpu.CompilerParams(dimension_semantics=("arbitrary",) * n_grid, vmem_limit_bytes=V7X_VMEM_LIMIT)


def _full(shape):
    return pl.BlockSpec(shape, lambda *_: (0,) * len(shape))


def _layout(pieces):
    offs, r = {}, 0
    for name, rows in pieces:
        offs[name] = (r, rows)
        r += rows
    return offs, r


HBM = pl.BlockSpec(memory_space=pl.ANY)


def _mesh_pos():
    return lax.axis_index("x"), lax.axis_index("y"), lax.axis_index("c")


def _remote(src, dst, send_sems, recv_sems, k, to):
    return pltpu.make_async_remote_copy(src_ref=src, dst_ref=dst, send_sem=send_sems.at[k], recv_sem=recv_sems.at[k],
                                        device_id=to, device_id_type=MESH_ID)


class _Gather:
    def __init__(self, shard):
        self.inputs = (shard,)
        self.out_shape = (jax.ShapeDtypeStruct((N_DEV, *shard.shape), shard.dtype),)
        self.scratch = (pltpu.SemaphoreType.DMA((7,)), pltpu.SemaphoreType.DMA((7,)), pltpu.SemaphoreType.DMA)

    def phases(self, ins, outs, scr):
        (x_ref,), (out_ref,), (send_sems, recv_sems, local_sem) = ins, outs, scr
        x, y, c = _mesh_pos()
        me, sibling = (x, y, c), (x, y, 1 - c)
        chips = [(1 - x, y), (x, 1 - y), (1 - x, 1 - y)]

        def block(px, py, pc):
            return out_ref.at[4 * px + 2 * py + pc]

        def copy(k, blk, to, src=None):
            return _remote(block(*blk) if src is None else src, block(*blk), send_sems, recv_sems, k, to)

        def mine():
            return pltpu.make_async_copy(x_ref, block(*me), local_sem)

        def first():
            return [copy(0, me, sibling, src=x_ref)] + [copy(1 + j, me, (*chip, c), src=x_ref) for j, chip in enumerate(chips)]

        def passed(j):
            return copy(4 + j, (*chips[j], c), sibling)

        def start():
            mine().start()
            for cp in first():
                cp.start()

        def forward():
            for j, chip in enumerate(chips):
                copy(1 + j, (*chip, c), me).wait_recv()
                passed(j).start()

        def finish():
            copy(0, sibling, me).wait_recv()
            for j, chip in enumerate(chips):
                copy(4 + j, (*chip, 1 - c), me).wait_recv()
            for cp in first() + [passed(j) for j in range(3)]:
                cp.wait_send()
            mine().wait()

        return [start, forward, finish]


class _SiblingExchange:
    def __init__(self, src):
        self.inputs = (src,)
        self.out_shape = (jax.ShapeDtypeStruct((4, *src.shape[1:]), src.dtype),)
        self.scratch = (pltpu.SemaphoreType.DMA((4,)), pltpu.SemaphoreType.DMA((4,)))

    def phases(self, ins, outs, scr):
        (g_ref,), (recv_ref,), (send_sems, recv_sems) = ins, outs, scr
        x, y, c = _mesh_pos()

        def copies():
            return [_remote(g_ref.at[2 * k + (1 - c)], recv_ref.at[k], send_sems, recv_sems, k, (x, y, 1 - c))
                    for k in range(4)]

        def start():
            for cp in copies():
                cp.start()

        def finish():
            for cp in copies():
                cp.wait()

        return [start, finish]


class _ChipsExchange:
    def __init__(self, src):
        self.inputs = (src,)
        self.out_shape = (jax.ShapeDtypeStruct(src.shape, src.dtype),)
        self.scratch = (pltpu.SemaphoreType.DMA((3,)), pltpu.SemaphoreType.DMA((3,)))

    def phases(self, ins, outs, scr):
        (p_ref,), (recv_ref,), (send_sems, recv_sems) = ins, outs, scr
        x, y, c = _mesh_pos()
        peers = [(1 - x, y, c), (x, 1 - y, c), (1 - x, 1 - y, c)]

        def copies():
            return [_remote(p_ref.at[k], recv_ref.at[k], send_sems, recv_sems, k, peer) for k, peer in enumerate(peers)]

        def start():
            for cp in copies():
                cp.start()

        def finish():
            for cp in copies():
                cp.wait()

        return [start, finish]


class _Together:
    def __init__(self, *plans):
        self.plans = plans
        self.inputs = tuple(a for p in plans for a in p.inputs)
        self.out_shape = tuple(o for p in plans for o in p.out_shape)
        self.scratch = tuple(s for p in plans for s in p.scratch)

    def phases(self, ins, outs, scr):
        each = []
        for p in self.plans:
            n_in, n_out, n_scr = len(p.inputs), len(p.out_shape), len(p.scratch)
            each.append(p.phases(ins[:n_in], outs[:n_out], scr[:n_scr]))
            ins, outs, scr = ins[n_in:], outs[n_out:], scr[n_scr:]

        def run(fns):
            def phase():
                for fn in fns:
                    fn()
            return phase

        middle = [fn for ph in each for fn in ph[1:-1]]
        return [run([ph[0] for ph in each]), *([run(middle)] if middle else []), run([ph[-1] for ph in each])]


def _run_comm(plan, name):
    n_in, n_out = len(plan.inputs), len(plan.out_shape)

    def body(*refs):
        for phase in plan.phases(refs[:n_in], refs[n_in:n_in + n_out], refs[n_in + n_out:]):
            phase()

    return pl.pallas_call(
        body, name=name, out_shape=list(plan.out_shape), in_specs=[HBM] * n_in, out_specs=[HBM] * n_out,
        scratch_shapes=list(plan.scratch))(*plan.inputs)


def _grid_call(body, *, name, nt, in_specs, out_specs, out_shape, scratch_shapes, args, plan=None):
    if plan is None:
        return pl.pallas_call(body, name=name, grid=(nt,), in_specs=in_specs, out_specs=out_specs, out_shape=out_shape,
                              scratch_shapes=scratch_shapes, compiler_params=_params(1))(*args)
    n_in, n_out, n_scr = len(in_specs), len(out_specs), len(scratch_shapes)
    p_in, p_out = len(plan.inputs), len(plan.out_shape)

    def with_plan(*refs):
        ins, refs = refs[:n_in], refs[n_in:]
        p_ins, refs = refs[:p_in], refs[p_in:]
        outs, refs = refs[:n_out], refs[n_out:]
        p_outs, refs = refs[:p_out], refs[p_out:]
        scr, p_scr = refs[:n_scr], refs[n_scr:]
        phases = plan.phases(p_ins, p_outs, p_scr)
        i = pl.program_id(0)
        pl.when(i == 0)(phases[0])
        for phase in phases[1:-1]:
            pl.when(i == min(max(nt - 3, 1), nt - 1))(phase)
        body(*ins, *outs, *scr)
        pl.when(i == nt - 1)(phases[-1])

    return pl.pallas_call(
        with_plan, name=name, grid=(nt,), in_specs=[*in_specs, *[HBM] * p_in], out_specs=[*out_specs, *[HBM] * p_out],
        out_shape=[*out_shape, *plan.out_shape], scratch_shapes=[*scratch_shapes, *plan.scratch],
        compiler_params=_params(1))(*args, *plan.inputs)


def _add_chunks(gslab, recv, gid, rid, out_dtype, name):
    n = gid.shape[0]
    _, rows, cols = gslab.shape
    tr = _row_tile(rows, 1024, 16)

    def body(gid_ref, rid_ref, a_ref, b_ref, o_ref):
        o_ref[...] = (a_ref[...] + b_ref[...]).astype(out_dtype)

    grid_spec = pltpu.PrefetchScalarGridSpec(
        num_scalar_prefetch=2, grid=(n, rows // tr),
        in_specs=[pl.BlockSpec((1, tr, cols), lambda k, i, g, r: (g[k], i, 0)),
                  pl.BlockSpec((1, tr, cols), lambda k, i, g, r: (r[k], i, 0))],
        out_specs=pl.BlockSpec((1, tr, cols), lambda k, i, g, r: (k, i, 0)))
    return pl.pallas_call(
        body, name=name, grid_spec=grid_spec,
        out_shape=jax.ShapeDtypeStruct((n, rows, cols), out_dtype),
        compiler_params=_params(2),
    )(gid, rid, gslab, recv)


def _sum_partials(own, recv, name):
    _, rows, cols = own.shape
    tr = _row_tile(rows, 1024, 16)

    def body(o_ref, r_ref, out_ref):
        acc = o_ref[0]
        for k in range(3):
            acc = acc + r_ref[k].astype(F32)
        out_ref[...] = acc

    return pl.pallas_call(
        body, name=name, grid=(rows // tr,),
        in_specs=[pl.BlockSpec((1, tr, cols), lambda i: (0, i, 0)), pl.BlockSpec((3, tr, cols), lambda i: (0, i, 0))],
        out_specs=pl.BlockSpec((tr, cols), lambda i: (i, 0)),
        out_shape=jax.ShapeDtypeStruct((rows, cols), F32),
        compiler_params=_params(1),
    )(own, recv)


def _chunk_ids():
    x, y, c = _mesh_pos()
    chips = [(x, y), (1 - x, y), (x, 1 - y), (1 - x, 1 - y)]
    gid = jnp.stack([4 * px + 2 * py + c for px, py in chips]).astype(jnp.int32)
    rid = jnp.stack([2 * px + py for px, py in chips]).astype(jnp.int32)
    return gid, rid


def _sibling_sums(slab, recv, tag):
    gid, rid = _chunk_ids()
    own = _add_chunks(slab, recv, gid[:1], rid[:1], F32, "rs_add_own_" + tag)
    part = _add_chunks(slab, recv, gid[1:], rid[1:], BF16, "rs_add_send_" + tag)
    return own, part


def _load_weights(slab_ref, offs, dsts, sems):
    cps = []
    for i, (off, dst) in enumerate(zip(offs, dsts)):
        f8 = dst.shape[0] // N_DEV
        cps += [pltpu.make_async_copy(slab_ref.at[j, pl.ds(off, f8), :], dst.at[pl.ds(j * f8, f8), :],
                                      sems.at[i * N_DEV + j]) for j in range(N_DEV)]
    for cp in cps:
        cp.start()
    for cp in cps:
        cp.wait()


def _chunk_rows(c):
    return pl.ds(pl.multiple_of(c * FF_CHUNK, FF_CHUNK), FF_CHUNK)


def _rms(xv):
    return lax.rsqrt(jnp.mean(xv * xv, axis=-1, keepdims=True) + RMS_EPS)


def _ffn_fwd(x, gain, slab, offs, f, name, plan=None):
    t, d = x.shape
    nc, tm = f // FF_CHUNK, TOKEN_TILE

    def body(x_ref, gain_ref, slab_ref, xo_ref, g_ref, u_ref, n_ref, wg_v, wu_v, wd_v, acc_ref, sems):
        @pl.when(pl.program_id(0) == 0)
        def _():
            _load_weights(slab_ref, offs, (wg_v, wu_v, wd_v), sems)

        xv = x_ref[...]
        n_ref[...] = ((xv * _rms(xv)) * gain_ref[...]).astype(BF16)
        acc_ref[...] = jnp.zeros_like(acc_ref)

        def chunk(c, carry):
            rows = _chunk_rows(c)
            nb = n_ref[...]
            g = _dot_nt(nb, wg_v[rows, :])
            u = _dot_nt(nb, wu_v[rows, :])
            g_ref[c] = g.astype(BF16)
            u_ref[c] = u.astype(BF16)
            h = (g * jax.nn.sigmoid(g)) * u
            acc_ref[...] += _dot_nn(h.astype(BF16), wd_v[rows, :])
            return carry

        lax.fori_loop(0, nc, chunk, 0, unroll=True)
        xo_ref[...] = xv + FFN_RES_WEIGHT * acc_ref[...]

    tile = pl.BlockSpec((tm, d), lambda i: (i, 0))
    act = pl.BlockSpec((nc, tm, FF_CHUNK), lambda i: (0, i, 0))
    return _grid_call(
        body, name=name, nt=t // tm, plan=plan,
        in_specs=[tile, _full((1, d)), HBM],
        out_specs=[tile, act, act, tile],
        out_shape=[jax.ShapeDtypeStruct((t, d), F32), jax.ShapeDtypeStruct((nc, t, FF_CHUNK), BF16),
                   jax.ShapeDtypeStruct((nc, t, FF_CHUNK), BF16), jax.ShapeDtypeStruct((t, d), BF16)],
        scratch_shapes=[pltpu.VMEM((f, d), BF16), pltpu.VMEM((f, d), BF16), pltpu.VMEM((f, d), BF16),
                        pltpu.VMEM((tm, d), F32), pltpu.SemaphoreType.DMA((3 * N_DEV,))],
        args=(x, gain, slab))


def _ffn_bwd(dxo, x, gain, gs, us, slab, offs, f, name, plan=None):
    t, d = x.shape
    nc, tm = f // FF_CHUNK, FFN_BWD_TILE
    nt = t // tm

    def body(dxo_ref, x_ref, gain_ref, g_ref, u_ref, slab_ref,
             dx_ref, h_ref, dg_ref, du_ref, df_ref, dgain_ref, wg_v, wu_v, wd_v, dn_ref, sems):
        i = pl.program_id(0)

        @pl.when(i == 0)
        def _():
            _load_weights(slab_ref, offs, (wg_v, wu_v, wd_v), sems)
            dgain_ref[...] = jnp.zeros_like(dgain_ref)

        df_ref[...] = (FFN_RES_WEIGHT * dxo_ref[...]).astype(BF16)
        dn_ref[...] = jnp.zeros_like(dn_ref)

        def chunk(c, carry):
            rows = _chunk_rows(c)
            g = g_ref[c].astype(F32)
            u = u_ref[c].astype(F32)
            sg = jax.nn.sigmoid(g)
            sil = g * sg
            dh = _dot_nt(df_ref[...], wd_v[rows, :])
            h_ref[c] = (sil * u).astype(BF16)
            du_ref[c] = (dh * sil).astype(BF16)
            dg_ref[c] = (dh * u * (sg * (1.0 + g * (1.0 - sg)))).astype(BF16)
            dn_ref[...] += _dot_nn(dg_ref[c], wg_v[rows, :]) + _dot_nn(du_ref[c], wu_v[rows, :])
            return carry

        lax.fori_loop(0, nc, chunk, 0, unroll=True)
        xv = x_ref[...]
        r = _rms(xv)
        xh = xv * r
        dn = dn_ref[...]
        dgain_ref[...] += _rowsum8(dn * xh)
        dxh = dn * gain_ref[...]
        dx_ref[...] = dxo_ref[...] + r * (dxh - xh * jnp.mean(dxh * xh, axis=-1, keepdims=True))

        @pl.when(i == nt - 1)
        def _():
            _fold8(dgain_ref)

    tile = pl.BlockSpec((tm, d), lambda i: (i, 0))
    act = pl.BlockSpec((nc, tm, FF_CHUNK), lambda i: (0, i, 0))
    act_shape = jax.ShapeDtypeStruct((nc, t, FF_CHUNK), BF16)
    return _grid_call(
        body, name=name, nt=nt, plan=plan,
        in_specs=[tile, tile, _full((1, d)), act, act, HBM],
        out_specs=[tile, act, act, act, tile, _full((SUBLANES, d))],
        out_shape=[jax.ShapeDtypeStruct((t, d), F32), act_shape, act_shape, act_shape,
                   jax.ShapeDtypeStruct((t, d), BF16), jax.ShapeDtypeStruct((SUBLANES, d), F32)],
        scratch_shapes=[pltpu.VMEM((f, d), BF16), pltpu.VMEM((f, d), BF16), pltpu.VMEM((f, d), BF16),
                        pltpu.VMEM((tm, d), F32), pltpu.SemaphoreType.DMA((3 * N_DEV,))],
        args=(dxo, x, gain, gs, us, slab))


def _tn_chunked(a, b, name, plan=None):
    nc, t, _ = a.shape
    n = b.shape[1]
    tb = _row_tile(t, 1024, TOKEN_TILE)

    def body(a_ref, b_ref, o_ref):
        @pl.when(pl.program_id(0) == 0)
        def _():
            o_ref[...] = jnp.zeros_like(o_ref)

        def chunk(c, carry):
            rows = _chunk_rows(c)
            o_ref[rows, :] += _dot_tn(a_ref[c], b_ref[...])
            return carry

        lax.fori_loop(0, nc, chunk, 0, unroll=True)

    return _grid_call(
        body, name=name, nt=t // tb, plan=plan,
        in_specs=[pl.BlockSpec((nc, tb, FF_CHUNK), lambda i: (0, i, 0)), pl.BlockSpec((tb, n), lambda i: (i, 0))],
        out_specs=[_full((nc * FF_CHUNK, n))],
        out_shape=[jax.ShapeDtypeStruct((nc * FF_CHUNK, n), F32)],
        scratch_shapes=[], args=(a, b))


def _tn(a, b, name, plan=None):
    t, k = a.shape
    n = b.shape[1]
    tb = _row_tile(t, 1024, TOKEN_TILE)

    def body(a_ref, b_ref, o_ref):
        @pl.when(pl.program_id(0) == 0)
        def _():
            o_ref[...] = jnp.zeros_like(o_ref)

        o_ref[...] += _dot_tn(a_ref[...].astype(BF16), b_ref[...].astype(BF16))

    return _grid_call(
        body, name=name, nt=t // tb, plan=plan,
        in_specs=[pl.BlockSpec((tb, k), lambda i: (i, 0)), pl.BlockSpec((tb, n), lambda i: (i, 0))],
        out_specs=[_full((k, n))],
        out_shape=[jax.ShapeDtypeStruct((k, n), F32)],
        scratch_shapes=[], args=(a, b))


def _loss_head(x, target, gain):
    t, d = x.shape
    tm = TOKEN_TILE
    nt = t // tm

    def body(x_ref, tgt_ref, gain_ref, dx_ref, dgain_ref, loss_ref):
        i = pl.program_id(0)

        @pl.when(i == 0)
        def _():
            dgain_ref[...] = jnp.zeros_like(dgain_ref)
            loss_ref[...] = jnp.zeros_like(loss_ref)

        xv = x_ref[...]
        r = _rms(xv)
        xh = xv * r
        err = xh * gain_ref[...] - tgt_ref[...]
        loss_ref[...] += 0.5 * jnp.sum(jnp.mean(err * err, axis=-1, keepdims=True))
        dy = err * (1.0 / d)
        dgain_ref[...] += _rowsum8(dy * xh)
        dxh = dy * gain_ref[...]
        dx_ref[...] = r * (dxh - xh * jnp.mean(dxh * xh, axis=-1, keepdims=True))

        @pl.when(i == nt - 1)
        def _():
            _fold8(dgain_ref)

    tile = pl.BlockSpec((tm, d), lambda i: (i, 0))
    return pl.pallas_call(
        body, name="loss_head", grid=(nt,),
        in_specs=[tile, tile, _full((1, d))],
        out_specs=[tile, _full((SUBLANES, d)), _full((SUBLANES, 128))],
        out_shape=[jax.ShapeDtypeStruct((t, d), F32), jax.ShapeDtypeStruct((SUBLANES, d), F32),
                   jax.ShapeDtypeStruct((SUBLANES, 128), F32)],
        compiler_params=_params(1),
    )(x, target, gain)


def _layernorm_stats(u1):
    mu = jnp.mean(u1, axis=-1, keepdims=True)
    xc = u1 - mu
    rstd = lax.rsqrt(jnp.mean(xc * xc, axis=-1, keepdims=True) + LN_EPS)
    return xc * rstd, rstd


def _positions(tile_index, tm):
    return (tile_index * tm + lax.broadcasted_iota(jnp.int32, (tm, 1), 0)).astype(F32)


def _shifted_taps(src_ref, sh_ref, tm, offset_of):
    groups = {}
    for k in range(CONV_WIDTH):
        groups.setdefault(offset_of(k) % SUBLANES, []).append(k)
    span = tm + HALO - SUBLANES
    for rem, taps in sorted(groups.items()):
        if rem:
            sh_ref[0:span, :] = src_ref[rem:rem + span, :]
        ref = sh_ref if rem else src_ref
        for k in taps:
            base = offset_of(k) - rem
            yield k, ref[base:base + tm, :]


def _mix_fwd(x, gm, win_t, cdw, cb, lg, lb, pw, poolw, ps, wout):
    t, d = x.shape
    tm = TOKEN_TILE

    def body(x_ref, gm_ref, win_ref, cdw_ref, cb_ref, lg_ref, lb_ref, pw_ref, poolw_ref, ps_ref, wout_ref,
             xo_ref, h_ref, ag_ref, u0_ref, u1_ref, u2_ref, mixed_ref, cat_ref, eu_ref, ep_ref, sh_ref):
        i = pl.program_id(0)

        @pl.when(i == 0)
        def _():
            eu_ref[0:HALO, :] = jnp.zeros((HALO, D_CONV), F32)
            ep_ref[0:HALO, :] = jnp.zeros((HALO, D_POOL), F32)

        @pl.when(i > 0)
        def _():
            eu_ref[0:HALO, :] = eu_ref[tm:tm + HALO, :]
            ep_ref[0:HALO, :] = ep_ref[tm:tm + HALO, :]

        xv = x_ref[...]
        hb = ((xv * _rms(xv)) * gm_ref[...]).astype(BF16)
        h_ref[...] = hb
        proj = _dot_nt(hb, win_ref[...])
        a = proj[:, :D_CONV]
        g = proj[:, D_CONV:2 * D_CONV]
        ag_ref[...] = proj[:, :2 * D_CONV]
        u0 = a * jax.nn.sigmoid(g)
        u0_ref[...] = u0
        eu_ref[HALO:HALO + tm, :] = u0
        ep_ref[HALO:HALO + tm, :] = proj[:, 2 * D_CONV:]

        u1 = jnp.broadcast_to(cb_ref[...], (tm, D_CONV))
        for k, rows in _shifted_taps(eu_ref, sh_ref, tm, lambda k: HALO - (CONV_WIDTH - 1) + k):
            u1 = u1 + cdw_ref[k:k + 1, :] * rows
        u1_ref[...] = u1
        lnh, _ = _layernorm_stats(u1)
        ln = lnh * lg_ref[...] + lb_ref[...]
        u2 = (ln * jax.nn.sigmoid(ln)).astype(BF16)
        u2_ref[...] = u2
        conv_out = _dot_nn(u2, pw_ref[...])

        pos = _positions(i, tm)
        outs = []
        for gi, w in enumerate(POOL_WINDOWS):
            lo = gi * POOL_GROUP
            p = ep_ref[HALO:HALO + tm, lo:lo + POOL_GROUP]
            s = p
            for j in range(1, w):
                s = s + ep_ref[HALO - j:HALO - j + tm, lo:lo + POOL_GROUP]
            mixed = (s / jnp.minimum(pos + 1.0, float(w)) - p).astype(BF16)
            mixed_ref[:, lo:lo + POOL_GROUP] = mixed
            outs.append(_dot_nn(mixed, poolw_ref[gi]))
        pool_out = jnp.concatenate(outs, axis=-1) * ps_ref[...]
        cat = jnp.concatenate([conv_out, pool_out], axis=-1).astype(BF16)
        cat_ref[...] = cat
        xo_ref[...] = xv + _dot_nn(cat, wout_ref[...])

    def tile(c):
        return pl.BlockSpec((tm, c), lambda i: (i, 0))

    def out(c, dt):
        return jax.ShapeDtypeStruct((t, c), dt)

    return pl.pallas_call(
        body, name="mix_fwd", grid=(t // tm,),
        in_specs=[tile(d), _full((1, d)), _full((D_IN, d)), _full((HALO, D_CONV)), _full((1, D_CONV)),
                  _full((1, D_CONV)), _full((1, D_CONV)), _full((D_CONV, D_CONV)),
                  _full((len(POOL_WINDOWS), POOL_GROUP, POOL_GROUP)), _full((1, D_POOL)), _full((d, d))],
        out_specs=[tile(d), tile(d), tile(2 * D_CONV), tile(D_CONV), tile(D_CONV), tile(D_CONV), tile(D_POOL), tile(d)],
        out_shape=[out(d, F32), out(d, BF16), out(2 * D_CONV, F32), out(D_CONV, F32), out(D_CONV, F32),
                   out(D_CONV, BF16), out(D_POOL, BF16), out(d, BF16)],
        scratch_shapes=[pltpu.VMEM((HALO + tm, D_CONV), F32), pltpu.VMEM((HALO + tm, D_POOL), F32),
                        pltpu.VMEM((HALO + tm, D_CONV), F32)],
        compiler_params=_params(1),
    )(x, gm, win_t, cdw, cb, lg, lb, pw, poolw, ps, wout)


def _mix_bwd(dxo, x, gm, ag, u0, u1, mixed, win_t, cdw, lg, lb, pw, poolw, ps, wout, plan=None):
    t, d = x.shape
    tm = TOKEN_TILE
    nt = t // tm
    halo_blocks = tm // HALO

    def body(dxo_ref, x_ref, gm_ref, ag_ref, u0_ref, u0h_ref, u1_ref, mixed_ref,
             win_ref, cdw_ref, lg_ref, lb_ref, pw_ref, poolw_ref, ps_ref, wout_ref,
             dx_ref, dproj_ref, dco_ref, dgm_ref, dcdw_ref, dcb_ref, dlg_ref, dlb_ref, dpoolw_ref, dps_ref,
             eu_ref, ed_ref, eq_ref, sh_ref):
        i = pl.program_id(0)
        ti = nt - 1 - i

        @pl.when(i == 0)
        def _():
            for ref in (dgm_ref, dcdw_ref, dcb_ref, dlg_ref, dlb_ref, dpoolw_ref, dps_ref):
                ref[...] = jnp.zeros_like(ref)
            ed_ref[tm:tm + HALO, :] = jnp.zeros((HALO, D_CONV), F32)
            eq_ref[tm:tm + HALO, :] = jnp.zeros((HALO, D_POOL), F32)

        @pl.when(i > 0)
        def _():
            ed_ref[tm:tm + HALO, :] = ed_ref[0:HALO, :]
            eq_ref[tm:tm + HALO, :] = eq_ref[0:HALO, :]

        @pl.when(ti == 0)
        def _():
            eu_ref[0:HALO, :] = jnp.zeros((HALO, D_CONV), F32)

        @pl.when(ti > 0)
        def _():
            eu_ref[0:HALO, :] = u0h_ref[...]

        eu_ref[HALO:HALO + tm, :] = u0_ref[...]

        dxo = dxo_ref[...]
        dcat = _dot_nt(dxo.astype(BF16), wout_ref[...])
        dco = dcat[:, :D_CONV].astype(BF16)
        dco_ref[...] = dco
        dpo = dcat[:, D_CONV:]

        lnh, rstd = _layernorm_stats(u1_ref[...])
        ln = lnh * lg_ref[...] + lb_ref[...]
        sl = jax.nn.sigmoid(ln)
        dln = _dot_nt(dco, pw_ref[...]) * (sl * (1.0 + ln * (1.0 - sl)))
        dlg_ref[...] += _rowsum8(dln * lnh)
        dlb_ref[...] += _rowsum8(dln)
        dlnh = dln * lg_ref[...]
        du1 = rstd * (dlnh - jnp.mean(dlnh, axis=-1, keepdims=True)
                      - lnh * jnp.mean(dlnh * lnh, axis=-1, keepdims=True))
        dcb_ref[...] += _rowsum8(du1)
        ed_ref[0:tm, :] = du1

        du0 = jnp.zeros((tm, D_CONV), F32)
        for k, rows in _shifted_taps(ed_ref, sh_ref, tm, lambda k: CONV_WIDTH - 1 - k):
            du0 = du0 + cdw_ref[k:k + 1, :] * rows
        for k, rows in _shifted_taps(eu_ref, sh_ref, tm, lambda k: HALO - (CONV_WIDTH - 1) + k):
            dcdw_ref[SUBLANES * k:SUBLANES * (k + 1), :] += _rowsum8(du1 * rows)
        a = ag_ref[:, :D_CONV]
        sg = jax.nn.sigmoid(ag_ref[:, D_CONV:])
        pieces = [du0 * sg, du0 * a * (sg * (1.0 - sg))]

        pos = _positions(ti, tm)
        for gi, w in enumerate(POOL_WINDOWS):
            lo = gi * POOL_GROUP
            mg = mixed_ref[:, lo:lo + POOL_GROUP]
            dpo_g = dpo[:, lo:lo + POOL_GROUP]
            dps_ref[:, lo:lo + POOL_GROUP] += _rowsum8(dpo_g * _dot_nn(mg, poolw_ref[gi]))
            dout = (dpo_g * ps_ref[:, lo:lo + POOL_GROUP]).astype(BF16)
            dpoolw_ref[gi] += _dot_tn(mg, dout)
            dmx = _dot_nt(dout, poolw_ref[gi])
            q = dmx / jnp.minimum(pos + 1.0, float(w))
            eq_ref[0:tm, lo:lo + POOL_GROUP] = q
            s = q
            for j in range(1, w):
                s = s + eq_ref[j:j + tm, lo:lo + POOL_GROUP]
            pieces.append(s - dmx)
        dproj = jnp.concatenate(pieces, axis=-1).astype(BF16)
        dproj_ref[...] = dproj

        dh = _dot_nn(dproj, win_ref[...])
        xv = x_ref[...]
        r = _rms(xv)
        xh = xv * r
        dgm_ref[...] += _rowsum8(dh * xh)
        dxh = dh * gm_ref[...]
        dx_ref[...] = dxo + r * (dxh - xh * jnp.mean(dxh * xh, axis=-1, keepdims=True))

        @pl.when(i == nt - 1)
        def _():
            for ref in (dgm_ref, dcb_ref, dlg_ref, dlb_ref, dps_ref):
                _fold8(ref)
            for k in range(CONV_WIDTH):
                dcdw_ref[SUBLANES * k:SUBLANES * k + 1, :] = jnp.sum(
                    dcdw_ref[SUBLANES * k:SUBLANES * (k + 1), :], axis=0, keepdims=True)

    def tile(c):
        return pl.BlockSpec((tm, c), lambda i: (nt - 1 - i, 0))

    halo = pl.BlockSpec((HALO, D_CONV), lambda i: (jnp.maximum((nt - 1 - i) * halo_blocks - 1, 0), 0))
    n_groups = len(POOL_WINDOWS)
    return _grid_call(
        body, name="mix_bwd", nt=nt, plan=plan,
        in_specs=[tile(d), tile(d), _full((1, d)), tile(2 * D_CONV), tile(D_CONV), halo, tile(D_CONV), tile(D_POOL),
                  _full((D_IN, d)), _full((HALO, D_CONV)), _full((1, D_CONV)), _full((1, D_CONV)),
                  _full((D_CONV, D_CONV)), _full((n_groups, POOL_GROUP, POOL_GROUP)), _full((1, D_POOL)), _full((d, d))],
        out_specs=[tile(d), tile(D_IN), tile(D_CONV), _full((SUBLANES, d)), _full((HALO * SUBLANES, D_CONV)),
                   _full((SUBLANES, D_CONV)), _full((SUBLANES, D_CONV)), _full((SUBLANES, D_CONV)),
                   _full((n_groups, POOL_GROUP, POOL_GROUP)), _full((SUBLANES, D_POOL))],
        out_shape=[jax.ShapeDtypeStruct((t, d), F32), jax.ShapeDtypeStruct((t, D_IN), BF16),
                   jax.ShapeDtypeStruct((t, D_CONV), BF16), jax.ShapeDtypeStruct((SUBLANES, d), F32),
                   jax.ShapeDtypeStruct((HALO * SUBLANES, D_CONV), F32), jax.ShapeDtypeStruct((SUBLANES, D_CONV), F32),
                   jax.ShapeDtypeStruct((SUBLANES, D_CONV), F32), jax.ShapeDtypeStruct((SUBLANES, D_CONV), F32),
                   jax.ShapeDtypeStruct((n_groups, POOL_GROUP, POOL_GROUP), F32),
                   jax.ShapeDtypeStruct((SUBLANES, D_POOL), F32)],
        scratch_shapes=[pltpu.VMEM((HALO + tm, D_CONV), F32), pltpu.VMEM((tm + HALO, D_CONV), F32),
                        pltpu.VMEM((tm + HALO, D_POOL), F32), pltpu.VMEM((tm + HALO, D_CONV), F32)],
        args=(dxo, x, gm, ag, u0, u0, u1, mixed, win_t, cdw, lg, lb, pw, poolw, ps, wout))


def _adam_step(gv, w_ref, m_ref, v_ref, d_ref, nm_ref, nv_ref):
    nm = ADAM_B1 * m_ref[...] + (1.0 - ADAM_B1) * gv
    nv = ADAM_B2 * v_ref[...] + (1.0 - ADAM_B2) * (gv * gv)
    m_hat = nm / (1.0 - ADAM_B1 ** ADAM_STEP)
    v_hat = nv / (1.0 - ADAM_B2 ** ADAM_STEP)
    d_ref[...] = -ADAM_LR * (m_hat / (jnp.sqrt(v_hat) + ADAM_EPS) + ADAM_WD * w_ref[...])
    nm_ref[...] = nm
    nv_ref[...] = nv


def _adam_tile(rows, cols):
    return _row_tile(rows, max(SUBLANES, (256 * 1024) // cols // SUBLANES * SUBLANES), SUBLANES)


def _adamw(w, g, m, v, name):
    rows, cols = w.shape
    tr = _adam_tile(rows, cols)

    def body(w_ref, g_ref, m_ref, v_ref, d_ref, nm_ref, nv_ref):
        _adam_step(g_ref[...], w_ref, m_ref, v_ref, d_ref, nm_ref, nv_ref)

    blk = pl.BlockSpec((tr, cols), lambda i: (i, 0))
    shape = jax.ShapeDtypeStruct((rows, cols), F32)
    return pl.pallas_call(
        body, name=name, grid=(rows // tr,),
        in_specs=[blk] * 4, out_specs=[blk] * 3, out_shape=[shape] * 3,
        compiler_params=_params(1),
    )(w, g, m, v)


def _adamw_reduced(w, own, recv, m, v, name):
    rows, cols = w.shape
    tr = _adam_tile(rows, cols)

    def body(w_ref, o_ref, r_ref, m_ref, v_ref, g_ref, d_ref, nm_ref, nv_ref):
        gv = o_ref[0]
        for k in range(3):
            gv = gv + r_ref[k].astype(F32)
        g_ref[...] = gv
        _adam_step(gv, w_ref, m_ref, v_ref, d_ref, nm_ref, nv_ref)

    blk = pl.BlockSpec((tr, cols), lambda i: (i, 0))
    shape = jax.ShapeDtypeStruct((rows, cols), F32)
    return pl.pallas_call(
        body, name=name, grid=(rows // tr,),
        in_specs=[blk, pl.BlockSpec((1, tr, cols), lambda i: (0, i, 0)), pl.BlockSpec((3, tr, cols), lambda i: (0, i, 0)),
                  blk, blk],
        out_specs=[blk] * 4, out_shape=[shape] * 4,
        compiler_params=_params(1),
    )(w, own, recv, m, v)


def _as_2d(a):
    if a.ndim == 1:
        return a.reshape(a.shape[0] // 128, 128)
    if a.ndim == 3:
        return a.reshape(a.shape[0] * a.shape[1], a.shape[2])
    return a


def _pack_weight_slabs(p):
    def bf(parts):
        return [a.astype(BF16) for a in parts]

    cdw_bits = lax.bitcast_convert_type(p["conv_dw"], BF16).reshape(CONV_WIDTH, 2 * D_CONV // N_DEV)
    cdw_bits = jnp.pad(cdw_bits, ((0, 1), (0, 0))).reshape(4, D_MODEL)
    cdw_bits = jnp.pad(cdw_bits, ((0, CDW_ROWS - 4), (0, 0)))
    first = bf([p["ffn1_w_gate"].T, p["ffn1_w_up"].T, p["ffn1_w_down"]])
    rest = bf([p["ffn2_w_gate"].T, p["ffn2_w_up"].T, p["ffn2_w_down"], p["w_in"].T, p["w_out"],
               p["conv_pw"].reshape(D_CONV // N_DEV // 2, D_MODEL)]) + [cdw_bits]
    return jnp.concatenate(first, axis=0), jnp.concatenate(rest, axis=0)


def _unpack_rows(slab, offs, names):
    out = {}
    for name in names:
        o, n = offs[name]
        out[name] = slab[:, o:o + n, :].reshape(N_DEV * n, D_MODEL)
    return out


def _unpack_conv_taps(slab, offs):
    o, _ = offs["cdw"]
    bits = slab[:, o:o + 4, :].reshape(N_DEV, CONV_WIDTH + 1, D_CONV // N_DEV, 2)[:, :CONV_WIDTH]
    cdw = lax.bitcast_convert_type(bits, F32)
    return jnp.transpose(cdw, (1, 0, 2)).reshape(CONV_WIDTH, D_CONV)


def kernel(x, ffn1_norm, ffn1_w_gate, ffn1_w_up, ffn1_w_down, mix_norm, w_in, conv_dw, conv_dw_b, conv_ln_g, conv_ln_b, conv_pw, pool_w, pool_scale, w_out, ffn2_norm, ffn2_w_gate, ffn2_w_up, ffn2_w_down, final_norm, loss_target, m_ffn1_norm, m_ffn1_w_gate, m_ffn1_w_up, m_ffn1_w_down, m_mix_norm, m_w_in, m_conv_dw, m_conv_dw_b, m_conv_ln_g, m_conv_ln_b, m_conv_pw, m_pool_w, m_pool_scale, m_w_out, m_ffn2_norm, m_ffn2_w_gate, m_ffn2_w_up, m_ffn2_w_down, m_final_norm, v_ffn1_norm, v_ffn1_w_gate, v_ffn1_w_up, v_ffn1_w_down, v_mix_norm, v_w_in, v_conv_dw, v_conv_dw_b, v_conv_ln_g, v_conv_ln_b, v_conv_pw, v_pool_w, v_pool_scale, v_w_out, v_ffn2_norm, v_ffn2_w_gate, v_ffn2_w_up, v_ffn2_w_down, v_final_norm):
    given = dict(locals())
    p = {n: given[n] for n in WEIGHTS}
    f8 = ffn1_w_gate.shape[1]
    f = N_DEV * f8
    ffn_rows = (0, f8, 2 * f8)
    small = (("win", D_IN // N_DEV), ("wout", D_MODEL // N_DEV), ("pw", D_CONV // N_DEV // 2), ("cdw", CDW_ROWS))
    w_offs, _ = _layout((("g2", f8), ("u2", f8), ("d2", f8)) + small)
    s_offs, _ = _layout(small + (("rep", REP_ROWS),))
    x0 = x[0]
    target = loss_target[0]

    def row(vec):
        return vec.reshape(1, vec.shape[0])

    slab_first, slab_rest = _pack_weight_slabs(p)
    w_first = _run_comm(_Gather(slab_first), "gather_ffn1")[0]
    x1, g1s, u1s, n1, w_rest = _ffn_fwd(x0, row(ffn1_norm), w_first, ffn_rows, f, "ffn1_fwd", _Gather(slab_rest))
    w = _unpack_rows(w_rest, w_offs, ("win", "wout", "pw"))
    w["pw"] = w["pw"].reshape(D_CONV, D_CONV)
    cdw = jnp.pad(_unpack_conv_taps(w_rest, w_offs), ((0, HALO - CONV_WIDTH), (0, 0)))
    poolw = pool_w.astype(BF16)

    x2, h, ag, u0, u1, u2, mixed, cat = _mix_fwd(
        x1, row(mix_norm), w["win"], cdw, row(conv_dw_b), row(conv_ln_g), row(conv_ln_b), w["pw"], poolw,
        row(pool_scale), w["wout"])
    x3, g2s, u2s, n2 = _ffn_fwd(x2, row(ffn2_norm), w_rest, ffn_rows, f, "ffn2_fwd")
    dx3, d_final_norm, loss_part = _loss_head(x3, target, row(final_norm))

    pending, reduced = {}, {}

    def chunks(a):
        return a.reshape(N_DEV, -1, D_MODEL)

    def after_sibling(name, slab, recv):
        pending[name], part = _sibling_sums(slab, recv, name)
        return _ChipsExchange(part)

    def after_chips(name, recv):
        reduced[name] = (pending.pop(name), recv)

    dx2, h2, dg2, du2, df2, d_ffn2_norm = _ffn_bwd(dx3, x2, row(ffn2_norm), g2s, u2s, w_rest, ffn_rows, f, "ffn2_bwd")
    s_g2 = chunks(_tn_chunked(dg2, n2, "ffn2_dgate")[0])
    s_u2, r = _tn_chunked(du2, n2, "ffn2_dup", _SiblingExchange(s_g2))
    s_u2 = chunks(s_u2)
    to_chips = after_sibling("g2", s_g2, r)
    s_d2, rc, r = _tn_chunked(h2, df2, "ffn2_ddown", _Together(to_chips, _SiblingExchange(s_u2)))
    s_d2 = chunks(s_d2)
    after_chips("g2", rc)
    to_chips = after_sibling("u2", s_u2, r)
    dx1, dproj, dco, d_mix_norm, d_cdw, d_cb, d_lg, d_lb, d_poolw, d_ps, rc, r = _mix_bwd(
        dx2, x1, row(mix_norm), ag, u0, u1, mixed, w["win"], cdw, row(conv_ln_g), row(conv_ln_b), w["pw"], poolw,
        row(pool_scale), w["wout"], _Together(to_chips, _SiblingExchange(s_d2)))
    after_chips("u2", rc)
    to_chips = after_sibling("d2", s_d2, r)
    d_win, rc = _tn(dproj, h, "mix_dwin", to_chips)
    after_chips("d2", rc)
    d_wout = _tn(cat, dx2, "mix_dwout")[0]
    d_pw = _tn(u2, dco, "mix_dpw")[0]
    dx0, h1, dg1, du1, df1, d_ffn1_norm = _ffn_bwd(dx1, x0, row(ffn1_norm), g1s, u1s, w_first, ffn_rows, f, "ffn1_bwd")

    d_cdw = d_cdw.reshape(HALO, SUBLANES, D_CONV)[:CONV_WIDTH, 0]
    d_cdw = jnp.transpose(d_cdw.reshape(CONV_WIDTH, N_DEV, D_CONV // N_DEV), (1, 0, 2)).reshape(N_DEV, -1)
    d_cdw = jnp.pad(d_cdw, ((0, 0), (0, CDW_ROWS * D_MODEL - d_cdw.shape[1]))).reshape(N_DEV, CDW_ROWS, D_MODEL)
    rep = jnp.concatenate([
        d_ffn1_norm[0:1], d_mix_norm[0:1], d_ffn2_norm[0:1], d_final_norm[0:1],
        jnp.concatenate([d_cb[0:1], d_lg[0:1]], axis=1), jnp.concatenate([d_lb[0:1], d_ps[0:1]], axis=1),
        jnp.zeros((2, D_MODEL), F32), d_poolw.reshape(-1, D_MODEL)], axis=0)
    rep = jnp.pad(rep, ((0, N_DEV * REP_ROWS - rep.shape[0]), (0, 0))).reshape(N_DEV, REP_ROWS, D_MODEL)
    s_small = jnp.concatenate([chunks(d_win), chunks(d_wout), chunks(d_pw), d_cdw, rep], axis=1)

    s_g1, r = _tn_chunked(dg1, n1, "ffn1_dgate", _SiblingExchange(s_small))
    s_g1 = chunks(s_g1)
    to_chips = after_sibling("small", s_small, r)
    s_u1, rc, r = _tn_chunked(du1, n1, "ffn1_dup", _Together(to_chips, _SiblingExchange(s_g1)))
    s_u1 = chunks(s_u1)
    mine_small = _sum_partials(pending.pop("small"), rc, "rs_sum_small")
    to_chips = after_sibling("g1", s_g1, r)
    o_rep, _ = s_offs["rep"]
    loss_rows = jnp.pad(loss_part, ((0, 0), (0, D_MODEL - loss_part.shape[1])))
    share = _Gather(jnp.concatenate([mine_small[o_rep:o_rep + REP_ROWS], loss_rows], axis=0))
    s_d1, rc, r, shared = _tn_chunked(h1, df1, "ffn1_ddown", _Together(to_chips, _SiblingExchange(s_u1), share))
    s_d1 = chunks(s_d1)
    after_chips("g1", rc)
    to_chips = after_sibling("u1", s_u1, r)
    rc, r = _run_comm(_Together(to_chips, _SiblingExchange(s_d1)), "rs_tail_up")
    after_chips("u1", rc)
    rc, = _run_comm(after_sibling("d1", s_d1, r), "rs_tail_down")
    after_chips("d1", rc)

    rep_all = shared[:, :REP_ROWS].reshape(N_DEV * REP_ROWS, D_MODEL)
    loss = jnp.sum(shared[:, REP_ROWS, 0])

    def small_rows(name):
        o, n = s_offs[name]
        return mine_small[o:o + n]

    g = {
        "ffn1_norm": rep_all[0], "mix_norm": rep_all[1], "ffn2_norm": rep_all[2], "final_norm": rep_all[3],
        "conv_dw_b": rep_all[4, :D_CONV], "conv_ln_g": rep_all[4, D_CONV:],
        "conv_ln_b": rep_all[5, :D_CONV], "pool_scale": rep_all[5, D_CONV:],
        "pool_w": rep_all[8:8 + pool_w.size // D_MODEL].reshape(pool_w.shape),
        "w_out": small_rows("wout"), "conv_pw": small_rows("pw").reshape(conv_pw.shape),
        "conv_dw": small_rows("cdw").reshape(-1)[:conv_dw.size].reshape(conv_dw.shape),
    }

    slab_of = {"ffn1_w_gate": "g1", "ffn1_w_up": "u1", "ffn1_w_down": "d1",
               "ffn2_w_gate": "g2", "ffn2_w_up": "u2", "ffn2_w_down": "d2"}
    transposed = ("ffn1_w_gate", "ffn1_w_up", "ffn2_w_gate", "ffn2_w_up", "w_in")
    delta, new_m, new_v = {}, {}, {}
    for n in WEIGHTS:
        wmv = [given[k] for k in (n, "m_" + n, "v_" + n)]
        if n in transposed:
            wmv = [a.T for a in wmv]
        if n in slab_of:
            outs = _adamw_reduced(wmv[0], *reduced[slab_of[n]], wmv[1], wmv[2], "adamw_" + n)
        else:
            gn = small_rows("win") if n == "w_in" else g[n]
            outs = [gn, *_adamw(_as_2d(wmv[0]), _as_2d(gn), _as_2d(wmv[1]), _as_2d(wmv[2]), "adamw_" + n)]
        outs = [a.T if n in transposed else a.reshape(p[n].shape) for a in outs]
        g[n], delta[n], new_m[n], new_v[n] = outs

    return (loss, dx0[None], *[g[n] for n in WEIGHTS], *[delta[n] for n in WEIGHTS],
            *[new_m[n] for n in WEIGHTS], *[new_v[n] for n in WEIGHTS])
```

```python
import functools

import jax
import jax.numpy as jnp
from jax import lax
from jax.experimental import pallas as pl
from jax.experimental.pallas import tpu as pltpu

F32 = jnp.float32
BF16 = jnp.bfloat16

D_MODEL = 1024
D_CONV = 512
D_POOL = 512
D_IN = 2 * D_CONV + D_POOL
POOL_WINDOWS = (2, 4, 8, 16)
POOL_GROUP = D_POOL // len(POOL_WINDOWS)
CONV_WIDTH = 31
RMS_EPS = 1e-6
LN_EPS = 1e-5
FFN_RES_WEIGHT = 0.5

ADAM_LR = 0.001
ADAM_B1 = 0.9
ADAM_B2 = 0.999
ADAM_EPS = 1e-08
ADAM_WD = 0.01
ADAM_STEP = 10

N_DEV = 8
MESH_ID = pl.DeviceIdType.MESH

SUBLANES = 8
TOKEN_TILE = 512
FFN_BWD_TILE = 256
FF_CHUNK = 256
HALO = 32
V7X_VMEM_LIMIT = 56 * 1024 * 1024
CDW_ROWS = 16
REP_ROWS = 16

WEIGHTS = ("ffn1_norm", "ffn1_w_gate", "ffn1_w_up", "ffn1_w_down", "mix_norm", "w_in", "conv_dw", "conv_dw_b",
           "conv_ln_g", "conv_ln_b", "conv_pw", "pool_w", "pool_scale", "w_out", "ffn2_norm", "ffn2_w_gate",
           "ffn2_w_up", "ffn2_w_down", "final_norm")


def _dot_nn(a, b):
    return lax.dot_general(a, b, (((1,), (0,)), ((), ())), preferred_element_type=F32)


def _dot_nt(a, b):
    return lax.dot_general(a, b, (((1,), (1,)), ((), ())), preferred_element_type=F32)


def _dot_tn(a, b):
    return lax.dot_general(a, b, (((0,), (0,)), ((), ())), preferred_element_type=F32)


def _rowsum8(v):
    r, c = v.shape
    return jnp.sum(v.reshape(r // SUBLANES, SUBLANES, c), axis=0)


def _fold8(ref):
    ref[0:1, :] = jnp.sum(ref[...], axis=0, keepdims=True)


def _row_tile(n, cap, mult):
    best = None
    for t in range(mult, min(n, cap) + 1, mult):
        if n % t == 0:
            best = t
    return n if best is None else best


def _params(n_grid):
    return pltpu.CompilerParams(dimension_semantics=("arbitrary",) * n_grid, vmem_limit_bytes=V7X_VMEM_LIMIT)


def _full(shape):
    return pl.BlockSpec(shape, lambda *_: (0,) * len(shape))


def _layout(pieces):
    offs, r = {}, 0
    for name, rows in pieces:
        offs[name] = (r, rows)
        r += rows
    return offs, r


HBM = pl.BlockSpec(memory_space=pl.ANY)


def _mesh_pos():
    return lax.axis_index("x"), lax.axis_index("y"), lax.axis_index("c")


def _remote(src, dst, send_sems, recv_sems, k, to):
    return pltpu.make_async_remote_copy(src_ref=src, dst_ref=dst, send_sem=send_sems.at[k], recv_sem=recv_sems.at[k],
                                        device_id=to, device_id_type=MESH_ID)


class _Gather:
    def __init__(self, shard):
        self.inputs = (shard,)
        self.out_shape = (jax.ShapeDtypeStruct((N_DEV, *shard.shape), shard.dtype),)
        self.scratch = (pltpu.SemaphoreType.DMA((7,)), pltpu.SemaphoreType.DMA((7,)), pltpu.SemaphoreType.DMA)

    def phases(self, ins, outs, scr):
        (x_ref,), (out_ref,), (send_sems, recv_sems, local_sem) = ins, outs, scr
        x, y, c = _mesh_pos()
        me, sibling = (x, y, c), (x, y, 1 - c)
        chips = [(1 - x, y), (x, 1 - y), (1 - x, 1 - y)]

        def block(px, py, pc):
            return out_ref.at[4 * px + 2 * py + pc]

        def copy(k, blk, to, src=None):
            return _remote(block(*blk) if src is None else src, block(*blk), send_sems, recv_sems, k, to)

        def mine():
            return pltpu.make_async_copy(x_ref, block(*me), local_sem)

        def first():
            return [copy(0, me, sibling, src=x_ref)] + [copy(1 + j, me, (*chip, c), src=x_ref) for j, chip in enumerate(chips)]

        def passed(j):
            return copy(4 + j, (*chips[j], c), sibling)

        def start():
            mine().start()
            for cp in first():
                cp.start()

        def forward():
            for j, chip in enumerate(chips):
                copy(1 + j, (*chip, c), me).wait_recv()
                passed(j).start()

        def finish():
            copy(0, sibling, me).wait_recv()
            for j, chip in enumerate(chips):
                copy(4 + j, (*chip, 1 - c), me).wait_recv()
            for cp in first() + [passed(j) for j in range(3)]:
                cp.wait_send()
            mine().wait()

        return [start, forward, finish]


class _SiblingExchange:
    def __init__(self, src):
        self.inputs = (src,)
        self.out_shape = (jax.ShapeDtypeStruct((4, *src.shape[1:]), src.dtype),)
        self.scratch = (pltpu.SemaphoreType.DMA((4,)), pltpu.SemaphoreType.DMA((4,)))

    def phases(self, ins, outs, scr):
        (g_ref,), (recv_ref,), (send_sems, recv_sems) = ins, outs, scr
        x, y, c = _mesh_pos()

        def copies():
            return [_remote(g_ref.at[2 * k + (1 - c)], recv_ref.at[k], send_sems, recv_sems, k, (x, y, 1 - c))
                    for k in range(4)]

        def start():
            for cp in copies():
                cp.start()

        def finish():
            for cp in copies():
                cp.wait()

        return [start, finish]


class _ChipsExchange:
    def __init__(self, src):
        self.inputs = (src,)
        self.out_shape = (jax.ShapeDtypeStruct(src.shape, src.dtype),)
        self.scratch = (pltpu.SemaphoreType.DMA((3,)), pltpu.SemaphoreType.DMA((3,)))

    def phases(self, ins, outs, scr):
        (p_ref,), (recv_ref,), (send_sems, recv_sems) = ins, outs, scr
        x, y, c = _mesh_pos()
        peers = [(1 - x, y, c), (x, 1 - y, c), (1 - x, 1 - y, c)]

        def copies():
            return [_remote(p_ref.at[k], recv_ref.at[k], send_sems, recv_sems, k, peer) for k, peer in enumerate(peers)]

        def start():
            for cp in copies():
                cp.start()

        def finish():
            for cp in copies():
                cp.wait()

        return [start, finish]


class _Together:
    def __init__(self, *plans):
        self.plans = plans
        self.inputs = tuple(a for p in plans for a in p.inputs)
        self.out_shape = tuple(o for p in plans for o in p.out_shape)
        self.scratch = tuple(s for p in plans for s in p.scratch)

    def phases(self, ins, outs, scr):
        each = []
        for p in self.plans:
            n_in, n_out, n_scr = len(p.inputs), len(p.out_shape), len(p.scratch)
            each.append(p.phases(ins[:n_in], outs[:n_out], scr[:n_scr]))
            ins, outs, scr = ins[n_in:], outs[n_out:], scr[n_scr:]

        def run(fns):
            def phase():
                for fn in fns:
                    fn()
            return phase

        middle = [fn for ph in each for fn in ph[1:-1]]
        return [run([ph[0] for ph in each]), *([run(middle)] if middle else []), run([ph[-1] for ph in each])]


def _run_comm(plan, name):
    n_in, n_out = len(plan.inputs), len(plan.out_shape)

    def body(*refs):
        for phase in plan.phases(refs[:n_in], refs[n_in:n_in + n_out], refs[n_in + n_out:]):
            phase()

    return pl.pallas_call(
        body, name=name, out_shape=list(plan.out_shape), in_specs=[HBM] * n_in, out_specs=[HBM] * n_out,
        scratch_shapes=list(plan.scratch))(*plan.inputs)


def _grid_call(body, *, name, nt, in_specs, out_specs, out_shape, scratch_shapes, args, plan=None):
    if plan is None:
        return pl.pallas_call(body, name=name, grid=(nt,), in_specs=in_specs, out_specs=out_specs, out_shape=out_shape,
                              scratch_shapes=scratch_shapes, compiler_params=_params(1))(*args)
    n_in, n_out, n_scr = len(in_specs), len(out_specs), len(scratch_shapes)
    p_in, p_out = len(plan.inputs), len(plan.out_shape)

    def with_plan(*refs):
        ins, refs = refs[:n_in], refs[n_in:]
        p_ins, refs = refs[:p_in], refs[p_in:]
        outs, refs = refs[:n_out], refs[n_out:]
        p_outs, refs = refs[:p_out], refs[p_out:]
        scr, p_scr = refs[:n_scr], refs[n_scr:]
        phases = plan.phases(p_ins, p_outs, p_scr)
        i = pl.program_id(0)
        pl.when(i == 0)(phases[0])
        for phase in phases[1:-1]:
            pl.when(i == min(max(nt - 3, 1), nt - 1))(phase)
        body(*ins, *outs, *scr)
        pl.when(i == nt - 1)(phases[-1])

    return pl.pallas_call(
        with_plan, name=name, grid=(nt,), in_specs=[*in_specs, *[HBM] * p_in], out_specs=[*out_specs, *[HBM] * p_out],
        out_shape=[*out_shape, *plan.out_shape], scratch_shapes=[*scratch_shapes, *plan.scratch],
        compiler_params=_params(1))(*args, *plan.inputs)


def _add_chunks(gslab, recv, gid, rid, out_dtype, name):
    n = gid.shape[0]
    _, rows, cols = gslab.shape
    tr = _row_tile(rows, 1024, 16)

    def body(gid_ref, rid_ref, a_ref, b_ref, o_ref):
        o_ref[...] = (a_ref[...] + b_ref[...]).astype(out_dtype)

    grid_spec = pltpu.PrefetchScalarGridSpec(
        num_scalar_prefetch=2, grid=(n, rows // tr),
        in_specs=[pl.BlockSpec((1, tr, cols), lambda k, i, g, r: (g[k], i, 0)),
                  pl.BlockSpec((1, tr, cols), lambda k, i, g, r: (r[k], i, 0))],
        out_specs=pl.BlockSpec((1, tr, cols), lambda k, i, g, r: (k, i, 0)))
    return pl.pallas_call(
        body, name=name, grid_spec=grid_spec,
        out_shape=jax.ShapeDtypeStruct((n, rows, cols), out_dtype),
        compiler_params=_params(2),
    )(gid, rid, gslab, recv)


def _sum_partials(own, recv, name):
    _, rows, cols = own.shape
    tr = _row_tile(rows, 1024, 16)

    def body(o_ref, r_ref, out_ref):
        acc = o_ref[0]
        for k in range(3):
            acc = acc + r_ref[k].astype(F32)
        out_ref[...] = acc

    return pl.pallas_call(
        body, name=name, grid=(rows // tr,),
        in_specs=[pl.BlockSpec((1, tr, cols), lambda i: (0, i, 0)), pl.BlockSpec((3, tr, cols), lambda i: (0, i, 0))],
        out_specs=pl.BlockSpec((tr, cols), lambda i: (i, 0)),
        out_shape=jax.ShapeDtypeStruct((rows, cols), F32),
        compiler_params=_params(1),
    )(own, recv)


def _chunk_ids():
    x, y, c = _mesh_pos()
    chips = [(x, y), (1 - x, y), (x, 1 - y), (1 - x, 1 - y)]
    gid = jnp.stack([4 * px + 2 * py + c for px, py in chips]).astype(jnp.int32)
    rid = jnp.stack([2 * px + py for px, py in chips]).astype(jnp.int32)
    return gid, rid


def _sibling_sums(slab, recv, tag):
    gid, rid = _chunk_ids()
    own = _add_chunks(slab, recv, gid[:1], rid[:1], F32, "rs_add_own_" + tag)
    part = _add_chunks(slab, recv, gid[1:], rid[1:], BF16, "rs_add_send_" + tag)
    return own, part


def _load_weights(slab_ref, offs, dsts, sems):
    cps = []
    for i, (off, dst) in enumerate(zip(offs, dsts)):
        f8 = dst.shape[0] // N_DEV
        cps += [pltpu.make_async_copy(slab_ref.at[j, pl.ds(off, f8), :], dst.at[pl.ds(j * f8, f8), :],
                                      sems.at[i * N_DEV + j]) for j in range(N_DEV)]
    for cp in cps:
        cp.start()
    for cp in cps:
        cp.wait()


def _chunk_rows(c):
    return pl.ds(pl.multiple_of(c * FF_CHUNK, FF_CHUNK), FF_CHUNK)


def _rms(xv):
    return lax.rsqrt(jnp.mean(xv * xv, axis=-1, keepdims=True) + RMS_EPS)


def _loss_terms(xv, tgt, gain):
    r = _rms(xv)
    xh = xv * r
    err = xh * gain - tgt
    loss = 0.5 * jnp.sum(jnp.mean(err * err, axis=-1, keepdims=True))
    dy = err * (1.0 / xv.shape[-1])
    dxh = dy * gain
    return loss, r * (dxh - xh * jnp.mean(dxh * xh, axis=-1, keepdims=True)), _rowsum8(dy * xh)


def _ffn_fwd(x, gain, slab, offs, f, name, plan=None, head=None):
    t, d = x.shape
    nc, tm = f // FF_CHUNK, TOKEN_TILE
    nt = t // tm
    n_head = 0 if head is None else 2

    def body(*refs):
        x_ref, gain_ref, slab_ref = refs[:3]
        xo_ref, g_ref, u_ref, n_ref = refs[3 + n_head:7 + n_head]
        wg_v, wu_v, wd_v, acc_ref, sems = refs[7 + 2 * n_head:]
        i = pl.program_id(0)

        @pl.when(i == 0)
        def _():
            _load_weights(slab_ref, offs, (wg_v, wu_v, wd_v), sems)

        xv = x_ref[...]
        n_ref[...] = ((xv * _rms(xv)) * gain_ref[...]).astype(BF16)
        acc_ref[...] = jnp.zeros_like(acc_ref)

        def chunk(c, carry):
            rows = _chunk_rows(c)
            nb = n_ref[...]
            g = _dot_nt(nb, wg_v[rows, :])
            u = _dot_nt(nb, wu_v[rows, :])
            g_ref[c] = g.astype(BF16)
            u_ref[c] = u.astype(BF16)
            h = (g * jax.nn.sigmoid(g)) * u
            acc_ref[...] += _dot_nn(h.astype(BF16), wd_v[rows, :])
            return carry

        lax.fori_loop(0, nc, chunk, 0, unroll=True)
        out = xv + FFN_RES_WEIGHT * acc_ref[...]
        if head is None:
            xo_ref[...] = out
            return
        tgt_ref, fgain_ref = refs[3:5]
        dgain_ref, loss_ref = refs[7 + n_head:7 + 2 * n_head]

        @pl.when(i == 0)
        def _():
            dgain_ref[...] = jnp.zeros_like(dgain_ref)
            loss_ref[...] = jnp.zeros_like(loss_ref)

        loss, dx, dgain = _loss_terms(out, tgt_ref[...], fgain_ref[...])
        xo_ref[...] = dx
        loss_ref[...] += loss
        dgain_ref[...] += dgain

        @pl.when(i == nt - 1)
        def _():
            _fold8(dgain_ref)

    tile = pl.BlockSpec((tm, d), lambda i: (i, 0))
    act = pl.BlockSpec((nc, tm, FF_CHUNK), lambda i: (0, i, 0))
    head_in = [] if head is None else [tile, _full((1, d))]
    head_out = [] if head is None else [_full((SUBLANES, d)), _full((SUBLANES, 128))]
    head_shape = [] if head is None else [jax.ShapeDtypeStruct((SUBLANES, d), F32), jax.ShapeDtypeStruct((SUBLANES, 128), F32)]
    return _grid_call(
        body, name=name, nt=nt, plan=plan,
        in_specs=[tile, _full((1, d)), HBM, *head_in],
        out_specs=[tile, act, act, tile, *head_out],
        out_shape=[jax.ShapeDtypeStruct((t, d), F32), jax.ShapeDtypeStruct((nc, t, FF_CHUNK), BF16),
                   jax.ShapeDtypeStruct((nc, t, FF_CHUNK), BF16), jax.ShapeDtypeStruct((t, d), BF16), *head_shape],
        scratch_shapes=[pltpu.VMEM((f, d), BF16), pltpu.VMEM((f, d), BF16), pltpu.VMEM((f, d), BF16),
                        pltpu.VMEM((tm, d), F32), pltpu.SemaphoreType.DMA((3 * N_DEV,))],
        args=(x, gain, slab, *([] if head is None else head)))


def _ffn_bwd(dxo, x, gain, gs, us, slab, offs, f, name, plan=None):
    t, d = x.shape
    nc, tm = f // FF_CHUNK, FFN_BWD_TILE
    nt = t // tm

    def body(dxo_ref, x_ref, gain_ref, g_ref, u_ref, slab_ref,
             dx_ref, h_ref, dg_ref, du_ref, df_ref, dgain_ref, wg_v, wu_v, wd_v, sems):
        i = pl.program_id(0)

        @pl.when(i == 0)
        def _():
            _load_weights(slab_ref, offs, (wg_v, wu_v, wd_v), sems)
            dgain_ref[...] = jnp.zeros_like(dgain_ref)

        df_ref[...] = (FFN_RES_WEIGHT * dxo_ref[...]).astype(BF16)

        def chunk(c, carry):
            rows = _chunk_rows(c)
            g = g_ref[c].astype(F32)
            u = u_ref[c].astype(F32)
            sg = jax.nn.sigmoid(g)
            sil = g * sg
            dh = _dot_nt(df_ref[...], wd_v[rows, :])
            h_ref[c] = (sil * u).astype(BF16)
            du_ref[c] = (dh * sil).astype(BF16)
            dg_ref[c] = (dh * u * (sg * (1.0 + g * (1.0 - sg)))).astype(BF16)
            return carry

        lax.fori_loop(0, nc, chunk, 0, unroll=True)

        def back(c, dn):
            rows = _chunk_rows(c)
            return dn + _dot_nn(dg_ref[c], wg_v[rows, :]) + _dot_nn(du_ref[c], wu_v[rows, :])

        dn = lax.fori_loop(0, nc, back, jnp.zeros((tm, d), F32), unroll=True)
        xv = x_ref[...]
        r = _rms(xv)
        xh = xv * r
        dgain_ref[...] += _rowsum8(dn * xh)
        dxh = dn * gain_ref[...]
        dx_ref[...] = dxo_ref[...] + r * (dxh - xh * jnp.mean(dxh * xh, axis=-1, keepdims=True))

        @pl.when(i == nt - 1)
        def _():
            _fold8(dgain_ref)

    tile = pl.BlockSpec((tm, d), lambda i: (i, 0))
    act = pl.BlockSpec((nc, tm, FF_CHUNK), lambda i: (0, i, 0))
    act_shape = jax.ShapeDtypeStruct((nc, t, FF_CHUNK), BF16)
    return _grid_call(
        body, name=name, nt=nt, plan=plan,
        in_specs=[tile, tile, _full((1, d)), act, act, HBM],
        out_specs=[tile, act, act, act, tile, _full((SUBLANES, d))],
        out_shape=[jax.ShapeDtypeStruct((t, d), F32), act_shape, act_shape, act_shape,
                   jax.ShapeDtypeStruct((t, d), BF16), jax.ShapeDtypeStruct((SUBLANES, d), F32)],
        scratch_shapes=[pltpu.VMEM((f, d), BF16), pltpu.VMEM((f, d), BF16), pltpu.VMEM((f, d), BF16),
                        pltpu.SemaphoreType.DMA((3 * N_DEV,))],
        args=(dxo, x, gain, gs, us, slab))


def _tn_chunked(a, b, name, plan=None):
    nc, t, _ = a.shape
    n = b.shape[1]
    tb = _row_tile(t, 1024, TOKEN_TILE)

    def body(a_ref, b_ref, o_ref):
        @pl.when(pl.program_id(0) == 0)
        def _():
            o_ref[...] = jnp.zeros_like(o_ref)

        def chunk(c, carry):
            rows = _chunk_rows(c)
            o_ref[rows, :] += _dot_tn(a_ref[c], b_ref[...])
            return carry

        lax.fori_loop(0, nc, chunk, 0, unroll=True)

    return _grid_call(
        body, name=name, nt=t // tb, plan=plan,
        in_specs=[pl.BlockSpec((nc, tb, FF_CHUNK), lambda i: (0, i, 0)), pl.BlockSpec((tb, n), lambda i: (i, 0))],
        out_specs=[_full((nc * FF_CHUNK, n))],
        out_shape=[jax.ShapeDtypeStruct((nc * FF_CHUNK, n), F32)],
        scratch_shapes=[], args=(a, b))


def _tn(a, b, name, plan=None):
    t, k = a.shape
    n = b.shape[1]
    tb = _row_tile(t, 1024, TOKEN_TILE)

    def body(a_ref, b_ref, o_ref):
        @pl.when(pl.program_id(0) == 0)
        def _():
            o_ref[...] = jnp.zeros_like(o_ref)

        o_ref[...] += _dot_tn(a_ref[...].astype(BF16), b_ref[...].astype(BF16))

    return _grid_call(
        body, name=name, nt=t // tb, plan=plan,
        in_specs=[pl.BlockSpec((tb, k), lambda i: (i, 0)), pl.BlockSpec((tb, n), lambda i: (i, 0))],
        out_specs=[_full((k, n))],
        out_shape=[jax.ShapeDtypeStruct((k, n), F32)],
        scratch_shapes=[], args=(a, b))


def _layernorm_stats(u1):
    mu = jnp.mean(u1, axis=-1, keepdims=True)
    xc = u1 - mu
    rstd = lax.rsqrt(jnp.mean(xc * xc, axis=-1, keepdims=True) + LN_EPS)
    return xc * rstd, rstd


def _positions(tile_index, tm):
    return (tile_index * tm + lax.broadcasted_iota(jnp.int32, (tm, 1), 0)).astype(F32)


def _shifted_taps(src_ref, sh_ref, tm, offset_of):
    groups = {}
    for k in range(CONV_WIDTH):
        groups.setdefault(offset_of(k) % SUBLANES, []).append(k)
    span = tm + HALO - SUBLANES
    for rem, taps in sorted(groups.items()):
        if rem:
            sh_ref[0:span, :] = src_ref[rem:rem + span, :]
        ref = sh_ref if rem else src_ref
        for k in taps:
            base = offset_of(k) - rem
            yield k, ref[base:base + tm, :]


def _mix_fwd(x, gm, win_t, cdw, cb, lg, lb, pw, poolw, ps, wout):
    t, d = x.shape
    tm = TOKEN_TILE

    def body(x_ref, gm_ref, win_ref, cdw_ref, cb_ref, lg_ref, lb_ref, pw_ref, poolw_ref, ps_ref, wout_ref,
             xo_ref, h_ref, ag_ref, u0_ref, u1_ref, u2_ref, mixed_ref, cat_ref, eu_ref, ep_ref, sh_ref):
        i = pl.program_id(0)

        @pl.when(i == 0)
        def _():
            eu_ref[0:HALO, :] = jnp.zeros((HALO, D_CONV), F32)
            ep_ref[0:HALO, :] = jnp.zeros((HALO, D_POOL), F32)

        @pl.when(i > 0)
        def _():
            eu_ref[0:HALO, :] = eu_ref[tm:tm + HALO, :]
            ep_ref[0:HALO, :] = ep_ref[tm:tm + HALO, :]

        xv = x_ref[...]
        hb = ((xv * _rms(xv)) * gm_ref[...]).astype(BF16)
        h_ref[...] = hb
        proj = _dot_nt(hb, win_ref[...])
        a = proj[:, :D_CONV]
        g = proj[:, D_CONV:2 * D_CONV]
        ag_ref[...] = proj[:, :2 * D_CONV]
        u0 = a * jax.nn.sigmoid(g)
        u0_ref[...] = u0
        eu_ref[HALO:HALO + tm, :] = u0
        ep_ref[HALO:HALO + tm, :] = proj[:, 2 * D_CONV:]

        u1 = jnp.broadcast_to(cb_ref[...], (tm, D_CONV))
        for k, rows in _shifted_taps(eu_ref, sh_ref, tm, lambda k: HALO - (CONV_WIDTH - 1) + k):
            u1 = u1 + cdw_ref[k:k + 1, :] * rows
        u1_ref[...] = u1
        lnh, _ = _layernorm_stats(u1)
        ln = lnh * lg_ref[...] + lb_ref[...]
        u2 = (ln * jax.nn.sigmoid(ln)).astype(BF16)
        u2_ref[...] = u2
        conv_out = _dot_nn(u2, pw_ref[...])

        pos = _positions(i, tm)
        outs = []
        for gi, w in enumerate(POOL_WINDOWS):
            lo = gi * POOL_GROUP
            p = ep_ref[HALO:HALO + tm, lo:lo + POOL_GROUP]
            s = p
            for j in range(1, w):
                s = s + ep_ref[HALO - j:HALO - j + tm, lo:lo + POOL_GROUP]
            mixed = (s / jnp.minimum(pos + 1.0, float(w)) - p).astype(BF16)
            mixed_ref[:, lo:lo + POOL_GROUP] = mixed
            outs.append(_dot_nn(mixed, poolw_ref[gi]))
        pool_out = jnp.concatenate(outs, axis=-1) * ps_ref[...]
        cat = jnp.concatenate([conv_out, pool_out], axis=-1).astype(BF16)
        cat_ref[...] = cat
        xo_ref[...] = xv + _dot_nn(cat, wout_ref[...])

    def tile(c):
        return pl.BlockSpec((tm, c), lambda i: (i, 0))

    def out(c, dt):
        return jax.ShapeDtypeStruct((t, c), dt)

    return pl.pallas_call(
        body, name="mix_fwd", grid=(t // tm,),
        in_specs=[tile(d), _full((1, d)), _full((D_IN, d)), _full((HALO, D_CONV)), _full((1, D_CONV)),
                  _full((1, D_CONV)), _full((1, D_CONV)), _full((D_CONV, D_CONV)),
                  _full((len(POOL_WINDOWS), POOL_GROUP, POOL_GROUP)), _full((1, D_POOL)), _full((d, d))],
        out_specs=[tile(d), tile(d), tile(2 * D_CONV), tile(D_CONV), tile(D_CONV), tile(D_CONV), tile(D_POOL), tile(d)],
        out_shape=[out(d, F32), out(d, BF16), out(2 * D_CONV, F32), out(D_CONV, F32), out(D_CONV, F32),
                   out(D_CONV, BF16), out(D_POOL, BF16), out(d, BF16)],
        scratch_shapes=[pltpu.VMEM((HALO + tm, D_CONV), F32), pltpu.VMEM((HALO + tm, D_POOL), F32),
                        pltpu.VMEM((HALO + tm, D_CONV), F32)],
        compiler_params=_params(1),
    )(x, gm, win_t, cdw, cb, lg, lb, pw, poolw, ps, wout)


def _mix_bwd(dxo, x, gm, ag, u0, u1, mixed, win_t, cdw, lg, lb, pw, poolw, ps, wout, plan=None):
    t, d = x.shape
    tm = TOKEN_TILE
    nt = t // tm
    halo_blocks = tm // HALO

    def body(dxo_ref, x_ref, gm_ref, ag_ref, u0_ref, u0h_ref, u1_ref, mixed_ref,
             win_ref, cdw_ref, lg_ref, lb_ref, pw_ref, poolw_ref, ps_ref, wout_ref,
             dx_ref, dproj_ref, dco_ref, dgm_ref, dcdw_ref, dcb_ref, dlg_ref, dlb_ref, dpoolw_ref, dps_ref,
             eu_ref, ed_ref, eq_ref, sh_ref):
        i = pl.program_id(0)
        ti = nt - 1 - i

        @pl.when(i == 0)
        def _():
            for ref in (dgm_ref, dcdw_ref, dcb_ref, dlg_ref, dlb_ref, dpoolw_ref, dps_ref):
                ref[...] = jnp.zeros_like(ref)
            ed_ref[tm:tm + HALO, :] = jnp.zeros((HALO, D_CONV), F32)
            eq_ref[tm:tm + HALO, :] = jnp.zeros((HALO, D_POOL), F32)

        @pl.when(i > 0)
        def _():
            ed_ref[tm:tm + HALO, :] = ed_ref[0:HALO, :]
            eq_ref[tm:tm + HALO, :] = eq_ref[0:HALO, :]

        @pl.when(ti == 0)
        def _():
            eu_ref[0:HALO, :] = jnp.zeros((HALO, D_CONV), F32)

        @pl.when(ti > 0)
        def _():
            eu_ref[0:HALO, :] = u0h_ref[...]

        eu_ref[HALO:HALO + tm, :] = u0_ref[...]

        dxo = dxo_ref[...]
        dcat = _dot_nt(dxo.astype(BF16), wout_ref[...])
        dco = dcat[:, :D_CONV].astype(BF16)
        dco_ref[...] = dco
        dpo = dcat[:, D_CONV:]

        lnh, rstd = _layernorm_stats(u1_ref[...])
        ln = lnh * lg_ref[...] + lb_ref[...]
        sl = jax.nn.sigmoid(ln)
        dln = _dot_nt(dco, pw_ref[...]) * (sl * (1.0 + ln * (1.0 - sl)))
        dlg_ref[...] += _rowsum8(dln * lnh)
        dlb_ref[...] += _rowsum8(dln)
        dlnh = dln * lg_ref[...]
        du1 = rstd * (dlnh - jnp.mean(dlnh, axis=-1, keepdims=True)
                      - lnh * jnp.mean(dlnh * lnh, axis=-1, keepdims=True))
        dcb_ref[...] += _rowsum8(du1)
        ed_ref[0:tm, :] = du1

        du0 = jnp.zeros((tm, D_CONV), F32)
        for k, rows in _shifted_taps(ed_ref, sh_ref, tm, lambda k: CONV_WIDTH - 1 - k):
            du0 = du0 + cdw_ref[k:k + 1, :] * rows
        for k, rows in _shifted_taps(eu_ref, sh_ref, tm, lambda k: HALO - (CONV_WIDTH - 1) + k):
            dcdw_ref[SUBLANES * k:SUBLANES * (k + 1), :] += _rowsum8(du1 * rows)
        a = ag_ref[:, :D_CONV]
        sg = jax.nn.sigmoid(ag_ref[:, D_CONV:])
        pieces = [du0 * sg, du0 * a * (sg * (1.0 - sg))]

        pos = _positions(ti, tm)
        for gi, w in enumerate(POOL_WINDOWS):
            lo = gi * POOL_GROUP
            mg = mixed_ref[:, lo:lo + POOL_GROUP]
            dpo_g = dpo[:, lo:lo + POOL_GROUP]
            dps_ref[:, lo:lo + POOL_GROUP] += _rowsum8(dpo_g * _dot_nn(mg, poolw_ref[gi]))
            dout = (dpo_g * ps_ref[:, lo:lo + POOL_GROUP]).astype(BF16)
            dpoolw_ref[gi] += _dot_tn(mg, dout)
            dmx = _dot_nt(dout, poolw_ref[gi])
            q = dmx / jnp.minimum(pos + 1.0, float(w))
            eq_ref[0:tm, lo:lo + POOL_GROUP] = q
            s = q
            for j in range(1, w):
                s = s + eq_ref[j:j + tm, lo:lo + POOL_GROUP]
            pieces.append(s - dmx)
        dproj = jnp.concatenate(pieces, axis=-1).astype(BF16)
        dproj_ref[...] = dproj

        dh = _dot_nn(dproj, win_ref[...])
        xv = x_ref[...]
        r = _rms(xv)
        xh = xv * r
        dgm_ref[...] += _rowsum8(dh * xh)
        dxh = dh * gm_ref[...]
        dx_ref[...] = dxo + r * (dxh - xh * jnp.mean(dxh * xh, axis=-1, keepdims=True))

        @pl.when(i == nt - 1)
        def _():
            for ref in (dgm_ref, dcb_ref, dlg_ref, dlb_ref, dps_ref):
                _fold8(ref)
            for k in range(CONV_WIDTH):
                dcdw_ref[SUBLANES * k:SUBLANES * k + 1, :] = jnp.sum(
                    dcdw_ref[SUBLANES * k:SUBLANES * (k + 1), :], axis=0, keepdims=True)

    def tile(c):
        return pl.BlockSpec((tm, c), lambda i: (nt - 1 - i, 0))

    halo = pl.BlockSpec((HALO, D_CONV), lambda i: (jnp.maximum((nt - 1 - i) * halo_blocks - 1, 0), 0))
    n_groups = len(POOL_WINDOWS)
    return _grid_call(
        body, name="mix_bwd", nt=nt, plan=plan,
        in_specs=[tile(d), tile(d), _full((1, d)), tile(2 * D_CONV), tile(D_CONV), halo, tile(D_CONV), tile(D_POOL),
                  _full((D_IN, d)), _full((HALO, D_CONV)), _full((1, D_CONV)), _full((1, D_CONV)),
                  _full((D_CONV, D_CONV)), _full((n_groups, POOL_GROUP, POOL_GROUP)), _full((1, D_POOL)), _full((d, d))],
        out_specs=[tile(d), tile(D_IN), tile(D_CONV), _full((SUBLANES, d)), _full((HALO * SUBLANES, D_CONV)),
                   _full((SUBLANES, D_CONV)), _full((SUBLANES, D_CONV)), _full((SUBLANES, D_CONV)),
                   _full((n_groups, POOL_GROUP, POOL_GROUP)), _full((SUBLANES, D_POOL))],
        out_shape=[jax.ShapeDtypeStruct((t, d), F32), jax.ShapeDtypeStruct((t, D_IN), BF16),
                   jax.ShapeDtypeStruct((t, D_CONV), BF16), jax.ShapeDtypeStruct((SUBLANES, d), F32),
                   jax.ShapeDtypeStruct((HALO * SUBLANES, D_CONV), F32), jax.ShapeDtypeStruct((SUBLANES, D_CONV), F32),
                   jax.ShapeDtypeStruct((SUBLANES, D_CONV), F32), jax.ShapeDtypeStruct((SUBLANES, D_CONV), F32),
                   jax.ShapeDtypeStruct((n_groups, POOL_GROUP, POOL_GROUP), F32),
                   jax.ShapeDtypeStruct((SUBLANES, D_POOL), F32)],
        scratch_shapes=[pltpu.VMEM((HALO + tm, D_CONV), F32), pltpu.VMEM((tm + HALO, D_CONV), F32),
                        pltpu.VMEM((tm + HALO, D_POOL), F32), pltpu.VMEM((tm + HALO, D_CONV), F32)],
        args=(dxo, x, gm, ag, u0, u0, u1, mixed, win_t, cdw, lg, lb, pw, poolw, ps, wout))


def _adam_step(gv, w_ref, m_ref, v_ref, d_ref, nm_ref, nv_ref):
    nm = ADAM_B1 * m_ref[...] + (1.0 - ADAM_B1) * gv
    nv = ADAM_B2 * v_ref[...] + (1.0 - ADAM_B2) * (gv * gv)
    m_hat = nm / (1.0 - ADAM_B1 ** ADAM_STEP)
    v_hat = nv / (1.0 - ADAM_B2 ** ADAM_STEP)
    d_ref[...] = -ADAM_LR * (m_hat / (jnp.sqrt(v_hat) + ADAM_EPS) + ADAM_WD * w_ref[...])
    nm_ref[...] = nm
    nv_ref[...] = nv


def _adam_tile(rows, cols):
    return _row_tile(rows, max(SUBLANES, (256 * 1024) // cols // SUBLANES * SUBLANES), SUBLANES)


def _adamw(w, g, m, v, name):
    rows, cols = w.shape
    tr = _adam_tile(rows, cols)

    def body(w_ref, g_ref, m_ref, v_ref, d_ref, nm_ref, nv_ref):
        _adam_step(g_ref[...], w_ref, m_ref, v_ref, d_ref, nm_ref, nv_ref)

    blk = pl.BlockSpec((tr, cols), lambda i: (i, 0))
    shape = jax.ShapeDtypeStruct((rows, cols), F32)
    return pl.pallas_call(
        body, name=name, grid=(rows // tr,),
        in_specs=[blk] * 4, out_specs=[blk] * 3, out_shape=[shape] * 3,
        compiler_params=_params(1),
    )(w, g, m, v)


def _adamw_reduced(w, own, recv, m, v, name):
    rows, cols = w.shape
    tr = _adam_tile(rows, cols)

    def body(w_ref, o_ref, r_ref, m_ref, v_ref, g_ref, d_ref, nm_ref, nv_ref):
        gv = o_ref[0]
        for k in range(3):
            gv = gv + r_ref[k].astype(F32)
        g_ref[...] = gv
        _adam_step(gv, w_ref, m_ref, v_ref, d_ref, nm_ref, nv_ref)

    blk = pl.BlockSpec((tr, cols), lambda i: (i, 0))
    shape = jax.ShapeDtypeStruct((rows, cols), F32)
    return pl.pallas_call(
        body, name=name, grid=(rows // tr,),
        in_specs=[blk, pl.BlockSpec((1, tr, cols), lambda i: (0, i, 0)), pl.BlockSpec((3, tr, cols), lambda i: (0, i, 0)),
                  blk, blk],
        out_specs=[blk] * 4, out_shape=[shape] * 4,
        compiler_params=_params(1),
    )(w, own, recv, m, v)


def _as_2d(a):
    if a.ndim == 1:
        return a.reshape(a.shape[0] // 128, 128)
    if a.ndim == 3:
        return a.reshape(a.shape[0] * a.shape[1], a.shape[2])
    return a


def _pack_weight_slabs(p):
    def bf(parts):
        return [a.astype(BF16) for a in parts]

    cdw_bits = lax.bitcast_convert_type(p["conv_dw"], BF16).reshape(CONV_WIDTH, 2 * D_CONV // N_DEV)
    cdw_bits = jnp.pad(cdw_bits, ((0, 1), (0, 0))).reshape(4, D_MODEL)
    cdw_bits = jnp.pad(cdw_bits, ((0, CDW_ROWS - 4), (0, 0)))
    first = bf([p["ffn1_w_gate"].T, p["ffn1_w_up"].T, p["ffn1_w_down"]])
    rest = bf([p["ffn2_w_gate"].T, p["ffn2_w_up"].T, p["ffn2_w_down"], p["w_in"].T, p["w_out"],
               p["conv_pw"].reshape(D_CONV // N_DEV // 2, D_MODEL)]) + [cdw_bits]
    return jnp.concatenate(first, axis=0), jnp.concatenate(rest, axis=0)


def _unpack_rows(slab, offs, names):
    out = {}
    for name in names:
        o, n = offs[name]
        out[name] = slab[:, o:o + n, :].reshape(N_DEV * n, D_MODEL)
    return out


def _unpack_conv_taps(slab, offs):
    o, _ = offs["cdw"]
    bits = slab[:, o:o + 4, :].reshape(N_DEV, CONV_WIDTH + 1, D_CONV // N_DEV, 2)[:, :CONV_WIDTH]
    cdw = lax.bitcast_convert_type(bits, F32)
    return jnp.transpose(cdw, (1, 0, 2)).reshape(CONV_WIDTH, D_CONV)


def kernel(x, ffn1_norm, ffn1_w_gate, ffn1_w_up, ffn1_w_down, mix_norm, w_in, conv_dw, conv_dw_b, conv_ln_g, conv_ln_b, conv_pw, pool_w, pool_scale, w_out, ffn2_norm, ffn2_w_gate, ffn2_w_up, ffn2_w_down, final_norm, loss_target, m_ffn1_norm, m_ffn1_w_gate, m_ffn1_w_up, m_ffn1_w_down, m_mix_norm, m_w_in, m_conv_dw, m_conv_dw_b, m_conv_ln_g, m_conv_ln_b, m_conv_pw, m_pool_w, m_pool_scale, m_w_out, m_ffn2_norm, m_ffn2_w_gate, m_ffn2_w_up, m_ffn2_w_down, m_final_norm, v_ffn1_norm, v_ffn1_w_gate, v_ffn1_w_up, v_ffn1_w_down, v_mix_norm, v_w_in, v_conv_dw, v_conv_dw_b, v_conv_ln_g, v_conv_ln_b, v_conv_pw, v_pool_w, v_pool_scale, v_w_out, v_ffn2_norm, v_ffn2_w_gate, v_ffn2_w_up, v_ffn2_w_down, v_final_norm):
    given = dict(locals())
    p = {n: given[n] for n in WEIGHTS}
    f8 = ffn1_w_gate.shape[1]
    f = N_DEV * f8
    ffn_rows = (0, f8, 2 * f8)
    small = (("win", D_IN // N_DEV), ("wout", D_MODEL // N_DEV), ("pw", D_CONV // N_DEV // 2), ("cdw", CDW_ROWS))
    w_offs, _ = _layout((("g2", f8), ("u2", f8), ("d2", f8)) + small)
    s_offs, _ = _layout(small + (("rep", REP_ROWS),))
    x0 = x[0]
    target = loss_target[0]

    def row(vec):
        return vec.reshape(1, vec.shape[0])

    slab_first, slab_rest = _pack_weight_slabs(p)
    w_first = _run_comm(_Gather(slab_first), "gather_ffn1")[0]
    x1, g1s, u1s, n1, w_rest = _ffn_fwd(x0, row(ffn1_norm), w_first, ffn_rows, f, "ffn1_fwd", _Gather(slab_rest))
    w = _unpack_rows(w_rest, w_offs, ("win", "wout", "pw"))
    w["pw"] = w["pw"].reshape(D_CONV, D_CONV)
    cdw = jnp.pad(_unpack_conv_taps(w_rest, w_offs), ((0, HALO - CONV_WIDTH), (0, 0)))
    poolw = pool_w.astype(BF16)

    x2, h, ag, u0, u1, u2, mixed, cat = _mix_fwd(
        x1, row(mix_norm), w["win"], cdw, row(conv_dw_b), row(conv_ln_g), row(conv_ln_b), w["pw"], poolw,
        row(pool_scale), w["wout"])
    dx3, g2s, u2s, n2, d_final_norm, loss_part = _ffn_fwd(
        x2, row(ffn2_norm), w_rest, ffn_rows, f, "ffn2_fwd", head=(target, row(final_norm)))

    pending, reduced = {}, {}

    def chunks(a):
        return a.reshape(N_DEV, -1, D_MODEL)

    def after_sibling(name, slab, recv):
        pending[name], part = _sibling_sums(slab, recv, name)
        return _ChipsExchange(part)

    def after_chips(name, recv):
        reduced[name] = (pending.pop(name), recv)

    dx2, h2, dg2, du2, df2, d_ffn2_norm = _ffn_bwd(dx3, x2, row(ffn2_norm), g2s, u2s, w_rest, ffn_rows, f, "ffn2_bwd")
    s_g2 = chunks(_tn_chunked(dg2, n2, "ffn2_dgate")[0])
    s_u2, r = _tn_chunked(du2, n2, "ffn2_dup", _SiblingExchange(s_g2))
    s_u2 = chunks(s_u2)
    to_chips = after_sibling("g2", s_g2, r)
    s_d2, rc, r = _tn_chunked(h2, df2, "ffn2_ddown", _Together(to_chips, _SiblingExchange(s_u2)))
    s_d2 = chunks(s_d2)
    after_chips("g2", rc)
    to_chips = after_sibling("u2", s_u2, r)
    dx1, dproj, dco, d_mix_norm, d_cdw, d_cb, d_lg, d_lb, d_poolw, d_ps, rc, r = _mix_bwd(
        dx2, x1, row(mix_norm), ag, u0, u1, mixed, w["win"], cdw, row(conv_ln_g), row(conv_ln_b), w["pw"], poolw,
        row(pool_scale), w["wout"], _Together(to_chips, _SiblingExchange(s_d2)))
    after_chips("u2", rc)
    to_chips = after_sibling("d2", s_d2, r)
    d_win, rc = _tn(dproj, h, "mix_dwin", to_chips)
    after_chips("d2", rc)
    d_wout = _tn(cat, dx2, "mix_dwout")[0]
    d_pw = _tn(u2, dco, "mix_dpw")[0]
    dx0, h1, dg1, du1, df1, d_ffn1_norm = _ffn_bwd(dx1, x0, row(ffn1_norm), g1s, u1s, w_first, ffn_rows, f, "ffn1_bwd")

    d_cdw = d_cdw.reshape(HALO, SUBLANES, D_CONV)[:CONV_WIDTH, 0]
    d_cdw = jnp.transpose(d_cdw.reshape(CONV_WIDTH, N_DEV, D_CONV // N_DEV), (1, 0, 2)).reshape(N_DEV, -1)
    d_cdw = jnp.pad(d_cdw, ((0, 0), (0, CDW_ROWS * D_MODEL - d_cdw.shape[1]))).reshape(N_DEV, CDW_ROWS, D_MODEL)
    rep = jnp.concatenate([
        d_ffn1_norm[0:1], d_mix_norm[0:1], d_ffn2_norm[0:1], d_final_norm[0:1],
        jnp.concatenate([d_cb[0:1], d_lg[0:1]], axis=1), jnp.concatenate([d_lb[0:1], d_ps[0:1]], axis=1),
        jnp.zeros((2, D_MODEL), F32), d_poolw.reshape(-1, D_MODEL)], axis=0)
    rep = jnp.pad(rep, ((0, N_DEV * REP_ROWS - rep.shape[0]), (0, 0))).reshape(N_DEV, REP_ROWS, D_MODEL)
    s_small = jnp.concatenate([chunks(d_win), chunks(d_wout), chunks(d_pw), d_cdw, rep], axis=1)

    s_g1, r = _tn_chunked(dg1, n1, "ffn1_dgate", _SiblingExchange(s_small))
    s_g1 = chunks(s_g1)
    to_chips = after_sibling("small", s_small, r)
    s_u1, rc, r = _tn_chunked(du1, n1, "ffn1_dup", _Together(to_chips, _SiblingExchange(s_g1)))
    s_u1 = chunks(s_u1)
    mine_small = _sum_partials(pending.pop("small"), rc, "rs_sum_small")
    to_chips = after_sibling("g1", s_g1, r)
    o_rep, _ = s_offs["rep"]
    loss_rows = jnp.pad(loss_part, ((0, 0), (0, D_MODEL - loss_part.shape[1])))
    share = _Gather(jnp.concatenate([mine_small[o_rep:o_rep + REP_ROWS], loss_rows], axis=0))
    s_d1, rc, r, shared = _tn_chunked(h1, df1, "ffn1_ddown", _Together(to_chips, _SiblingExchange(s_u1), share))
    s_d1 = chunks(s_d1)
    after_chips("g1", rc)
    to_chips = after_sibling("u1", s_u1, r)
    rc, r = _run_comm(_Together(to_chips, _SiblingExchange(s_d1)), "rs_tail_up")
    after_chips("u1", rc)
    rc, = _run_comm(after_sibling("d1", s_d1, r), "rs_tail_down")
    after_chips("d1", rc)

    rep_all = shared[:, :REP_ROWS].reshape(N_DEV * REP_ROWS, D_MODEL)
    loss = jnp.sum(shared[:, REP_ROWS, 0])

    def small_rows(name):
        o, n = s_offs[name]
        return mine_small[o:o + n]

    g = {
        "ffn1_norm": rep_all[0], "mix_norm": rep_all[1], "ffn2_norm": rep_all[2], "final_norm": rep_all[3],
        "conv_dw_b": rep_all[4, :D_CONV], "conv_ln_g": rep_all[4, D_CONV:],
        "conv_ln_b": rep_all[5, :D_CONV], "pool_scale": rep_all[5, D_CONV:],
        "pool_w": rep_all[8:8 + pool_w.size // D_MODEL].reshape(pool_w.shape),
        "w_out": small_rows("wout"), "conv_pw": small_rows("pw").reshape(conv_pw.shape),
        "conv_dw": small_rows("cdw").reshape(-1)[:conv_dw.size].reshape(conv_dw.shape),
    }

    slab_of = {"ffn1_w_gate": "g1", "ffn1_w_up": "u1", "ffn1_w_down": "d1",
               "ffn2_w_gate": "g2", "ffn2_w_up": "u2", "ffn2_w_down": "d2"}
    transposed = ("ffn1_w_gate", "ffn1_w_up", "ffn2_w_gate", "ffn2_w_up", "w_in")
    delta, new_m, new_v = {}, {}, {}
    for n in WEIGHTS:
        wmv = [given[k] for k in (n, "m_" + n, "v_" + n)]
        if n in transposed:
            wmv = [a.T for a in wmv]
        if n in slab_of:
            outs = _adamw_reduced(wmv[0], *reduced[slab_of[n]], wmv[1], wmv[2], "adamw_" + n)
        else:
            gn = small_rows("win") if n == "w_in" else g[n]
            outs = [gn, *_adamw(_as_2d(wmv[0]), _as_2d(gn), _as_2d(wmv[1]), _as_2d(wmv[2]), "adamw_" + n)]
        outs = [a.T if n in transposed else a.reshape(p[n].shape) for a in outs]
        g[n], delta[n], new_m[n], new_v[n] = outs

    return (loss, dx0[None], *[g[n] for n in WEIGHTS], *[delta[n] for n in WEIGHTS],
            *[new_m[n] for n in WEIGHTS], *[new_v[n] for n in WEIGHTS])
```

```python
import functools

import jax
import jax.numpy as jnp
from jax import lax
from jax.experimental import pallas as pl
from jax.experimental.pallas import tpu as pltpu

F32 = jnp.float32
BF16 = jnp.bfloat16

D_MODEL = 1024
D_CONV = 512
D_POOL = 512
D_IN = 2 * D_CONV + D_POOL
POOL_WINDOWS = (2, 4, 8, 16)
POOL_GROUP = D_POOL // len(POOL_WINDOWS)
CONV_WIDTH = 31
RMS_EPS = 1e-6
LN_EPS = 1e-5
FFN_RES_WEIGHT = 0.5

ADAM_LR = 0.001
ADAM_B1 = 0.9
ADAM_B2 = 0.999
ADAM_EPS = 1e-08
ADAM_WD = 0.01
ADAM_STEP = 10

N_DEV = 8
MESH_ID = pl.DeviceIdType.MESH

SUBLANES = 8
TOKEN_TILE = 512
FFN_BWD_TILE = 256
FF_CHUNK = 256
HALO = 32
V7X_VMEM_LIMIT = 56 * 1024 * 1024
CDW_ROWS = 16
REP_ROWS = 16

WEIGHTS = ("ffn1_norm", "ffn1_w_gate", "ffn1_w_up", "ffn1_w_down", "mix_norm", "w_in", "conv_dw", "conv_dw_b",
           "conv_ln_g", "conv_ln_b", "conv_pw", "pool_w", "pool_scale", "w_out", "ffn2_norm", "ffn2_w_gate",
           "ffn2_w_up", "ffn2_w_down", "final_norm")


def _dot_nn(a, b):
    return lax.dot_general(a, b, (((1,), (0,)), ((), ())), preferred_element_type=F32)


def _dot_nt(a, b):
    return lax.dot_general(a, b, (((1,), (1,)), ((), ())), preferred_element_type=F32)


def _dot_tn(a, b):
    return lax.dot_general(a, b, (((0,), (0,)), ((), ())), preferred_element_type=F32)


def _rowsum8(v):
    r, c = v.shape
    return jnp.sum(v.reshape(r // SUBLANES, SUBLANES, c), axis=0)


def _fold8(ref):
    ref[0:1, :] = jnp.sum(ref[...], axis=0, keepdims=True)


def _row_tile(n, cap, mult):
    best = None
    for t in range(mult, min(n, cap) + 1, mult):
        if n % t == 0:
            best = t
    return n if best is None else best


def _params(n_grid):
    return pltpu.CompilerParams(dimension_semantics=("arbitrary",) * n_grid, vmem_limit_bytes=V7X_VMEM_LIMIT)


def _full(shape):
    return pl.BlockSpec(shape, lambda *_: (0,) * len(shape))


def _layout(pieces):
    offs, r = {}, 0
    for name, rows in pieces:
        offs[name] = (r, rows)
        r += rows
    return offs, r


HBM = pl.BlockSpec(memory_space=pl.ANY)


def _mesh_pos():
    return lax.axis_index("x"), lax.axis_index("y"), lax.axis_index("c")


def _remote(src, dst, send_sems, recv_sems, k, to):
    return pltpu.make_async_remote_copy(src_ref=src, dst_ref=dst, send_sem=send_sems.at[k], recv_sem=recv_sems.at[k],
                                        device_id=to, device_id_type=MESH_ID)


class _Gather:
    def __init__(self, shard):
        self.inputs = (shard,)
        self.out_shape = (jax.ShapeDtypeStruct((N_DEV, *shard.shape), shard.dtype),)
        self.scratch = (pltpu.SemaphoreType.DMA((7,)), pltpu.SemaphoreType.DMA((7,)), pltpu.SemaphoreType.DMA)

    def phases(self, ins, outs, scr):
        (x_ref,), (out_ref,), (send_sems, recv_sems, local_sem) = ins, outs, scr
        x, y, c = _mesh_pos()
        me, sibling = (x, y, c), (x, y, 1 - c)
        chips = [(1 - x, y), (x, 1 - y), (1 - x, 1 - y)]

        def block(px, py, pc):
            return out_ref.at[4 * px + 2 * py + pc]

        def copy(k, blk, to, src=None):
            return _remote(block(*blk) if src is None else src, block(*blk), send_sems, recv_sems, k, to)

        def mine():
            return pltpu.make_async_copy(x_ref, block(*me), local_sem)

        def first():
            return [copy(0, me, sibling, src=x_ref)] + [copy(1 + j, me, (*chip, c), src=x_ref) for j, chip in enumerate(chips)]

        def passed(j):
            return copy(4 + j, (*chips[j], c), sibling)

        def start():
            mine().start()
            for cp in first():
                cp.start()

        def forward():
            for j, chip in enumerate(chips):
                copy(1 + j, (*chip, c), me).wait_recv()
                passed(j).start()

        def finish():
            copy(0, sibling, me).wait_recv()
            for j, chip in enumerate(chips):
                copy(4 + j, (*chip, 1 - c), me).wait_recv()
            for cp in first() + [passed(j) for j in range(3)]:
                cp.wait_send()
            mine().wait()

        return [start, forward, finish]


class _SiblingExchange:
    def __init__(self, src):
        self.inputs = (src,)
        self.out_shape = (jax.ShapeDtypeStruct((4, *src.shape[1:]), src.dtype),)
        self.scratch = (pltpu.SemaphoreType.DMA((4,)), pltpu.SemaphoreType.DMA((4,)))

    def phases(self, ins, outs, scr):
        (g_ref,), (recv_ref,), (send_sems, recv_sems) = ins, outs, scr
        x, y, c = _mesh_pos()

        def copies():
            return [_remote(g_ref.at[2 * k + (1 - c)], recv_ref.at[k], send_sems, recv_sems, k, (x, y, 1 - c))
                    for k in range(4)]

        def start():
            for cp in copies():
                cp.start()

        def finish():
            for cp in copies():
                cp.wait()

        return [start, finish]


class _ChipsExchange:
    def __init__(self, src):
        self.inputs = (src,)
        self.out_shape = (jax.ShapeDtypeStruct(src.shape, src.dtype),)
        self.scratch = (pltpu.SemaphoreType.DMA((3,)), pltpu.SemaphoreType.DMA((3,)))

    def phases(self, ins, outs, scr):
        (p_ref,), (recv_ref,), (send_sems, recv_sems) = ins, outs, scr
        x, y, c = _mesh_pos()
        peers = [(1 - x, y, c), (x, 1 - y, c), (1 - x, 1 - y, c)]

        def copies():
            return [_remote(p_ref.at[k], recv_ref.at[k], send_sems, recv_sems, k, peer) for k, peer in enumerate(peers)]

        def start():
            for cp in copies():
                cp.start()

        def finish():
            for cp in copies():
                cp.wait()

        return [start, finish]


class _Together:
    def __init__(self, *plans):
        self.plans = plans
        self.inputs = tuple(a for p in plans for a in p.inputs)
        self.out_shape = tuple(o for p in plans for o in p.out_shape)
        self.scratch = tuple(s for p in plans for s in p.scratch)

    def phases(self, ins, outs, scr):
        each = []
        for p in self.plans:
            n_in, n_out, n_scr = len(p.inputs), len(p.out_shape), len(p.scratch)
            each.append(p.phases(ins[:n_in], outs[:n_out], scr[:n_scr]))
            ins, outs, scr = ins[n_in:], outs[n_out:], scr[n_scr:]

        def run(fns):
            def phase():
                for fn in fns:
                    fn()
            return phase

        middle = [fn for ph in each for fn in ph[1:-1]]
        return [run([ph[0] for ph in each]), *([run(middle)] if middle else []), run([ph[-1] for ph in each])]


def _run_comm(plan, name):
    n_in, n_out = len(plan.inputs), len(plan.out_shape)

    def body(*refs):
        for phase in plan.phases(refs[:n_in], refs[n_in:n_in + n_out], refs[n_in + n_out:]):
            phase()

    return pl.pallas_call(
        body, name=name, out_shape=list(plan.out_shape), in_specs=[HBM] * n_in, out_specs=[HBM] * n_out,
        scratch_shapes=list(plan.scratch))(*plan.inputs)


def _grid_call(body, *, name, nt, in_specs, out_specs, out_shape, scratch_shapes, args, plan=None):
    if plan is None:
        return pl.pallas_call(body, name=name, grid=(nt,), in_specs=in_specs, out_specs=out_specs, out_shape=out_shape,
                              scratch_shapes=scratch_shapes, compiler_params=_params(1))(*args)
    n_in, n_out, n_scr = len(in_specs), len(out_specs), len(scratch_shapes)
    p_in, p_out = len(plan.inputs), len(plan.out_shape)

    def with_plan(*refs):
        ins, refs = refs[:n_in], refs[n_in:]
        p_ins, refs = refs[:p_in], refs[p_in:]
        outs, refs = refs[:n_out], refs[n_out:]
        p_outs, refs = refs[:p_out], refs[p_out:]
        scr, p_scr = refs[:n_scr], refs[n_scr:]
        phases = plan.phases(p_ins, p_outs, p_scr)
        i = pl.program_id(0)
        pl.when(i == 0)(phases[0])
        for phase in phases[1:-1]:
            pl.when(i == min(max(nt - 3, 1), nt - 1))(phase)
        body(*ins, *outs, *scr)
        pl.when(i == nt - 1)(phases[-1])

    return pl.pallas_call(
        with_plan, name=name, grid=(nt,), in_specs=[*in_specs, *[HBM] * p_in], out_specs=[*out_specs, *[HBM] * p_out],
        out_shape=[*out_shape, *plan.out_shape], scratch_shapes=[*scratch_shapes, *plan.scratch],
        compiler_params=_params(1))(*args, *plan.inputs)


def _add_chunks(gslab, recv, gid, rid, name):
    _, rows, cols = gslab.shape

    def body(gid_ref, rid_ref, a_ref, b_ref, own_ref, part_ref):
        s = a_ref[...] + b_ref[...]

        @pl.when(pl.program_id(0) == 0)
        def _():
            own_ref[...] = s

        part_ref[...] = s.astype(BF16)

    blk = (1, rows, cols)
    grid_spec = pltpu.PrefetchScalarGridSpec(
        num_scalar_prefetch=2, grid=(4,),
        in_specs=[pl.BlockSpec(blk, lambda k, g, r: (g[k], 0, 0)), pl.BlockSpec(blk, lambda k, g, r: (r[k], 0, 0))],
        out_specs=[pl.BlockSpec(blk, lambda k, g, r: (0, 0, 0)),
                   pl.BlockSpec(blk, lambda k, g, r: (jnp.maximum(k - 1, 0), 0, 0))])
    return pl.pallas_call(
        body, name=name, grid_spec=grid_spec,
        out_shape=[jax.ShapeDtypeStruct(blk, F32), jax.ShapeDtypeStruct((3, rows, cols), BF16)],
        compiler_params=_params(1),
    )(gid, rid, gslab, recv)


def _sum_partials(own, recv, name):
    _, rows, cols = own.shape
    tr = _row_tile(rows, 1024, 16)

    def body(o_ref, r_ref, out_ref):
        acc = o_ref[0]
        for k in range(3):
            acc = acc + r_ref[k].astype(F32)
        out_ref[...] = acc

    return pl.pallas_call(
        body, name=name, grid=(rows // tr,),
        in_specs=[pl.BlockSpec((1, tr, cols), lambda i: (0, i, 0)), pl.BlockSpec((3, tr, cols), lambda i: (0, i, 0))],
        out_specs=pl.BlockSpec((tr, cols), lambda i: (i, 0)),
        out_shape=jax.ShapeDtypeStruct((rows, cols), F32),
        compiler_params=_params(1),
    )(own, recv)


def _chunk_ids():
    x, y, c = _mesh_pos()
    chips = [(x, y), (1 - x, y), (x, 1 - y), (1 - x, 1 - y)]
    gid = jnp.stack([4 * px + 2 * py + c for px, py in chips]).astype(jnp.int32)
    rid = jnp.stack([2 * px + py for px, py in chips]).astype(jnp.int32)
    return gid, rid


def _sibling_sums(slab, recv, tag):
    gid, rid = _chunk_ids()
    return _add_chunks(slab, recv, gid, rid, "rs_add_" + tag)


def _load_weights(slab_ref, offs, dsts, sems):
    cps = []
    for i, (off, dst) in enumerate(zip(offs, dsts)):
        f8 = dst.shape[0] // N_DEV
        cps += [pltpu.make_async_copy(slab_ref.at[j, pl.ds(off, f8), :], dst.at[pl.ds(j * f8, f8), :],
                                      sems.at[i * N_DEV + j]) for j in range(N_DEV)]
    for cp in cps:
        cp.start()
    for cp in cps:
        cp.wait()


def _chunk_rows(c):
    return pl.ds(pl.multiple_of(c * FF_CHUNK, FF_CHUNK), FF_CHUNK)


def _rms(xv):
    return lax.rsqrt(jnp.mean(xv * xv, axis=-1, keepdims=True) + RMS_EPS)


def _loss_terms(xv, tgt, gain):
    r = _rms(xv)
    xh = xv * r
    err = xh * gain - tgt
    loss = 0.5 * jnp.sum(jnp.mean(err * err, axis=-1, keepdims=True))
    dy = err * (1.0 / xv.shape[-1])
    dxh = dy * gain
    return loss, r * (dxh - xh * jnp.mean(dxh * xh, axis=-1, keepdims=True)), _rowsum8(dy * xh)


def _ffn_fwd(x, gain, slab, offs, f, name, plan=None, head=None):
    t, d = x.shape
    nc, tm = f // FF_CHUNK, TOKEN_TILE
    nt = t // tm
    n_head = 0 if head is None else 2

    def body(*refs):
        x_ref, gain_ref, slab_ref = refs[:3]
        xo_ref, g_ref, u_ref, n_ref = refs[3 + n_head:7 + n_head]
        wg_v, wu_v, wd_v, acc_ref, sems = refs[7 + 2 * n_head:]
        i = pl.program_id(0)

        @pl.when(i == 0)
        def _():
            _load_weights(slab_ref, offs, (wg_v, wu_v, wd_v), sems)

        xv = x_ref[...]
        n_ref[...] = ((xv * _rms(xv)) * gain_ref[...]).astype(BF16)
        acc_ref[...] = jnp.zeros_like(acc_ref)

        def chunk(c, carry):
            rows = _chunk_rows(c)
            nb = n_ref[...]
            g = _dot_nt(nb, wg_v[rows, :])
            u = _dot_nt(nb, wu_v[rows, :])
            g_ref[c] = g.astype(BF16)
            u_ref[c] = u.astype(BF16)
            h = (g * jax.nn.sigmoid(g)) * u
            acc_ref[...] += _dot_nn(h.astype(BF16), wd_v[rows, :])
            return carry

        lax.fori_loop(0, nc, chunk, 0, unroll=True)
        out = xv + FFN_RES_WEIGHT * acc_ref[...]
        if head is None:
            xo_ref[...] = out
            return
        tgt_ref, fgain_ref = refs[3:5]
        dgain_ref, loss_ref = refs[7 + n_head:7 + 2 * n_head]

        @pl.when(i == 0)
        def _():
            dgain_ref[...] = jnp.zeros_like(dgain_ref)
            loss_ref[...] = jnp.zeros_like(loss_ref)

        loss, dx, dgain = _loss_terms(out, tgt_ref[...], fgain_ref[...])
        xo_ref[...] = dx
        loss_ref[...] += loss
        dgain_ref[...] += dgain

        @pl.when(i == nt - 1)
        def _():
            _fold8(dgain_ref)

    tile = pl.BlockSpec((tm, d), lambda i: (i, 0))
    act = pl.BlockSpec((nc, tm, FF_CHUNK), lambda i: (0, i, 0))
    head_in = [] if head is None else [tile, _full((1, d))]
    head_out = [] if head is None else [_full((SUBLANES, d)), _full((SUBLANES, 128))]
    head_shape = [] if head is None else [jax.ShapeDtypeStruct((SUBLANES, d), F32), jax.ShapeDtypeStruct((SUBLANES, 128), F32)]
    return _grid_call(
        body, name=name, nt=nt, plan=plan,
        in_specs=[tile, _full((1, d)), HBM, *head_in],
        out_specs=[tile, act, act, tile, *head_out],
        out_shape=[jax.ShapeDtypeStruct((t, d), F32), jax.ShapeDtypeStruct((nc, t, FF_CHUNK), BF16),
                   jax.ShapeDtypeStruct((nc, t, FF_CHUNK), BF16), jax.ShapeDtypeStruct((t, d), BF16), *head_shape],
        scratch_shapes=[pltpu.VMEM((f, d), BF16), pltpu.VMEM((f, d), BF16), pltpu.VMEM((f, d), BF16),
                        pltpu.VMEM((tm, d), F32), pltpu.SemaphoreType.DMA((3 * N_DEV,))],
        args=(x, gain, slab, *([] if head is None else head)))


def _ffn_bwd(dxo, x, gain, gs, us, slab, offs, f, name, plan=None):
    t, d = x.shape
    nc, tm = f // FF_CHUNK, FFN_BWD_TILE
    nt = t // tm

    def body(dxo_ref, x_ref, gain_ref, g_ref, u_ref, slab_ref,
             dx_ref, h_ref, dg_ref, du_ref, df_ref, dgain_ref, wg_v, wu_v, wd_v, sems):
        i = pl.program_id(0)

        @pl.when(i == 0)
        def _():
            _load_weights(slab_ref, offs, (wg_v, wu_v, wd_v), sems)
            dgain_ref[...] = jnp.zeros_like(dgain_ref)

        df_ref[...] = (FFN_RES_WEIGHT * dxo_ref[...]).astype(BF16)

        def chunk(c, carry):
            rows = _chunk_rows(c)
            g = g_ref[c].astype(F32)
            u = u_ref[c].astype(F32)
            sg = jax.nn.sigmoid(g)
            sil = g * sg
            dh = _dot_nt(df_ref[...], wd_v[rows, :])
            h_ref[c] = (sil * u).astype(BF16)
            du_ref[c] = (dh * sil).astype(BF16)
            dg_ref[c] = (dh * u * (sg * (1.0 + g * (1.0 - sg)))).astype(BF16)
            return carry

        lax.fori_loop(0, nc, chunk, 0, unroll=True)

        def back(c, dn):
            rows = _chunk_rows(c)
            return dn + _dot_nn(dg_ref[c], wg_v[rows, :]) + _dot_nn(du_ref[c], wu_v[rows, :])

        dn = lax.fori_loop(0, nc, back, jnp.zeros((tm, d), F32), unroll=True)
        xv = x_ref[...]
        r = _rms(xv)
        xh = xv * r
        dgain_ref[...] += _rowsum8(dn * xh)
        dxh = dn * gain_ref[...]
        dx_ref[...] = dxo_ref[...] + r * (dxh - xh * jnp.mean(dxh * xh, axis=-1, keepdims=True))

        @pl.when(i == nt - 1)
        def _():
            _fold8(dgain_ref)

    tile = pl.BlockSpec((tm, d), lambda i: (i, 0))
    act = pl.BlockSpec((nc, tm, FF_CHUNK), lambda i: (0, i, 0))
    act_shape = jax.ShapeDtypeStruct((nc, t, FF_CHUNK), BF16)
    return _grid_call(
        body, name=name, nt=nt, plan=plan,
        in_specs=[tile, tile, _full((1, d)), act, act, HBM],
        out_specs=[tile, act, act, act, tile, _full((SUBLANES, d))],
        out_shape=[jax.ShapeDtypeStruct((t, d), F32), act_shape, act_shape, act_shape,
                   jax.ShapeDtypeStruct((t, d), BF16), jax.ShapeDtypeStruct((SUBLANES, d), F32)],
        scratch_shapes=[pltpu.VMEM((f, d), BF16), pltpu.VMEM((f, d), BF16), pltpu.VMEM((f, d), BF16),
                        pltpu.SemaphoreType.DMA((3 * N_DEV,))],
        args=(dxo, x, gain, gs, us, slab))


def _tn_chunked(a, b, name, plan=None):
    nc, t, _ = a.shape
    n = b.shape[1]
    tb = _row_tile(t, 1024, TOKEN_TILE)

    def body(a_ref, b_ref, o_ref):
        @pl.when(pl.program_id(0) == 0)
        def _():
            o_ref[...] = jnp.zeros_like(o_ref)

        def chunk(c, carry):
            rows = _chunk_rows(c)
            o_ref[rows, :] += _dot_tn(a_ref[c], b_ref[...])
            return carry

        lax.fori_loop(0, nc, chunk, 0, unroll=True)

    return _grid_call(
        body, name=name, nt=t // tb, plan=plan,
        in_specs=[pl.BlockSpec((nc, tb, FF_CHUNK), lambda i: (0, i, 0)), pl.BlockSpec((tb, n), lambda i: (i, 0))],
        out_specs=[_full((nc * FF_CHUNK, n))],
        out_shape=[jax.ShapeDtypeStruct((nc * FF_CHUNK, n), F32)],
        scratch_shapes=[], args=(a, b))


def _tn(a, b, name, plan=None):
    t, k = a.shape
    n = b.shape[1]
    tb = _row_tile(t, 1024, TOKEN_TILE)

    def body(a_ref, b_ref, o_ref):
        @pl.when(pl.program_id(0) == 0)
        def _():
            o_ref[...] = jnp.zeros_like(o_ref)

        o_ref[...] += _dot_tn(a_ref[...].astype(BF16), b_ref[...].astype(BF16))

    return _grid_call(
        body, name=name, nt=t // tb, plan=plan,
        in_specs=[pl.BlockSpec((tb, k), lambda i: (i, 0)), pl.BlockSpec((tb, n), lambda i: (i, 0))],
        out_specs=[_full((k, n))],
        out_shape=[jax.ShapeDtypeStruct((k, n), F32)],
        scratch_shapes=[], args=(a, b))


def _layernorm_stats(u1):
    mu = jnp.mean(u1, axis=-1, keepdims=True)
    xc = u1 - mu
    rstd = lax.rsqrt(jnp.mean(xc * xc, axis=-1, keepdims=True) + LN_EPS)
    return xc * rstd, rstd


def _positions(tile_index, tm):
    return (tile_index * tm + lax.broadcasted_iota(jnp.int32, (tm, 1), 0)).astype(F32)


def _shifted_taps(src_ref, sh_ref, tm, offset_of):
    groups = {}
    for k in range(CONV_WIDTH):
        groups.setdefault(offset_of(k) % SUBLANES, []).append(k)
    span = tm + HALO - SUBLANES
    for rem, taps in sorted(groups.items()):
        if rem:
            sh_ref[0:span, :] = src_ref[rem:rem + span, :]
        ref = sh_ref if rem else src_ref
        for k in taps:
            base = offset_of(k) - rem
            yield k, ref[base:base + tm, :]


def _mix_fwd(x, gm, win_t, cdw, cb, lg, lb, pw, poolw, ps, wout):
    t, d = x.shape
    tm = TOKEN_TILE

    def body(x_ref, gm_ref, win_ref, cdw_ref, cb_ref, lg_ref, lb_ref, pw_ref, poolw_ref, ps_ref, wout_ref,
             xo_ref, h_ref, ag_ref, u0_ref, u1_ref, u2_ref, mixed_ref, cat_ref, eu_ref, ep_ref, sh_ref):
        i = pl.program_id(0)

        @pl.when(i == 0)
        def _():
            eu_ref[0:HALO, :] = jnp.zeros((HALO, D_CONV), F32)
            ep_ref[0:HALO, :] = jnp.zeros((HALO, D_POOL), F32)

        @pl.when(i > 0)
        def _():
            eu_ref[0:HALO, :] = eu_ref[tm:tm + HALO, :]
            ep_ref[0:HALO, :] = ep_ref[tm:tm + HALO, :]

        xv = x_ref[...]
        hb = ((xv * _rms(xv)) * gm_ref[...]).astype(BF16)
        h_ref[...] = hb
        proj = _dot_nt(hb, win_ref[...])
        a = proj[:, :D_CONV]
        g = proj[:, D_CONV:2 * D_CONV]
        ag_ref[...] = proj[:, :2 * D_CONV]
        u0 = a * jax.nn.sigmoid(g)
        u0_ref[...] = u0
        eu_ref[HALO:HALO + tm, :] = u0
        ep_ref[HALO:HALO + tm, :] = proj[:, 2 * D_CONV:]

        u1 = jnp.broadcast_to(cb_ref[...], (tm, D_CONV))
        for k, rows in _shifted_taps(eu_ref, sh_ref, tm, lambda k: HALO - (CONV_WIDTH - 1) + k):
            u1 = u1 + cdw_ref[k:k + 1, :] * rows
        u1_ref[...] = u1
        lnh, _ = _layernorm_stats(u1)
        ln = lnh * lg_ref[...] + lb_ref[...]
        u2 = (ln * jax.nn.sigmoid(ln)).astype(BF16)
        u2_ref[...] = u2
        conv_out = _dot_nn(u2, pw_ref[...])

        pos = _positions(i, tm)
        outs = []
        for gi, w in enumerate(POOL_WINDOWS):
            lo = gi * POOL_GROUP
            p = ep_ref[HALO:HALO + tm, lo:lo + POOL_GROUP]
            s = p
            for j in range(1, w):
                s = s + ep_ref[HALO - j:HALO - j + tm, lo:lo + POOL_GROUP]
            mixed = (s / jnp.minimum(pos + 1.0, float(w)) - p).astype(BF16)
            mixed_ref[:, lo:lo + POOL_GROUP] = mixed
            outs.append(_dot_nn(mixed, poolw_ref[gi]))
        pool_out = jnp.concatenate(outs, axis=-1) * ps_ref[...]
        cat = jnp.concatenate([conv_out, pool_out], axis=-1).astype(BF16)
        cat_ref[...] = cat
        xo_ref[...] = xv + _dot_nn(cat, wout_ref[...])

    def tile(c):
        return pl.BlockSpec((tm, c), lambda i: (i, 0))

    def out(c, dt):
        return jax.ShapeDtypeStruct((t, c), dt)

    return pl.pallas_call(
        body, name="mix_fwd", grid=(t // tm,),
        in_specs=[tile(d), _full((1, d)), _full((D_IN, d)), _full((HALO, D_CONV)), _full((1, D_CONV)),
                  _full((1, D_CONV)), _full((1, D_CONV)), _full((D_CONV, D_CONV)),
                  _full((len(POOL_WINDOWS), POOL_GROUP, POOL_GROUP)), _full((1, D_POOL)), _full((d, d))],
        out_specs=[tile(d), tile(d), tile(2 * D_CONV), tile(D_CONV), tile(D_CONV), tile(D_CONV), tile(D_POOL), tile(d)],
        out_shape=[out(d, F32), out(d, BF16), out(2 * D_CONV, F32), out(D_CONV, F32), out(D_CONV, F32),
                   out(D_CONV, BF16), out(D_POOL, BF16), out(d, BF16)],
        scratch_shapes=[pltpu.VMEM((HALO + tm, D_CONV), F32), pltpu.VMEM((HALO + tm, D_POOL), F32),
                        pltpu.VMEM((HALO + tm, D_CONV), F32)],
        compiler_params=_params(1),
    )(x, gm, win_t, cdw, cb, lg, lb, pw, poolw, ps, wout)


def _mix_bwd(dxo, x, gm, ag, u0, u1, mixed, win_t, cdw, lg, lb, pw, poolw, ps, wout, plan=None):
    t, d = x.shape
    tm = TOKEN_TILE
    nt = t // tm
    halo_blocks = tm // HALO

    def body(dxo_ref, x_ref, gm_ref, ag_ref, u0_ref, u0h_ref, u1_ref, mixed_ref,
             win_ref, cdw_ref, lg_ref, lb_ref, pw_ref, poolw_ref, ps_ref, wout_ref,
             dx_ref, dproj_ref, dco_ref, dgm_ref, dcdw_ref, dcb_ref, dlg_ref, dlb_ref, dpoolw_ref, dps_ref,
             eu_ref, ed_ref, eq_ref, sh_ref):
        i = pl.program_id(0)
        ti = nt - 1 - i

        @pl.when(i == 0)
        def _():
            for ref in (dgm_ref, dcdw_ref, dcb_ref, dlg_ref, dlb_ref, dpoolw_ref, dps_ref):
                ref[...] = jnp.zeros_like(ref)
            ed_ref[tm:tm + HALO, :] = jnp.zeros((HALO, D_CONV), F32)
            eq_ref[tm:tm + HALO, :] = jnp.zeros((HALO, D_POOL), F32)

        @pl.when(i > 0)
        def _():
            ed_ref[tm:tm + HALO, :] = ed_ref[0:HALO, :]
            eq_ref[tm:tm + HALO, :] = eq_ref[0:HALO, :]

        @pl.when(ti == 0)
        def _():
            eu_ref[0:HALO, :] = jnp.zeros((HALO, D_CONV), F32)

        @pl.when(ti > 0)
        def _():
            eu_ref[0:HALO, :] = u0h_ref[...]

        eu_ref[HALO:HALO + tm, :] = u0_ref[...]

        dxo = dxo_ref[...]
        dcat = _dot_nt(dxo.astype(BF16), wout_ref[...])
        dco = dcat[:, :D_CONV].astype(BF16)
        dco_ref[...] = dco
        dpo = dcat[:, D_CONV:]

        lnh, rstd = _layernorm_stats(u1_ref[...])
        ln = lnh * lg_ref[...] + lb_ref[...]
        sl = jax.nn.sigmoid(ln)
        dln = _dot_nt(dco, pw_ref[...]) * (sl * (1.0 + ln * (1.0 - sl)))
        dlg_ref[...] += _rowsum8(dln * lnh)
        dlb_ref[...] += _rowsum8(dln)
        dlnh = dln * lg_ref[...]
        du1 = rstd * (dlnh - jnp.mean(dlnh, axis=-1, keepdims=True)
                      - lnh * jnp.mean(dlnh * lnh, axis=-1, keepdims=True))
        dcb_ref[...] += _rowsum8(du1)
        ed_ref[0:tm, :] = du1

        du0 = jnp.zeros((tm, D_CONV), F32)
        for k, rows in _shifted_taps(ed_ref, sh_ref, tm, lambda k: CONV_WIDTH - 1 - k):
            du0 = du0 + cdw_ref[k:k + 1, :] * rows
        for k, rows in _shifted_taps(eu_ref, sh_ref, tm, lambda k: HALO - (CONV_WIDTH - 1) + k):
            dcdw_ref[SUBLANES * k:SUBLANES * (k + 1), :] += _rowsum8(du1 * rows)
        a = ag_ref[:, :D_CONV]
        sg = jax.nn.sigmoid(ag_ref[:, D_CONV:])
        pieces = [du0 * sg, du0 * a * (sg * (1.0 - sg))]

        pos = _positions(ti, tm)
        for gi, w in enumerate(POOL_WINDOWS):
            lo = gi * POOL_GROUP
            mg = mixed_ref[:, lo:lo + POOL_GROUP]
            dpo_g = dpo[:, lo:lo + POOL_GROUP]
            dps_ref[:, lo:lo + POOL_GROUP] += _rowsum8(dpo_g * _dot_nn(mg, poolw_ref[gi]))
            dout = (dpo_g * ps_ref[:, lo:lo + POOL_GROUP]).astype(BF16)
            dpoolw_ref[gi] += _dot_tn(mg, dout)
            dmx = _dot_nt(dout, poolw_ref[gi])
            q = dmx / jnp.minimum(pos + 1.0, float(w))
            eq_ref[0:tm, lo:lo + POOL_GROUP] = q
            s = q
            for j in range(1, w):
                s = s + eq_ref[j:j + tm, lo:lo + POOL_GROUP]
            pieces.append(s - dmx)
        dproj = jnp.concatenate(pieces, axis=-1).astype(BF16)
        dproj_ref[...] = dproj

        dh = _dot_nn(dproj, win_ref[...])
        xv = x_ref[...]
        r = _rms(xv)
        xh = xv * r
        dgm_ref[...] += _rowsum8(dh * xh)
        dxh = dh * gm_ref[...]
        dx_ref[...] = dxo + r * (dxh - xh * jnp.mean(dxh * xh, axis=-1, keepdims=True))

        @pl.when(i == nt - 1)
        def _():
            for ref in (dgm_ref, dcb_ref, dlg_ref, dlb_ref, dps_ref):
                _fold8(ref)
            for k in range(CONV_WIDTH):
                dcdw_ref[SUBLANES * k:SUBLANES * k + 1, :] = jnp.sum(
                    dcdw_ref[SUBLANES * k:SUBLANES * (k + 1), :], axis=0, keepdims=True)

    def tile(c):
        return pl.BlockSpec((tm, c), lambda i: (nt - 1 - i, 0))

    halo = pl.BlockSpec((HALO, D_CONV), lambda i: (jnp.maximum((nt - 1 - i) * halo_blocks - 1, 0), 0))
    n_groups = len(POOL_WINDOWS)
    return _grid_call(
        body, name="mix_bwd", nt=nt, plan=plan,
        in_specs=[tile(d), tile(d), _full((1, d)), tile(2 * D_CONV), tile(D_CONV), halo, tile(D_CONV), tile(D_POOL),
                  _full((D_IN, d)), _full((HALO, D_CONV)), _full((1, D_CONV)), _full((1, D_CONV)),
                  _full((D_CONV, D_CONV)), _full((n_groups, POOL_GROUP, POOL_GROUP)), _full((1, D_POOL)), _full((d, d))],
        out_specs=[tile(d), tile(D_IN), tile(D_CONV), _full((SUBLANES, d)), _full((HALO * SUBLANES, D_CONV)),
                   _full((SUBLANES, D_CONV)), _full((SUBLANES, D_CONV)), _full((SUBLANES, D_CONV)),
                   _full((n_groups, POOL_GROUP, POOL_GROUP)), _full((SUBLANES, D_POOL))],
        out_shape=[jax.ShapeDtypeStruct((t, d), F32), jax.ShapeDtypeStruct((t, D_IN), BF16),
                   jax.ShapeDtypeStruct((t, D_CONV), BF16), jax.ShapeDtypeStruct((SUBLANES, d), F32),
                   jax.ShapeDtypeStruct((HALO * SUBLANES, D_CONV), F32), jax.ShapeDtypeStruct((SUBLANES, D_CONV), F32),
                   jax.ShapeDtypeStruct((SUBLANES, D_CONV), F32), jax.ShapeDtypeStruct((SUBLANES, D_CONV), F32),
                   jax.ShapeDtypeStruct((n_groups, POOL_GROUP, POOL_GROUP), F32),
                   jax.ShapeDtypeStruct((SUBLANES, D_POOL), F32)],
        scratch_shapes=[pltpu.VMEM((HALO + tm, D_CONV), F32), pltpu.VMEM((tm + HALO, D_CONV), F32),
                        pltpu.VMEM((tm + HALO, D_POOL), F32), pltpu.VMEM((tm + HALO, D_CONV), F32)],
        args=(dxo, x, gm, ag, u0, u0, u1, mixed, win_t, cdw, lg, lb, pw, poolw, ps, wout))


def _adam_step(gv, w_ref, m_ref, v_ref, d_ref, nm_ref, nv_ref):
    nm = ADAM_B1 * m_ref[...] + (1.0 - ADAM_B1) * gv
    nv = ADAM_B2 * v_ref[...] + (1.0 - ADAM_B2) * (gv * gv)
    m_hat = nm / (1.0 - ADAM_B1 ** ADAM_STEP)
    v_hat = nv / (1.0 - ADAM_B2 ** ADAM_STEP)
    d_ref[...] = -ADAM_LR * (m_hat / (jnp.sqrt(v_hat) + ADAM_EPS) + ADAM_WD * w_ref[...])
    nm_ref[...] = nm
    nv_ref[...] = nv


def _adam_tile(rows, cols):
    return _row_tile(rows, max(SUBLANES, (256 * 1024) // cols // SUBLANES * SUBLANES), SUBLANES)


def _adamw_many(items, name):
    n = len(items)

    def body(*refs):
        ins, outs = refs[:4 * n], refs[4 * n:]
        for k in range(n):
            w_ref, g_ref, m_ref, v_ref = ins[4 * k:4 * k + 4]
            _adam_step(g_ref[...], w_ref, m_ref, v_ref, *outs[3 * k:3 * k + 3])

    return pl.pallas_call(
        body, name=name, out_shape=[jax.ShapeDtypeStruct(it[0].shape, F32) for it in items for _ in range(3)],
        compiler_params=_params(0))(*[a for it in items for a in it])


def _adamw_reduced(w, own, recv, m, v, name):
    rows, cols = w.shape
    tr = _adam_tile(rows, cols)

    def body(w_ref, o_ref, r_ref, m_ref, v_ref, g_ref, d_ref, nm_ref, nv_ref):
        gv = o_ref[0]
        for k in range(3):
            gv = gv + r_ref[k].astype(F32)
        g_ref[...] = gv
        _adam_step(gv, w_ref, m_ref, v_ref, d_ref, nm_ref, nv_ref)

    blk = pl.BlockSpec((tr, cols), lambda i: (i, 0))
    shape = jax.ShapeDtypeStruct((rows, cols), F32)
    return pl.pallas_call(
        body, name=name, grid=(rows // tr,),
        in_specs=[blk, pl.BlockSpec((1, tr, cols), lambda i: (0, i, 0)), pl.BlockSpec((3, tr, cols), lambda i: (0, i, 0)),
                  blk, blk],
        out_specs=[blk] * 4, out_shape=[shape] * 4,
        compiler_params=_params(1),
    )(w, own, recv, m, v)


def _as_2d(a):
    if a.ndim == 1:
        return a.reshape(a.shape[0] // 128, 128)
    if a.ndim == 3:
        return a.reshape(a.shape[0] * a.shape[1], a.shape[2])
    return a


def _pack_weight_slabs(p):
    def bf(parts):
        return [a.astype(BF16) for a in parts]

    cdw_bits = lax.bitcast_convert_type(p["conv_dw"], BF16).reshape(CONV_WIDTH, 2 * D_CONV // N_DEV)
    cdw_bits = jnp.pad(cdw_bits, ((0, 1), (0, 0))).reshape(4, D_MODEL)
    cdw_bits = jnp.pad(cdw_bits, ((0, CDW_ROWS - 4), (0, 0)))
    first = bf([p["ffn1_w_gate"].T, p["ffn1_w_up"].T, p["ffn1_w_down"]])
    rest = bf([p["ffn2_w_gate"].T, p["ffn2_w_up"].T, p["ffn2_w_down"], p["w_in"].T, p["w_out"],
               p["conv_pw"].reshape(D_CONV // N_DEV // 2, D_MODEL)]) + [cdw_bits]
    return jnp.concatenate(first, axis=0), jnp.concatenate(rest, axis=0)


def _unpack_rows(slab, offs, names):
    out = {}
    for name in names:
        o, n = offs[name]
        out[name] = slab[:, o:o + n, :].reshape(N_DEV * n, D_MODEL)
    return out


def _unpack_conv_taps(slab, offs):
    o, _ = offs["cdw"]
    bits = slab[:, o:o + 4, :].reshape(N_DEV, CONV_WIDTH + 1, D_CONV // N_DEV, 2)[:, :CONV_WIDTH]
    cdw = lax.bitcast_convert_type(bits, F32)
    return jnp.transpose(cdw, (1, 0, 2)).reshape(CONV_WIDTH, D_CONV)


def kernel(x, ffn1_norm, ffn1_w_gate, ffn1_w_up, ffn1_w_down, mix_norm, w_in, conv_dw, conv_dw_b, conv_ln_g, conv_ln_b, conv_pw, pool_w, pool_scale, w_out, ffn2_norm, ffn2_w_gate, ffn2_w_up, ffn2_w_down, final_norm, loss_target, m_ffn1_norm, m_ffn1_w_gate, m_ffn1_w_up, m_ffn1_w_down, m_mix_norm, m_w_in, m_conv_dw, m_conv_dw_b, m_conv_ln_g, m_conv_ln_b, m_conv_pw, m_pool_w, m_pool_scale, m_w_out, m_ffn2_norm, m_ffn2_w_gate, m_ffn2_w_up, m_ffn2_w_down, m_final_norm, v_ffn1_norm, v_ffn1_w_gate, v_ffn1_w_up, v_ffn1_w_down, v_mix_norm, v_w_in, v_conv_dw, v_conv_dw_b, v_conv_ln_g, v_conv_ln_b, v_conv_pw, v_pool_w, v_pool_scale, v_w_out, v_ffn2_norm, v_ffn2_w_gate, v_ffn2_w_up, v_ffn2_w_down, v_final_norm):
    given = dict(locals())
    p = {n: given[n] for n in WEIGHTS}
    f8 = ffn1_w_gate.shape[1]
    f = N_DEV * f8
    ffn_rows = (0, f8, 2 * f8)
    small = (("win", D_IN // N_DEV), ("wout", D_MODEL // N_DEV), ("pw", D_CONV // N_DEV // 2), ("cdw", CDW_ROWS))
    w_offs, _ = _layout((("g2", f8), ("u2", f8), ("d2", f8)) + small)
    s_offs, _ = _layout(small + (("rep", REP_ROWS),))
    x0 = x[0]
    target = loss_target[0]

    def row(vec):
        return vec.reshape(1, vec.shape[0])

    slab_first, slab_rest = _pack_weight_slabs(p)
    w_first = _run_comm(_Gather(slab_first), "gather_ffn1")[0]
    x1, g1s, u1s, n1, w_rest = _ffn_fwd(x0, row(ffn1_norm), w_first, ffn_rows, f, "ffn1_fwd", _Gather(slab_rest))
    w = _unpack_rows(w_rest, w_offs, ("win", "wout", "pw"))
    w["pw"] = w["pw"].reshape(D_CONV, D_CONV)
    cdw = jnp.pad(_unpack_conv_taps(w_rest, w_offs), ((0, HALO - CONV_WIDTH), (0, 0)))
    poolw = pool_w.astype(BF16)

    x2, h, ag, u0, u1, u2, mixed, cat = _mix_fwd(
        x1, row(mix_norm), w["win"], cdw, row(conv_dw_b), row(conv_ln_g), row(conv_ln_b), w["pw"], poolw,
        row(pool_scale), w["wout"])
    dx3, g2s, u2s, n2, d_final_norm, loss_part = _ffn_fwd(
        x2, row(ffn2_norm), w_rest, ffn_rows, f, "ffn2_fwd", head=(target, row(final_norm)))

    pending, reduced = {}, {}

    def chunks(a):
        return a.reshape(N_DEV, -1, D_MODEL)

    def after_sibling(name, slab, recv):
        pending[name], part = _sibling_sums(slab, recv, name)
        return _ChipsExchange(part)

    def after_chips(name, recv):
        reduced[name] = (pending.pop(name), recv)

    dx2, h2, dg2, du2, df2, d_ffn2_norm = _ffn_bwd(dx3, x2, row(ffn2_norm), g2s, u2s, w_rest, ffn_rows, f, "ffn2_bwd")
    s_g2 = chunks(_tn_chunked(dg2, n2, "ffn2_dgate")[0])
    s_u2, r = _tn_chunked(du2, n2, "ffn2_dup", _SiblingExchange(s_g2))
    s_u2 = chunks(s_u2)
    to_chips = after_sibling("g2", s_g2, r)
    s_d2, rc, r = _tn_chunked(h2, df2, "ffn2_ddown", _Together(to_chips, _SiblingExchange(s_u2)))
    s_d2 = chunks(s_d2)
    after_chips("g2", rc)
    to_chips = after_sibling("u2", s_u2, r)
    dx1, dproj, dco, d_mix_norm, d_cdw, d_cb, d_lg, d_lb, d_poolw, d_ps, rc, r = _mix_bwd(
        dx2, x1, row(mix_norm), ag, u0, u1, mixed, w["win"], cdw, row(conv_ln_g), row(conv_ln_b), w["pw"], poolw,
        row(pool_scale), w["wout"], _Together(to_chips, _SiblingExchange(s_d2)))
    after_chips("u2", rc)
    to_chips = after_sibling("d2", s_d2, r)
    d_win, rc = _tn(dproj, h, "mix_dwin", to_chips)
    after_chips("d2", rc)
    d_wout = _tn(cat, dx2, "mix_dwout")[0]
    d_pw = _tn(u2, dco, "mix_dpw")[0]
    dx0, h1, dg1, du1, df1, d_ffn1_norm = _ffn_bwd(dx1, x0, row(ffn1_norm), g1s, u1s, w_first, ffn_rows, f, "ffn1_bwd")

    d_cdw = d_cdw.reshape(HALO, SUBLANES, D_CONV)[:CONV_WIDTH, 0]
    d_cdw = jnp.transpose(d_cdw.reshape(CONV_WIDTH, N_DEV, D_CONV // N_DEV), (1, 0, 2)).reshape(N_DEV, -1)
    d_cdw = jnp.pad(d_cdw, ((0, 0), (0, CDW_ROWS * D_MODEL - d_cdw.shape[1]))).reshape(N_DEV, CDW_ROWS, D_MODEL)
    rep = jnp.concatenate([
        d_ffn1_norm[0:1], d_mix_norm[0:1], d_ffn2_norm[0:1], d_final_norm[0:1],
        jnp.concatenate([d_cb[0:1], d_lg[0:1]], axis=1), jnp.concatenate([d_lb[0:1], d_ps[0:1]], axis=1),
        jnp.zeros((2, D_MODEL), F32), d_poolw.reshape(-1, D_MODEL)], axis=0)
    rep = jnp.pad(rep, ((0, N_DEV * REP_ROWS - rep.shape[0]), (0, 0))).reshape(N_DEV, REP_ROWS, D_MODEL)
    s_small = jnp.concatenate([chunks(d_win), chunks(d_wout), chunks(d_pw), d_cdw, rep], axis=1)

    s_g1, r = _tn_chunked(dg1, n1, "ffn1_dgate", _SiblingExchange(s_small))
    s_g1 = chunks(s_g1)
    to_chips = after_sibling("small", s_small, r)
    s_u1, rc, r = _tn_chunked(du1, n1, "ffn1_dup", _Together(to_chips, _SiblingExchange(s_g1)))
    s_u1 = chunks(s_u1)
    mine_small = _sum_partials(pending.pop("small"), rc, "rs_sum_small")
    to_chips = after_sibling("g1", s_g1, r)
    o_rep, _ = s_offs["rep"]
    loss_rows = jnp.pad(loss_part, ((0, 0), (0, D_MODEL - loss_part.shape[1])))
    share = _Gather(jnp.concatenate([mine_small[o_rep:o_rep + REP_ROWS], loss_rows], axis=0))
    s_d1, rc, r, shared = _tn_chunked(h1, df1, "ffn1_ddown", _Together(to_chips, _SiblingExchange(s_u1), share))
    s_d1 = chunks(s_d1)
    after_chips("g1", rc)
    to_chips = after_sibling("u1", s_u1, r)
    rc, r = _run_comm(_Together(to_chips, _SiblingExchange(s_d1)), "rs_tail_up")
    after_chips("u1", rc)
    rc, = _run_comm(after_sibling("d1", s_d1, r), "rs_tail_down")
    after_chips("d1", rc)

    rep_all = shared[:, :REP_ROWS].reshape(N_DEV * REP_ROWS, D_MODEL)
    loss = jnp.sum(shared[:, REP_ROWS, 0])

    def small_rows(name):
        o, n = s_offs[name]
        return mine_small[o:o + n]

    g = {
        "ffn1_norm": rep_all[0], "mix_norm": rep_all[1], "ffn2_norm": rep_all[2], "final_norm": rep_all[3],
        "conv_dw_b": rep_all[4, :D_CONV], "conv_ln_g": rep_all[4, D_CONV:],
        "conv_ln_b": rep_all[5, :D_CONV], "pool_scale": rep_all[5, D_CONV:],
        "pool_w": rep_all[8:8 + pool_w.size // D_MODEL].reshape(pool_w.shape),
        "w_out": small_rows("wout"), "conv_pw": small_rows("pw").reshape(conv_pw.shape),
        "conv_dw": small_rows("cdw").reshape(-1)[:conv_dw.size].reshape(conv_dw.shape),
    }

    slab_of = {"ffn1_w_gate": "g1", "ffn1_w_up": "u1", "ffn1_w_down": "d1",
               "ffn2_w_gate": "g2", "ffn2_w_up": "u2", "ffn2_w_down": "d2"}
    transposed = ("ffn1_w_gate", "ffn1_w_up", "ffn2_w_gate", "ffn2_w_up", "w_in")
    g["w_in"] = small_rows("win")
    delta, new_m, new_v = {}, {}, {}

    def operands(n):
        wmv = [given[k] for k in (n, "m_" + n, "v_" + n)]
        return [a.T for a in wmv] if n in transposed else wmv

    def restore(n, a):
        return a.T if n in transposed else a.reshape(p[n].shape)

    others = [n for n in WEIGHTS if n not in slab_of]
    flat = _adamw_many([[_as_2d(a) for a in (wn, g[n], mn, vn)] for n in others for wn, mn, vn in [operands(n)]],
                       "adamw_small")
    for k, n in enumerate(others):
        delta[n], new_m[n], new_v[n] = (restore(n, a) for a in flat[3 * k:3 * k + 3])
    g["w_in"] = g["w_in"].T
    for n, slab in slab_of.items():
        wn, mn, vn = operands(n)
        outs = _adamw_reduced(wn, *reduced[slab], mn, vn, "adamw_" + n)
        g[n], delta[n], new_m[n], new_v[n] = (restore(n, a) for a in outs)

    return (loss, dx0[None], *[g[n] for n in WEIGHTS], *[delta[n] for n in WEIGHTS],
            *[new_m[n] for n in WEIGHTS], *[new_v[n] for n in WEIGHTS])
```

```python
import functools

import jax
import jax.numpy as jnp
from jax import lax
from jax.experimental import pallas as pl
from jax.experimental.pallas import tpu as pltpu

F32 = jnp.float32
BF16 = jnp.bfloat16

D_MODEL = 1024
D_CONV = 512
D_POOL = 512
D_IN = 2 * D_CONV + D_POOL
POOL_WINDOWS = (2, 4, 8, 16)
POOL_GROUP = D_POOL // len(POOL_WINDOWS)
CONV_WIDTH = 31
RMS_EPS = 1e-6
LN_EPS = 1e-5
FFN_RES_WEIGHT = 0.5

ADAM_LR = 0.001
ADAM_B1 = 0.9
ADAM_B2 = 0.999
ADAM_EPS = 1e-08
ADAM_WD = 0.01
ADAM_STEP = 10

N_DEV = 8
MESH_ID = pl.DeviceIdType.MESH

SUBLANES = 8
TOKEN_TILE = 512
FFN_BWD_TILE = 256
FF_CHUNK = 256
HALO = 32
V7X_VMEM_LIMIT = 56 * 1024 * 1024
CDW_ROWS = 16
REP_ROWS = 16

WEIGHTS = ("ffn1_norm", "ffn1_w_gate", "ffn1_w_up", "ffn1_w_down", "mix_norm", "w_in", "conv_dw", "conv_dw_b",
           "conv_ln_g", "conv_ln_b", "conv_pw", "pool_w", "pool_scale", "w_out", "ffn2_norm", "ffn2_w_gate",
           "ffn2_w_up", "ffn2_w_down", "final_norm")


def _dot_nn(a, b):
    return lax.dot_general(a, b, (((1,), (0,)), ((), ())), preferred_element_type=F32)


def _dot_nt(a, b):
    return lax.dot_general(a, b, (((1,), (1,)), ((), ())), preferred_element_type=F32)


def _dot_tn(a, b):
    return lax.dot_general(a, b, (((0,), (0,)), ((), ())), preferred_element_type=F32)


def _rowsum8(v):
    r, c = v.shape
    return jnp.sum(v.reshape(r // SUBLANES, SUBLANES, c), axis=0)


def _fold8(ref):
    ref[0:1, :] = jnp.sum(ref[...], axis=0, keepdims=True)


def _row_tile(n, cap, mult):
    best = None
    for t in range(mult, min(n, cap) + 1, mult):
        if n % t == 0:
            best = t
    return n if best is None else best


def _params(n_grid):
    return pltpu.CompilerParams(dimension_semantics=("arbitrary",) * n_grid, vmem_limit_bytes=V7X_VMEM_LIMIT)


def _full(shape):
    return pl.BlockSpec(shape, lambda *_: (0,) * len(shape))


def _layout(pieces):
    offs, r = {}, 0
    for name, rows in pieces:
        offs[name] = (r, rows)
        r += rows
    return offs, r


HBM = pl.BlockSpec(memory_space=pl.ANY)


def _mesh_pos():
    return lax.axis_index("x"), lax.axis_index("y"), lax.axis_index("c")


def _remote(src, dst, send_sems, recv_sems, k, to):
    return pltpu.make_async_remote_copy(src_ref=src, dst_ref=dst, send_sem=send_sems.at[k], recv_sem=recv_sems.at[k],
                                        device_id=to, device_id_type=MESH_ID)


class _Gather:
    def __init__(self, shard):
        self.inputs = (shard,)
        self.out_shape = (jax.ShapeDtypeStruct((N_DEV, *shard.shape), shard.dtype),)
        self.scratch = (pltpu.SemaphoreType.DMA((7,)), pltpu.SemaphoreType.DMA((7,)), pltpu.SemaphoreType.DMA)

    def phases(self, ins, outs, scr):
        (x_ref,), (out_ref,), (send_sems, recv_sems, local_sem) = ins, outs, scr
        x, y, c = _mesh_pos()
        me, sibling = (x, y, c), (x, y, 1 - c)
        chips = [(1 - x, y), (x, 1 - y), (1 - x, 1 - y)]

        def block(px, py, pc):
            return out_ref.at[4 * px + 2 * py + pc]

        def copy(k, blk, to, src=None):
            return _remote(block(*blk) if src is None else src, block(*blk), send_sems, recv_sems, k, to)

        def mine():
            return pltpu.make_async_copy(x_ref, block(*me), local_sem)

        def first():
            return [copy(0, me, sibling, src=x_ref)] + [copy(1 + j, me, (*chip, c), src=x_ref) for j, chip in enumerate(chips)]

        def passed(j):
            return copy(4 + j, (*chips[j], c), sibling)

        def start():
            mine().start()
            for cp in first():
                cp.start()

        def forward():
            for j, chip in enumerate(chips):
                copy(1 + j, (*chip, c), me).wait_recv()
                passed(j).start()

        def finish():
            copy(0, sibling, me).wait_recv()
            for j, chip in enumerate(chips):
                copy(4 + j, (*chip, 1 - c), me).wait_recv()
            for cp in first() + [passed(j) for j in range(3)]:
                cp.wait_send()
            mine().wait()

        return [start, forward, finish]


class _SiblingExchange:
    def __init__(self, src):
        self.inputs = (src,)
        self.out_shape = (jax.ShapeDtypeStruct((4, *src.shape[1:]), src.dtype),)
        self.scratch = (pltpu.SemaphoreType.DMA((4,)), pltpu.SemaphoreType.DMA((4,)))

    def phases(self, ins, outs, scr):
        (g_ref,), (recv_ref,), (send_sems, recv_sems) = ins, outs, scr
        x, y, c = _mesh_pos()

        def copies():
            return [_remote(g_ref.at[2 * k + (1 - c)], recv_ref.at[k], send_sems, recv_sems, k, (x, y, 1 - c))
                    for k in range(4)]

        def start():
            for cp in copies():
                cp.start()

        def finish():
            for cp in copies():
                cp.wait()

        return [start, finish]


class _ChipsExchange:
    def __init__(self, src):
        self.inputs = (src,)
        self.out_shape = (jax.ShapeDtypeStruct(src.shape, src.dtype),)
        self.scratch = (pltpu.SemaphoreType.DMA((3,)), pltpu.SemaphoreType.DMA((3,)))

    def phases(self, ins, outs, scr):
        (p_ref,), (recv_ref,), (send_sems, recv_sems) = ins, outs, scr
        x, y, c = _mesh_pos()
        peers = [(1 - x, y, c), (x, 1 - y, c), (1 - x, 1 - y, c)]

        def copies():
            return [_remote(p_ref.at[k], recv_ref.at[k], send_sems, recv_sems, k, peer) for k, peer in enumerate(peers)]

        def start():
            for cp in copies():
                cp.start()

        def finish():
            for cp in copies():
                cp.wait()

        return [start, finish]


class _Together:
    def __init__(self, *plans):
        self.plans = plans
        self.inputs = tuple(a for p in plans for a in p.inputs)
        self.out_shape = tuple(o for p in plans for o in p.out_shape)
        self.scratch = tuple(s for p in plans for s in p.scratch)

    def phases(self, ins, outs, scr):
        each = []
        for p in self.plans:
            n_in, n_out, n_scr = len(p.inputs), len(p.out_shape), len(p.scratch)
            each.append(p.phases(ins[:n_in], outs[:n_out], scr[:n_scr]))
            ins, outs, scr = ins[n_in:], outs[n_out:], scr[n_scr:]

        def run(fns):
            def phase():
                for fn in fns:
                    fn()
            return phase

        middle = [fn for ph in each for fn in ph[1:-1]]
        return [run([ph[0] for ph in each]), *([run(middle)] if middle else []), run([ph[-1] for ph in each])]


def _run_comm(plan, name):
    n_in, n_out = len(plan.inputs), len(plan.out_shape)

    def body(*refs):
        for phase in plan.phases(refs[:n_in], refs[n_in:n_in + n_out], refs[n_in + n_out:]):
            phase()

    return pl.pallas_call(
        body, name=name, out_shape=list(plan.out_shape), in_specs=[HBM] * n_in, out_specs=[HBM] * n_out,
        scratch_shapes=list(plan.scratch))(*plan.inputs)


def _grid_call(body, *, name, nt, in_specs, out_specs, out_shape, scratch_shapes, args, plan=None):
    if plan is None:
        return pl.pallas_call(body, name=name, grid=(nt,), in_specs=in_specs, out_specs=out_specs, out_shape=out_shape,
                              scratch_shapes=scratch_shapes, compiler_params=_params(1))(*args)
    n_in, n_out, n_scr = len(in_specs), len(out_specs), len(scratch_shapes)
    p_in, p_out = len(plan.inputs), len(plan.out_shape)

    def with_plan(*refs):
        ins, refs = refs[:n_in], refs[n_in:]
        p_ins, refs = refs[:p_in], refs[p_in:]
        outs, refs = refs[:n_out], refs[n_out:]
        p_outs, refs = refs[:p_out], refs[p_out:]
        scr, p_scr = refs[:n_scr], refs[n_scr:]
        phases = plan.phases(p_ins, p_outs, p_scr)
        i = pl.program_id(0)
        pl.when(i == 0)(phases[0])
        for phase in phases[1:-1]:
            pl.when(i == min(max(nt - 3, 1), nt - 1))(phase)
        body(*ins, *outs, *scr)
        pl.when(i == nt - 1)(phases[-1])

    return pl.pallas_call(
        with_plan, name=name, grid=(nt,), in_specs=[*in_specs, *[HBM] * p_in], out_specs=[*out_specs, *[HBM] * p_out],
        out_shape=[*out_shape, *plan.out_shape], scratch_shapes=[*scratch_shapes, *plan.scratch],
        compiler_params=_params(1))(*args, *plan.inputs)


def _add_chunks(gslab, recv, gid, rid, name):
    _, rows, cols = gslab.shape

    def body(gid_ref, rid_ref, a_ref, b_ref, own_ref, part_ref):
        s = a_ref[...] + b_ref[...]

        @pl.when(pl.program_id(0) == 0)
        def _():
            own_ref[...] = s

        part_ref[...] = s.astype(BF16)

    blk = (1, rows, cols)
    grid_spec = pltpu.PrefetchScalarGridSpec(
        num_scalar_prefetch=2, grid=(4,),
        in_specs=[pl.BlockSpec(blk, lambda k, g, r: (g[k], 0, 0)), pl.BlockSpec(blk, lambda k, g, r: (r[k], 0, 0))],
        out_specs=[pl.BlockSpec(blk, lambda k, g, r: (0, 0, 0)),
                   pl.BlockSpec(blk, lambda k, g, r: (jnp.maximum(k - 1, 0), 0, 0))])
    return pl.pallas_call(
        body, name=name, grid_spec=grid_spec,
        out_shape=[jax.ShapeDtypeStruct(blk, F32), jax.ShapeDtypeStruct((3, rows, cols), BF16)],
        compiler_params=_params(1),
    )(gid, rid, gslab, recv)


def _sum_partials(own, recv, name):
    _, rows, cols = own.shape
    tr = _row_tile(rows, 1024, 16)

    def body(o_ref, r_ref, out_ref):
        acc = o_ref[0]
        for k in range(3):
            acc = acc + r_ref[k].astype(F32)
        out_ref[...] = acc

    return pl.pallas_call(
        body, name=name, grid=(rows // tr,),
        in_specs=[pl.BlockSpec((1, tr, cols), lambda i: (0, i, 0)), pl.BlockSpec((3, tr, cols), lambda i: (0, i, 0))],
        out_specs=pl.BlockSpec((tr, cols), lambda i: (i, 0)),
        out_shape=jax.ShapeDtypeStruct((rows, cols), F32),
        compiler_params=_params(1),
    )(own, recv)


def _chunk_ids():
    x, y, c = _mesh_pos()
    chips = [(x, y), (1 - x, y), (x, 1 - y), (1 - x, 1 - y)]
    gid = jnp.stack([4 * px + 2 * py + c for px, py in chips]).astype(jnp.int32)
    rid = jnp.stack([2 * px + py for px, py in chips]).astype(jnp.int32)
    return gid, rid


def _sibling_sums(slab, recv, tag):
    gid, rid = _chunk_ids()
    return _add_chunks(slab, recv, gid, rid, "rs_add_" + tag)


def _load_weights(slab_ref, offs, dsts, sems):
    cps = []
    for i, (off, dst) in enumerate(zip(offs, dsts)):
        f8 = dst.shape[0] // N_DEV
        cps += [pltpu.make_async_copy(slab_ref.at[j, pl.ds(off, f8), :], dst.at[pl.ds(j * f8, f8), :],
                                      sems.at[i * N_DEV + j]) for j in range(N_DEV)]
    for cp in cps:
        cp.start()
    for cp in cps:
        cp.wait()


def _chunk_rows(c):
    return pl.ds(pl.multiple_of(c * FF_CHUNK, FF_CHUNK), FF_CHUNK)


def _rms(xv):
    return lax.rsqrt(jnp.mean(xv * xv, axis=-1, keepdims=True) + RMS_EPS)


def _loss_terms(xv, tgt, gain):
    r = _rms(xv)
    xh = xv * r
    err = xh * gain - tgt
    loss = 0.5 * jnp.sum(jnp.mean(err * err, axis=-1, keepdims=True))
    dy = err * (1.0 / xv.shape[-1])
    dxh = dy * gain
    return loss, r * (dxh - xh * jnp.mean(dxh * xh, axis=-1, keepdims=True)), _rowsum8(dy * xh)


def _ffn_fwd(x, gain, slab, offs, f, name, plan=None, head=None):
    t, d = x.shape
    nc, tm = f // FF_CHUNK, TOKEN_TILE
    nt = t // tm
    n_head = 0 if head is None else 2

    def body(*refs):
        x_ref, gain_ref, slab_ref = refs[:3]
        xo_ref, g_ref, u_ref, n_ref = refs[3 + n_head:7 + n_head]
        wg_v, wu_v, wd_v, acc_ref, sems = refs[7 + 2 * n_head:]
        i = pl.program_id(0)

        @pl.when(i == 0)
        def _():
            _load_weights(slab_ref, offs, (wg_v, wu_v, wd_v), sems)

        xv = x_ref[...]
        n_ref[...] = ((xv * _rms(xv)) * gain_ref[...]).astype(BF16)
        acc_ref[...] = jnp.zeros_like(acc_ref)

        def chunk(c, carry):
            rows = _chunk_rows(c)
            nb = n_ref[...]
            g = _dot_nt(nb, wg_v[rows, :])
            u = _dot_nt(nb, wu_v[rows, :])
            g_ref[c] = g.astype(BF16)
            u_ref[c] = u.astype(BF16)
            h = (g * jax.nn.sigmoid(g)) * u
            acc_ref[...] += _dot_nn(h.astype(BF16), wd_v[rows, :])
            return carry

        lax.fori_loop(0, nc, chunk, 0, unroll=True)
        out = xv + FFN_RES_WEIGHT * acc_ref[...]
        if head is None:
            xo_ref[...] = out
            return
        tgt_ref, fgain_ref = refs[3:5]
        dgain_ref, loss_ref = refs[7 + n_head:7 + 2 * n_head]

        @pl.when(i == 0)
        def _():
            dgain_ref[...] = jnp.zeros_like(dgain_ref)
            loss_ref[...] = jnp.zeros_like(loss_ref)

        loss, dx, dgain = _loss_terms(out, tgt_ref[...], fgain_ref[...])
        xo_ref[...] = dx
        loss_ref[...] += loss
        dgain_ref[...] += dgain

        @pl.when(i == nt - 1)
        def _():
            _fold8(dgain_ref)

    tile = pl.BlockSpec((tm, d), lambda i: (i, 0))
    act = pl.BlockSpec((nc, tm, FF_CHUNK), lambda i: (0, i, 0))
    head_in = [] if head is None else [tile, _full((1, d))]
    head_out = [] if head is None else [_full((SUBLANES, d)), _full((SUBLANES, 128))]
    head_shape = [] if head is None else [jax.ShapeDtypeStruct((SUBLANES, d), F32), jax.ShapeDtypeStruct((SUBLANES, 128), F32)]
    return _grid_call(
        body, name=name, nt=nt, plan=plan,
        in_specs=[tile, _full((1, d)), HBM, *head_in],
        out_specs=[tile, act, act, tile, *head_out],
        out_shape=[jax.ShapeDtypeStruct((t, d), F32), jax.ShapeDtypeStruct((nc, t, FF_CHUNK), BF16),
                   jax.ShapeDtypeStruct((nc, t, FF_CHUNK), BF16), jax.ShapeDtypeStruct((t, d), BF16), *head_shape],
        scratch_shapes=[pltpu.VMEM((f, d), BF16), pltpu.VMEM((f, d), BF16), pltpu.VMEM((f, d), BF16),
                        pltpu.VMEM((tm, d), F32), pltpu.SemaphoreType.DMA((3 * N_DEV,))],
        args=(x, gain, slab, *([] if head is None else head)))


def _ffn_bwd(dxo, x, gain, gs, us, slab, offs, f, name, plan=None):
    t, d = x.shape
    nc, tm = f // FF_CHUNK, FFN_BWD_TILE
    nt = t // tm

    def body(dxo_ref, x_ref, gain_ref, g_ref, u_ref, slab_ref,
             dx_ref, h_ref, dg_ref, du_ref, df_ref, dgain_ref, wg_v, wu_v, wd_v, sems):
        i = pl.program_id(0)

        @pl.when(i == 0)
        def _():
            _load_weights(slab_ref, offs, (wg_v, wu_v, wd_v), sems)
            dgain_ref[...] = jnp.zeros_like(dgain_ref)

        df_ref[...] = (FFN_RES_WEIGHT * dxo_ref[...]).astype(BF16)

        def chunk(c, carry):
            rows = _chunk_rows(c)
            g = g_ref[c].astype(F32)
            u = u_ref[c].astype(F32)
            sg = jax.nn.sigmoid(g)
            sil = g * sg
            dh = _dot_nt(df_ref[...], wd_v[rows, :])
            h_ref[c] = (sil * u).astype(BF16)
            du_ref[c] = (dh * sil).astype(BF16)
            dg_ref[c] = (dh * u * (sg * (1.0 + g * (1.0 - sg)))).astype(BF16)
            return carry

        lax.fori_loop(0, nc, chunk, 0, unroll=True)

        def back(c, dn):
            rows = _chunk_rows(c)
            return dn + _dot_nn(dg_ref[c], wg_v[rows, :]) + _dot_nn(du_ref[c], wu_v[rows, :])

        dn = lax.fori_loop(0, nc, back, jnp.zeros((tm, d), F32), unroll=True)
        xv = x_ref[...]
        r = _rms(xv)
        xh = xv * r
        dgain_ref[...] += _rowsum8(dn * xh)
        dxh = dn * gain_ref[...]
        dx_ref[...] = dxo_ref[...] + r * (dxh - xh * jnp.mean(dxh * xh, axis=-1, keepdims=True))

        @pl.when(i == nt - 1)
        def _():
            _fold8(dgain_ref)

    tile = pl.BlockSpec((tm, d), lambda i: (i, 0))
    act = pl.BlockSpec((nc, tm, FF_CHUNK), lambda i: (0, i, 0))
    act_shape = jax.ShapeDtypeStruct((nc, t, FF_CHUNK), BF16)
    return _grid_call(
        body, name=name, nt=nt, plan=plan,
        in_specs=[tile, tile, _full((1, d)), act, act, HBM],
        out_specs=[tile, act, act, act, tile, _full((SUBLANES, d))],
        out_shape=[jax.ShapeDtypeStruct((t, d), F32), act_shape, act_shape, act_shape,
                   jax.ShapeDtypeStruct((t, d), BF16), jax.ShapeDtypeStruct((SUBLANES, d), F32)],
        scratch_shapes=[pltpu.VMEM((f, d), BF16), pltpu.VMEM((f, d), BF16), pltpu.VMEM((f, d), BF16),
                        pltpu.SemaphoreType.DMA((3 * N_DEV,))],
        args=(dxo, x, gain, gs, us, slab))


def _tn_chunked(a, b, name, plan=None):
    nc, t, _ = a.shape
    n = b.shape[1]
    tb = _row_tile(t, 1024, TOKEN_TILE)

    def body(a_ref, b_ref, o_ref):
        @pl.when(pl.program_id(0) == 0)
        def _():
            o_ref[...] = jnp.zeros_like(o_ref)

        def chunk(c, carry):
            rows = _chunk_rows(c)
            o_ref[rows, :] += _dot_tn(a_ref[c], b_ref[...])
            return carry

        lax.fori_loop(0, nc, chunk, 0, unroll=True)

    return _grid_call(
        body, name=name, nt=t // tb, plan=plan,
        in_specs=[pl.BlockSpec((nc, tb, FF_CHUNK), lambda i: (0, i, 0)), pl.BlockSpec((tb, n), lambda i: (i, 0))],
        out_specs=[_full((nc * FF_CHUNK, n))],
        out_shape=[jax.ShapeDtypeStruct((nc * FF_CHUNK, n), F32)],
        scratch_shapes=[], args=(a, b))


def _tn(a, b, name, plan=None):
    t, k = a.shape
    n = b.shape[1]
    tb = _row_tile(t, 1024, TOKEN_TILE)

    def body(a_ref, b_ref, o_ref):
        @pl.when(pl.program_id(0) == 0)
        def _():
            o_ref[...] = jnp.zeros_like(o_ref)

        o_ref[...] += _dot_tn(a_ref[...].astype(BF16), b_ref[...].astype(BF16))

    return _grid_call(
        body, name=name, nt=t // tb, plan=plan,
        in_specs=[pl.BlockSpec((tb, k), lambda i: (i, 0)), pl.BlockSpec((tb, n), lambda i: (i, 0))],
        out_specs=[_full((k, n))],
        out_shape=[jax.ShapeDtypeStruct((k, n), F32)],
        scratch_shapes=[], args=(a, b))


def _layernorm_stats(u1):
    mu = jnp.mean(u1, axis=-1, keepdims=True)
    xc = u1 - mu
    rstd = lax.rsqrt(jnp.mean(xc * xc, axis=-1, keepdims=True) + LN_EPS)
    return xc * rstd, rstd


def _positions(tile_index, tm):
    return (tile_index * tm + lax.broadcasted_iota(jnp.int32, (tm, 1), 0)).astype(F32)


def _shifted_taps(src_ref, sh_ref, tm, offset_of):
    groups = {}
    for k in range(CONV_WIDTH):
        groups.setdefault(offset_of(k) % SUBLANES, []).append(k)
    span = tm + HALO - SUBLANES
    for rem, taps in sorted(groups.items()):
        if rem:
            sh_ref[0:span, :] = src_ref[rem:rem + span, :]
        ref = sh_ref if rem else src_ref
        for k in taps:
            base = offset_of(k) - rem
            yield k, ref[base:base + tm, :]


def _mix_fwd(x, gm, win_t, cdw, cb, lg, lb, pw, poolw, ps, wout):
    t, d = x.shape
    tm = TOKEN_TILE

    def body(x_ref, gm_ref, win_ref, cdw_ref, cb_ref, lg_ref, lb_ref, pw_ref, poolw_ref, ps_ref, wout_ref,
             xo_ref, h_ref, ag_ref, u0_ref, u1_ref, u2_ref, mixed_ref, cat_ref, eu_ref, ep_ref, sh_ref):
        i = pl.program_id(0)

        @pl.when(i == 0)
        def _():
            eu_ref[0:HALO, :] = jnp.zeros((HALO, D_CONV), F32)
            ep_ref[0:HALO, :] = jnp.zeros((HALO, D_POOL), F32)

        @pl.when(i > 0)
        def _():
            eu_ref[0:HALO, :] = eu_ref[tm:tm + HALO, :]
            ep_ref[0:HALO, :] = ep_ref[tm:tm + HALO, :]

        xv = x_ref[...]
        hb = ((xv * _rms(xv)) * gm_ref[...]).astype(BF16)
        h_ref[...] = hb
        proj = _dot_nt(hb, win_ref[...])
        a = proj[:, :D_CONV]
        g = proj[:, D_CONV:2 * D_CONV]
        ag_ref[...] = proj[:, :2 * D_CONV]
        u0 = a * jax.nn.sigmoid(g)
        u0_ref[...] = u0
        eu_ref[HALO:HALO + tm, :] = u0
        ep_ref[HALO:HALO + tm, :] = proj[:, 2 * D_CONV:]

        u1 = jnp.broadcast_to(cb_ref[...], (tm, D_CONV))
        for k, rows in _shifted_taps(eu_ref, sh_ref, tm, lambda k: HALO - (CONV_WIDTH - 1) + k):
            u1 = u1 + cdw_ref[k:k + 1, :] * rows
        u1_ref[...] = u1
        lnh, _ = _layernorm_stats(u1)
        ln = lnh * lg_ref[...] + lb_ref[...]
        u2 = (ln * jax.nn.sigmoid(ln)).astype(BF16)
        u2_ref[...] = u2
        conv_out = _dot_nn(u2, pw_ref[...])

        pos = _positions(i, tm)
        outs = []
        for gi, w in enumerate(POOL_WINDOWS):
            lo = gi * POOL_GROUP
            p = ep_ref[HALO:HALO + tm, lo:lo + POOL_GROUP]
            s = p
            for j in range(1, w):
                s = s + ep_ref[HALO - j:HALO - j + tm, lo:lo + POOL_GROUP]
            mixed = (s / jnp.minimum(pos + 1.0, float(w)) - p).astype(BF16)
            mixed_ref[:, lo:lo + POOL_GROUP] = mixed
            outs.append(_dot_nn(mixed, poolw_ref[gi]))
        pool_out = jnp.concatenate(outs, axis=-1) * ps_ref[...]
        cat = jnp.concatenate([conv_out, pool_out], axis=-1).astype(BF16)
        cat_ref[...] = cat
        xo_ref[...] = xv + _dot_nn(cat, wout_ref[...])

    def tile(c):
        return pl.BlockSpec((tm, c), lambda i: (i, 0))

    def out(c, dt):
        return jax.ShapeDtypeStruct((t, c), dt)

    return pl.pallas_call(
        body, name="mix_fwd", grid=(t // tm,),
        in_specs=[tile(d), _full((1, d)), _full((D_IN, d)), _full((HALO, D_CONV)), _full((1, D_CONV)),
                  _full((1, D_CONV)), _full((1, D_CONV)), _full((D_CONV, D_CONV)),
                  _full((len(POOL_WINDOWS), POOL_GROUP, POOL_GROUP)), _full((1, D_POOL)), _full((d, d))],
        out_specs=[tile(d), tile(d), tile(2 * D_CONV), tile(D_CONV), tile(D_CONV), tile(D_CONV), tile(D_POOL), tile(d)],
        out_shape=[out(d, F32), out(d, BF16), out(2 * D_CONV, F32), out(D_CONV, F32), out(D_CONV, F32),
                   out(D_CONV, BF16), out(D_POOL, BF16), out(d, BF16)],
        scratch_shapes=[pltpu.VMEM((HALO + tm, D_CONV), F32), pltpu.VMEM((HALO + tm, D_POOL), F32),
                        pltpu.VMEM((HALO + tm, D_CONV), F32)],
        compiler_params=_params(1),
    )(x, gm, win_t, cdw, cb, lg, lb, pw, poolw, ps, wout)


def _mix_bwd(dxo, x, gm, ag, u0, u1, mixed, win_t, cdw, lg, lb, pw, poolw, ps, wout, plan=None):
    t, d = x.shape
    tm = TOKEN_TILE
    nt = t // tm
    halo_blocks = tm // HALO

    def body(dxo_ref, x_ref, gm_ref, ag_ref, u0_ref, u0h_ref, u1_ref, mixed_ref,
             win_ref, cdw_ref, lg_ref, lb_ref, pw_ref, poolw_ref, ps_ref, wout_ref,
             dx_ref, dproj_ref, dco_ref, dgm_ref, dcdw_ref, dcb_ref, dlg_ref, dlb_ref, dpoolw_ref, dps_ref,
             eu_ref, ed_ref, eq_ref, sh_ref):
        i = pl.program_id(0)
        ti = nt - 1 - i

        @pl.when(i == 0)
        def _():
            for ref in (dgm_ref, dcdw_ref, dcb_ref, dlg_ref, dlb_ref, dpoolw_ref, dps_ref):
                ref[...] = jnp.zeros_like(ref)
            ed_ref[tm:tm + HALO, :] = jnp.zeros((HALO, D_CONV), F32)
            eq_ref[tm:tm + HALO, :] = jnp.zeros((HALO, D_POOL), F32)

        @pl.when(i > 0)
        def _():
            ed_ref[tm:tm + HALO, :] = ed_ref[0:HALO, :]
            eq_ref[tm:tm + HALO, :] = eq_ref[0:HALO, :]

        @pl.when(ti == 0)
        def _():
            eu_ref[0:HALO, :] = jnp.zeros((HALO, D_CONV), F32)

        @pl.when(ti > 0)
        def _():
            eu_ref[0:HALO, :] = u0h_ref[...]

        eu_ref[HALO:HALO + tm, :] = u0_ref[...]

        dxo = dxo_ref[...]
        dcat = _dot_nt(dxo.astype(BF16), wout_ref[...])
        dco = dcat[:, :D_CONV].astype(BF16)
        dco_ref[...] = dco
        dpo = dcat[:, D_CONV:]

        lnh, rstd = _layernorm_stats(u1_ref[...])
        ln = lnh * lg_ref[...] + lb_ref[...]
        sl = jax.nn.sigmoid(ln)
        dln = _dot_nt(dco, pw_ref[...]) * (sl * (1.0 + ln * (1.0 - sl)))
        dlg_ref[...] += _rowsum8(dln * lnh)
        dlb_ref[...] += _rowsum8(dln)
        dlnh = dln * lg_ref[...]
        du1 = rstd * (dlnh - jnp.mean(dlnh, axis=-1, keepdims=True)
                      - lnh * jnp.mean(dlnh * lnh, axis=-1, keepdims=True))
        dcb_ref[...] += _rowsum8(du1)
        ed_ref[0:tm, :] = du1

        du0 = jnp.zeros((tm, D_CONV), F32)
        for k, rows in _shifted_taps(ed_ref, sh_ref, tm, lambda k: CONV_WIDTH - 1 - k):
            du0 = du0 + cdw_ref[k:k + 1, :] * rows
        for k, rows in _shifted_taps(eu_ref, sh_ref, tm, lambda k: HALO - (CONV_WIDTH - 1) + k):
            dcdw_ref[SUBLANES * k:SUBLANES * (k + 1), :] += _rowsum8(du1 * rows)
        a = ag_ref[:, :D_CONV]
        sg = jax.nn.sigmoid(ag_ref[:, D_CONV:])
        pieces = [du0 * sg, du0 * a * (sg * (1.0 - sg))]

        pos = _positions(ti, tm)
        for gi, w in enumerate(POOL_WINDOWS):
            lo = gi * POOL_GROUP
            mg = mixed_ref[:, lo:lo + POOL_GROUP]
            dpo_g = dpo[:, lo:lo + POOL_GROUP]
            dps_ref[:, lo:lo + POOL_GROUP] += _rowsum8(dpo_g * _dot_nn(mg, poolw_ref[gi]))
            dout = (dpo_g * ps_ref[:, lo:lo + POOL_GROUP]).astype(BF16)
            dpoolw_ref[gi] += _dot_tn(mg, dout)
            dmx = _dot_nt(dout, poolw_ref[gi])
            q = dmx / jnp.minimum(pos + 1.0, float(w))
            eq_ref[0:tm, lo:lo + POOL_GROUP] = q
            s = q
            for j in range(1, w):
                s = s + eq_ref[j:j + tm, lo:lo + POOL_GROUP]
            pieces.append(s - dmx)
        dproj = jnp.concatenate(pieces, axis=-1).astype(BF16)
        dproj_ref[...] = dproj

        dh = _dot_nn(dproj, win_ref[...])
        xv = x_ref[...]
        r = _rms(xv)
        xh = xv * r
        dgm_ref[...] += _rowsum8(dh * xh)
        dxh = dh * gm_ref[...]
        dx_ref[...] = dxo + r * (dxh - xh * jnp.mean(dxh * xh, axis=-1, keepdims=True))

        @pl.when(i == nt - 1)
        def _():
            for ref in (dgm_ref, dcb_ref, dlg_ref, dlb_ref, dps_ref):
                _fold8(ref)
            for k in range(CONV_WIDTH):
                dcdw_ref[SUBLANES * k:SUBLANES * k + 1, :] = jnp.sum(
                    dcdw_ref[SUBLANES * k:SUBLANES * (k + 1), :], axis=0, keepdims=True)

    def tile(c):
        return pl.BlockSpec((tm, c), lambda i: (nt - 1 - i, 0))

    halo = pl.BlockSpec((HALO, D_CONV), lambda i: (jnp.maximum((nt - 1 - i) * halo_blocks - 1, 0), 0))
    n_groups = len(POOL_WINDOWS)
    return _grid_call(
        body, name="mix_bwd", nt=nt, plan=plan,
        in_specs=[tile(d), tile(d), _full((1, d)), tile(2 * D_CONV), tile(D_CONV), halo, tile(D_CONV), tile(D_POOL),
                  _full((D_IN, d)), _full((HALO, D_CONV)), _full((1, D_CONV)), _full((1, D_CONV)),
                  _full((D_CONV, D_CONV)), _full((n_groups, POOL_GROUP, POOL_GROUP)), _full((1, D_POOL)), _full((d, d))],
        out_specs=[tile(d), tile(D_IN), tile(D_CONV), _full((SUBLANES, d)), _full((HALO * SUBLANES, D_CONV)),
                   _full((SUBLANES, D_CONV)), _full((SUBLANES, D_CONV)), _full((SUBLANES, D_CONV)),
                   _full((n_groups, POOL_GROUP, POOL_GROUP)), _full((SUBLANES, D_POOL))],
        out_shape=[jax.ShapeDtypeStruct((t, d), F32), jax.ShapeDtypeStruct((t, D_IN), BF16),
                   jax.ShapeDtypeStruct((t, D_CONV), BF16), jax.ShapeDtypeStruct((SUBLANES, d), F32),
                   jax.ShapeDtypeStruct((HALO * SUBLANES, D_CONV), F32), jax.ShapeDtypeStruct((SUBLANES, D_CONV), F32),
                   jax.ShapeDtypeStruct((SUBLANES, D_CONV), F32), jax.ShapeDtypeStruct((SUBLANES, D_CONV), F32),
                   jax.ShapeDtypeStruct((n_groups, POOL_GROUP, POOL_GROUP), F32),
                   jax.ShapeDtypeStruct((SUBLANES, D_POOL), F32)],
        scratch_shapes=[pltpu.VMEM((HALO + tm, D_CONV), F32), pltpu.VMEM((tm + HALO, D_CONV), F32),
                        pltpu.VMEM((tm + HALO, D_POOL), F32), pltpu.VMEM((tm + HALO, D_CONV), F32)],
        args=(dxo, x, gm, ag, u0, u0, u1, mixed, win_t, cdw, lg, lb, pw, poolw, ps, wout))


def _adam_step(gv, w_ref, m_ref, v_ref, d_ref, nm_ref, nv_ref):
    nm = ADAM_B1 * m_ref[...] + (1.0 - ADAM_B1) * gv
    nv = ADAM_B2 * v_ref[...] + (1.0 - ADAM_B2) * (gv * gv)
    m_hat = nm / (1.0 - ADAM_B1 ** ADAM_STEP)
    v_hat = nv / (1.0 - ADAM_B2 ** ADAM_STEP)
    d_ref[...] = -ADAM_LR * (m_hat / (jnp.sqrt(v_hat) + ADAM_EPS) + ADAM_WD * w_ref[...])
    nm_ref[...] = nm
    nv_ref[...] = nv


def _adam_tile(rows, cols):
    return _row_tile(rows, max(SUBLANES, (256 * 1024) // cols // SUBLANES * SUBLANES), SUBLANES)


def _adamw_many(items, name):
    n = len(items)

    def body(*refs):
        ins, outs = refs[:4 * n], refs[4 * n:]
        for k in range(n):
            w_ref, g_ref, m_ref, v_ref = ins[4 * k:4 * k + 4]
            _adam_step(g_ref[...], w_ref, m_ref, v_ref, *outs[3 * k:3 * k + 3])

    return pl.pallas_call(
        body, name=name, out_shape=[jax.ShapeDtypeStruct(it[0].shape, F32) for it in items for _ in range(3)],
        compiler_params=_params(0))(*[a for it in items for a in it])


def _adamw_reduced(items, name, plan=None):
    n = len(items)
    rows, cols = items[0][0].shape
    tr = _adam_tile(rows, cols)
    tiles = rows // tr

    def body(*refs):
        ins, outs = refs[:5 * n], refs[5 * n:]
        step = pl.program_id(0)
        for k in range(n):
            w_ref, o_ref, r_ref, m_ref, v_ref = ins[5 * k:5 * k + 5]
            g_ref, d_ref, nm_ref, nv_ref = outs[4 * k:4 * k + 4]

            @pl.when(step // tiles == k)
            def _():
                gv = o_ref[0]
                for j in range(3):
                    gv = gv + r_ref[j].astype(F32)
                g_ref[...] = gv
                _adam_step(gv, w_ref, m_ref, v_ref, d_ref, nm_ref, nv_ref)

    def blk(k, lead=None):
        def tile_of(step):
            return jnp.clip(step - k * tiles, 0, tiles - 1)

        if lead is None:
            return pl.BlockSpec((tr, cols), lambda step: (tile_of(step), 0))
        return pl.BlockSpec((lead, tr, cols), lambda step: (0, tile_of(step), 0))

    return _grid_call(
        body, name=name, nt=n * tiles, plan=plan,
        in_specs=[spec for k in range(n) for spec in (blk(k), blk(k, 1), blk(k, 3), blk(k), blk(k))],
        out_specs=[blk(k) for k in range(n) for _ in range(4)],
        out_shape=[jax.ShapeDtypeStruct((rows, cols), F32)] * (4 * n),
        scratch_shapes=[], args=[a for it in items for a in it])


def _as_2d(a):
    if a.ndim == 1:
        return a.reshape(a.shape[0] // 128, 128)
    if a.ndim == 3:
        return a.reshape(a.shape[0] * a.shape[1], a.shape[2])
    return a


def _pack_weight_slabs(p):
    def bf(parts):
        return [a.astype(BF16) for a in parts]

    cdw_bits = lax.bitcast_convert_type(p["conv_dw"], BF16).reshape(CONV_WIDTH, 2 * D_CONV // N_DEV)
    cdw_bits = jnp.pad(cdw_bits, ((0, 1), (0, 0))).reshape(4, D_MODEL)
    cdw_bits = jnp.pad(cdw_bits, ((0, CDW_ROWS - 4), (0, 0)))
    first = bf([p["ffn1_w_gate"].T, p["ffn1_w_up"].T, p["ffn1_w_down"]])
    rest = bf([p["ffn2_w_gate"].T, p["ffn2_w_up"].T, p["ffn2_w_down"], p["w_in"].T, p["w_out"],
               p["conv_pw"].reshape(D_CONV // N_DEV // 2, D_MODEL)]) + [cdw_bits]
    return jnp.concatenate(first, axis=0), jnp.concatenate(rest, axis=0)


def _unpack_rows(slab, offs, names):
    out = {}
    for name in names:
        o, n = offs[name]
        out[name] = slab[:, o:o + n, :].reshape(N_DEV * n, D_MODEL)
    return out


def _unpack_conv_taps(slab, offs):
    o, _ = offs["cdw"]
    bits = slab[:, o:o + 4, :].reshape(N_DEV, CONV_WIDTH + 1, D_CONV // N_DEV, 2)[:, :CONV_WIDTH]
    cdw = lax.bitcast_convert_type(bits, F32)
    return jnp.transpose(cdw, (1, 0, 2)).reshape(CONV_WIDTH, D_CONV)


def kernel(x, ffn1_norm, ffn1_w_gate, ffn1_w_up, ffn1_w_down, mix_norm, w_in, conv_dw, conv_dw_b, conv_ln_g, conv_ln_b, conv_pw, pool_w, pool_scale, w_out, ffn2_norm, ffn2_w_gate, ffn2_w_up, ffn2_w_down, final_norm, loss_target, m_ffn1_norm, m_ffn1_w_gate, m_ffn1_w_up, m_ffn1_w_down, m_mix_norm, m_w_in, m_conv_dw, m_conv_dw_b, m_conv_ln_g, m_conv_ln_b, m_conv_pw, m_pool_w, m_pool_scale, m_w_out, m_ffn2_norm, m_ffn2_w_gate, m_ffn2_w_up, m_ffn2_w_down, m_final_norm, v_ffn1_norm, v_ffn1_w_gate, v_ffn1_w_up, v_ffn1_w_down, v_mix_norm, v_w_in, v_conv_dw, v_conv_dw_b, v_conv_ln_g, v_conv_ln_b, v_conv_pw, v_pool_w, v_pool_scale, v_w_out, v_ffn2_norm, v_ffn2_w_gate, v_ffn2_w_up, v_ffn2_w_down, v_final_norm):
    given = dict(locals())
    p = {n: given[n] for n in WEIGHTS}
    f8 = ffn1_w_gate.shape[1]
    f = N_DEV * f8
    ffn_rows = (0, f8, 2 * f8)
    small = (("win", D_IN // N_DEV), ("wout", D_MODEL // N_DEV), ("pw", D_CONV // N_DEV // 2), ("cdw", CDW_ROWS))
    w_offs, _ = _layout((("g2", f8), ("u2", f8), ("d2", f8)) + small)
    s_offs, _ = _layout(small + (("rep", REP_ROWS),))
    x0 = x[0]
    target = loss_target[0]

    def row(vec):
        return vec.reshape(1, vec.shape[0])

    slab_first, slab_rest = _pack_weight_slabs(p)
    w_first = _run_comm(_Gather(slab_first), "gather_ffn1")[0]
    x1, g1s, u1s, n1, w_rest = _ffn_fwd(x0, row(ffn1_norm), w_first, ffn_rows, f, "ffn1_fwd", _Gather(slab_rest))
    w = _unpack_rows(w_rest, w_offs, ("win", "wout", "pw"))
    w["pw"] = w["pw"].reshape(D_CONV, D_CONV)
    cdw = jnp.pad(_unpack_conv_taps(w_rest, w_offs), ((0, HALO - CONV_WIDTH), (0, 0)))
    poolw = pool_w.astype(BF16)

    x2, h, ag, u0, u1, u2, mixed, cat = _mix_fwd(
        x1, row(mix_norm), w["win"], cdw, row(conv_dw_b), row(conv_ln_g), row(conv_ln_b), w["pw"], poolw,
        row(pool_scale), w["wout"])
    dx3, g2s, u2s, n2, d_final_norm, loss_part = _ffn_fwd(
        x2, row(ffn2_norm), w_rest, ffn_rows, f, "ffn2_fwd", head=(target, row(final_norm)))

    pending, reduced = {}, {}

    def chunks(a):
        return a.reshape(N_DEV, -1, D_MODEL)

    def after_sibling(name, slab, recv):
        pending[name], part = _sibling_sums(slab, recv, name)
        return _ChipsExchange(part)

    def after_chips(name, recv):
        reduced[name] = (pending.pop(name), recv)

    dx2, h2, dg2, du2, df2, d_ffn2_norm = _ffn_bwd(dx3, x2, row(ffn2_norm), g2s, u2s, w_rest, ffn_rows, f, "ffn2_bwd")
    s_g2 = chunks(_tn_chunked(dg2, n2, "ffn2_dgate")[0])
    s_u2, r = _tn_chunked(du2, n2, "ffn2_dup", _SiblingExchange(s_g2))
    s_u2 = chunks(s_u2)
    to_chips = after_sibling("g2", s_g2, r)
    s_d2, rc, r = _tn_chunked(h2, df2, "ffn2_ddown", _Together(to_chips, _SiblingExchange(s_u2)))
    s_d2 = chunks(s_d2)
    after_chips("g2", rc)
    to_chips = after_sibling("u2", s_u2, r)
    dx1, dproj, dco, d_mix_norm, d_cdw, d_cb, d_lg, d_lb, d_poolw, d_ps, rc, r = _mix_bwd(
        dx2, x1, row(mix_norm), ag, u0, u1, mixed, w["win"], cdw, row(conv_ln_g), row(conv_ln_b), w["pw"], poolw,
        row(pool_scale), w["wout"], _Together(to_chips, _SiblingExchange(s_d2)))
    after_chips("u2", rc)
    to_chips = after_sibling("d2", s_d2, r)
    d_win, rc = _tn(dproj, h, "mix_dwin", to_chips)
    after_chips("d2", rc)
    d_wout = _tn(cat, dx2, "mix_dwout")[0]
    d_pw = _tn(u2, dco, "mix_dpw")[0]
    dx0, h1, dg1, du1, df1, d_ffn1_norm = _ffn_bwd(dx1, x0, row(ffn1_norm), g1s, u1s, w_first, ffn_rows, f, "ffn1_bwd")

    d_cdw = d_cdw.reshape(HALO, SUBLANES, D_CONV)[:CONV_WIDTH, 0]
    d_cdw = jnp.transpose(d_cdw.reshape(CONV_WIDTH, N_DEV, D_CONV // N_DEV), (1, 0, 2)).reshape(N_DEV, -1)
    d_cdw = jnp.pad(d_cdw, ((0, 0), (0, CDW_ROWS * D_MODEL - d_cdw.shape[1]))).reshape(N_DEV, CDW_ROWS, D_MODEL)
    rep = jnp.concatenate([
        d_ffn1_norm[0:1], d_mix_norm[0:1], d_ffn2_norm[0:1], d_final_norm[0:1],
        jnp.concatenate([d_cb[0:1], d_lg[0:1]], axis=1), jnp.concatenate([d_lb[0:1], d_ps[0:1]], axis=1),
        jnp.zeros((2, D_MODEL), F32), d_poolw.reshape(-1, D_MODEL)], axis=0)
    rep = jnp.pad(rep, ((0, N_DEV * REP_ROWS - rep.shape[0]), (0, 0))).reshape(N_DEV, REP_ROWS, D_MODEL)
    s_small = jnp.concatenate([chunks(d_win), chunks(d_wout), chunks(d_pw), d_cdw, rep], axis=1)

    s_g1, r = _tn_chunked(dg1, n1, "ffn1_dgate", _SiblingExchange(s_small))
    s_g1 = chunks(s_g1)
    to_chips = after_sibling("small", s_small, r)
    s_u1, rc, r = _tn_chunked(du1, n1, "ffn1_dup", _Together(to_chips, _SiblingExchange(s_g1)))
    s_u1 = chunks(s_u1)
    mine_small = _sum_partials(pending.pop("small"), rc, "rs_sum_small")
    to_chips = after_sibling("g1", s_g1, r)
    o_rep, _ = s_offs["rep"]
    loss_rows = jnp.pad(loss_part, ((0, 0), (0, D_MODEL - loss_part.shape[1])))
    share = _Gather(jnp.concatenate([mine_small[o_rep:o_rep + REP_ROWS], loss_rows], axis=0))
    s_d1, rc, r, shared = _tn_chunked(h1, df1, "ffn1_ddown", _Together(to_chips, _SiblingExchange(s_u1), share))
    s_d1 = chunks(s_d1)
    after_chips("g1", rc)
    to_chips = after_sibling("u1", s_u1, r)

    slab_of = {"ffn1_w_gate": "g1", "ffn1_w_up": "u1", "ffn1_w_down": "d1",
               "ffn2_w_gate": "g2", "ffn2_w_up": "u2", "ffn2_w_down": "d2"}
    transposed = ("ffn1_w_gate", "ffn1_w_up", "ffn2_w_gate", "ffn2_w_up", "w_in")
    g, delta, new_m, new_v = {}, {}, {}, {}

    def operands(n):
        wmv = [given[k] for k in (n, "m_" + n, "v_" + n)]
        return [a.T for a in wmv] if n in transposed else wmv

    def restore(n, a):
        return a.T if n in transposed else a.reshape(p[n].shape)

    def update(names, tag, plan=None):
        items = [[wn, *reduced[slab_of[n]], mn, vn] for n in names for wn, mn, vn in [operands(n)]]
        outs = _adamw_reduced(items, "adamw_" + tag, plan)
        for k, n in enumerate(names):
            g[n], delta[n], new_m[n], new_v[n] = (restore(n, a) for a in outs[4 * k:4 * k + 4])
        return outs[4 * len(names):]

    rc, r = update(("ffn2_w_gate", "ffn2_w_up", "ffn2_w_down"), "ffn2", _Together(to_chips, _SiblingExchange(s_d1)))
    after_chips("u1", rc)
    rc, = update(("ffn1_w_gate", "ffn1_w_up"), "ffn1_gate_up", after_sibling("d1", s_d1, r))
    after_chips("d1", rc)
    update(("ffn1_w_down",), "ffn1_down")

    rep_all = shared[:, :REP_ROWS].reshape(N_DEV * REP_ROWS, D_MODEL)
    loss = jnp.sum(shared[:, REP_ROWS, 0])

    def small_rows(name):
        o, n = s_offs[name]
        return mine_small[o:o + n]

    g.update({
        "ffn1_norm": rep_all[0], "mix_norm": rep_all[1], "ffn2_norm": rep_all[2], "final_norm": rep_all[3],
        "conv_dw_b": rep_all[4, :D_CONV], "conv_ln_g": rep_all[4, D_CONV:],
        "conv_ln_b": rep_all[5, :D_CONV], "pool_scale": rep_all[5, D_CONV:],
        "pool_w": rep_all[8:8 + pool_w.size // D_MODEL].reshape(pool_w.shape),
        "w_out": small_rows("wout"), "conv_pw": small_rows("pw").reshape(conv_pw.shape),
        "conv_dw": small_rows("cdw").reshape(-1)[:conv_dw.size].reshape(conv_dw.shape),
        "w_in": small_rows("win"),
    })
    others = [n for n in WEIGHTS if n not in slab_of]
    flat = _adamw_many([[_as_2d(a) for a in (wn, g[n], mn, vn)] for n in others for wn, mn, vn in [operands(n)]],
                       "adamw_small")
    for k, n in enumerate(others):
        delta[n], new_m[n], new_v[n] = (restore(n, a) for a in flat[3 * k:3 * k + 3])
    g["w_in"] = g["w_in"].T

    return (loss, dx0[None], *[g[n] for n in WEIGHTS], *[delta[n] for n in WEIGHTS],
            *[new_m[n] for n in WEIGHTS], *[new_v[n] for n in WEIGHTS])
```

```python
import functools

import jax
import jax.numpy as jnp
from jax import lax
from jax.experimental import pallas as pl
from jax.experimental.pallas import tpu as pltpu

F32 = jnp.float32
BF16 = jnp.bfloat16

D_MODEL = 1024
D_CONV = 512
D_POOL = 512
D_IN = 2 * D_CONV + D_POOL
POOL_WINDOWS = (2, 4, 8, 16)
POOL_GROUP = D_POOL // len(POOL_WINDOWS)
CONV_WIDTH = 31
RMS_EPS = 1e-6
LN_EPS = 1e-5
FFN_RES_WEIGHT = 0.5

ADAM_LR = 0.001
ADAM_B1 = 0.9
ADAM_B2 = 0.999
ADAM_EPS = 1e-08
ADAM_WD = 0.01
ADAM_STEP = 10

N_DEV = 8
MESH_ID = pl.DeviceIdType.MESH

SUBLANES = 8
TOKEN_TILE = 512
FFN_BWD_TILE = 256
FF_CHUNK = 256
HALO = 32
V7X_VMEM_LIMIT = 56 * 1024 * 1024
CDW_ROWS = 16
REP_ROWS = 16

WEIGHTS = ("ffn1_norm", "ffn1_w_gate", "ffn1_w_up", "ffn1_w_down", "mix_norm", "w_in", "conv_dw", "conv_dw_b",
           "conv_ln_g", "conv_ln_b", "conv_pw", "pool_w", "pool_scale", "w_out", "ffn2_norm", "ffn2_w_gate",
           "ffn2_w_up", "ffn2_w_down", "final_norm")


def _dot_nn(a, b):
    return lax.dot_general(a, b, (((1,), (0,)), ((), ())), preferred_element_type=F32)


def _dot_nt(a, b):
    return lax.dot_general(a, b, (((1,), (1,)), ((), ())), preferred_element_type=F32)


def _dot_tn(a, b):
    return lax.dot_general(a, b, (((0,), (0,)), ((), ())), preferred_element_type=F32)


def _rowsum8(v):
    r, c = v.shape
    return jnp.sum(v.reshape(r // SUBLANES, SUBLANES, c), axis=0)


def _fold8(ref):
    ref[0:1, :] = jnp.sum(ref[...], axis=0, keepdims=True)


def _row_tile(n, cap, mult):
    best = None
    for t in range(mult, min(n, cap) + 1, mult):
        if n % t == 0:
            best = t
    return n if best is None else best


def _params(n_grid):
    return pltpu.CompilerParams(dimension_semantics=("arbitrary",) * n_grid, vmem_limit_bytes=V7X_VMEM_LIMIT)


def _full(shape):
    return pl.BlockSpec(shape, lambda *_: (0,) * len(shape))


def _layout(pieces):
    offs, r = {}, 0
    for name, rows in pieces:
        offs[name] = (r, rows)
        r += rows
    return offs, r


HBM = pl.BlockSpec(memory_space=pl.ANY)


def _mesh_pos():
    return lax.axis_index("x"), lax.axis_index("y"), lax.axis_index("c")


def _remote(src, dst, send_sems, recv_sems, k, to):
    return pltpu.make_async_remote_copy(src_ref=src, dst_ref=dst, send_sem=send_sems.at[k], recv_sem=recv_sems.at[k],
                                        device_id=to, device_id_type=MESH_ID)


class _Gather:
    def __init__(self, shard):
        self.inputs = (shard,)
        self.out_shape = (jax.ShapeDtypeStruct((N_DEV, *shard.shape), shard.dtype),)
        self.scratch = (pltpu.SemaphoreType.DMA((7,)), pltpu.SemaphoreType.DMA((7,)), pltpu.SemaphoreType.DMA)

    def phases(self, ins, outs, scr):
        (x_ref,), (out_ref,), (send_sems, recv_sems, local_sem) = ins, outs, scr
        x, y, c = _mesh_pos()
        me, sibling = (x, y, c), (x, y, 1 - c)
        chips = [(1 - x, y), (x, 1 - y), (1 - x, 1 - y)]

        def block(px, py, pc):
            return out_ref.at[4 * px + 2 * py + pc]

        def copy(k, blk, to, src=None):
            return _remote(block(*blk) if src is None else src, block(*blk), send_sems, recv_sems, k, to)

        def mine():
            return pltpu.make_async_copy(x_ref, block(*me), local_sem)

        def first():
            return [copy(0, me, sibling, src=x_ref)] + [copy(1 + j, me, (*chip, c), src=x_ref) for j, chip in enumerate(chips)]

        def passed(j):
            return copy(4 + j, (*chips[j], c), sibling)

        def start():
            mine().start()
            for cp in first():
                cp.start()

        def forward():
            for j, chip in enumerate(chips):
                copy(1 + j, (*chip, c), me).wait_recv()
                passed(j).start()

        def finish():
            copy(0, sibling, me).wait_recv()
            for j, chip in enumerate(chips):
                copy(4 + j, (*chip, 1 - c), me).wait_recv()
            for cp in first() + [passed(j) for j in range(3)]:
                cp.wait_send()
            mine().wait()

        return [start, forward, finish]


class _SiblingExchange:
    def __init__(self, src):
        self.inputs = (src,)
        self.out_shape = (jax.ShapeDtypeStruct((4, *src.shape[1:]), src.dtype),)
        self.scratch = (pltpu.SemaphoreType.DMA((4,)), pltpu.SemaphoreType.DMA((4,)))

    def phases(self, ins, outs, scr):
        (g_ref,), (recv_ref,), (send_sems, recv_sems) = ins, outs, scr
        x, y, c = _mesh_pos()

        def copies():
            return [_remote(g_ref.at[2 * k + (1 - c)], recv_ref.at[k], send_sems, recv_sems, k, (x, y, 1 - c))
                    for k in range(4)]

        def start():
            for cp in copies():
                cp.start()

        def finish():
            for cp in copies():
                cp.wait()

        return [start, finish]


class _ChipsExchange:
    def __init__(self, src):
        self.inputs = (src,)
        self.out_shape = (jax.ShapeDtypeStruct(src.shape, src.dtype),)
        self.scratch = (pltpu.SemaphoreType.DMA((3,)), pltpu.SemaphoreType.DMA((3,)))

    def phases(self, ins, outs, scr):
        (p_ref,), (recv_ref,), (send_sems, recv_sems) = ins, outs, scr
        x, y, c = _mesh_pos()
        peers = [(1 - x, y, c), (x, 1 - y, c), (1 - x, 1 - y, c)]

        def copies():
            return [_remote(p_ref.at[k], recv_ref.at[k], send_sems, recv_sems, k, peer) for k, peer in enumerate(peers)]

        def start():
            for cp in copies():
                cp.start()

        def finish():
            for cp in copies():
                cp.wait()

        return [start, finish]


class _Together:
    def __init__(self, *plans):
        self.plans = plans
        self.inputs = tuple(a for p in plans for a in p.inputs)
        self.out_shape = tuple(o for p in plans for o in p.out_shape)
        self.scratch = tuple(s for p in plans for s in p.scratch)

    def phases(self, ins, outs, scr):
        each = []
        for p in self.plans:
            n_in, n_out, n_scr = len(p.inputs), len(p.out_shape), len(p.scratch)
            each.append(p.phases(ins[:n_in], outs[:n_out], scr[:n_scr]))
            ins, outs, scr = ins[n_in:], outs[n_out:], scr[n_scr:]

        def run(fns):
            def phase():
                for fn in fns:
                    fn()
            return phase

        middle = [fn for ph in each for fn in ph[1:-1]]
        return [run([ph[0] for ph in each]), *([run(middle)] if middle else []), run([ph[-1] for ph in each])]


def _run_comm(plan, name):
    n_in, n_out = len(plan.inputs), len(plan.out_shape)

    def body(*refs):
        for phase in plan.phases(refs[:n_in], refs[n_in:n_in + n_out], refs[n_in + n_out:]):
            phase()

    return pl.pallas_call(
        body, name=name, out_shape=list(plan.out_shape), in_specs=[HBM] * n_in, out_specs=[HBM] * n_out,
        scratch_shapes=list(plan.scratch))(*plan.inputs)


def _grid_call(body, *, name, nt, in_specs, out_specs, out_shape, scratch_shapes, args, plan=None):
    if plan is None:
        return pl.pallas_call(body, name=name, grid=(nt,), in_specs=in_specs, out_specs=out_specs, out_shape=out_shape,
                              scratch_shapes=scratch_shapes, compiler_params=_params(1))(*args)
    n_in, n_out, n_scr = len(in_specs), len(out_specs), len(scratch_shapes)
    p_in, p_out = len(plan.inputs), len(plan.out_shape)

    def with_plan(*refs):
        ins, refs = refs[:n_in], refs[n_in:]
        p_ins, refs = refs[:p_in], refs[p_in:]
        outs, refs = refs[:n_out], refs[n_out:]
        p_outs, refs = refs[:p_out], refs[p_out:]
        scr, p_scr = refs[:n_scr], refs[n_scr:]
        phases = plan.phases(p_ins, p_outs, p_scr)
        i = pl.program_id(0)
        pl.when(i == 0)(phases[0])
        for phase in phases[1:-1]:
            pl.when(i == min(max(nt - 3, 1), nt - 1))(phase)
        body(*ins, *outs, *scr)
        pl.when(i == nt - 1)(phases[-1])

    return pl.pallas_call(
        with_plan, name=name, grid=(nt,), in_specs=[*in_specs, *[HBM] * p_in], out_specs=[*out_specs, *[HBM] * p_out],
        out_shape=[*out_shape, *plan.out_shape], scratch_shapes=[*scratch_shapes, *plan.scratch],
        compiler_params=_params(1))(*args, *plan.inputs)


def _add_chunks(gslab, recv, gid, rid, name):
    _, rows, cols = gslab.shape

    def body(gid_ref, rid_ref, a_ref, b_ref, own_ref, part_ref):
        s = a_ref[...] + b_ref[...]

        @pl.when(pl.program_id(0) == 0)
        def _():
            own_ref[...] = s

        part_ref[...] = s.astype(BF16)

    blk = (1, rows, cols)
    grid_spec = pltpu.PrefetchScalarGridSpec(
        num_scalar_prefetch=2, grid=(4,),
        in_specs=[pl.BlockSpec(blk, lambda k, g, r: (g[k], 0, 0)), pl.BlockSpec(blk, lambda k, g, r: (r[k], 0, 0))],
        out_specs=[pl.BlockSpec(blk, lambda k, g, r: (0, 0, 0)),
                   pl.BlockSpec(blk, lambda k, g, r: (jnp.maximum(k - 1, 0), 0, 0))])
    return pl.pallas_call(
        body, name=name, grid_spec=grid_spec,
        out_shape=[jax.ShapeDtypeStruct(blk, F32), jax.ShapeDtypeStruct((3, rows, cols), BF16)],
        compiler_params=_params(1),
    )(gid, rid, gslab, recv)


def _sum_partials(own, recv, name):
    _, rows, cols = own.shape
    tr = _row_tile(rows, 1024, 16)

    def body(o_ref, r_ref, out_ref):
        acc = o_ref[0]
        for k in range(3):
            acc = acc + r_ref[k].astype(F32)
        out_ref[...] = acc

    return pl.pallas_call(
        body, name=name, grid=(rows // tr,),
        in_specs=[pl.BlockSpec((1, tr, cols), lambda i: (0, i, 0)), pl.BlockSpec((3, tr, cols), lambda i: (0, i, 0))],
        out_specs=pl.BlockSpec((tr, cols), lambda i: (i, 0)),
        out_shape=jax.ShapeDtypeStruct((rows, cols), F32),
        compiler_params=_params(1),
    )(own, recv)


def _chunk_ids():
    x, y, c = _mesh_pos()
    chips = [(x, y), (1 - x, y), (x, 1 - y), (1 - x, 1 - y)]
    gid = jnp.stack([4 * px + 2 * py + c for px, py in chips]).astype(jnp.int32)
    rid = jnp.stack([2 * px + py for px, py in chips]).astype(jnp.int32)
    return gid, rid


def _sibling_sums(slab, recv, tag):
    gid, rid = _chunk_ids()
    return _add_chunks(slab, recv, gid, rid, "rs_add_" + tag)


def _load_weights(slab_ref, offs, dsts, sems):
    cps = []
    for i, (off, dst) in enumerate(zip(offs, dsts)):
        f8 = dst.shape[0] // N_DEV
        cps += [pltpu.make_async_copy(slab_ref.at[j, pl.ds(off, f8), :], dst.at[pl.ds(j * f8, f8), :],
                                      sems.at[i * N_DEV + j]) for j in range(N_DEV)]
    for cp in cps:
        cp.start()
    for cp in cps:
        cp.wait()


def _chunk_rows(c):
    return pl.ds(pl.multiple_of(c * FF_CHUNK, FF_CHUNK), FF_CHUNK)


def _rms(xv):
    return lax.rsqrt(jnp.mean(xv * xv, axis=-1, keepdims=True) + RMS_EPS)


def _loss_terms(xv, tgt, gain):
    r = _rms(xv)
    xh = xv * r
    err = xh * gain - tgt
    loss = 0.5 * jnp.sum(jnp.mean(err * err, axis=-1, keepdims=True))
    dy = err * (1.0 / xv.shape[-1])
    dxh = dy * gain
    return loss, r * (dxh - xh * jnp.mean(dxh * xh, axis=-1, keepdims=True)), _rowsum8(dy * xh)


def _ffn_fwd(x, gain, slab, offs, f, name, plan=None, head=None):
    t, d = x.shape
    nc, tm = f // FF_CHUNK, TOKEN_TILE
    nt = t // tm
    n_head = 0 if head is None else 2

    def body(*refs):
        x_ref, gain_ref, slab_ref = refs[:3]
        xo_ref, g_ref, u_ref, n_ref = refs[3 + n_head:7 + n_head]
        wg_v, wu_v, wd_v, acc_ref, sems = refs[7 + 2 * n_head:]
        i = pl.program_id(0)

        @pl.when(i == 0)
        def _():
            _load_weights(slab_ref, offs, (wg_v, wu_v, wd_v), sems)

        xv = x_ref[...]
        n_ref[...] = ((xv * _rms(xv)) * gain_ref[...]).astype(BF16)
        acc_ref[...] = jnp.zeros_like(acc_ref)

        def chunk(c, carry):
            rows = _chunk_rows(c)
            nb = n_ref[...]
            g = _dot_nt(nb, wg_v[rows, :])
            u = _dot_nt(nb, wu_v[rows, :])
            g_ref[c] = g.astype(BF16)
            u_ref[c] = u.astype(BF16)
            h = (g * jax.nn.sigmoid(g)) * u
            acc_ref[...] += _dot_nn(h.astype(BF16), wd_v[rows, :])
            return carry

        lax.fori_loop(0, nc, chunk, 0, unroll=True)
        out = xv + FFN_RES_WEIGHT * acc_ref[...]
        if head is None:
            xo_ref[...] = out
            return
        tgt_ref, fgain_ref = refs[3:5]
        dgain_ref, loss_ref = refs[7 + n_head:7 + 2 * n_head]

        @pl.when(i == 0)
        def _():
            dgain_ref[...] = jnp.zeros_like(dgain_ref)
            loss_ref[...] = jnp.zeros_like(loss_ref)

        loss, dx, dgain = _loss_terms(out, tgt_ref[...], fgain_ref[...])
        xo_ref[...] = dx
        loss_ref[...] += loss
        dgain_ref[...] += dgain

        @pl.when(i == nt - 1)
        def _():
            _fold8(dgain_ref)

    tile = pl.BlockSpec((tm, d), lambda i: (i, 0))
    act = pl.BlockSpec((nc, tm, FF_CHUNK), lambda i: (0, i, 0))
    head_in = [] if head is None else [tile, _full((1, d))]
    head_out = [] if head is None else [_full((SUBLANES, d)), _full((SUBLANES, 128))]
    head_shape = [] if head is None else [jax.ShapeDtypeStruct((SUBLANES, d), F32), jax.ShapeDtypeStruct((SUBLANES, 128), F32)]
    return _grid_call(
        body, name=name, nt=nt, plan=plan,
        in_specs=[tile, _full((1, d)), HBM, *head_in],
        out_specs=[tile, act, act, tile, *head_out],
        out_shape=[jax.ShapeDtypeStruct((t, d), F32), jax.ShapeDtypeStruct((nc, t, FF_CHUNK), BF16),
                   jax.ShapeDtypeStruct((nc, t, FF_CHUNK), BF16), jax.ShapeDtypeStruct((t, d), BF16), *head_shape],
        scratch_shapes=[pltpu.VMEM((f, d), BF16), pltpu.VMEM((f, d), BF16), pltpu.VMEM((f, d), BF16),
                        pltpu.VMEM((tm, d), F32), pltpu.SemaphoreType.DMA((3 * N_DEV,))],
        args=(x, gain, slab, *([] if head is None else head)))


def _ffn_bwd(dxo, x, gain, gs, us, slab, offs, f, name, plan=None):
    t, d = x.shape
    nc, tm = f // FF_CHUNK, FFN_BWD_TILE
    nt = t // tm

    def body(dxo_ref, x_ref, gain_ref, g_ref, u_ref, slab_ref,
             dx_ref, h_ref, dg_ref, du_ref, df_ref, dgain_ref, wg_v, wu_v, wd_v, sems):
        i = pl.program_id(0)

        @pl.when(i == 0)
        def _():
            _load_weights(slab_ref, offs, (wg_v, wu_v, wd_v), sems)
            dgain_ref[...] = jnp.zeros_like(dgain_ref)

        df_ref[...] = (FFN_RES_WEIGHT * dxo_ref[...]).astype(BF16)

        def chunk(c):
            rows = pl.ds(c * FF_CHUNK, FF_CHUNK)
            g = g_ref[c].astype(F32)
            u = u_ref[c].astype(F32)
            sg = jax.nn.sigmoid(g)
            sil = g * sg
            dh = _dot_nt(df_ref[...], wd_v[rows, :])
            h_ref[c] = (sil * u).astype(BF16)
            du_ref[c] = (dh * sil).astype(BF16)
            dg_ref[c] = (dh * u * (sg * (1.0 + g * (1.0 - sg)))).astype(BF16)

        def back(c, dn):
            rows = pl.ds(c * FF_CHUNK, FF_CHUNK)
            return dn + _dot_nn(dg_ref[c], wg_v[rows, :]) + _dot_nn(du_ref[c], wu_v[rows, :])

        dn = jnp.zeros((tm, d), F32)
        for c in range(nc):
            chunk(c)
            if c:
                dn = back(c - 1, dn)
        dn = back(nc - 1, dn)
        xv = x_ref[...]
        r = _rms(xv)
        xh = xv * r
        dgain_ref[...] += _rowsum8(dn * xh)
        dxh = dn * gain_ref[...]
        dx_ref[...] = dxo_ref[...] + r * (dxh - xh * jnp.mean(dxh * xh, axis=-1, keepdims=True))

        @pl.when(i == nt - 1)
        def _():
            _fold8(dgain_ref)

    tile = pl.BlockSpec((tm, d), lambda i: (i, 0))
    act = pl.BlockSpec((nc, tm, FF_CHUNK), lambda i: (0, i, 0))
    act_shape = jax.ShapeDtypeStruct((nc, t, FF_CHUNK), BF16)
    return _grid_call(
        body, name=name, nt=nt, plan=plan,
        in_specs=[tile, tile, _full((1, d)), act, act, HBM],
        out_specs=[tile, act, act, act, tile, _full((SUBLANES, d))],
        out_shape=[jax.ShapeDtypeStruct((t, d), F32), act_shape, act_shape, act_shape,
                   jax.ShapeDtypeStruct((t, d), BF16), jax.ShapeDtypeStruct((SUBLANES, d), F32)],
        scratch_shapes=[pltpu.VMEM((f, d), BF16), pltpu.VMEM((f, d), BF16), pltpu.VMEM((f, d), BF16),
                        pltpu.SemaphoreType.DMA((3 * N_DEV,))],
        args=(dxo, x, gain, gs, us, slab))


def _tn_chunked(a, b, name, plan=None):
    nc, t, _ = a.shape
    n = b.shape[1]
    tb = _row_tile(t, 1024, TOKEN_TILE)

    def body(a_ref, b_ref, o_ref):
        @pl.when(pl.program_id(0) == 0)
        def _():
            o_ref[...] = jnp.zeros_like(o_ref)

        def chunk(c, carry):
            rows = _chunk_rows(c)
            o_ref[rows, :] += _dot_tn(a_ref[c], b_ref[...])
            return carry

        lax.fori_loop(0, nc, chunk, 0, unroll=True)

    return _grid_call(
        body, name=name, nt=t // tb, plan=plan,
        in_specs=[pl.BlockSpec((nc, tb, FF_CHUNK), lambda i: (0, i, 0)), pl.BlockSpec((tb, n), lambda i: (i, 0))],
        out_specs=[_full((nc * FF_CHUNK, n))],
        out_shape=[jax.ShapeDtypeStruct((nc * FF_CHUNK, n), F32)],
        scratch_shapes=[], args=(a, b))


def _tn(a, b, name, plan=None):
    t, k = a.shape
    n = b.shape[1]
    tb = _row_tile(t, 1024, TOKEN_TILE)

    def body(a_ref, b_ref, o_ref):
        @pl.when(pl.program_id(0) == 0)
        def _():
            o_ref[...] = jnp.zeros_like(o_ref)

        o_ref[...] += _dot_tn(a_ref[...].astype(BF16), b_ref[...].astype(BF16))

    return _grid_call(
        body, name=name, nt=t // tb, plan=plan,
        in_specs=[pl.BlockSpec((tb, k), lambda i: (i, 0)), pl.BlockSpec((tb, n), lambda i: (i, 0))],
        out_specs=[_full((k, n))],
        out_shape=[jax.ShapeDtypeStruct((k, n), F32)],
        scratch_shapes=[], args=(a, b))


def _layernorm_stats(u1):
    mu = jnp.mean(u1, axis=-1, keepdims=True)
    xc = u1 - mu
    rstd = lax.rsqrt(jnp.mean(xc * xc, axis=-1, keepdims=True) + LN_EPS)
    return xc * rstd, rstd


def _positions(tile_index, tm):
    return (tile_index * tm + lax.broadcasted_iota(jnp.int32, (tm, 1), 0)).astype(F32)


def _shifted_taps(src_ref, sh_ref, tm, offset_of):
    groups = {}
    for k in range(CONV_WIDTH):
        groups.setdefault(offset_of(k) % SUBLANES, []).append(k)
    span = tm + HALO - SUBLANES
    for rem, taps in sorted(groups.items()):
        if rem:
            sh_ref[0:span, :] = src_ref[rem:rem + span, :]
        ref = sh_ref if rem else src_ref
        for k in taps:
            base = offset_of(k) - rem
            yield k, ref[base:base + tm, :]


def _mix_fwd(x, gm, win_t, cdw, cb, lg, lb, pw, poolw, ps, wout):
    t, d = x.shape
    tm = TOKEN_TILE

    def body(x_ref, gm_ref, win_ref, cdw_ref, cb_ref, lg_ref, lb_ref, pw_ref, poolw_ref, ps_ref, wout_ref,
             xo_ref, h_ref, ag_ref, u0_ref, u1_ref, u2_ref, mixed_ref, cat_ref, eu_ref, ep_ref, sh_ref):
        i = pl.program_id(0)

        @pl.when(i == 0)
        def _():
            eu_ref[0:HALO, :] = jnp.zeros((HALO, D_CONV), F32)
            ep_ref[0:HALO, :] = jnp.zeros((HALO, D_POOL), F32)

        @pl.when(i > 0)
        def _():
            eu_ref[0:HALO, :] = eu_ref[tm:tm + HALO, :]
            ep_ref[0:HALO, :] = ep_ref[tm:tm + HALO, :]

        xv = x_ref[...]
        hb = ((xv * _rms(xv)) * gm_ref[...]).astype(BF16)
        h_ref[...] = hb
        proj = _dot_nt(hb, win_ref[...])
        a = proj[:, :D_CONV]
        g = proj[:, D_CONV:2 * D_CONV]
        ag_ref[...] = proj[:, :2 * D_CONV]
        u0 = a * jax.nn.sigmoid(g)
        u0_ref[...] = u0
        eu_ref[HALO:HALO + tm, :] = u0
        ep_ref[HALO:HALO + tm, :] = proj[:, 2 * D_CONV:]

        u1 = jnp.broadcast_to(cb_ref[...], (tm, D_CONV))
        for k, rows in _shifted_taps(eu_ref, sh_ref, tm, lambda k: HALO - (CONV_WIDTH - 1) + k):
            u1 = u1 + cdw_ref[k:k + 1, :] * rows
        u1_ref[...] = u1
        lnh, _ = _layernorm_stats(u1)
        ln = lnh * lg_ref[...] + lb_ref[...]
        u2 = (ln * jax.nn.sigmoid(ln)).astype(BF16)
        u2_ref[...] = u2
        conv_out = _dot_nn(u2, pw_ref[...])

        pos = _positions(i, tm)
        outs = []
        for gi, w in enumerate(POOL_WINDOWS):
            lo = gi * POOL_GROUP
            p = ep_ref[HALO:HALO + tm, lo:lo + POOL_GROUP]
            s = p
            for j in range(1, w):
                s = s + ep_ref[HALO - j:HALO - j + tm, lo:lo + POOL_GROUP]
            mixed = (s / jnp.minimum(pos + 1.0, float(w)) - p).astype(BF16)
            mixed_ref[:, lo:lo + POOL_GROUP] = mixed
            outs.append(_dot_nn(mixed, poolw_ref[gi]))
        pool_out = jnp.concatenate(outs, axis=-1) * ps_ref[...]
        cat = jnp.concatenate([conv_out, pool_out], axis=-1).astype(BF16)
        cat_ref[...] = cat
        xo_ref[...] = xv + _dot_nn(cat, wout_ref[...])

    def tile(c):
        return pl.BlockSpec((tm, c), lambda i: (i, 0))

    def out(c, dt):
        return jax.ShapeDtypeStruct((t, c), dt)

    return pl.pallas_call(
        body, name="mix_fwd", grid=(t // tm,),
        in_specs=[tile(d), _full((1, d)), _full((D_IN, d)), _full((HALO, D_CONV)), _full((1, D_CONV)),
                  _full((1, D_CONV)), _full((1, D_CONV)), _full((D_CONV, D_CONV)),
                  _full((len(POOL_WINDOWS), POOL_GROUP, POOL_GROUP)), _full((1, D_POOL)), _full((d, d))],
        out_specs=[tile(d), tile(d), tile(2 * D_CONV), tile(D_CONV), tile(D_CONV), tile(D_CONV), tile(D_POOL), tile(d)],
        out_shape=[out(d, F32), out(d, BF16), out(2 * D_CONV, F32), out(D_CONV, F32), out(D_CONV, F32),
                   out(D_CONV, BF16), out(D_POOL, BF16), out(d, BF16)],
        scratch_shapes=[pltpu.VMEM((HALO + tm, D_CONV), F32), pltpu.VMEM((HALO + tm, D_POOL), F32),
                        pltpu.VMEM((HALO + tm, D_CONV), F32)],
        compiler_params=_params(1),
    )(x, gm, win_t, cdw, cb, lg, lb, pw, poolw, ps, wout)


def _mix_bwd(dxo, x, gm, ag, u0, u1, mixed, win_t, cdw, lg, lb, pw, poolw, ps, wout, plan=None):
    t, d = x.shape
    tm = TOKEN_TILE
    nt = t // tm
    halo_blocks = tm // HALO

    def body(dxo_ref, x_ref, gm_ref, ag_ref, u0_ref, u0h_ref, u1_ref, mixed_ref,
             win_ref, cdw_ref, lg_ref, lb_ref, pw_ref, poolw_ref, ps_ref, wout_ref,
             dx_ref, dproj_ref, dco_ref, dgm_ref, dcdw_ref, dcb_ref, dlg_ref, dlb_ref, dpoolw_ref, dps_ref,
             eu_ref, ed_ref, eq_ref, sh_ref):
        i = pl.program_id(0)
        ti = nt - 1 - i

        @pl.when(i == 0)
        def _():
            for ref in (dgm_ref, dcdw_ref, dcb_ref, dlg_ref, dlb_ref, dpoolw_ref, dps_ref):
                ref[...] = jnp.zeros_like(ref)
            ed_ref[tm:tm + HALO, :] = jnp.zeros((HALO, D_CONV), F32)
            eq_ref[tm:tm + HALO, :] = jnp.zeros((HALO, D_POOL), F32)

        @pl.when(i > 0)
        def _():
            ed_ref[tm:tm + HALO, :] = ed_ref[0:HALO, :]
            eq_ref[tm:tm + HALO, :] = eq_ref[0:HALO, :]

        @pl.when(ti == 0)
        def _():
            eu_ref[0:HALO, :] = jnp.zeros((HALO, D_CONV), F32)

        @pl.when(ti > 0)
        def _():
            eu_ref[0:HALO, :] = u0h_ref[...]

        eu_ref[HALO:HALO + tm, :] = u0_ref[...]

        dxo = dxo_ref[...]
        dcat = _dot_nt(dxo.astype(BF16), wout_ref[...])
        dco = dcat[:, :D_CONV].astype(BF16)
        dco_ref[...] = dco
        dpo = dcat[:, D_CONV:]

        lnh, rstd = _layernorm_stats(u1_ref[...])
        ln = lnh * lg_ref[...] + lb_ref[...]
        sl = jax.nn.sigmoid(ln)
        dln = _dot_nt(dco, pw_ref[...]) * (sl * (1.0 + ln * (1.0 - sl)))
        dlg_ref[...] += _rowsum8(dln * lnh)
        dlb_ref[...] += _rowsum8(dln)
        dlnh = dln * lg_ref[...]
        du1 = rstd * (dlnh - jnp.mean(dlnh, axis=-1, keepdims=True)
                      - lnh * jnp.mean(dlnh * lnh, axis=-1, keepdims=True))
        dcb_ref[...] += _rowsum8(du1)
        ed_ref[0:tm, :] = du1

        du0 = jnp.zeros((tm, D_CONV), F32)
        for k, rows in _shifted_taps(ed_ref, sh_ref, tm, lambda k: CONV_WIDTH - 1 - k):
            du0 = du0 + cdw_ref[k:k + 1, :] * rows
        for k, rows in _shifted_taps(eu_ref, sh_ref, tm, lambda k: HALO - (CONV_WIDTH - 1) + k):
            dcdw_ref[SUBLANES * k:SUBLANES * (k + 1), :] += _rowsum8(du1 * rows)
        a = ag_ref[:, :D_CONV]
        sg = jax.nn.sigmoid(ag_ref[:, D_CONV:])
        pieces = [du0 * sg, du0 * a * (sg * (1.0 - sg))]

        pos = _positions(ti, tm)
        for gi, w in enumerate(POOL_WINDOWS):
            lo = gi * POOL_GROUP
            mg = mixed_ref[:, lo:lo + POOL_GROUP]
            dpo_g = dpo[:, lo:lo + POOL_GROUP]
            dps_ref[:, lo:lo + POOL_GROUP] += _rowsum8(dpo_g * _dot_nn(mg, poolw_ref[gi]))
            dout = (dpo_g * ps_ref[:, lo:lo + POOL_GROUP]).astype(BF16)
            dpoolw_ref[gi] += _dot_tn(mg, dout)
            dmx = _dot_nt(dout, poolw_ref[gi])
            q = dmx / jnp.minimum(pos + 1.0, float(w))
            eq_ref[0:tm, lo:lo + POOL_GROUP] = q
            s = q
            for j in range(1, w):
                s = s + eq_ref[j:j + tm, lo:lo + POOL_GROUP]
            pieces.append(s - dmx)
        dproj = jnp.concatenate(pieces, axis=-1).astype(BF16)
        dproj_ref[...] = dproj

        dh = _dot_nn(dproj, win_ref[...])
        xv = x_ref[...]
        r = _rms(xv)
        xh = xv * r
        dgm_ref[...] += _rowsum8(dh * xh)
        dxh = dh * gm_ref[...]
        dx_ref[...] = dxo + r * (dxh - xh * jnp.mean(dxh * xh, axis=-1, keepdims=True))

        @pl.when(i == nt - 1)
        def _():
            for ref in (dgm_ref, dcb_ref, dlg_ref, dlb_ref, dps_ref):
                _fold8(ref)
            for k in range(CONV_WIDTH):
                dcdw_ref[SUBLANES * k:SUBLANES * k + 1, :] = jnp.sum(
                    dcdw_ref[SUBLANES * k:SUBLANES * (k + 1), :], axis=0, keepdims=True)

    def tile(c):
        return pl.BlockSpec((tm, c), lambda i: (nt - 1 - i, 0))

    halo = pl.BlockSpec((HALO, D_CONV), lambda i: (jnp.maximum((nt - 1 - i) * halo_blocks - 1, 0), 0))
    n_groups = len(POOL_WINDOWS)
    return _grid_call(
        body, name="mix_bwd", nt=nt, plan=plan,
        in_specs=[tile(d), tile(d), _full((1, d)), tile(2 * D_CONV), tile(D_CONV), halo, tile(D_CONV), tile(D_POOL),
                  _full((D_IN, d)), _full((HALO, D_CONV)), _full((1, D_CONV)), _full((1, D_CONV)),
                  _full((D_CONV, D_CONV)), _full((n_groups, POOL_GROUP, POOL_GROUP)), _full((1, D_POOL)), _full((d, d))],
        out_specs=[tile(d), tile(D_IN), tile(D_CONV), _full((SUBLANES, d)), _full((HALO * SUBLANES, D_CONV)),
                   _full((SUBLANES, D_CONV)), _full((SUBLANES, D_CONV)), _full((SUBLANES, D_CONV)),
                   _full((n_groups, POOL_GROUP, POOL_GROUP)), _full((SUBLANES, D_POOL))],
        out_shape=[jax.ShapeDtypeStruct((t, d), F32), jax.ShapeDtypeStruct((t, D_IN), BF16),
                   jax.ShapeDtypeStruct((t, D_CONV), BF16), jax.ShapeDtypeStruct((SUBLANES, d), F32),
                   jax.ShapeDtypeStruct((HALO * SUBLANES, D_CONV), F32), jax.ShapeDtypeStruct((SUBLANES, D_CONV), F32),
                   jax.ShapeDtypeStruct((SUBLANES, D_CONV), F32), jax.ShapeDtypeStruct((SUBLANES, D_CONV), F32),
                   jax.ShapeDtypeStruct((n_groups, POOL_GROUP, POOL_GROUP), F32),
                   jax.ShapeDtypeStruct((SUBLANES, D_POOL), F32)],
        scratch_shapes=[pltpu.VMEM((HALO + tm, D_CONV), F32), pltpu.VMEM((tm + HALO, D_CONV), F32),
                        pltpu.VMEM((tm + HALO, D_POOL), F32), pltpu.VMEM((tm + HALO, D_CONV), F32)],
        args=(dxo, x, gm, ag, u0, u0, u1, mixed, win_t, cdw, lg, lb, pw, poolw, ps, wout))


def _adam_step(gv, w_ref, m_ref, v_ref, d_ref, nm_ref, nv_ref):
    nm = ADAM_B1 * m_ref[...] + (1.0 - ADAM_B1) * gv
    nv = ADAM_B2 * v_ref[...] + (1.0 - ADAM_B2) * (gv * gv)
    m_hat = nm / (1.0 - ADAM_B1 ** ADAM_STEP)
    v_hat = nv / (1.0 - ADAM_B2 ** ADAM_STEP)
    d_ref[...] = -ADAM_LR * (m_hat / (jnp.sqrt(v_hat) + ADAM_EPS) + ADAM_WD * w_ref[...])
    nm_ref[...] = nm
    nv_ref[...] = nv


def _adam_tile(rows, cols):
    return _row_tile(rows, max(SUBLANES, (256 * 1024) // cols // SUBLANES * SUBLANES), SUBLANES)


def _adamw_many(items, name):
    n = len(items)

    def body(*refs):
        ins, outs = refs[:4 * n], refs[4 * n:]
        for k in range(n):
            w_ref, g_ref, m_ref, v_ref = ins[4 * k:4 * k + 4]
            _adam_step(g_ref[...], w_ref, m_ref, v_ref, *outs[3 * k:3 * k + 3])

    return pl.pallas_call(
        body, name=name, out_shape=[jax.ShapeDtypeStruct(it[0].shape, F32) for it in items for _ in range(3)],
        compiler_params=_params(0))(*[a for it in items for a in it])


def _adamw_reduced(w, own, recv, m, v, name):
    rows, cols = w.shape
    tr = _adam_tile(rows, cols)

    def body(w_ref, o_ref, r_ref, m_ref, v_ref, g_ref, d_ref, nm_ref, nv_ref):
        gv = o_ref[0]
        for k in range(3):
            gv = gv + r_ref[k].astype(F32)
        g_ref[...] = gv
        _adam_step(gv, w_ref, m_ref, v_ref, d_ref, nm_ref, nv_ref)

    blk = pl.BlockSpec((tr, cols), lambda i: (i, 0))
    shape = jax.ShapeDtypeStruct((rows, cols), F32)
    return pl.pallas_call(
        body, name=name, grid=(rows // tr,),
        in_specs=[blk, pl.BlockSpec((1, tr, cols), lambda i: (0, i, 0)), pl.BlockSpec((3, tr, cols), lambda i: (0, i, 0)),
                  blk, blk],
        out_specs=[blk] * 4, out_shape=[shape] * 4,
        compiler_params=_params(1),
    )(w, own, recv, m, v)


def _as_2d(a):
    if a.ndim == 1:
        return a.reshape(a.shape[0] // 128, 128)
    if a.ndim == 3:
        return a.reshape(a.shape[0] * a.shape[1], a.shape[2])
    return a


def _pack_weight_slabs(p):
    def bf(parts):
        return [a.astype(BF16) for a in parts]

    cdw_bits = lax.bitcast_convert_type(p["conv_dw"], BF16).reshape(CONV_WIDTH, 2 * D_CONV // N_DEV)
    cdw_bits = jnp.pad(cdw_bits, ((0, 1), (0, 0))).reshape(4, D_MODEL)
    cdw_bits = jnp.pad(cdw_bits, ((0, CDW_ROWS - 4), (0, 0)))
    first = bf([p["ffn1_w_gate"].T, p["ffn1_w_up"].T, p["ffn1_w_down"]])
    rest = bf([p["ffn2_w_gate"].T, p["ffn2_w_up"].T, p["ffn2_w_down"], p["w_in"].T, p["w_out"],
               p["conv_pw"].reshape(D_CONV // N_DEV // 2, D_MODEL)]) + [cdw_bits]
    return jnp.concatenate(first, axis=0), jnp.concatenate(rest, axis=0)


def _unpack_rows(slab, offs, names):
    out = {}
    for name in names:
        o, n = offs[name]
        out[name] = slab[:, o:o + n, :].reshape(N_DEV * n, D_MODEL)
    return out


def _unpack_conv_taps(slab, offs):
    o, _ = offs["cdw"]
    bits = slab[:, o:o + 4, :].reshape(N_DEV, CONV_WIDTH + 1, D_CONV // N_DEV, 2)[:, :CONV_WIDTH]
    cdw = lax.bitcast_convert_type(bits, F32)
    return jnp.transpose(cdw, (1, 0, 2)).reshape(CONV_WIDTH, D_CONV)


def kernel(x, ffn1_norm, ffn1_w_gate, ffn1_w_up, ffn1_w_down, mix_norm, w_in, conv_dw, conv_dw_b, conv_ln_g, conv_ln_b, conv_pw, pool_w, pool_scale, w_out, ffn2_norm, ffn2_w_gate, ffn2_w_up, ffn2_w_down, final_norm, loss_target, m_ffn1_norm, m_ffn1_w_gate, m_ffn1_w_up, m_ffn1_w_down, m_mix_norm, m_w_in, m_conv_dw, m_conv_dw_b, m_conv_ln_g, m_conv_ln_b, m_conv_pw, m_pool_w, m_pool_scale, m_w_out, m_ffn2_norm, m_ffn2_w_gate, m_ffn2_w_up, m_ffn2_w_down, m_final_norm, v_ffn1_norm, v_ffn1_w_gate, v_ffn1_w_up, v_ffn1_w_down, v_mix_norm, v_w_in, v_conv_dw, v_conv_dw_b, v_conv_ln_g, v_conv_ln_b, v_conv_pw, v_pool_w, v_pool_scale, v_w_out, v_ffn2_norm, v_ffn2_w_gate, v_ffn2_w_up, v_ffn2_w_down, v_final_norm):
    given = dict(locals())
    p = {n: given[n] for n in WEIGHTS}
    f8 = ffn1_w_gate.shape[1]
    f = N_DEV * f8
    ffn_rows = (0, f8, 2 * f8)
    small = (("win", D_IN // N_DEV), ("wout", D_MODEL // N_DEV), ("pw", D_CONV // N_DEV // 2), ("cdw", CDW_ROWS))
    w_offs, _ = _layout((("g2", f8), ("u2", f8), ("d2", f8)) + small)
    s_offs, _ = _layout(small + (("rep", REP_ROWS),))
    x0 = x[0]
    target = loss_target[0]

    def row(vec):
        return vec.reshape(1, vec.shape[0])

    slab_first, slab_rest = _pack_weight_slabs(p)
    w_first = _run_comm(_Gather(slab_first), "gather_ffn1")[0]
    x1, g1s, u1s, n1, w_rest = _ffn_fwd(x0, row(ffn1_norm), w_first, ffn_rows, f, "ffn1_fwd", _Gather(slab_rest))
    w = _unpack_rows(w_rest, w_offs, ("win", "wout", "pw"))
    w["pw"] = w["pw"].reshape(D_CONV, D_CONV)
    cdw = jnp.pad(_unpack_conv_taps(w_rest, w_offs), ((0, HALO - CONV_WIDTH), (0, 0)))
    poolw = pool_w.astype(BF16)

    x2, h, ag, u0, u1, u2, mixed, cat = _mix_fwd(
        x1, row(mix_norm), w["win"], cdw, row(conv_dw_b), row(conv_ln_g), row(conv_ln_b), w["pw"], poolw,
        row(pool_scale), w["wout"])
    dx3, g2s, u2s, n2, d_final_norm, loss_part = _ffn_fwd(
        x2, row(ffn2_norm), w_rest, ffn_rows, f, "ffn2_fwd", head=(target, row(final_norm)))

    pending, reduced = {}, {}

    def chunks(a):
        return a.reshape(N_DEV, -1, D_MODEL)

    def after_sibling(name, slab, recv):
        pending[name], part = _sibling_sums(slab, recv, name)
        return _ChipsExchange(part)

    def after_chips(name, recv):
        reduced[name] = (pending.pop(name), recv)

    dx2, h2, dg2, du2, df2, d_ffn2_norm = _ffn_bwd(dx3, x2, row(ffn2_norm), g2s, u2s, w_rest, ffn_rows, f, "ffn2_bwd")
    s_g2 = chunks(_tn_chunked(dg2, n2, "ffn2_dgate")[0])
    s_u2, r = _tn_chunked(du2, n2, "ffn2_dup", _SiblingExchange(s_g2))
    s_u2 = chunks(s_u2)
    to_chips = after_sibling("g2", s_g2, r)
    s_d2, rc, r = _tn_chunked(h2, df2, "ffn2_ddown", _Together(to_chips, _SiblingExchange(s_u2)))
    s_d2 = chunks(s_d2)
    after_chips("g2", rc)
    to_chips = after_sibling("u2", s_u2, r)
    dx1, dproj, dco, d_mix_norm, d_cdw, d_cb, d_lg, d_lb, d_poolw, d_ps, rc, r = _mix_bwd(
        dx2, x1, row(mix_norm), ag, u0, u1, mixed, w["win"], cdw, row(conv_ln_g), row(conv_ln_b), w["pw"], poolw,
        row(pool_scale), w["wout"], _Together(to_chips, _SiblingExchange(s_d2)))
    after_chips("u2", rc)
    to_chips = after_sibling("d2", s_d2, r)
    d_win, rc = _tn(dproj, h, "mix_dwin", to_chips)
    after_chips("d2", rc)
    d_wout = _tn(cat, dx2, "mix_dwout")[0]
    d_pw = _tn(u2, dco, "mix_dpw")[0]
    dx0, h1, dg1, du1, df1, d_ffn1_norm = _ffn_bwd(dx1, x0, row(ffn1_norm), g1s, u1s, w_first, ffn_rows, f, "ffn1_bwd")

    d_cdw = d_cdw.reshape(HALO, SUBLANES, D_CONV)[:CONV_WIDTH, 0]
    d_cdw = jnp.transpose(d_cdw.reshape(CONV_WIDTH, N_DEV, D_CONV // N_DEV), (1, 0, 2)).reshape(N_DEV, -1)
    d_cdw = jnp.pad(d_cdw, ((0, 0), (0, CDW_ROWS * D_MODEL - d_cdw.shape[1]))).reshape(N_DEV, CDW_ROWS, D_MODEL)
    rep = jnp.concatenate([
        d_ffn1_norm[0:1], d_mix_norm[0:1], d_ffn2_norm[0:1], d_final_norm[0:1],
        jnp.concatenate([d_cb[0:1], d_lg[0:1]], axis=1), jnp.concatenate([d_lb[0:1], d_ps[0:1]], axis=1),
        jnp.zeros((2, D_MODEL), F32), d_poolw.reshape(-1, D_MODEL)], axis=0)
    rep = jnp.pad(rep, ((0, N_DEV * REP_ROWS - rep.shape[0]), (0, 0))).reshape(N_DEV, REP_ROWS, D_MODEL)
    s_small = jnp.concatenate([chunks(d_win), chunks(d_wout), chunks(d_pw), d_cdw, rep], axis=1)

    s_g1, r = _tn_chunked(dg1, n1, "ffn1_dgate", _SiblingExchange(s_small))
    s_g1 = chunks(s_g1)
    to_chips = after_sibling("small", s_small, r)
    s_u1, rc, r = _tn_chunked(du1, n1, "ffn1_dup", _Together(to_chips, _SiblingExchange(s_g1)))
    s_u1 = chunks(s_u1)
    mine_small = _sum_partials(pending.pop("small"), rc, "rs_sum_small")
    to_chips = after_sibling("g1", s_g1, r)
    o_rep, _ = s_offs["rep"]
    loss_rows = jnp.pad(loss_part, ((0, 0), (0, D_MODEL - loss_part.shape[1])))
    share = _Gather(jnp.concatenate([mine_small[o_rep:o_rep + REP_ROWS], loss_rows], axis=0))
    s_d1, rc, r, shared = _tn_chunked(h1, df1, "ffn1_ddown", _Together(to_chips, _SiblingExchange(s_u1), share))
    s_d1 = chunks(s_d1)
    after_chips("g1", rc)
    to_chips = after_sibling("u1", s_u1, r)
    rc, r = _run_comm(_Together(to_chips, _SiblingExchange(s_d1)), "rs_tail_up")
    after_chips("u1", rc)
    rc, = _run_comm(after_sibling("d1", s_d1, r), "rs_tail_down")
    after_chips("d1", rc)

    rep_all = shared[:, :REP_ROWS].reshape(N_DEV * REP_ROWS, D_MODEL)
    loss = jnp.sum(shared[:, REP_ROWS, 0])

    def small_rows(name):
        o, n = s_offs[name]
        return mine_small[o:o + n]

    g = {
        "ffn1_norm": rep_all[0], "mix_norm": rep_all[1], "ffn2_norm": rep_all[2], "final_norm": rep_all[3],
        "conv_dw_b": rep_all[4, :D_CONV], "conv_ln_g": rep_all[4, D_CONV:],
        "conv_ln_b": rep_all[5, :D_CONV], "pool_scale": rep_all[5, D_CONV:],
        "pool_w": rep_all[8:8 + pool_w.size // D_MODEL].reshape(pool_w.shape),
        "w_out": small_rows("wout"), "conv_pw": small_rows("pw").reshape(conv_pw.shape),
        "conv_dw": small_rows("cdw").reshape(-1)[:conv_dw.size].reshape(conv_dw.shape),
    }

    slab_of = {"ffn1_w_gate": "g1", "ffn1_w_up": "u1", "ffn1_w_down": "d1",
               "ffn2_w_gate": "g2", "ffn2_w_up": "u2", "ffn2_w_down": "d2"}
    transposed = ("ffn1_w_gate", "ffn1_w_up", "ffn2_w_gate", "ffn2_w_up", "w_in")
    g["w_in"] = small_rows("win")
    delta, new_m, new_v = {}, {}, {}

    def operands(n):
        wmv = [given[k] for k in (n, "m_" + n, "v_" + n)]
        return [a.T for a in wmv] if n in transposed else wmv

    def restore(n, a):
        return a.T if n in transposed else a.reshape(p[n].shape)

    others = [n for n in WEIGHTS if n not in slab_of]
    flat = _adamw_many([[_as_2d(a) for a in (wn, g[n], mn, vn)] for n in others for wn, mn, vn in [operands(n)]],
                       "adamw_small")
    for k, n in enumerate(others):
        delta[n], new_m[n], new_v[n] = (restore(n, a) for a in flat[3 * k:3 * k + 3])
    g["w_in"] = g["w_in"].T
    for n, slab in slab_of.items():
        wn, mn, vn = operands(n)
        outs = _adamw_reduced(wn, *reduced[slab], mn, vn, "adamw_" + n)
        g[n], delta[n], new_m[n], new_v[n] = (restore(n, a) for a in outs)

    return (loss, dx0[None], *[g[n] for n in WEIGHTS], *[delta[n] for n in WEIGHTS],
            *[new_m[n] for n in WEIGHTS], *[new_v[n] for n in WEIGHTS])
```

```python
import functools

import jax
import jax.numpy as jnp
from jax import lax
from jax.experimental import pallas as pl
from jax.experimental.pallas import tpu as pltpu

F32 = jnp.float32
BF16 = jnp.bfloat16

D_MODEL = 1024
D_CONV = 512
D_POOL = 512
D_IN = 2 * D_CONV + D_POOL
POOL_WINDOWS = (2, 4, 8, 16)
POOL_GROUP = D_POOL // len(POOL_WINDOWS)
CONV_WIDTH = 31
RMS_EPS = 1e-6
LN_EPS = 1e-5
FFN_RES_WEIGHT = 0.5

ADAM_LR = 0.001
ADAM_B1 = 0.9
ADAM_B2 = 0.999
ADAM_EPS = 1e-08
ADAM_WD = 0.01
ADAM_STEP = 10

N_DEV = 8
MESH_ID = pl.DeviceIdType.MESH

SUBLANES = 8
TOKEN_TILE = 512
FFN_BWD_TILE = 256
FF_CHUNK = 256
HALO = 32
V7X_VMEM_LIMIT = 56 * 1024 * 1024
CDW_ROWS = 16
REP_ROWS = 16

WEIGHTS = ("ffn1_norm", "ffn1_w_gate", "ffn1_w_up", "ffn1_w_down", "mix_norm", "w_in", "conv_dw", "conv_dw_b",
           "conv_ln_g", "conv_ln_b", "conv_pw", "pool_w", "pool_scale", "w_out", "ffn2_norm", "ffn2_w_gate",
           "ffn2_w_up", "ffn2_w_down", "final_norm")


def _dot_nn(a, b):
    return lax.dot_general(a, b, (((1,), (0,)), ((), ())), preferred_element_type=F32)


def _dot_nt(a, b):
    return lax.dot_general(a, b, (((1,), (1,)), ((), ())), preferred_element_type=F32)


def _dot_tn(a, b):
    return lax.dot_general(a, b, (((0,), (0,)), ((), ())), preferred_element_type=F32)


def _rowsum8(v):
    r, c = v.shape
    return jnp.sum(v.reshape(r // SUBLANES, SUBLANES, c), axis=0)


def _fold8(ref):
    ref[0:1, :] = jnp.sum(ref[...], axis=0, keepdims=True)


def _row_tile(n, cap, mult):
    best = None
    for t in range(mult, min(n, cap) + 1, mult):
        if n % t == 0:
            best = t
    return n if best is None else best


def _params(n_grid):
    return pltpu.CompilerParams(dimension_semantics=("arbitrary",) * n_grid, vmem_limit_bytes=V7X_VMEM_LIMIT)


def _full(shape):
    return pl.BlockSpec(shape, lambda *_: (0,) * len(shape))


def _layout(pieces):
    offs, r = {}, 0
    for name, rows in pieces:
        offs[name] = (r, rows)
        r += rows
    return offs, r


HBM = pl.BlockSpec(memory_space=pl.ANY)


def _mesh_pos():
    return lax.axis_index("x"), lax.axis_index("y"), lax.axis_index("c")


def _remote(src, dst, send_sems, recv_sems, k, to):
    return pltpu.make_async_remote_copy(src_ref=src, dst_ref=dst, send_sem=send_sems.at[k], recv_sem=recv_sems.at[k],
                                        device_id=to, device_id_type=MESH_ID)


class _Gather:
    def __init__(self, shard):
        self.inputs = (shard,)
        self.out_shape = (jax.ShapeDtypeStruct((N_DEV, *shard.shape), shard.dtype),)
        self.scratch = (pltpu.SemaphoreType.DMA((7,)), pltpu.SemaphoreType.DMA((7,)), pltpu.SemaphoreType.DMA)

    def phases(self, ins, outs, scr):
        (x_ref,), (out_ref,), (send_sems, recv_sems, local_sem) = ins, outs, scr
        x, y, c = _mesh_pos()
        me, sibling = (x, y, c), (x, y, 1 - c)
        chips = [(1 - x, y), (x, 1 - y), (1 - x, 1 - y)]

        def block(px, py, pc):
            return out_ref.at[4 * px + 2 * py + pc]

        def copy(k, blk, to, src=None):
            return _remote(block(*blk) if src is None else src, block(*blk), send_sems, recv_sems, k, to)

        def mine():
            return pltpu.make_async_copy(x_ref, block(*me), local_sem)

        def first():
            return [copy(0, me, sibling, src=x_ref)] + [copy(1 + j, me, (*chip, c), src=x_ref) for j, chip in enumerate(chips)]

        def passed(j):
            return copy(4 + j, (*chips[j], c), sibling)

        def start():
            mine().start()
            for cp in first():
                cp.start()

        def forward():
            for j, chip in enumerate(chips):
                copy(1 + j, (*chip, c), me).wait_recv()
                passed(j).start()

        def finish():
            copy(0, sibling, me).wait_recv()
            for j, chip in enumerate(chips):
                copy(4 + j, (*chip, 1 - c), me).wait_recv()
            for cp in first() + [passed(j) for j in range(3)]:
                cp.wait_send()
            mine().wait()

        return [start, forward, finish]


class _GatherRelay(_Gather):
    def __init__(self, shard):
        super().__init__(shard)
        self.scratch = (pltpu.SemaphoreType.DMA((9,)), pltpu.SemaphoreType.DMA((9,)), pltpu.SemaphoreType.DMA)
        self.split = shard.shape[0] // 32 * 16

    def phases(self, ins, outs, scr):
        (x_ref,), (out_ref,), (send_sems, recv_sems, local_sem) = ins, outs, scr
        x, y, c = _mesh_pos()
        rows = x_ref.shape[0]
        me, sibling = (x, y, c), (x, y, 1 - c)
        xn, yn, dg = (1 - x, y), (x, 1 - y), (1 - x, 1 - y)
        halves = (pl.ds(0, self.split), pl.ds(self.split, rows - self.split))

        def block(chip, pc, half=None):
            ref = out_ref.at[4 * chip[0] + 2 * chip[1] + pc]
            return ref if half is None else ref.at[halves[half], :]

        def copy(k, dst, to, src=None):
            return _remote(dst if src is None else src, dst, send_sems, recv_sems, k, to)

        def mine():
            return pltpu.make_async_copy(x_ref, block((x, y), c), local_sem)

        sends = {
            0: lambda: copy(0, block((x, y), c), sibling, src=x_ref),
            1: lambda: copy(1, block((x, y), c), (*xn, c), src=x_ref),
            2: lambda: copy(2, block((x, y), c), (*yn, c), src=x_ref),
            3: lambda: copy(3, block(yn, c, 0), (*xn, c)),
            4: lambda: copy(4, block(xn, c, 1), (*yn, c)),
            5: lambda: copy(5, block(xn, c), sibling),
            6: lambda: copy(6, block(yn, c), sibling),
            7: lambda: copy(7, block(dg, c, 0), sibling),
            8: lambda: copy(8, block(dg, c, 1), sibling),
        }
        lands = {
            0: lambda: copy(0, block((x, y), 1 - c), me), 1: lambda: copy(1, block(xn, c), me),
            2: lambda: copy(2, block(yn, c), me), 3: lambda: copy(3, block(dg, c, 0), me),
            4: lambda: copy(4, block(dg, c, 1), me), 5: lambda: copy(5, block(xn, 1 - c), me),
            6: lambda: copy(6, block(yn, 1 - c), me), 7: lambda: copy(7, block(dg, 1 - c, 0), me),
            8: lambda: copy(8, block(dg, 1 - c, 1), me),
        }

        def start():
            mine().start()
            for k in (0, 1, 2):
                sends[k]().start()

        def forward():
            for arrived, then in ((1, (4, 5)), (2, (3, 6)), (3, (7,)), (4, (8,))):
                lands[arrived]().wait_recv()
                for k in then:
                    sends[k]().start()

        def finish():
            for k in (0, 5, 6, 7, 8):
                lands[k]().wait_recv()
            for k in range(9):
                sends[k]().wait_send()
            mine().wait()

        return [start, forward, finish]


class _SiblingExchange:
    def __init__(self, src):
        self.inputs = (src,)
        self.out_shape = (jax.ShapeDtypeStruct((4, *src.shape[1:]), src.dtype),)
        self.scratch = (pltpu.SemaphoreType.DMA((4,)), pltpu.SemaphoreType.DMA((4,)))

    def phases(self, ins, outs, scr):
        (g_ref,), (recv_ref,), (send_sems, recv_sems) = ins, outs, scr
        x, y, c = _mesh_pos()

        def copies():
            return [_remote(g_ref.at[2 * k + (1 - c)], recv_ref.at[k], send_sems, recv_sems, k, (x, y, 1 - c))
                    for k in range(4)]

        def start():
            for cp in copies():
                cp.start()

        def finish():
            for cp in copies():
                cp.wait()

        return [start, finish]


class _ChipsExchange:
    def __init__(self, src):
        self.inputs = (src,)
        self.out_shape = (jax.ShapeDtypeStruct(src.shape, src.dtype),)
        self.scratch = (pltpu.SemaphoreType.DMA((3,)), pltpu.SemaphoreType.DMA((3,)))

    def phases(self, ins, outs, scr):
        (p_ref,), (recv_ref,), (send_sems, recv_sems) = ins, outs, scr
        x, y, c = _mesh_pos()
        peers = [(1 - x, y, c), (x, 1 - y, c), (1 - x, 1 - y, c)]

        def copies():
            return [_remote(p_ref.at[k], recv_ref.at[k], send_sems, recv_sems, k, peer) for k, peer in enumerate(peers)]

        def start():
            for cp in copies():
                cp.start()

        def finish():
            for cp in copies():
                cp.wait()

        return [start, finish]


class _Together:
    def __init__(self, *plans):
        self.plans = plans
        self.inputs = tuple(a for p in plans for a in p.inputs)
        self.out_shape = tuple(o for p in plans for o in p.out_shape)
        self.scratch = tuple(s for p in plans for s in p.scratch)

    def phases(self, ins, outs, scr):
        each = []
        for p in self.plans:
            n_in, n_out, n_scr = len(p.inputs), len(p.out_shape), len(p.scratch)
            each.append(p.phases(ins[:n_in], outs[:n_out], scr[:n_scr]))
            ins, outs, scr = ins[n_in:], outs[n_out:], scr[n_scr:]

        def run(fns):
            def phase():
                for fn in fns:
                    fn()
            return phase

        middle = [fn for ph in each for fn in ph[1:-1]]
        return [run([ph[0] for ph in each]), *([run(middle)] if middle else []), run([ph[-1] for ph in each])]


def _run_comm(plan, name):
    n_in, n_out = len(plan.inputs), len(plan.out_shape)

    def body(*refs):
        for phase in plan.phases(refs[:n_in], refs[n_in:n_in + n_out], refs[n_in + n_out:]):
            phase()

    return pl.pallas_call(
        body, name=name, out_shape=list(plan.out_shape), in_specs=[HBM] * n_in, out_specs=[HBM] * n_out,
        scratch_shapes=list(plan.scratch))(*plan.inputs)


def _grid_call(body, *, name, nt, in_specs, out_specs, out_shape, scratch_shapes, args, plan=None):
    if plan is None:
        return pl.pallas_call(body, name=name, grid=(nt,), in_specs=in_specs, out_specs=out_specs, out_shape=out_shape,
                              scratch_shapes=scratch_shapes, compiler_params=_params(1))(*args)
    n_in, n_out, n_scr = len(in_specs), len(out_specs), len(scratch_shapes)
    p_in, p_out = len(plan.inputs), len(plan.out_shape)

    def with_plan(*refs):
        ins, refs = refs[:n_in], refs[n_in:]
        p_ins, refs = refs[:p_in], refs[p_in:]
        outs, refs = refs[:n_out], refs[n_out:]
        p_outs, refs = refs[:p_out], refs[p_out:]
        scr, p_scr = refs[:n_scr], refs[n_scr:]
        phases = plan.phases(p_ins, p_outs, p_scr)
        i = pl.program_id(0)
        pl.when(i == 0)(phases[0])
        for phase in phases[1:-1]:
            pl.when(i == min(max(nt - 3, 1), nt - 1))(phase)
        body(*ins, *outs, *scr)
        pl.when(i == nt - 1)(phases[-1])

    return pl.pallas_call(
        with_plan, name=name, grid=(nt,), in_specs=[*in_specs, *[HBM] * p_in], out_specs=[*out_specs, *[HBM] * p_out],
        out_shape=[*out_shape, *plan.out_shape], scratch_shapes=[*scratch_shapes, *plan.scratch],
        compiler_params=_params(1))(*args, *plan.inputs)


def _add_chunks(gslab, recv, gid, rid, name):
    _, rows, cols = gslab.shape

    def body(gid_ref, rid_ref, a_ref, b_ref, own_ref, part_ref):
        s = a_ref[...] + b_ref[...]

        @pl.when(pl.program_id(0) == 0)
        def _():
            own_ref[...] = s

        part_ref[...] = s.astype(BF16)

    blk = (1, rows, cols)
    grid_spec = pltpu.PrefetchScalarGridSpec(
        num_scalar_prefetch=2, grid=(4,),
        in_specs=[pl.BlockSpec(blk, lambda k, g, r: (g[k], 0, 0)), pl.BlockSpec(blk, lambda k, g, r: (r[k], 0, 0))],
        out_specs=[pl.BlockSpec(blk, lambda k, g, r: (0, 0, 0)),
                   pl.BlockSpec(blk, lambda k, g, r: (jnp.maximum(k - 1, 0), 0, 0))])
    return pl.pallas_call(
        body, name=name, grid_spec=grid_spec,
        out_shape=[jax.ShapeDtypeStruct(blk, F32), jax.ShapeDtypeStruct((3, rows, cols), BF16)],
        compiler_params=_params(1),
    )(gid, rid, gslab, recv)


def _sum_partials(own, recv, name):
    _, rows, cols = own.shape
    tr = _row_tile(rows, 1024, 16)

    def body(o_ref, r_ref, out_ref):
        acc = o_ref[0]
        for k in range(3):
            acc = acc + r_ref[k].astype(F32)
        out_ref[...] = acc

    return pl.pallas_call(
        body, name=name, grid=(rows // tr,),
        in_specs=[pl.BlockSpec((1, tr, cols), lambda i: (0, i, 0)), pl.BlockSpec((3, tr, cols), lambda i: (0, i, 0))],
        out_specs=pl.BlockSpec((tr, cols), lambda i: (i, 0)),
        out_shape=jax.ShapeDtypeStruct((rows, cols), F32),
        compiler_params=_params(1),
    )(own, recv)


def _chunk_ids():
    x, y, c = _mesh_pos()
    chips = [(x, y), (1 - x, y), (x, 1 - y), (1 - x, 1 - y)]
    gid = jnp.stack([4 * px + 2 * py + c for px, py in chips]).astype(jnp.int32)
    rid = jnp.stack([2 * px + py for px, py in chips]).astype(jnp.int32)
    return gid, rid


def _sibling_sums(slab, recv, tag):
    gid, rid = _chunk_ids()
    return _add_chunks(slab, recv, gid, rid, "rs_add_" + tag)


def _load_weights(slab_ref, offs, dsts, sems):
    cps = []
    for i, (off, dst) in enumerate(zip(offs, dsts)):
        f8 = dst.shape[0] // N_DEV
        cps += [pltpu.make_async_copy(slab_ref.at[j, pl.ds(off, f8), :], dst.at[pl.ds(j * f8, f8), :],
                                      sems.at[i * N_DEV + j]) for j in range(N_DEV)]
    for cp in cps:
        cp.start()
    for cp in cps:
        cp.wait()


def _chunk_rows(c):
    return pl.ds(pl.multiple_of(c * FF_CHUNK, FF_CHUNK), FF_CHUNK)


def _rms(xv):
    return lax.rsqrt(jnp.mean(xv * xv, axis=-1, keepdims=True) + RMS_EPS)


def _loss_terms(xv, tgt, gain):
    r = _rms(xv)
    xh = xv * r
    err = xh * gain - tgt
    loss = 0.5 * jnp.sum(jnp.mean(err * err, axis=-1, keepdims=True))
    dy = err * (1.0 / xv.shape[-1])
    dxh = dy * gain
    return loss, r * (dxh - xh * jnp.mean(dxh * xh, axis=-1, keepdims=True)), _rowsum8(dy * xh)


def _ffn_fwd(x, gain, slab, offs, f, name, plan=None, head=None):
    t, d = x.shape
    nc, tm = f // FF_CHUNK, TOKEN_TILE
    nt = t // tm
    n_head = 0 if head is None else 2

    def body(*refs):
        x_ref, gain_ref, slab_ref = refs[:3]
        xo_ref, g_ref, u_ref, n_ref = refs[3 + n_head:7 + n_head]
        wg_v, wu_v, wd_v, acc_ref, sems = refs[7 + 2 * n_head:]
        i = pl.program_id(0)

        @pl.when(i == 0)
        def _():
            _load_weights(slab_ref, offs, (wg_v, wu_v, wd_v), sems)

        xv = x_ref[...]
        n_ref[...] = ((xv * _rms(xv)) * gain_ref[...]).astype(BF16)
        acc_ref[...] = jnp.zeros_like(acc_ref)

        def chunk(c, carry):
            rows = _chunk_rows(c)
            nb = n_ref[...]
            g = _dot_nt(nb, wg_v[rows, :])
            u = _dot_nt(nb, wu_v[rows, :])
            g_ref[c] = g.astype(BF16)
            u_ref[c] = u.astype(BF16)
            h = (g * jax.nn.sigmoid(g)) * u
            acc_ref[...] += _dot_nn(h.astype(BF16), wd_v[rows, :])
            return carry

        lax.fori_loop(0, nc, chunk, 0, unroll=True)
        out = xv + FFN_RES_WEIGHT * acc_ref[...]
        if head is None:
            xo_ref[...] = out
            return
        tgt_ref, fgain_ref = refs[3:5]
        dgain_ref, loss_ref = refs[7 + n_head:7 + 2 * n_head]

        @pl.when(i == 0)
        def _():
            dgain_ref[...] = jnp.zeros_like(dgain_ref)
            loss_ref[...] = jnp.zeros_like(loss_ref)

        loss, dx, dgain = _loss_terms(out, tgt_ref[...], fgain_ref[...])
        xo_ref[...] = dx
        loss_ref[...] += loss
        dgain_ref[...] += dgain

        @pl.when(i == nt - 1)
        def _():
            _fold8(dgain_ref)

    tile = pl.BlockSpec((tm, d), lambda i: (i, 0))
    act = pl.BlockSpec((nc, tm, FF_CHUNK), lambda i: (0, i, 0))
    head_in = [] if head is None else [tile, _full((1, d))]
    head_out = [] if head is None else [_full((SUBLANES, d)), _full((SUBLANES, 128))]
    head_shape = [] if head is None else [jax.ShapeDtypeStruct((SUBLANES, d), F32), jax.ShapeDtypeStruct((SUBLANES, 128), F32)]
    return _grid_call(
        body, name=name, nt=nt, plan=plan,
        in_specs=[tile, _full((1, d)), HBM, *head_in],
        out_specs=[tile, act, act, tile, *head_out],
        out_shape=[jax.ShapeDtypeStruct((t, d), F32), jax.ShapeDtypeStruct((nc, t, FF_CHUNK), BF16),
                   jax.ShapeDtypeStruct((nc, t, FF_CHUNK), BF16), jax.ShapeDtypeStruct((t, d), BF16), *head_shape],
        scratch_shapes=[pltpu.VMEM((f, d), BF16), pltpu.VMEM((f, d), BF16), pltpu.VMEM((f, d), BF16),
                        pltpu.VMEM((tm, d), F32), pltpu.SemaphoreType.DMA((3 * N_DEV,))],
        args=(x, gain, slab, *([] if head is None else head)))


def _ffn_bwd(dxo, x, gain, gs, us, slab, offs, f, name, plan=None):
    t, d = x.shape
    nc, tm = f // FF_CHUNK, FFN_BWD_TILE
    nt = t // tm

    def body(dxo_ref, x_ref, gain_ref, g_ref, u_ref, slab_ref,
             dx_ref, h_ref, dg_ref, du_ref, df_ref, dgain_ref, wg_v, wu_v, wd_v, sems):
        i = pl.program_id(0)

        @pl.when(i == 0)
        def _():
            _load_weights(slab_ref, offs, (wg_v, wu_v, wd_v), sems)
            dgain_ref[...] = jnp.zeros_like(dgain_ref)

        df_ref[...] = (FFN_RES_WEIGHT * dxo_ref[...]).astype(BF16)

        def chunk(c):
            rows = pl.ds(c * FF_CHUNK, FF_CHUNK)
            g = g_ref[c].astype(F32)
            u = u_ref[c].astype(F32)
            sg = jax.nn.sigmoid(g)
            sil = g * sg
            dh = _dot_nt(df_ref[...], wd_v[rows, :])
            h_ref[c] = (sil * u).astype(BF16)
            du_ref[c] = (dh * sil).astype(BF16)
            dg_ref[c] = (dh * u * (sg * (1.0 + g * (1.0 - sg)))).astype(BF16)

        def back(c, dn):
            rows = pl.ds(c * FF_CHUNK, FF_CHUNK)
            return dn + _dot_nn(dg_ref[c], wg_v[rows, :]) + _dot_nn(du_ref[c], wu_v[rows, :])

        dn = jnp.zeros((tm, d), F32)
        for c in range(nc):
            chunk(c)
            if c:
                dn = back(c - 1, dn)
        dn = back(nc - 1, dn)
        xv = x_ref[...]
        r = _rms(xv)
        xh = xv * r
        dgain_ref[...] += _rowsum8(dn * xh)
        dxh = dn * gain_ref[...]
        dx_ref[...] = dxo_ref[...] + r * (dxh - xh * jnp.mean(dxh * xh, axis=-1, keepdims=True))

        @pl.when(i == nt - 1)
        def _():
            _fold8(dgain_ref)

    tile = pl.BlockSpec((tm, d), lambda i: (i, 0))
    act = pl.BlockSpec((nc, tm, FF_CHUNK), lambda i: (0, i, 0))
    act_shape = jax.ShapeDtypeStruct((nc, t, FF_CHUNK), BF16)
    return _grid_call(
        body, name=name, nt=nt, plan=plan,
        in_specs=[tile, tile, _full((1, d)), act, act, HBM],
        out_specs=[tile, act, act, act, tile, _full((SUBLANES, d))],
        out_shape=[jax.ShapeDtypeStruct((t, d), F32), act_shape, act_shape, act_shape,
                   jax.ShapeDtypeStruct((t, d), BF16), jax.ShapeDtypeStruct((SUBLANES, d), F32)],
        scratch_shapes=[pltpu.VMEM((f, d), BF16), pltpu.VMEM((f, d), BF16), pltpu.VMEM((f, d), BF16),
                        pltpu.SemaphoreType.DMA((3 * N_DEV,))],
        args=(dxo, x, gain, gs, us, slab))


def _tn_chunked(a, b, name, plan=None):
    nc, t, _ = a.shape
    n = b.shape[1]
    tb = _row_tile(t, 1024, TOKEN_TILE)

    def body(a_ref, b_ref, o_ref):
        @pl.when(pl.program_id(0) == 0)
        def _():
            o_ref[...] = jnp.zeros_like(o_ref)

        def chunk(c, carry):
            rows = _chunk_rows(c)
            o_ref[rows, :] += _dot_tn(a_ref[c], b_ref[...])
            return carry

        lax.fori_loop(0, nc, chunk, 0, unroll=True)

    return _grid_call(
        body, name=name, nt=t // tb, plan=plan,
        in_specs=[pl.BlockSpec((nc, tb, FF_CHUNK), lambda i: (0, i, 0)), pl.BlockSpec((tb, n), lambda i: (i, 0))],
        out_specs=[_full((nc * FF_CHUNK, n))],
        out_shape=[jax.ShapeDtypeStruct((nc * FF_CHUNK, n), F32)],
        scratch_shapes=[], args=(a, b))


def _tn(a, b, name, plan=None):
    t, k = a.shape
    n = b.shape[1]
    tb = _row_tile(t, 1024, TOKEN_TILE)

    def body(a_ref, b_ref, o_ref):
        @pl.when(pl.program_id(0) == 0)
        def _():
            o_ref[...] = jnp.zeros_like(o_ref)

        o_ref[...] += _dot_tn(a_ref[...].astype(BF16), b_ref[...].astype(BF16))

    return _grid_call(
        body, name=name, nt=t // tb, plan=plan,
        in_specs=[pl.BlockSpec((tb, k), lambda i: (i, 0)), pl.BlockSpec((tb, n), lambda i: (i, 0))],
        out_specs=[_full((k, n))],
        out_shape=[jax.ShapeDtypeStruct((k, n), F32)],
        scratch_shapes=[], args=(a, b))


def _layernorm_stats(u1):
    mu = jnp.mean(u1, axis=-1, keepdims=True)
    xc = u1 - mu
    rstd = lax.rsqrt(jnp.mean(xc * xc, axis=-1, keepdims=True) + LN_EPS)
    return xc * rstd, rstd


def _positions(tile_index, tm):
    return (tile_index * tm + lax.broadcasted_iota(jnp.int32, (tm, 1), 0)).astype(F32)


def _shifted_taps(src_ref, sh_ref, tm, offset_of):
    groups = {}
    for k in range(CONV_WIDTH):
        groups.setdefault(offset_of(k) % SUBLANES, []).append(k)
    span = tm + HALO - SUBLANES
    for rem, taps in sorted(groups.items()):
        if rem:
            sh_ref[0:span, :] = src_ref[rem:rem + span, :]
        ref = sh_ref if rem else src_ref
        for k in taps:
            base = offset_of(k) - rem
            yield k, ref[base:base + tm, :]


def _mix_fwd(x, gm, win_t, cdw, cb, lg, lb, pw, poolw, ps, wout):
    t, d = x.shape
    tm = TOKEN_TILE

    def body(x_ref, gm_ref, win_ref, cdw_ref, cb_ref, lg_ref, lb_ref, pw_ref, poolw_ref, ps_ref, wout_ref,
             xo_ref, h_ref, ag_ref, u0_ref, u1_ref, u2_ref, mixed_ref, cat_ref, eu_ref, ep_ref, sh_ref):
        i = pl.program_id(0)

        @pl.when(i == 0)
        def _():
            eu_ref[0:HALO, :] = jnp.zeros((HALO, D_CONV), F32)
            ep_ref[0:HALO, :] = jnp.zeros((HALO, D_POOL), F32)

        @pl.when(i > 0)
        def _():
            eu_ref[0:HALO, :] = eu_ref[tm:tm + HALO, :]
            ep_ref[0:HALO, :] = ep_ref[tm:tm + HALO, :]

        xv = x_ref[...]
        hb = ((xv * _rms(xv)) * gm_ref[...]).astype(BF16)
        h_ref[...] = hb
        proj = _dot_nt(hb, win_ref[...])
        a = proj[:, :D_CONV]
        g = proj[:, D_CONV:2 * D_CONV]
        ag_ref[...] = proj[:, :2 * D_CONV]
        u0 = a * jax.nn.sigmoid(g)
        u0_ref[...] = u0
        eu_ref[HALO:HALO + tm, :] = u0
        ep_ref[HALO:HALO + tm, :] = proj[:, 2 * D_CONV:]

        u1 = jnp.broadcast_to(cb_ref[...], (tm, D_CONV))
        for k, rows in _shifted_taps(eu_ref, sh_ref, tm, lambda k: HALO - (CONV_WIDTH - 1) + k):
            u1 = u1 + cdw_ref[k:k + 1, :] * rows
        u1_ref[...] = u1
        lnh, _ = _layernorm_stats(u1)
        ln = lnh * lg_ref[...] + lb_ref[...]
        u2 = (ln * jax.nn.sigmoid(ln)).astype(BF16)
        u2_ref[...] = u2
        conv_out = _dot_nn(u2, pw_ref[...])

        pos = _positions(i, tm)
        outs = []
        for gi, w in enumerate(POOL_WINDOWS):
            lo = gi * POOL_GROUP
            p = ep_ref[HALO:HALO + tm, lo:lo + POOL_GROUP]
            s = p
            for j in range(1, w):
                s = s + ep_ref[HALO - j:HALO - j + tm, lo:lo + POOL_GROUP]
            mixed = (s / jnp.minimum(pos + 1.0, float(w)) - p).astype(BF16)
            mixed_ref[:, lo:lo + POOL_GROUP] = mixed
            outs.append(_dot_nn(mixed, poolw_ref[gi]))
        pool_out = jnp.concatenate(outs, axis=-1) * ps_ref[...]
        cat = jnp.concatenate([conv_out, pool_out], axis=-1).astype(BF16)
        cat_ref[...] = cat
        xo_ref[...] = xv + _dot_nn(cat, wout_ref[...])

    def tile(c):
        return pl.BlockSpec((tm, c), lambda i: (i, 0))

    def out(c, dt):
        return jax.ShapeDtypeStruct((t, c), dt)

    return pl.pallas_call(
        body, name="mix_fwd", grid=(t // tm,),
        in_specs=[tile(d), _full((1, d)), _full((D_IN, d)), _full((HALO, D_CONV)), _full((1, D_CONV)),
                  _full((1, D_CONV)), _full((1, D_CONV)), _full((D_CONV, D_CONV)),
                  _full((len(POOL_WINDOWS), POOL_GROUP, POOL_GROUP)), _full((1, D_POOL)), _full((d, d))],
        out_specs=[tile(d), tile(d), tile(2 * D_CONV), tile(D_CONV), tile(D_CONV), tile(D_CONV), tile(D_POOL), tile(d)],
        out_shape=[out(d, F32), out(d, BF16), out(2 * D_CONV, F32), out(D_CONV, F32), out(D_CONV, F32),
                   out(D_CONV, BF16), out(D_POOL, BF16), out(d, BF16)],
        scratch_shapes=[pltpu.VMEM((HALO + tm, D_CONV), F32), pltpu.VMEM((HALO + tm, D_POOL), F32),
                        pltpu.VMEM((HALO + tm, D_CONV), F32)],
        compiler_params=_params(1),
    )(x, gm, win_t, cdw, cb, lg, lb, pw, poolw, ps, wout)


def _mix_bwd(dxo, x, gm, ag, u0, u1, mixed, win_t, cdw, lg, lb, pw, poolw, ps, wout, plan=None):
    t, d = x.shape
    tm = TOKEN_TILE
    nt = t // tm
    halo_blocks = tm // HALO

    def body(dxo_ref, x_ref, gm_ref, ag_ref, u0_ref, u0h_ref, u1_ref, mixed_ref,
             win_ref, cdw_ref, lg_ref, lb_ref, pw_ref, poolw_ref, ps_ref, wout_ref,
             dx_ref, dproj_ref, dco_ref, dgm_ref, dcdw_ref, dcb_ref, dlg_ref, dlb_ref, dpoolw_ref, dps_ref,
             eu_ref, ed_ref, eq_ref, sh_ref):
        i = pl.program_id(0)
        ti = nt - 1 - i

        @pl.when(i == 0)
        def _():
            for ref in (dgm_ref, dcdw_ref, dcb_ref, dlg_ref, dlb_ref, dpoolw_ref, dps_ref):
                ref[...] = jnp.zeros_like(ref)
            ed_ref[tm:tm + HALO, :] = jnp.zeros((HALO, D_CONV), F32)
            eq_ref[tm:tm + HALO, :] = jnp.zeros((HALO, D_POOL), F32)

        @pl.when(i > 0)
        def _():
            ed_ref[tm:tm + HALO, :] = ed_ref[0:HALO, :]
            eq_ref[tm:tm + HALO, :] = eq_ref[0:HALO, :]

        @pl.when(ti == 0)
        def _():
            eu_ref[0:HALO, :] = jnp.zeros((HALO, D_CONV), F32)

        @pl.when(ti > 0)
        def _():
            eu_ref[0:HALO, :] = u0h_ref[...]

        eu_ref[HALO:HALO + tm, :] = u0_ref[...]

        dxo = dxo_ref[...]
        dcat = _dot_nt(dxo.astype(BF16), wout_ref[...])
        dco = dcat[:, :D_CONV].astype(BF16)
        dco_ref[...] = dco
        dpo = dcat[:, D_CONV:]

        lnh, rstd = _layernorm_stats(u1_ref[...])
        ln = lnh * lg_ref[...] + lb_ref[...]
        sl = jax.nn.sigmoid(ln)
        dln = _dot_nt(dco, pw_ref[...]) * (sl * (1.0 + ln * (1.0 - sl)))
        dlg_ref[...] += _rowsum8(dln * lnh)
        dlb_ref[...] += _rowsum8(dln)
        dlnh = dln * lg_ref[...]
        du1 = rstd * (dlnh - jnp.mean(dlnh, axis=-1, keepdims=True)
                      - lnh * jnp.mean(dlnh * lnh, axis=-1, keepdims=True))
        dcb_ref[...] += _rowsum8(du1)
        ed_ref[0:tm, :] = du1

        du0 = jnp.zeros((tm, D_CONV), F32)
        for k, rows in _shifted_taps(ed_ref, sh_ref, tm, lambda k: CONV_WIDTH - 1 - k):
            du0 = du0 + cdw_ref[k:k + 1, :] * rows
        for k, rows in _shifted_taps(eu_ref, sh_ref, tm, lambda k: HALO - (CONV_WIDTH - 1) + k):
            dcdw_ref[SUBLANES * k:SUBLANES * (k + 1), :] += _rowsum8(du1 * rows)
        a = ag_ref[:, :D_CONV]
        sg = jax.nn.sigmoid(ag_ref[:, D_CONV:])
        pieces = [du0 * sg, du0 * a * (sg * (1.0 - sg))]

        pos = _positions(ti, tm)
        for gi, w in enumerate(POOL_WINDOWS):
            lo = gi * POOL_GROUP
            mg = mixed_ref[:, lo:lo + POOL_GROUP]
            dpo_g = dpo[:, lo:lo + POOL_GROUP]
            dps_ref[:, lo:lo + POOL_GROUP] += _rowsum8(dpo_g * _dot_nn(mg, poolw_ref[gi]))
            dout = (dpo_g * ps_ref[:, lo:lo + POOL_GROUP]).astype(BF16)
            dpoolw_ref[gi] += _dot_tn(mg, dout)
            dmx = _dot_nt(dout, poolw_ref[gi])
            q = dmx / jnp.minimum(pos + 1.0, float(w))
            eq_ref[0:tm, lo:lo + POOL_GROUP] = q
            s = q
            for j in range(1, w):
                s = s + eq_ref[j:j + tm, lo:lo + POOL_GROUP]
            pieces.append(s - dmx)
        dproj = jnp.concatenate(pieces, axis=-1).astype(BF16)
        dproj_ref[...] = dproj

        dh = _dot_nn(dproj, win_ref[...])
        xv = x_ref[...]
        r = _rms(xv)
        xh = xv * r
        dgm_ref[...] += _rowsum8(dh * xh)
        dxh = dh * gm_ref[...]
        dx_ref[...] = dxo + r * (dxh - xh * jnp.mean(dxh * xh, axis=-1, keepdims=True))

        @pl.when(i == nt - 1)
        def _():
            for ref in (dgm_ref, dcb_ref, dlg_ref, dlb_ref, dps_ref):
                _fold8(ref)
            for k in range(CONV_WIDTH):
                dcdw_ref[SUBLANES * k:SUBLANES * k + 1, :] = jnp.sum(
                    dcdw_ref[SUBLANES * k:SUBLANES * (k + 1), :], axis=0, keepdims=True)

    def tile(c):
        return pl.BlockSpec((tm, c), lambda i: (nt - 1 - i, 0))

    halo = pl.BlockSpec((HALO, D_CONV), lambda i: (jnp.maximum((nt - 1 - i) * halo_blocks - 1, 0), 0))
    n_groups = len(POOL_WINDOWS)
    return _grid_call(
        body, name="mix_bwd", nt=nt, plan=plan,
        in_specs=[tile(d), tile(d), _full((1, d)), tile(2 * D_CONV), tile(D_CONV), halo, tile(D_CONV), tile(D_POOL),
                  _full((D_IN, d)), _full((HALO, D_CONV)), _full((1, D_CONV)), _full((1, D_CONV)),
                  _full((D_CONV, D_CONV)), _full((n_groups, POOL_GROUP, POOL_GROUP)), _full((1, D_POOL)), _full((d, d))],
        out_specs=[tile(d), tile(D_IN), tile(D_CONV), _full((SUBLANES, d)), _full((HALO * SUBLANES, D_CONV)),
                   _full((SUBLANES, D_CONV)), _full((SUBLANES, D_CONV)), _full((SUBLANES, D_CONV)),
                   _full((n_groups, POOL_GROUP, POOL_GROUP)), _full((SUBLANES, D_POOL))],
        out_shape=[jax.ShapeDtypeStruct((t, d), F32), jax.ShapeDtypeStruct((t, D_IN), BF16),
                   jax.ShapeDtypeStruct((t, D_CONV), BF16), jax.ShapeDtypeStruct((SUBLANES, d), F32),
                   jax.ShapeDtypeStruct((HALO * SUBLANES, D_CONV), F32), jax.ShapeDtypeStruct((SUBLANES, D_CONV), F32),
                   jax.ShapeDtypeStruct((SUBLANES, D_CONV), F32), jax.ShapeDtypeStruct((SUBLANES, D_CONV), F32),
                   jax.ShapeDtypeStruct((n_groups, POOL_GROUP, POOL_GROUP), F32),
                   jax.ShapeDtypeStruct((SUBLANES, D_POOL), F32)],
        scratch_shapes=[pltpu.VMEM((HALO + tm, D_CONV), F32), pltpu.VMEM((tm + HALO, D_CONV), F32),
                        pltpu.VMEM((tm + HALO, D_POOL), F32), pltpu.VMEM((tm + HALO, D_CONV), F32)],
        args=(dxo, x, gm, ag, u0, u0, u1, mixed, win_t, cdw, lg, lb, pw, poolw, ps, wout))


def _adam_step(gv, w_ref, m_ref, v_ref, d_ref, nm_ref, nv_ref):
    nm = ADAM_B1 * m_ref[...] + (1.0 - ADAM_B1) * gv
    nv = ADAM_B2 * v_ref[...] + (1.0 - ADAM_B2) * (gv * gv)
    m_hat = nm / (1.0 - ADAM_B1 ** ADAM_STEP)
    v_hat = nv / (1.0 - ADAM_B2 ** ADAM_STEP)
    d_ref[...] = -ADAM_LR * (m_hat / (jnp.sqrt(v_hat) + ADAM_EPS) + ADAM_WD * w_ref[...])
    nm_ref[...] = nm
    nv_ref[...] = nv


def _adam_tile(rows, cols):
    return _row_tile(rows, max(SUBLANES, (256 * 1024) // cols // SUBLANES * SUBLANES), SUBLANES)


def _adamw_many(items, name):
    n = len(items)

    def body(*refs):
        ins, outs = refs[:4 * n], refs[4 * n:]
        for k in range(n):
            w_ref, g_ref, m_ref, v_ref = ins[4 * k:4 * k + 4]
            _adam_step(g_ref[...], w_ref, m_ref, v_ref, *outs[3 * k:3 * k + 3])

    return pl.pallas_call(
        body, name=name, out_shape=[jax.ShapeDtypeStruct(it[0].shape, F32) for it in items for _ in range(3)],
        compiler_params=_params(0))(*[a for it in items for a in it])


def _adamw_reduced(w, own, recv, m, v, name):
    rows, cols = w.shape
    tr = _adam_tile(rows, cols)

    def body(w_ref, o_ref, r_ref, m_ref, v_ref, g_ref, d_ref, nm_ref, nv_ref):
        gv = o_ref[0]
        for k in range(3):
            gv = gv + r_ref[k].astype(F32)
        g_ref[...] = gv
        _adam_step(gv, w_ref, m_ref, v_ref, d_ref, nm_ref, nv_ref)

    blk = pl.BlockSpec((tr, cols), lambda i: (i, 0))
    shape = jax.ShapeDtypeStruct((rows, cols), F32)
    return pl.pallas_call(
        body, name=name, grid=(rows // tr,),
        in_specs=[blk, pl.BlockSpec((1, tr, cols), lambda i: (0, i, 0)), pl.BlockSpec((3, tr, cols), lambda i: (0, i, 0)),
                  blk, blk],
        out_specs=[blk] * 4, out_shape=[shape] * 4,
        compiler_params=_params(1),
    )(w, own, recv, m, v)


def _as_2d(a):
    if a.ndim == 1:
        return a.reshape(a.shape[0] // 128, 128)
    if a.ndim == 3:
        return a.reshape(a.shape[0] * a.shape[1], a.shape[2])
    return a


def _pack_weight_slabs(p):
    def bf(parts):
        return [a.astype(BF16) for a in parts]

    cdw_bits = lax.bitcast_convert_type(p["conv_dw"], BF16).reshape(CONV_WIDTH, 2 * D_CONV // N_DEV)
    cdw_bits = jnp.pad(cdw_bits, ((0, 1), (0, 0))).reshape(4, D_MODEL)
    cdw_bits = jnp.pad(cdw_bits, ((0, CDW_ROWS - 4), (0, 0)))
    first = bf([p["ffn1_w_gate"].T, p["ffn1_w_up"].T, p["ffn1_w_down"]])
    rest = bf([p["ffn2_w_gate"].T, p["ffn2_w_up"].T, p["ffn2_w_down"], p["w_in"].T, p["w_out"],
               p["conv_pw"].reshape(D_CONV // N_DEV // 2, D_MODEL)]) + [cdw_bits]
    return jnp.concatenate(first, axis=0), jnp.concatenate(rest, axis=0)


def _unpack_rows(slab, offs, names):
    out = {}
    for name in names:
        o, n = offs[name]
        out[name] = slab[:, o:o + n, :].reshape(N_DEV * n, D_MODEL)
    return out


def _unpack_conv_taps(slab, offs):
    o, _ = offs["cdw"]
    bits = slab[:, o:o + 4, :].reshape(N_DEV, CONV_WIDTH + 1, D_CONV // N_DEV, 2)[:, :CONV_WIDTH]
    cdw = lax.bitcast_convert_type(bits, F32)
    return jnp.transpose(cdw, (1, 0, 2)).reshape(CONV_WIDTH, D_CONV)


def kernel(x, ffn1_norm, ffn1_w_gate, ffn1_w_up, ffn1_w_down, mix_norm, w_in, conv_dw, conv_dw_b, conv_ln_g, conv_ln_b, conv_pw, pool_w, pool_scale, w_out, ffn2_norm, ffn2_w_gate, ffn2_w_up, ffn2_w_down, final_norm, loss_target, m_ffn1_norm, m_ffn1_w_gate, m_ffn1_w_up, m_ffn1_w_down, m_mix_norm, m_w_in, m_conv_dw, m_conv_dw_b, m_conv_ln_g, m_conv_ln_b, m_conv_pw, m_pool_w, m_pool_scale, m_w_out, m_ffn2_norm, m_ffn2_w_gate, m_ffn2_w_up, m_ffn2_w_down, m_final_norm, v_ffn1_norm, v_ffn1_w_gate, v_ffn1_w_up, v_ffn1_w_down, v_mix_norm, v_w_in, v_conv_dw, v_conv_dw_b, v_conv_ln_g, v_conv_ln_b, v_conv_pw, v_pool_w, v_pool_scale, v_w_out, v_ffn2_norm, v_ffn2_w_gate, v_ffn2_w_up, v_ffn2_w_down, v_final_norm):
    given = dict(locals())
    p = {n: given[n] for n in WEIGHTS}
    f8 = ffn1_w_gate.shape[1]
    f = N_DEV * f8
    ffn_rows = (0, f8, 2 * f8)
    small = (("win", D_IN // N_DEV), ("wout", D_MODEL // N_DEV), ("pw", D_CONV // N_DEV // 2), ("cdw", CDW_ROWS))
    w_offs, _ = _layout((("g2", f8), ("u2", f8), ("d2", f8)) + small)
    s_offs, _ = _layout(small + (("rep", REP_ROWS),))
    x0 = x[0]
    target = loss_target[0]

    def row(vec):
        return vec.reshape(1, vec.shape[0])

    slab_first, slab_rest = _pack_weight_slabs(p)
    w_first = _run_comm(_GatherRelay(slab_first), "gather_ffn1")[0]
    x1, g1s, u1s, n1, w_rest = _ffn_fwd(x0, row(ffn1_norm), w_first, ffn_rows, f, "ffn1_fwd", _Gather(slab_rest))
    w = _unpack_rows(w_rest, w_offs, ("win", "wout", "pw"))
    w["pw"] = w["pw"].reshape(D_CONV, D_CONV)
    cdw = jnp.pad(_unpack_conv_taps(w_rest, w_offs), ((0, HALO - CONV_WIDTH), (0, 0)))
    poolw = pool_w.astype(BF16)

    x2, h, ag, u0, u1, u2, mixed, cat = _mix_fwd(
        x1, row(mix_norm), w["win"], cdw, row(conv_dw_b), row(conv_ln_g), row(conv_ln_b), w["pw"], poolw,
        row(pool_scale), w["wout"])
    dx3, g2s, u2s, n2, d_final_norm, loss_part = _ffn_fwd(
        x2, row(ffn2_norm), w_rest, ffn_rows, f, "ffn2_fwd", head=(target, row(final_norm)))

    pending, reduced = {}, {}

    def chunks(a):
        return a.reshape(N_DEV, -1, D_MODEL)

    def after_sibling(name, slab, recv):
        pending[name], part = _sibling_sums(slab, recv, name)
        return _ChipsExchange(part)

    def after_chips(name, recv):
        reduced[name] = (pending.pop(name), recv)

    dx2, h2, dg2, du2, df2, d_ffn2_norm = _ffn_bwd(dx3, x2, row(ffn2_norm), g2s, u2s, w_rest, ffn_rows, f, "ffn2_bwd")
    s_g2 = chunks(_tn_chunked(dg2, n2, "ffn2_dgate")[0])
    s_u2, r = _tn_chunked(du2, n2, "ffn2_dup", _SiblingExchange(s_g2))
    s_u2 = chunks(s_u2)
    to_chips = after_sibling("g2", s_g2, r)
    s_d2, rc, r = _tn_chunked(h2, df2, "ffn2_ddown", _Together(to_chips, _SiblingExchange(s_u2)))
    s_d2 = chunks(s_d2)
    after_chips("g2", rc)
    to_chips = after_sibling("u2", s_u2, r)
    dx1, dproj, dco, d_mix_norm, d_cdw, d_cb, d_lg, d_lb, d_poolw, d_ps, rc, r = _mix_bwd(
        dx2, x1, row(mix_norm), ag, u0, u1, mixed, w["win"], cdw, row(conv_ln_g), row(conv_ln_b), w["pw"], poolw,
        row(pool_scale), w["wout"], _Together(to_chips, _SiblingExchange(s_d2)))
    after_chips("u2", rc)
    to_chips = after_sibling("d2", s_d2, r)
    d_win, rc = _tn(dproj, h, "mix_dwin", to_chips)
    after_chips("d2", rc)
    d_wout = _tn(cat, dx2, "mix_dwout")[0]
    d_pw = _tn(u2, dco, "mix_dpw")[0]
    dx0, h1, dg1, du1, df1, d_ffn1_norm = _ffn_bwd(dx1, x0, row(ffn1_norm), g1s, u1s, w_first, ffn_rows, f, "ffn1_bwd")

    d_cdw = d_cdw.reshape(HALO, SUBLANES, D_CONV)[:CONV_WIDTH, 0]
    d_cdw = jnp.transpose(d_cdw.reshape(CONV_WIDTH, N_DEV, D_CONV // N_DEV), (1, 0, 2)).reshape(N_DEV, -1)
    d_cdw = jnp.pad(d_cdw, ((0, 0), (0, CDW_ROWS * D_MODEL - d_cdw.shape[1]))).reshape(N_DEV, CDW_ROWS, D_MODEL)
    rep = jnp.concatenate([
        d_ffn1_norm[0:1], d_mix_norm[0:1], d_ffn2_norm[0:1], d_final_norm[0:1],
        jnp.concatenate([d_cb[0:1], d_lg[0:1]], axis=1), jnp.concatenate([d_lb[0:1], d_ps[0:1]], axis=1),
        jnp.zeros((2, D_MODEL), F32), d_poolw.reshape(-1, D_MODEL)], axis=0)
    rep = jnp.pad(rep, ((0, N_DEV * REP_ROWS - rep.shape[0]), (0, 0))).reshape(N_DEV, REP_ROWS, D_MODEL)
    s_small = jnp.concatenate([chunks(d_win), chunks(d_wout), chunks(d_pw), d_cdw, rep], axis=1)

    s_g1, r = _tn_chunked(dg1, n1, "ffn1_dgate", _SiblingExchange(s_small))
    s_g1 = chunks(s_g1)
    to_chips = after_sibling("small", s_small, r)
    s_u1, rc, r = _tn_chunked(du1, n1, "ffn1_dup", _Together(to_chips, _SiblingExchange(s_g1)))
    s_u1 = chunks(s_u1)
    mine_small = _sum_partials(pending.pop("small"), rc, "rs_sum_small")
    to_chips = after_sibling("g1", s_g1, r)
    o_rep, _ = s_offs["rep"]
    loss_rows = jnp.pad(loss_part, ((0, 0), (0, D_MODEL - loss_part.shape[1])))
    share = _Gather(jnp.concatenate([mine_small[o_rep:o_rep + REP_ROWS], loss_rows], axis=0))
    s_d1, rc, r, shared = _tn_chunked(h1, df1, "ffn1_ddown", _Together(to_chips, _SiblingExchange(s_u1), share))
    s_d1 = chunks(s_d1)
    after_chips("g1", rc)
    to_chips = after_sibling("u1", s_u1, r)
    rc, r = _run_comm(_Together(to_chips, _SiblingExchange(s_d1)), "rs_tail_up")
    after_chips("u1", rc)
    rc, = _run_comm(after_sibling("d1", s_d1, r), "rs_tail_down")
    after_chips("d1", rc)

    rep_all = shared[:, :REP_ROWS].reshape(N_DEV * REP_ROWS, D_MODEL)
    loss = jnp.sum(shared[:, REP_ROWS, 0])

    def small_rows(name):
        o, n = s_offs[name]
        return mine_small[o:o + n]

    g = {
        "ffn1_norm": rep_all[0], "mix_norm": rep_all[1], "ffn2_norm": rep_all[2], "final_norm": rep_all[3],
        "conv_dw_b": rep_all[4, :D_CONV], "conv_ln_g": rep_all[4, D_CONV:],
        "conv_ln_b": rep_all[5, :D_CONV], "pool_scale": rep_all[5, D_CONV:],
        "pool_w": rep_all[8:8 + pool_w.size // D_MODEL].reshape(pool_w.shape),
        "w_out": small_rows("wout"), "conv_pw": small_rows("pw").reshape(conv_pw.shape),
        "conv_dw": small_rows("cdw").reshape(-1)[:conv_dw.size].reshape(conv_dw.shape),
    }

    slab_of = {"ffn1_w_gate": "g1", "ffn1_w_up": "u1", "ffn1_w_down": "d1",
               "ffn2_w_gate": "g2", "ffn2_w_up": "u2", "ffn2_w_down": "d2"}
    transposed = ("ffn1_w_gate", "ffn1_w_up", "ffn2_w_gate", "ffn2_w_up", "w_in")
    g["w_in"] = small_rows("win")
    delta, new_m, new_v = {}, {}, {}

    def operands(n):
        wmv = [given[k] for k in (n, "m_" + n, "v_" + n)]
        return [a.T for a in wmv] if n in transposed else wmv

    def restore(n, a):
        return a.T if n in transposed else a.reshape(p[n].shape)

    others = [n for n in WEIGHTS if n not in slab_of]
    flat = _adamw_many([[_as_2d(a) for a in (wn, g[n], mn, vn)] for n in others for wn, mn, vn in [operands(n)]],
                       "adamw_small")
    for k, n in enumerate(others):
        delta[n], new_m[n], new_v[n] = (restore(n, a) for a in flat[3 * k:3 * k + 3])
    g["w_in"] = g["w_in"].T
    for n, slab in slab_of.items():
        wn, mn, vn = operands(n)
        outs = _adamw_reduced(wn, *reduced[slab], mn, vn, "adamw_" + n)
        g[n], delta[n], new_m[n], new_v[n] = (restore(n, a) for a in outs)

    return (loss, dx0[None], *[g[n] for n in WEIGHTS], *[delta[n] for n in WEIGHTS],
            *[new_m[n] for n in WEIGHTS], *[new_v[n] for n in WEIGHTS])
```

```python
import functools

import jax
import jax.numpy as jnp
from jax import lax
from jax.experimental import pallas as pl
from jax.experimental.pallas import tpu as pltpu

F32 = jnp.float32
BF16 = jnp.bfloat16

D_MODEL = 1024
D_CONV = 512
D_POOL = 512
D_IN = 2 * D_CONV + D_POOL
POOL_WINDOWS = (2, 4, 8, 16)
POOL_GROUP = D_POOL // len(POOL_WINDOWS)
CONV_WIDTH = 31
RMS_EPS = 1e-6
LN_EPS = 1e-5
FFN_RES_WEIGHT = 0.5

ADAM_LR = 0.001
ADAM_B1 = 0.9
ADAM_B2 = 0.999
ADAM_EPS = 1e-08
ADAM_WD = 0.01
ADAM_STEP = 10

N_DEV = 8
MESH_ID = pl.DeviceIdType.MESH

SUBLANES = 8
TOKEN_TILE = 512
FFN_BWD_TILE = 256
FF_CHUNK = 256
HALO = 32
V7X_VMEM_LIMIT = 56 * 1024 * 1024
CDW_ROWS = 16
REP_ROWS = 16

WEIGHTS = ("ffn1_norm", "ffn1_w_gate", "ffn1_w_up", "ffn1_w_down", "mix_norm", "w_in", "conv_dw", "conv_dw_b",
           "conv_ln_g", "conv_ln_b", "conv_pw", "pool_w", "pool_scale", "w_out", "ffn2_norm", "ffn2_w_gate",
           "ffn2_w_up", "ffn2_w_down", "final_norm")


def _dot_nn(a, b):
    return lax.dot_general(a, b, (((1,), (0,)), ((), ())), preferred_element_type=F32)


def _dot_nt(a, b):
    return lax.dot_general(a, b, (((1,), (1,)), ((), ())), preferred_element_type=F32)


def _dot_tn(a, b):
    return lax.dot_general(a, b, (((0,), (0,)), ((), ())), preferred_element_type=F32)


def _rowsum8(v):
    r, c = v.shape
    return jnp.sum(v.reshape(r // SUBLANES, SUBLANES, c), axis=0)


def _fold8(ref):
    ref[0:1, :] = jnp.sum(ref[...], axis=0, keepdims=True)


def _row_tile(n, cap, mult):
    best = None
    for t in range(mult, min(n, cap) + 1, mult):
        if n % t == 0:
            best = t
    return n if best is None else best


def _params(n_grid):
    return pltpu.CompilerParams(dimension_semantics=("arbitrary",) * n_grid, vmem_limit_bytes=V7X_VMEM_LIMIT)


def _full(shape):
    return pl.BlockSpec(shape, lambda *_: (0,) * len(shape))


def _layout(pieces):
    offs, r = {}, 0
    for name, rows in pieces:
        offs[name] = (r, rows)
        r += rows
    return offs, r


HBM = pl.BlockSpec(memory_space=pl.ANY)


def _mesh_pos():
    return lax.axis_index("x"), lax.axis_index("y"), lax.axis_index("c")


def _remote(src, dst, send_sems, recv_sems, k, to):
    return pltpu.make_async_remote_copy(src_ref=src, dst_ref=dst, send_sem=send_sems.at[k], recv_sem=recv_sems.at[k],
                                        device_id=to, device_id_type=MESH_ID)


class _Gather:
    def __init__(self, shard):
        self.inputs = (shard,)
        self.out_shape = (jax.ShapeDtypeStruct((N_DEV, *shard.shape), shard.dtype),)
        self.scratch = (pltpu.SemaphoreType.DMA((7,)), pltpu.SemaphoreType.DMA((7,)), pltpu.SemaphoreType.DMA)

    def phases(self, ins, outs, scr):
        (x_ref,), (out_ref,), (send_sems, recv_sems, local_sem) = ins, outs, scr
        x, y, c = _mesh_pos()
        me, sibling = (x, y, c), (x, y, 1 - c)
        chips = [(1 - x, y), (x, 1 - y), (1 - x, 1 - y)]

        def block(px, py, pc):
            return out_ref.at[4 * px + 2 * py + pc]

        def copy(k, blk, to, src=None):
            return _remote(block(*blk) if src is None else src, block(*blk), send_sems, recv_sems, k, to)

        def mine():
            return pltpu.make_async_copy(x_ref, block(*me), local_sem)

        def first():
            return [copy(0, me, sibling, src=x_ref)] + [copy(1 + j, me, (*chip, c), src=x_ref) for j, chip in enumerate(chips)]

        def passed(j):
            return copy(4 + j, (*chips[j], c), sibling)

        def start():
            mine().start()
            for cp in first():
                cp.start()

        def forward():
            for j, chip in enumerate(chips):
                copy(1 + j, (*chip, c), me).wait_recv()
                passed(j).start()

        def finish():
            copy(0, sibling, me).wait_recv()
            for j, chip in enumerate(chips):
                copy(4 + j, (*chip, 1 - c), me).wait_recv()
            for cp in first() + [passed(j) for j in range(3)]:
                cp.wait_send()
            mine().wait()

        return [start, forward, finish]


class _GatherRelay(_Gather):
    def __init__(self, shard):
        super().__init__(shard)
        self.scratch = (pltpu.SemaphoreType.DMA((9,)), pltpu.SemaphoreType.DMA((9,)), pltpu.SemaphoreType.DMA)
        self.split = shard.shape[0] // 32 * 16

    def phases(self, ins, outs, scr):
        (x_ref,), (out_ref,), (send_sems, recv_sems, local_sem) = ins, outs, scr
        x, y, c = _mesh_pos()
        rows = x_ref.shape[0]
        me, sibling = (x, y, c), (x, y, 1 - c)
        xn, yn, dg = (1 - x, y), (x, 1 - y), (1 - x, 1 - y)
        halves = (pl.ds(0, self.split), pl.ds(self.split, rows - self.split))

        def block(chip, pc, half=None):
            ref = out_ref.at[4 * chip[0] + 2 * chip[1] + pc]
            return ref if half is None else ref.at[halves[half], :]

        def copy(k, dst, to, src=None):
            return _remote(dst if src is None else src, dst, send_sems, recv_sems, k, to)

        def mine():
            return pltpu.make_async_copy(x_ref, block((x, y), c), local_sem)

        sends = {
            0: lambda: copy(0, block((x, y), c), sibling, src=x_ref),
            1: lambda: copy(1, block((x, y), c), (*xn, c), src=x_ref),
            2: lambda: copy(2, block((x, y), c), (*yn, c), src=x_ref),
            3: lambda: copy(3, block(yn, c, 0), (*xn, c)),
            4: lambda: copy(4, block(xn, c, 1), (*yn, c)),
            5: lambda: copy(5, block(xn, c), sibling),
            6: lambda: copy(6, block(yn, c), sibling),
            7: lambda: copy(7, block(dg, c, 0), sibling),
            8: lambda: copy(8, block(dg, c, 1), sibling),
        }
        lands = {
            0: lambda: copy(0, block((x, y), 1 - c), me), 1: lambda: copy(1, block(xn, c), me),
            2: lambda: copy(2, block(yn, c), me), 3: lambda: copy(3, block(dg, c, 0), me),
            4: lambda: copy(4, block(dg, c, 1), me), 5: lambda: copy(5, block(xn, 1 - c), me),
            6: lambda: copy(6, block(yn, 1 - c), me), 7: lambda: copy(7, block(dg, 1 - c, 0), me),
            8: lambda: copy(8, block(dg, 1 - c, 1), me),
        }

        def start():
            mine().start()
            for k in (0, 1, 2):
                sends[k]().start()

        def forward():
            for arrived, then in ((1, (4, 5)), (2, (3, 6)), (3, (7,)), (4, (8,))):
                lands[arrived]().wait_recv()
                for k in then:
                    sends[k]().start()

        def finish():
            for k in (0, 5, 6, 7, 8):
                lands[k]().wait_recv()
            for k in range(9):
                sends[k]().wait_send()
            mine().wait()

        return [start, forward, finish]


class _SiblingExchange:
    def __init__(self, src):
        self.inputs = (src,)
        self.out_shape = (jax.ShapeDtypeStruct((4, *src.shape[1:]), src.dtype),)
        self.scratch = (pltpu.SemaphoreType.DMA((4,)), pltpu.SemaphoreType.DMA((4,)))

    def phases(self, ins, outs, scr):
        (g_ref,), (recv_ref,), (send_sems, recv_sems) = ins, outs, scr
        x, y, c = _mesh_pos()

        def copies():
            return [_remote(g_ref.at[2 * k + (1 - c)], recv_ref.at[k], send_sems, recv_sems, k, (x, y, 1 - c))
                    for k in range(4)]

        def start():
            for cp in copies():
                cp.start()

        def finish():
            for cp in copies():
                cp.wait()

        return [start, finish]


class _ChipsExchange:
    def __init__(self, src):
        self.inputs = (src,)
        self.out_shape = (jax.ShapeDtypeStruct(src.shape, src.dtype),)
        self.scratch = (pltpu.SemaphoreType.DMA((3,)), pltpu.SemaphoreType.DMA((3,)))

    def phases(self, ins, outs, scr):
        (p_ref,), (recv_ref,), (send_sems, recv_sems) = ins, outs, scr
        x, y, c = _mesh_pos()
        peers = [(1 - x, y, c), (x, 1 - y, c), (1 - x, 1 - y, c)]

        def copies():
            return [_remote(p_ref.at[k], recv_ref.at[k], send_sems, recv_sems, k, peer) for k, peer in enumerate(peers)]

        def start():
            for cp in copies():
                cp.start()

        def finish():
            for cp in copies():
                cp.wait()

        return [start, finish]


class _Together:
    def __init__(self, *plans):
        self.plans = plans
        self.inputs = tuple(a for p in plans for a in p.inputs)
        self.out_shape = tuple(o for p in plans for o in p.out_shape)
        self.scratch = tuple(s for p in plans for s in p.scratch)

    def phases(self, ins, outs, scr):
        each = []
        for p in self.plans:
            n_in, n_out, n_scr = len(p.inputs), len(p.out_shape), len(p.scratch)
            each.append(p.phases(ins[:n_in], outs[:n_out], scr[:n_scr]))
            ins, outs, scr = ins[n_in:], outs[n_out:], scr[n_scr:]

        def run(fns):
            def phase():
                for fn in fns:
                    fn()
            return phase

        middle = [fn for ph in each for fn in ph[1:-1]]
        return [run([ph[0] for ph in each]), *([run(middle)] if middle else []), run([ph[-1] for ph in each])]


def _run_comm(plan, name):
    n_in, n_out = len(plan.inputs), len(plan.out_shape)

    def body(*refs):
        for phase in plan.phases(refs[:n_in], refs[n_in:n_in + n_out], refs[n_in + n_out:]):
            phase()

    return pl.pallas_call(
        body, name=name, out_shape=list(plan.out_shape), in_specs=[HBM] * n_in, out_specs=[HBM] * n_out,
        scratch_shapes=list(plan.scratch))(*plan.inputs)


def _grid_call(body, *, name, nt, in_specs, out_specs, out_shape, scratch_shapes, args, plan=None):
    if plan is None:
        return pl.pallas_call(body, name=name, grid=(nt,), in_specs=in_specs, out_specs=out_specs, out_shape=out_shape,
                              scratch_shapes=scratch_shapes, compiler_params=_params(1))(*args)
    n_in, n_out, n_scr = len(in_specs), len(out_specs), len(scratch_shapes)
    p_in, p_out = len(plan.inputs), len(plan.out_shape)

    def with_plan(*refs):
        ins, refs = refs[:n_in], refs[n_in:]
        p_ins, refs = refs[:p_in], refs[p_in:]
        outs, refs = refs[:n_out], refs[n_out:]
        p_outs, refs = refs[:p_out], refs[p_out:]
        scr, p_scr = refs[:n_scr], refs[n_scr:]
        phases = plan.phases(p_ins, p_outs, p_scr)
        i = pl.program_id(0)
        pl.when(i == 0)(phases[0])
        for phase in phases[1:-1]:
            pl.when(i == min(max(nt - 3, 1), nt - 1))(phase)
        body(*ins, *outs, *scr)
        pl.when(i == nt - 1)(phases[-1])

    return pl.pallas_call(
        with_plan, name=name, grid=(nt,), in_specs=[*in_specs, *[HBM] * p_in], out_specs=[*out_specs, *[HBM] * p_out],
        out_shape=[*out_shape, *plan.out_shape], scratch_shapes=[*scratch_shapes, *plan.scratch],
        compiler_params=_params(1))(*args, *plan.inputs)


def _add_chunks(gslab, recv, gid, rid, name):
    _, rows, cols = gslab.shape
    tr = _row_tile(rows, rows // 2, 16)

    def body(gid_ref, rid_ref, a_ref, b_ref, own_ref, part_ref):
        s = a_ref[...] + b_ref[...]

        @pl.when(pl.program_id(1) == 0)
        def _():
            own_ref[...] = s

        part_ref[...] = s.astype(BF16)

    blk = (1, tr, cols)
    grid_spec = pltpu.PrefetchScalarGridSpec(
        num_scalar_prefetch=2, grid=(rows // tr, 4),
        in_specs=[pl.BlockSpec(blk, lambda i, k, g, r: (g[k], i, 0)), pl.BlockSpec(blk, lambda i, k, g, r: (r[k], i, 0))],
        out_specs=[pl.BlockSpec(blk, lambda i, k, g, r: (0, i, 0)),
                   pl.BlockSpec(blk, lambda i, k, g, r: (jnp.maximum(k - 1, 0), i, 0))])
    return pl.pallas_call(
        body, name=name, grid_spec=grid_spec,
        out_shape=[jax.ShapeDtypeStruct((1, rows, cols), F32), jax.ShapeDtypeStruct((3, rows, cols), BF16)],
        compiler_params=_params(2),
    )(gid, rid, gslab, recv)


def _sum_partials(own, recv, name):
    _, rows, cols = own.shape
    tr = _row_tile(rows, 1024, 16)

    def body(o_ref, r_ref, out_ref):
        acc = o_ref[0]
        for k in range(3):
            acc = acc + r_ref[k].astype(F32)
        out_ref[...] = acc

    return pl.pallas_call(
        body, name=name, grid=(rows // tr,),
        in_specs=[pl.BlockSpec((1, tr, cols), lambda i: (0, i, 0)), pl.BlockSpec((3, tr, cols), lambda i: (0, i, 0))],
        out_specs=pl.BlockSpec((tr, cols), lambda i: (i, 0)),
        out_shape=jax.ShapeDtypeStruct((rows, cols), F32),
        compiler_params=_params(1),
    )(own, recv)


def _chunk_ids():
    x, y, c = _mesh_pos()
    chips = [(x, y), (1 - x, y), (x, 1 - y), (1 - x, 1 - y)]
    gid = jnp.stack([4 * px + 2 * py + c for px, py in chips]).astype(jnp.int32)
    rid = jnp.stack([2 * px + py for px, py in chips]).astype(jnp.int32)
    return gid, rid


def _sibling_sums(slab, recv, tag):
    gid, rid = _chunk_ids()
    return _add_chunks(slab, recv, gid, rid, "rs_add_" + tag)


def _load_weights(slab_ref, offs, dsts, sems):
    cps = []
    for i, (off, dst) in enumerate(zip(offs, dsts)):
        f8 = dst.shape[0] // N_DEV
        cps += [pltpu.make_async_copy(slab_ref.at[j, pl.ds(off, f8), :], dst.at[pl.ds(j * f8, f8), :],
                                      sems.at[i * N_DEV + j]) for j in range(N_DEV)]
    for cp in cps:
        cp.start()
    for cp in cps:
        cp.wait()


def _chunk_rows(c):
    return pl.ds(pl.multiple_of(c * FF_CHUNK, FF_CHUNK), FF_CHUNK)


def _rms(xv):
    return lax.rsqrt(jnp.mean(xv * xv, axis=-1, keepdims=True) + RMS_EPS)


def _loss_terms(xv, tgt, gain):
    r = _rms(xv)
    xh = xv * r
    err = xh * gain - tgt
    loss = 0.5 * jnp.sum(jnp.mean(err * err, axis=-1, keepdims=True))
    dy = err * (1.0 / xv.shape[-1])
    dxh = dy * gain
    return loss, r * (dxh - xh * jnp.mean(dxh * xh, axis=-1, keepdims=True)), _rowsum8(dy * xh)


def _ffn_fwd(x, gain, slab, offs, f, name, plan=None, head=None):
    t, d = x.shape
    nc, tm = f // FF_CHUNK, TOKEN_TILE
    nt = t // tm
    n_head = 0 if head is None else 2

    def body(*refs):
        x_ref, gain_ref, slab_ref = refs[:3]
        xo_ref, g_ref, u_ref, n_ref = refs[3 + n_head:7 + n_head]
        wg_v, wu_v, wd_v, acc_ref, sems = refs[7 + 2 * n_head:]
        i = pl.program_id(0)

        @pl.when(i == 0)
        def _():
            _load_weights(slab_ref, offs, (wg_v, wu_v, wd_v), sems)

        xv = x_ref[...]
        n_ref[...] = ((xv * _rms(xv)) * gain_ref[...]).astype(BF16)
        acc_ref[...] = jnp.zeros_like(acc_ref)

        def chunk(c, carry):
            rows = _chunk_rows(c)
            nb = n_ref[...]
            g = _dot_nt(nb, wg_v[rows, :])
            u = _dot_nt(nb, wu_v[rows, :])
            g_ref[c] = g.astype(BF16)
            u_ref[c] = u.astype(BF16)
            h = (g * jax.nn.sigmoid(g)) * u
            acc_ref[...] += _dot_nn(h.astype(BF16), wd_v[rows, :])
            return carry

        lax.fori_loop(0, nc, chunk, 0, unroll=True)
        out = xv + FFN_RES_WEIGHT * acc_ref[...]
        if head is None:
            xo_ref[...] = out
            return
        tgt_ref, fgain_ref = refs[3:5]
        dgain_ref, loss_ref = refs[7 + n_head:7 + 2 * n_head]

        @pl.when(i == 0)
        def _():
            dgain_ref[...] = jnp.zeros_like(dgain_ref)
            loss_ref[...] = jnp.zeros_like(loss_ref)

        loss, dx, dgain = _loss_terms(out, tgt_ref[...], fgain_ref[...])
        xo_ref[...] = dx
        loss_ref[...] += loss
        dgain_ref[...] += dgain

        @pl.when(i == nt - 1)
        def _():
            _fold8(dgain_ref)

    tile = pl.BlockSpec((tm, d), lambda i: (i, 0))
    act = pl.BlockSpec((nc, tm, FF_CHUNK), lambda i: (0, i, 0))
    head_in = [] if head is None else [tile, _full((1, d))]
    head_out = [] if head is None else [_full((SUBLANES, d)), _full((SUBLANES, 128))]
    head_shape = [] if head is None else [jax.ShapeDtypeStruct((SUBLANES, d), F32), jax.ShapeDtypeStruct((SUBLANES, 128), F32)]
    return _grid_call(
        body, name=name, nt=nt, plan=plan,
        in_specs=[tile, _full((1, d)), HBM, *head_in],
        out_specs=[tile, act, act, tile, *head_out],
        out_shape=[jax.ShapeDtypeStruct((t, d), F32), jax.ShapeDtypeStruct((nc, t, FF_CHUNK), BF16),
                   jax.ShapeDtypeStruct((nc, t, FF_CHUNK), BF16), jax.ShapeDtypeStruct((t, d), BF16), *head_shape],
        scratch_shapes=[pltpu.VMEM((f, d), BF16), pltpu.VMEM((f, d), BF16), pltpu.VMEM((f, d), BF16),
                        pltpu.VMEM((tm, d), F32), pltpu.SemaphoreType.DMA((3 * N_DEV,))],
        args=(x, gain, slab, *([] if head is None else head)))


def _ffn_bwd(dxo, x, gain, gs, us, slab, offs, f, name, plan=None):
    t, d = x.shape
    nc, tm = f // FF_CHUNK, FFN_BWD_TILE
    nt = t // tm

    def body(dxo_ref, x_ref, gain_ref, g_ref, u_ref, slab_ref,
             dx_ref, h_ref, dg_ref, du_ref, df_ref, dgain_ref, wg_v, wu_v, wd_v, sems):
        i = pl.program_id(0)

        @pl.when(i == 0)
        def _():
            _load_weights(slab_ref, offs, (wg_v, wu_v, wd_v), sems)
            dgain_ref[...] = jnp.zeros_like(dgain_ref)

        df_ref[...] = (FFN_RES_WEIGHT * dxo_ref[...]).astype(BF16)

        def chunk(c):
            rows = pl.ds(c * FF_CHUNK, FF_CHUNK)
            g = g_ref[c].astype(F32)
            u = u_ref[c].astype(F32)
            sg = jax.nn.sigmoid(g)
            sil = g * sg
            dh = _dot_nt(df_ref[...], wd_v[rows, :])
            h_ref[c] = (sil * u).astype(BF16)
            du_ref[c] = (dh * sil).astype(BF16)
            dg_ref[c] = (dh * u * (sg * (1.0 + g * (1.0 - sg)))).astype(BF16)

        def back(c, dn):
            rows = pl.ds(c * FF_CHUNK, FF_CHUNK)
            return dn + _dot_nn(dg_ref[c], wg_v[rows, :]) + _dot_nn(du_ref[c], wu_v[rows, :])

        dn = jnp.zeros((tm, d), F32)
        for c in range(nc):
            chunk(c)
            if c:
                dn = back(c - 1, dn)
        dn = back(nc - 1, dn)
        xv = x_ref[...]
        r = _rms(xv)
        xh = xv * r
        dgain_ref[...] += _rowsum8(dn * xh)
        dxh = dn * gain_ref[...]
        dx_ref[...] = dxo_ref[...] + r * (dxh - xh * jnp.mean(dxh * xh, axis=-1, keepdims=True))

        @pl.when(i == nt - 1)
        def _():
            _fold8(dgain_ref)

    tile = pl.BlockSpec((tm, d), lambda i: (i, 0))
    act = pl.BlockSpec((nc, tm, FF_CHUNK), lambda i: (0, i, 0))
    act_shape = jax.ShapeDtypeStruct((nc, t, FF_CHUNK), BF16)
    return _grid_call(
        body, name=name, nt=nt, plan=plan,
        in_specs=[tile, tile, _full((1, d)), act, act, HBM],
        out_specs=[tile, act, act, act, tile, _full((SUBLANES, d))],
        out_shape=[jax.ShapeDtypeStruct((t, d), F32), act_shape, act_shape, act_shape,
                   jax.ShapeDtypeStruct((t, d), BF16), jax.ShapeDtypeStruct((SUBLANES, d), F32)],
        scratch_shapes=[pltpu.VMEM((f, d), BF16), pltpu.VMEM((f, d), BF16), pltpu.VMEM((f, d), BF16),
                        pltpu.SemaphoreType.DMA((3 * N_DEV,))],
        args=(dxo, x, gain, gs, us, slab))


def _tn_chunked(a, b, name, plan=None):
    nc, t, _ = a.shape
    n = b.shape[1]
    tb = _row_tile(t, 1024, TOKEN_TILE)

    def body(a_ref, b_ref, o_ref):
        @pl.when(pl.program_id(0) == 0)
        def _():
            o_ref[...] = jnp.zeros_like(o_ref)

        def chunk(c, carry):
            rows = _chunk_rows(c)
            o_ref[rows, :] += _dot_tn(a_ref[c], b_ref[...])
            return carry

        lax.fori_loop(0, nc, chunk, 0, unroll=True)

    return _grid_call(
        body, name=name, nt=t // tb, plan=plan,
        in_specs=[pl.BlockSpec((nc, tb, FF_CHUNK), lambda i: (0, i, 0)), pl.BlockSpec((tb, n), lambda i: (i, 0))],
        out_specs=[_full((nc * FF_CHUNK, n))],
        out_shape=[jax.ShapeDtypeStruct((nc * FF_CHUNK, n), F32)],
        scratch_shapes=[], args=(a, b))


def _tn(a, b, name, plan=None):
    t, k = a.shape
    n = b.shape[1]
    tb = _row_tile(t, 1024, TOKEN_TILE)

    def body(a_ref, b_ref, o_ref):
        @pl.when(pl.program_id(0) == 0)
        def _():
            o_ref[...] = jnp.zeros_like(o_ref)

        o_ref[...] += _dot_tn(a_ref[...].astype(BF16), b_ref[...].astype(BF16))

    return _grid_call(
        body, name=name, nt=t // tb, plan=plan,
        in_specs=[pl.BlockSpec((tb, k), lambda i: (i, 0)), pl.BlockSpec((tb, n), lambda i: (i, 0))],
        out_specs=[_full((k, n))],
        out_shape=[jax.ShapeDtypeStruct((k, n), F32)],
        scratch_shapes=[], args=(a, b))


def _layernorm_stats(u1):
    mu = jnp.mean(u1, axis=-1, keepdims=True)
    xc = u1 - mu
    rstd = lax.rsqrt(jnp.mean(xc * xc, axis=-1, keepdims=True) + LN_EPS)
    return xc * rstd, rstd


def _positions(tile_index, tm):
    return (tile_index * tm + lax.broadcasted_iota(jnp.int32, (tm, 1), 0)).astype(F32)


def _shifted_taps(src_ref, sh_ref, tm, offset_of):
    groups = {}
    for k in range(CONV_WIDTH):
        groups.setdefault(offset_of(k) % SUBLANES, []).append(k)
    span = tm + HALO - SUBLANES
    for rem, taps in sorted(groups.items()):
        if rem:
            sh_ref[0:span, :] = src_ref[rem:rem + span, :]
        ref = sh_ref if rem else src_ref
        for k in taps:
            base = offset_of(k) - rem
            yield k, ref[base:base + tm, :]


def _mix_fwd(x, gm, win_t, cdw, cb, lg, lb, pw, poolw, ps, wout):
    t, d = x.shape
    tm = TOKEN_TILE

    def body(x_ref, gm_ref, win_ref, cdw_ref, cb_ref, lg_ref, lb_ref, pw_ref, poolw_ref, ps_ref, wout_ref,
             xo_ref, h_ref, ag_ref, u0_ref, u1_ref, u2_ref, mixed_ref, cat_ref, eu_ref, ep_ref, sh_ref):
        i = pl.program_id(0)

        @pl.when(i == 0)
        def _():
            eu_ref[0:HALO, :] = jnp.zeros((HALO, D_CONV), F32)
            ep_ref[0:HALO, :] = jnp.zeros((HALO, D_POOL), F32)

        @pl.when(i > 0)
        def _():
            eu_ref[0:HALO, :] = eu_ref[tm:tm + HALO, :]
            ep_ref[0:HALO, :] = ep_ref[tm:tm + HALO, :]

        xv = x_ref[...]
        hb = ((xv * _rms(xv)) * gm_ref[...]).astype(BF16)
        h_ref[...] = hb
        proj = _dot_nt(hb, win_ref[...])
        a = proj[:, :D_CONV]
        g = proj[:, D_CONV:2 * D_CONV]
        ag_ref[...] = proj[:, :2 * D_CONV]
        u0 = a * jax.nn.sigmoid(g)
        u0_ref[...] = u0
        eu_ref[HALO:HALO + tm, :] = u0
        ep_ref[HALO:HALO + tm, :] = proj[:, 2 * D_CONV:]

        u1 = jnp.broadcast_to(cb_ref[...], (tm, D_CONV))
        for k, rows in _shifted_taps(eu_ref, sh_ref, tm, lambda k: HALO - (CONV_WIDTH - 1) + k):
            u1 = u1 + cdw_ref[k:k + 1, :] * rows
        u1_ref[...] = u1
        lnh, _ = _layernorm_stats(u1)
        ln = lnh * lg_ref[...] + lb_ref[...]
        u2 = (ln * jax.nn.sigmoid(ln)).astype(BF16)
        u2_ref[...] = u2
        conv_out = _dot_nn(u2, pw_ref[...])

        pos = _positions(i, tm)
        outs = []
        for gi, w in enumerate(POOL_WINDOWS):
            lo = gi * POOL_GROUP
            p = ep_ref[HALO:HALO + tm, lo:lo + POOL_GROUP]
            s = p
            for j in range(1, w):
                s = s + ep_ref[HALO - j:HALO - j + tm, lo:lo + POOL_GROUP]
            mixed = (s / jnp.minimum(pos + 1.0, float(w)) - p).astype(BF16)
            mixed_ref[:, lo:lo + POOL_GROUP] = mixed
            outs.append(_dot_nn(mixed, poolw_ref[gi]))
        pool_out = jnp.concatenate(outs, axis=-1) * ps_ref[...]
        cat = jnp.concatenate([conv_out, pool_out], axis=-1).astype(BF16)
        cat_ref[...] = cat
        xo_ref[...] = xv + _dot_nn(cat, wout_ref[...])

    def tile(c):
        return pl.BlockSpec((tm, c), lambda i: (i, 0))

    def out(c, dt):
        return jax.ShapeDtypeStruct((t, c), dt)

    return pl.pallas_call(
        body, name="mix_fwd", grid=(t // tm,),
        in_specs=[tile(d), _full((1, d)), _full((D_IN, d)), _full((HALO, D_CONV)), _full((1, D_CONV)),
                  _full((1, D_CONV)), _full((1, D_CONV)), _full((D_CONV, D_CONV)),
                  _full((len(POOL_WINDOWS), POOL_GROUP, POOL_GROUP)), _full((1, D_POOL)), _full((d, d))],
        out_specs=[tile(d), tile(d), tile(2 * D_CONV), tile(D_CONV), tile(D_CONV), tile(D_CONV), tile(D_POOL), tile(d)],
        out_shape=[out(d, F32), out(d, BF16), out(2 * D_CONV, F32), out(D_CONV, F32), out(D_CONV, F32),
                   out(D_CONV, BF16), out(D_POOL, BF16), out(d, BF16)],
        scratch_shapes=[pltpu.VMEM((HALO + tm, D_CONV), F32), pltpu.VMEM((HALO + tm, D_POOL), F32),
                        pltpu.VMEM((HALO + tm, D_CONV), F32)],
        compiler_params=_params(1),
    )(x, gm, win_t, cdw, cb, lg, lb, pw, poolw, ps, wout)


def _mix_bwd(dxo, x, gm, ag, u0, u1, mixed, win_t, cdw, lg, lb, pw, poolw, ps, wout, plan=None):
    t, d = x.shape
    tm = TOKEN_TILE
    nt = t // tm
    halo_blocks = tm // HALO

    def body(dxo_ref, x_ref, gm_ref, ag_ref, u0_ref, u0h_ref, u1_ref, mixed_ref,
             win_ref, cdw_ref, lg_ref, lb_ref, pw_ref, poolw_ref, ps_ref, wout_ref,
             dx_ref, dproj_ref, dco_ref, dgm_ref, dcdw_ref, dcb_ref, dlg_ref, dlb_ref, dpoolw_ref, dps_ref,
             eu_ref, ed_ref, eq_ref, sh_ref):
        i = pl.program_id(0)
        ti = nt - 1 - i

        @pl.when(i == 0)
        def _():
            for ref in (dgm_ref, dcdw_ref, dcb_ref, dlg_ref, dlb_ref, dpoolw_ref, dps_ref):
                ref[...] = jnp.zeros_like(ref)
            ed_ref[tm:tm + HALO, :] = jnp.zeros((HALO, D_CONV), F32)
            eq_ref[tm:tm + HALO, :] = jnp.zeros((HALO, D_POOL), F32)

        @pl.when(i > 0)
        def _():
            ed_ref[tm:tm + HALO, :] = ed_ref[0:HALO, :]
            eq_ref[tm:tm + HALO, :] = eq_ref[0:HALO, :]

        @pl.when(ti == 0)
        def _():
            eu_ref[0:HALO, :] = jnp.zeros((HALO, D_CONV), F32)

        @pl.when(ti > 0)
        def _():
            eu_ref[0:HALO, :] = u0h_ref[...]

        eu_ref[HALO:HALO + tm, :] = u0_ref[...]

        dxo = dxo_ref[...]
        dcat = _dot_nt(dxo.astype(BF16), wout_ref[...])
        dco = dcat[:, :D_CONV].astype(BF16)
        dco_ref[...] = dco
        dpo = dcat[:, D_CONV:]

        lnh, rstd = _layernorm_stats(u1_ref[...])
        ln = lnh * lg_ref[...] + lb_ref[...]
        sl = jax.nn.sigmoid(ln)
        dln = _dot_nt(dco, pw_ref[...]) * (sl * (1.0 + ln * (1.0 - sl)))
        dlg_ref[...] += _rowsum8(dln * lnh)
        dlb_ref[...] += _rowsum8(dln)
        dlnh = dln * lg_ref[...]
        du1 = rstd * (dlnh - jnp.mean(dlnh, axis=-1, keepdims=True)
                      - lnh * jnp.mean(dlnh * lnh, axis=-1, keepdims=True))
        dcb_ref[...] += _rowsum8(du1)
        ed_ref[0:tm, :] = du1

        du0 = jnp.zeros((tm, D_CONV), F32)
        for k, rows in _shifted_taps(ed_ref, sh_ref, tm, lambda k: CONV_WIDTH - 1 - k):
            du0 = du0 + cdw_ref[k:k + 1, :] * rows
        for k, rows in _shifted_taps(eu_ref, sh_ref, tm, lambda k: HALO - (CONV_WIDTH - 1) + k):
            dcdw_ref[SUBLANES * k:SUBLANES * (k + 1), :] += _rowsum8(du1 * rows)
        a = ag_ref[:, :D_CONV]
        sg = jax.nn.sigmoid(ag_ref[:, D_CONV:])
        pieces = [du0 * sg, du0 * a * (sg * (1.0 - sg))]

        pos = _positions(ti, tm)
        for gi, w in enumerate(POOL_WINDOWS):
            lo = gi * POOL_GROUP
            mg = mixed_ref[:, lo:lo + POOL_GROUP]
            dpo_g = dpo[:, lo:lo + POOL_GROUP]
            dps_ref[:, lo:lo + POOL_GROUP] += _rowsum8(dpo_g * _dot_nn(mg, poolw_ref[gi]))
            dout = (dpo_g * ps_ref[:, lo:lo + POOL_GROUP]).astype(BF16)
            dpoolw_ref[gi] += _dot_tn(mg, dout)
            dmx = _dot_nt(dout, poolw_ref[gi])
            q = dmx / jnp.minimum(pos + 1.0, float(w))
            eq_ref[0:tm, lo:lo + POOL_GROUP] = q
            s = q
            for j in range(1, w):
                s = s + eq_ref[j:j + tm, lo:lo + POOL_GROUP]
            pieces.append(s - dmx)
        dproj = jnp.concatenate(pieces, axis=-1).astype(BF16)
        dproj_ref[...] = dproj

        dh = _dot_nn(dproj, win_ref[...])
        xv = x_ref[...]
        r = _rms(xv)
        xh = xv * r
        dgm_ref[...] += _rowsum8(dh * xh)
        dxh = dh * gm_ref[...]
        dx_ref[...] = dxo + r * (dxh - xh * jnp.mean(dxh * xh, axis=-1, keepdims=True))

        @pl.when(i == nt - 1)
        def _():
            for ref in (dgm_ref, dcb_ref, dlg_ref, dlb_ref, dps_ref):
                _fold8(ref)
            for k in range(CONV_WIDTH):
                dcdw_ref[SUBLANES * k:SUBLANES * k + 1, :] = jnp.sum(
                    dcdw_ref[SUBLANES * k:SUBLANES * (k + 1), :], axis=0, keepdims=True)

    def tile(c):
        return pl.BlockSpec((tm, c), lambda i: (nt - 1 - i, 0))

    halo = pl.BlockSpec((HALO, D_CONV), lambda i: (jnp.maximum((nt - 1 - i) * halo_blocks - 1, 0), 0))
    n_groups = len(POOL_WINDOWS)
    return _grid_call(
        body, name="mix_bwd", nt=nt, plan=plan,
        in_specs=[tile(d), tile(d), _full((1, d)), tile(2 * D_CONV), tile(D_CONV), halo, tile(D_CONV), tile(D_POOL),
                  _full((D_IN, d)), _full((HALO, D_CONV)), _full((1, D_CONV)), _full((1, D_CONV)),
                  _full((D_CONV, D_CONV)), _full((n_groups, POOL_GROUP, POOL_GROUP)), _full((1, D_POOL)), _full((d, d))],
        out_specs=[tile(d), tile(D_IN), tile(D_CONV), _full((SUBLANES, d)), _full((HALO * SUBLANES, D_CONV)),
                   _full((SUBLANES, D_CONV)), _full((SUBLANES, D_CONV)), _full((SUBLANES, D_CONV)),
                   _full((n_groups, POOL_GROUP, POOL_GROUP)), _full((SUBLANES, D_POOL))],
        out_shape=[jax.ShapeDtypeStruct((t, d), F32), jax.ShapeDtypeStruct((t, D_IN), BF16),
                   jax.ShapeDtypeStruct((t, D_CONV), BF16), jax.ShapeDtypeStruct((SUBLANES, d), F32),
                   jax.ShapeDtypeStruct((HALO * SUBLANES, D_CONV), F32), jax.ShapeDtypeStruct((SUBLANES, D_CONV), F32),
                   jax.ShapeDtypeStruct((SUBLANES, D_CONV), F32), jax.ShapeDtypeStruct((SUBLANES, D_CONV), F32),
                   jax.ShapeDtypeStruct((n_groups, POOL_GROUP, POOL_GROUP), F32),
                   jax.ShapeDtypeStruct((SUBLANES, D_POOL), F32)],
        scratch_shapes=[pltpu.VMEM((HALO + tm, D_CONV), F32), pltpu.VMEM((tm + HALO, D_CONV), F32),
                        pltpu.VMEM((tm + HALO, D_POOL), F32), pltpu.VMEM((tm + HALO, D_CONV), F32)],
        args=(dxo, x, gm, ag, u0, u0, u1, mixed, win_t, cdw, lg, lb, pw, poolw, ps, wout))


def _adam_step(gv, w_ref, m_ref, v_ref, d_ref, nm_ref, nv_ref):
    nm = ADAM_B1 * m_ref[...] + (1.0 - ADAM_B1) * gv
    nv = ADAM_B2 * v_ref[...] + (1.0 - ADAM_B2) * (gv * gv)
    m_hat = nm / (1.0 - ADAM_B1 ** ADAM_STEP)
    v_hat = nv / (1.0 - ADAM_B2 ** ADAM_STEP)
    d_ref[...] = -ADAM_LR * (m_hat / (jnp.sqrt(v_hat) + ADAM_EPS) + ADAM_WD * w_ref[...])
    nm_ref[...] = nm
    nv_ref[...] = nv


def _adam_tile(rows, cols):
    return _row_tile(rows, max(SUBLANES, (256 * 1024) // cols // SUBLANES * SUBLANES), SUBLANES)


def _adamw_many(items, name):
    n = len(items)

    def body(*refs):
        ins, outs = refs[:4 * n], refs[4 * n:]
        for k in range(n):
            w_ref, g_ref, m_ref, v_ref = ins[4 * k:4 * k + 4]
            _adam_step(g_ref[...], w_ref, m_ref, v_ref, *outs[3 * k:3 * k + 3])

    return pl.pallas_call(
        body, name=name, out_shape=[jax.ShapeDtypeStruct(it[0].shape, F32) for it in items for _ in range(3)],
        compiler_params=_params(0))(*[a for it in items for a in it])


def _adamw_reduced(w, own, recv, m, v, name):
    rows, cols = w.shape
    tr = _adam_tile(rows, cols)

    def body(w_ref, o_ref, r_ref, m_ref, v_ref, g_ref, d_ref, nm_ref, nv_ref):
        gv = o_ref[0]
        for k in range(3):
            gv = gv + r_ref[k].astype(F32)
        g_ref[...] = gv
        _adam_step(gv, w_ref, m_ref, v_ref, d_ref, nm_ref, nv_ref)

    blk = pl.BlockSpec((tr, cols), lambda i: (i, 0))
    shape = jax.ShapeDtypeStruct((rows, cols), F32)
    return pl.pallas_call(
        body, name=name, grid=(rows // tr,),
        in_specs=[blk, pl.BlockSpec((1, tr, cols), lambda i: (0, i, 0)), pl.BlockSpec((3, tr, cols), lambda i: (0, i, 0)),
                  blk, blk],
        out_specs=[blk] * 4, out_shape=[shape] * 4,
        compiler_params=_params(1),
    )(w, own, recv, m, v)


def _as_2d(a):
    if a.ndim == 1:
        return a.reshape(a.shape[0] // 128, 128)
    if a.ndim == 3:
        return a.reshape(a.shape[0] * a.shape[1], a.shape[2])
    return a


def _pack_weight_slabs(p):
    def bf(parts):
        return [a.astype(BF16) for a in parts]

    cdw_bits = lax.bitcast_convert_type(p["conv_dw"], BF16).reshape(CONV_WIDTH, 2 * D_CONV // N_DEV)
    cdw_bits = jnp.pad(cdw_bits, ((0, 1), (0, 0))).reshape(4, D_MODEL)
    cdw_bits = jnp.pad(cdw_bits, ((0, CDW_ROWS - 4), (0, 0)))
    first = bf([p["ffn1_w_gate"].T, p["ffn1_w_up"].T, p["ffn1_w_down"]])
    rest = bf([p["ffn2_w_gate"].T, p["ffn2_w_up"].T, p["ffn2_w_down"], p["w_in"].T, p["w_out"],
               p["conv_pw"].reshape(D_CONV // N_DEV // 2, D_MODEL)]) + [cdw_bits]
    return jnp.concatenate(first, axis=0), jnp.concatenate(rest, axis=0)


def _unpack_rows(slab, offs, names):
    out = {}
    for name in names:
        o, n = offs[name]
        out[name] = slab[:, o:o + n, :].reshape(N_DEV * n, D_MODEL)
    return out


def _unpack_conv_taps(slab, offs):
    o, _ = offs["cdw"]
    bits = slab[:, o:o + 4, :].reshape(N_DEV, CONV_WIDTH + 1, D_CONV // N_DEV, 2)[:, :CONV_WIDTH]
    cdw = lax.bitcast_convert_type(bits, F32)
    return jnp.transpose(cdw, (1, 0, 2)).reshape(CONV_WIDTH, D_CONV)


def kernel(x, ffn1_norm, ffn1_w_gate, ffn1_w_up, ffn1_w_down, mix_norm, w_in, conv_dw, conv_dw_b, conv_ln_g, conv_ln_b, conv_pw, pool_w, pool_scale, w_out, ffn2_norm, ffn2_w_gate, ffn2_w_up, ffn2_w_down, final_norm, loss_target, m_ffn1_norm, m_ffn1_w_gate, m_ffn1_w_up, m_ffn1_w_down, m_mix_norm, m_w_in, m_conv_dw, m_conv_dw_b, m_conv_ln_g, m_conv_ln_b, m_conv_pw, m_pool_w, m_pool_scale, m_w_out, m_ffn2_norm, m_ffn2_w_gate, m_ffn2_w_up, m_ffn2_w_down, m_final_norm, v_ffn1_norm, v_ffn1_w_gate, v_ffn1_w_up, v_ffn1_w_down, v_mix_norm, v_w_in, v_conv_dw, v_conv_dw_b, v_conv_ln_g, v_conv_ln_b, v_conv_pw, v_pool_w, v_pool_scale, v_w_out, v_ffn2_norm, v_ffn2_w_gate, v_ffn2_w_up, v_ffn2_w_down, v_final_norm):
    given = dict(locals())
    p = {n: given[n] for n in WEIGHTS}
    f8 = ffn1_w_gate.shape[1]
    f = N_DEV * f8
    ffn_rows = (0, f8, 2 * f8)
    small = (("win", D_IN // N_DEV), ("wout", D_MODEL // N_DEV), ("pw", D_CONV // N_DEV // 2), ("cdw", CDW_ROWS))
    w_offs, _ = _layout((("g2", f8), ("u2", f8), ("d2", f8)) + small)
    s_offs, _ = _layout(small + (("rep", REP_ROWS),))
    x0 = x[0]
    target = loss_target[0]

    def row(vec):
        return vec.reshape(1, vec.shape[0])

    slab_first, slab_rest = _pack_weight_slabs(p)
    w_first = _run_comm(_GatherRelay(slab_first), "gather_ffn1")[0]
    x1, g1s, u1s, n1, w_rest = _ffn_fwd(x0, row(ffn1_norm), w_first, ffn_rows, f, "ffn1_fwd", _Gather(slab_rest))
    w = _unpack_rows(w_rest, w_offs, ("win", "wout", "pw"))
    w["pw"] = w["pw"].reshape(D_CONV, D_CONV)
    cdw = jnp.pad(_unpack_conv_taps(w_rest, w_offs), ((0, HALO - CONV_WIDTH), (0, 0)))
    poolw = pool_w.astype(BF16)

    x2, h, ag, u0, u1, u2, mixed, cat = _mix_fwd(
        x1, row(mix_norm), w["win"], cdw, row(conv_dw_b), row(conv_ln_g), row(conv_ln_b), w["pw"], poolw,
        row(pool_scale), w["wout"])
    dx3, g2s, u2s, n2, d_final_norm, loss_part = _ffn_fwd(
        x2, row(ffn2_norm), w_rest, ffn_rows, f, "ffn2_fwd", head=(target, row(final_norm)))

    pending, reduced = {}, {}

    def chunks(a):
        return a.reshape(N_DEV, -1, D_MODEL)

    def after_sibling(name, slab, recv):
        pending[name], part = _sibling_sums(slab, recv, name)
        return _ChipsExchange(part)

    def after_chips(name, recv):
        reduced[name] = (pending.pop(name), recv)

    dx2, h2, dg2, du2, df2, d_ffn2_norm = _ffn_bwd(dx3, x2, row(ffn2_norm), g2s, u2s, w_rest, ffn_rows, f, "ffn2_bwd")
    s_g2 = chunks(_tn_chunked(dg2, n2, "ffn2_dgate")[0])
    s_u2, r = _tn_chunked(du2, n2, "ffn2_dup", _SiblingExchange(s_g2))
    s_u2 = chunks(s_u2)
    to_chips = after_sibling("g2", s_g2, r)
    s_d2, rc, r = _tn_chunked(h2, df2, "ffn2_ddown", _Together(to_chips, _SiblingExchange(s_u2)))
    s_d2 = chunks(s_d2)
    after_chips("g2", rc)
    to_chips = after_sibling("u2", s_u2, r)
    dx1, dproj, dco, d_mix_norm, d_cdw, d_cb, d_lg, d_lb, d_poolw, d_ps, rc, r = _mix_bwd(
        dx2, x1, row(mix_norm), ag, u0, u1, mixed, w["win"], cdw, row(conv_ln_g), row(conv_ln_b), w["pw"], poolw,
        row(pool_scale), w["wout"], _Together(to_chips, _SiblingExchange(s_d2)))
    after_chips("u2", rc)
    to_chips = after_sibling("d2", s_d2, r)
    d_win, rc = _tn(dproj, h, "mix_dwin", to_chips)
    after_chips("d2", rc)
    d_wout = _tn(cat, dx2, "mix_dwout")[0]
    d_pw = _tn(u2, dco, "mix_dpw")[0]
    dx0, h1, dg1, du1, df1, d_ffn1_norm = _ffn_bwd(dx1, x0, row(ffn1_norm), g1s, u1s, w_first, ffn_rows, f, "ffn1_bwd")

    d_cdw = d_cdw.reshape(HALO, SUBLANES, D_CONV)[:CONV_WIDTH, 0]
    d_cdw = jnp.transpose(d_cdw.reshape(CONV_WIDTH, N_DEV, D_CONV // N_DEV), (1, 0, 2)).reshape(N_DEV, -1)
    d_cdw = jnp.pad(d_cdw, ((0, 0), (0, CDW_ROWS * D_MODEL - d_cdw.shape[1]))).reshape(N_DEV, CDW_ROWS, D_MODEL)
    rep = jnp.concatenate([
        d_ffn1_norm[0:1], d_mix_norm[0:1], d_ffn2_norm[0:1], d_final_norm[0:1],
        jnp.concatenate([d_cb[0:1], d_lg[0:1]], axis=1), jnp.concatenate([d_lb[0:1], d_ps[0:1]], axis=1),
        jnp.zeros((2, D_MODEL), F32), d_poolw.reshape(-1, D_MODEL)], axis=0)
    rep = jnp.pad(rep, ((0, N_DEV * REP_ROWS - rep.shape[0]), (0, 0))).reshape(N_DEV, REP_ROWS, D_MODEL)
    s_small = jnp.concatenate([chunks(d_win), chunks(d_wout), chunks(d_pw), d_cdw, rep], axis=1)

    s_g1, r = _tn_chunked(dg1, n1, "ffn1_dgate", _SiblingExchange(s_small))
    s_g1 = chunks(s_g1)
    to_chips = after_sibling("small", s_small, r)
    s_u1, rc, r = _tn_chunked(du1, n1, "ffn1_dup", _Together(to_chips, _SiblingExchange(s_g1)))
    s_u1 = chunks(s_u1)
    mine_small = _sum_partials(pending.pop("small"), rc, "rs_sum_small")
    to_chips = after_sibling("g1", s_g1, r)
    o_rep, _ = s_offs["rep"]
    loss_rows = jnp.pad(loss_part, ((0, 0), (0, D_MODEL - loss_part.shape[1])))
    share = _Gather(jnp.concatenate([mine_small[o_rep:o_rep + REP_ROWS], loss_rows], axis=0))
    s_d1, rc, r, shared = _tn_chunked(h1, df1, "ffn1_ddown", _Together(to_chips, _SiblingExchange(s_u1), share))
    s_d1 = chunks(s_d1)
    after_chips("g1", rc)
    to_chips = after_sibling("u1", s_u1, r)
    rc, r = _run_comm(_Together(to_chips, _SiblingExchange(s_d1)), "rs_tail_up")
    after_chips("u1", rc)
    rc, = _run_comm(after_sibling("d1", s_d1, r), "rs_tail_down")
    after_chips("d1", rc)

    rep_all = shared[:, :REP_ROWS].reshape(N_DEV * REP_ROWS, D_MODEL)
    loss = jnp.sum(shared[:, REP_ROWS, 0])

    def small_rows(name):
        o, n = s_offs[name]
        return mine_small[o:o + n]

    g = {
        "ffn1_norm": rep_all[0], "mix_norm": rep_all[1], "ffn2_norm": rep_all[2], "final_norm": rep_all[3],
        "conv_dw_b": rep_all[4, :D_CONV], "conv_ln_g": rep_all[4, D_CONV:],
        "conv_ln_b": rep_all[5, :D_CONV], "pool_scale": rep_all[5, D_CONV:],
        "pool_w": rep_all[8:8 + pool_w.size // D_MODEL].reshape(pool_w.shape),
        "w_out": small_rows("wout"), "conv_pw": small_rows("pw").reshape(conv_pw.shape),
        "conv_dw": small_rows("cdw").reshape(-1)[:conv_dw.size].reshape(conv_dw.shape),
    }

    slab_of = {"ffn1_w_gate": "g1", "ffn1_w_up": "u1", "ffn1_w_down": "d1",
               "ffn2_w_gate": "g2", "ffn2_w_up": "u2", "ffn2_w_down": "d2"}
    transposed = ("ffn1_w_gate", "ffn1_w_up", "ffn2_w_gate", "ffn2_w_up", "w_in")
    g["w_in"] = small_rows("win")
    delta, new_m, new_v = {}, {}, {}

    def operands(n):
        wmv = [given[k] for k in (n, "m_" + n, "v_" + n)]
        return [a.T for a in wmv] if n in transposed else wmv

    def restore(n, a):
        return a.T if n in transposed else a.reshape(p[n].shape)

    others = [n for n in WEIGHTS if n not in slab_of]
    flat = _adamw_many([[_as_2d(a) for a in (wn, g[n], mn, vn)] for n in others for wn, mn, vn in [operands(n)]],
                       "adamw_small")
    for k, n in enumerate(others):
        delta[n], new_m[n], new_v[n] = (restore(n, a) for a in flat[3 * k:3 * k + 3])
    g["w_in"] = g["w_in"].T
    for n, slab in slab_of.items():
        wn, mn, vn = operands(n)
        outs = _adamw_reduced(wn, *reduced[slab], mn, vn, "adamw_" + n)
        g[n], delta[n], new_m[n], new_v[n] = (restore(n, a) for a in outs)

    return (loss, dx0[None], *[g[n] for n in WEIGHTS], *[delta[n] for n in WEIGHTS],
            *[new_m[n] for n in WEIGHTS], *[new_v[n] for n in WEIGHTS])
```

```python
import functools

import jax
import jax.numpy as jnp
from jax import lax
from jax.experimental import pallas as pl
from jax.experimental.pallas import tpu as pltpu

F32 = jnp.float32
BF16 = jnp.bfloat16

D_MODEL = 1024
D_CONV = 512
D_POOL = 512
D_IN = 2 * D_CONV + D_POOL
POOL_WINDOWS = (2, 4, 8, 16)
POOL_GROUP = D_POOL // len(POOL_WINDOWS)
CONV_WIDTH = 31
RMS_EPS = 1e-6
LN_EPS = 1e-5
FFN_RES_WEIGHT = 0.5

ADAM_LR = 0.001
ADAM_B1 = 0.9
ADAM_B2 = 0.999
ADAM_EPS = 1e-08
ADAM_WD = 0.01
ADAM_STEP = 10

N_DEV = 8
MESH_ID = pl.DeviceIdType.MESH

SUBLANES = 8
TOKEN_TILE = 512
FFN_BWD_TILE = 256
FF_CHUNK = 256
HALO = 32
V7X_VMEM_LIMIT = 56 * 1024 * 1024
CDW_ROWS = 16
REP_ROWS = 16

WEIGHTS = ("ffn1_norm", "ffn1_w_gate", "ffn1_w_up", "ffn1_w_down", "mix_norm", "w_in", "conv_dw", "conv_dw_b",
           "conv_ln_g", "conv_ln_b", "conv_pw", "pool_w", "pool_scale", "w_out", "ffn2_norm", "ffn2_w_gate",
           "ffn2_w_up", "ffn2_w_down", "final_norm")


def _dot_nn(a, b):
    return lax.dot_general(a, b, (((1,), (0,)), ((), ())), preferred_element_type=F32)


def _dot_nt(a, b):
    return lax.dot_general(a, b, (((1,), (1,)), ((), ())), preferred_element_type=F32)


def _dot_tn(a, b):
    return lax.dot_general(a, b, (((0,), (0,)), ((), ())), preferred_element_type=F32)


def _rowsum8(v):
    r, c = v.shape
    return jnp.sum(v.reshape(r // SUBLANES, SUBLANES, c), axis=0)


def _fold8(ref):
    ref[0:1, :] = jnp.sum(ref[...], axis=0, keepdims=True)


def _row_tile(n, cap, mult):
    best = None
    for t in range(mult, min(n, cap) + 1, mult):
        if n % t == 0:
            best = t
    return n if best is None else best


def _params(n_grid):
    return pltpu.CompilerParams(dimension_semantics=("arbitrary",) * n_grid, vmem_limit_bytes=V7X_VMEM_LIMIT)


def _full(shape):
    return pl.BlockSpec(shape, lambda *_: (0,) * len(shape))


def _layout(pieces):
    offs, r = {}, 0
    for name, rows in pieces:
        offs[name] = (r, rows)
        r += rows
    return offs, r


HBM = pl.BlockSpec(memory_space=pl.ANY)


def _mesh_pos():
    return lax.axis_index("x"), lax.axis_index("y"), lax.axis_index("c")


def _remote(src, dst, send_sems, recv_sems, k, to):
    return pltpu.make_async_remote_copy(src_ref=src, dst_ref=dst, send_sem=send_sems.at[k], recv_sem=recv_sems.at[k],
                                        device_id=to, device_id_type=MESH_ID)


class _Gather:
    def __init__(self, shard):
        self.inputs = (shard,)
        self.out_shape = (jax.ShapeDtypeStruct((N_DEV, *shard.shape), shard.dtype),)
        self.scratch = (pltpu.SemaphoreType.DMA((7,)), pltpu.SemaphoreType.DMA((7,)), pltpu.SemaphoreType.DMA)

    def phases(self, ins, outs, scr):
        (x_ref,), (out_ref,), (send_sems, recv_sems, local_sem) = ins, outs, scr
        x, y, c = _mesh_pos()
        me, sibling = (x, y, c), (x, y, 1 - c)
        chips = [(1 - x, y), (x, 1 - y), (1 - x, 1 - y)]

        def block(px, py, pc):
            return out_ref.at[4 * px + 2 * py + pc]

        def copy(k, blk, to, src=None):
            return _remote(block(*blk) if src is None else src, block(*blk), send_sems, recv_sems, k, to)

        def mine():
            return pltpu.make_async_copy(x_ref, block(*me), local_sem)

        def first():
            return [copy(0, me, sibling, src=x_ref)] + [copy(1 + j, me, (*chip, c), src=x_ref) for j, chip in enumerate(chips)]

        def passed(j):
            return copy(4 + j, (*chips[j], c), sibling)

        def start():
            mine().start()
            for cp in first():
                cp.start()

        def forward():
            for j, chip in enumerate(chips):
                copy(1 + j, (*chip, c), me).wait_recv()
                passed(j).start()

        def finish():
            copy(0, sibling, me).wait_recv()
            for j, chip in enumerate(chips):
                copy(4 + j, (*chip, 1 - c), me).wait_recv()
            for cp in first() + [passed(j) for j in range(3)]:
                cp.wait_send()
            mine().wait()

        return [start, forward, finish]


class _GatherRelay(_Gather):
    def __init__(self, shard):
        super().__init__(shard)
        self.scratch = (pltpu.SemaphoreType.DMA((9,)), pltpu.SemaphoreType.DMA((9,)), pltpu.SemaphoreType.DMA)
        self.split = shard.shape[0] // 32 * 16

    def phases(self, ins, outs, scr):
        (x_ref,), (out_ref,), (send_sems, recv_sems, local_sem) = ins, outs, scr
        x, y, c = _mesh_pos()
        rows = x_ref.shape[0]
        me, sibling = (x, y, c), (x, y, 1 - c)
        xn, yn, dg = (1 - x, y), (x, 1 - y), (1 - x, 1 - y)
        halves = (pl.ds(0, self.split), pl.ds(self.split, rows - self.split))

        def block(chip, pc, half=None):
            ref = out_ref.at[4 * chip[0] + 2 * chip[1] + pc]
            return ref if half is None else ref.at[halves[half], :]

        def copy(k, dst, to, src=None):
            return _remote(dst if src is None else src, dst, send_sems, recv_sems, k, to)

        def mine():
            return pltpu.make_async_copy(x_ref, block((x, y), c), local_sem)

        sends = {
            0: lambda: copy(0, block((x, y), c), sibling, src=x_ref),
            1: lambda: copy(1, block((x, y), c), (*xn, c), src=x_ref),
            2: lambda: copy(2, block((x, y), c), (*yn, c), src=x_ref),
            3: lambda: copy(3, block(yn, c, 0), (*xn, c)),
            4: lambda: copy(4, block(xn, c, 1), (*yn, c)),
            5: lambda: copy(5, block(xn, c), sibling),
            6: lambda: copy(6, block(yn, c), sibling),
            7: lambda: copy(7, block(dg, c, 0), sibling),
            8: lambda: copy(8, block(dg, c, 1), sibling),
        }
        lands = {
            0: lambda: copy(0, block((x, y), 1 - c), me), 1: lambda: copy(1, block(xn, c), me),
            2: lambda: copy(2, block(yn, c), me), 3: lambda: copy(3, block(dg, c, 0), me),
            4: lambda: copy(4, block(dg, c, 1), me), 5: lambda: copy(5, block(xn, 1 - c), me),
            6: lambda: copy(6, block(yn, 1 - c), me), 7: lambda: copy(7, block(dg, 1 - c, 0), me),
            8: lambda: copy(8, block(dg, 1 - c, 1), me),
        }

        def start():
            mine().start()
            for k in (0, 1, 2):
                sends[k]().start()

        def forward():
            for arrived, then in ((1, (4, 5)), (2, (3, 6)), (3, (7,)), (4, (8,))):
                lands[arrived]().wait_recv()
                for k in then:
                    sends[k]().start()

        def finish():
            for k in (0, 5, 6, 7, 8):
                lands[k]().wait_recv()
            for k in range(9):
                sends[k]().wait_send()
            mine().wait()

        return [start, forward, finish]


class _SiblingExchange:
    def __init__(self, src):
        self.inputs = (src,)
        self.out_shape = (jax.ShapeDtypeStruct((4, *src.shape[1:]), src.dtype),)
        self.scratch = (pltpu.SemaphoreType.DMA((4,)), pltpu.SemaphoreType.DMA((4,)))

    def phases(self, ins, outs, scr):
        (g_ref,), (recv_ref,), (send_sems, recv_sems) = ins, outs, scr
        x, y, c = _mesh_pos()

        def copies():
            return [_remote(g_ref.at[2 * k + (1 - c)], recv_ref.at[k], send_sems, recv_sems, k, (x, y, 1 - c))
                    for k in range(4)]

        def start():
            for cp in copies():
                cp.start()

        def finish():
            for cp in copies():
                cp.wait()

        return [start, finish]


class _ChipsExchange:
    def __init__(self, src):
        self.inputs = (src,)
        self.out_shape = (jax.ShapeDtypeStruct(src.shape, src.dtype),)
        self.scratch = (pltpu.SemaphoreType.DMA((3,)), pltpu.SemaphoreType.DMA((3,)))

    def phases(self, ins, outs, scr):
        (p_ref,), (recv_ref,), (send_sems, recv_sems) = ins, outs, scr
        x, y, c = _mesh_pos()
        peers = [(1 - x, y, c), (x, 1 - y, c), (1 - x, 1 - y, c)]

        def copies():
            return [_remote(p_ref.at[k], recv_ref.at[k], send_sems, recv_sems, k, peer) for k, peer in enumerate(peers)]

        def start():
            for cp in copies():
                cp.start()

        def finish():
            for cp in copies():
                cp.wait()

        return [start, finish]


class _Together:
    def __init__(self, *plans):
        self.plans = plans
        self.inputs = tuple(a for p in plans for a in p.inputs)
        self.out_shape = tuple(o for p in plans for o in p.out_shape)
        self.scratch = tuple(s for p in plans for s in p.scratch)

    def phases(self, ins, outs, scr):
        each = []
        for p in self.plans:
            n_in, n_out, n_scr = len(p.inputs), len(p.out_shape), len(p.scratch)
            each.append(p.phases(ins[:n_in], outs[:n_out], scr[:n_scr]))
            ins, outs, scr = ins[n_in:], outs[n_out:], scr[n_scr:]

        def run(fns):
            def phase():
                for fn in fns:
                    fn()
            return phase

        middle = [fn for ph in each for fn in ph[1:-1]]
        return [run([ph[0] for ph in each]), *([run(middle)] if middle else []), run([ph[-1] for ph in each])]


def _run_comm(plan, name):
    n_in, n_out = len(plan.inputs), len(plan.out_shape)

    def body(*refs):
        for phase in plan.phases(refs[:n_in], refs[n_in:n_in + n_out], refs[n_in + n_out:]):
            phase()

    return pl.pallas_call(
        body, name=name, out_shape=list(plan.out_shape), in_specs=[HBM] * n_in, out_specs=[HBM] * n_out,
        scratch_shapes=list(plan.scratch))(*plan.inputs)


def _grid_call(body, *, name, nt, in_specs, out_specs, out_shape, scratch_shapes, args, plan=None):
    if plan is None:
        return pl.pallas_call(body, name=name, grid=(nt,), in_specs=in_specs, out_specs=out_specs, out_shape=out_shape,
                              scratch_shapes=scratch_shapes, compiler_params=_params(1))(*args)
    n_in, n_out, n_scr = len(in_specs), len(out_specs), len(scratch_shapes)
    p_in, p_out = len(plan.inputs), len(plan.out_shape)

    def with_plan(*refs):
        ins, refs = refs[:n_in], refs[n_in:]
        p_ins, refs = refs[:p_in], refs[p_in:]
        outs, refs = refs[:n_out], refs[n_out:]
        p_outs, refs = refs[:p_out], refs[p_out:]
        scr, p_scr = refs[:n_scr], refs[n_scr:]
        phases = plan.phases(p_ins, p_outs, p_scr)
        i = pl.program_id(0)
        pl.when(i == 0)(phases[0])
        for phase in phases[1:-1]:
            pl.when(i == min(max(nt - 3, 1), nt - 1))(phase)
        body(*ins, *outs, *scr)
        pl.when(i == nt - 1)(phases[-1])

    return pl.pallas_call(
        with_plan, name=name, grid=(nt,), in_specs=[*in_specs, *[HBM] * p_in], out_specs=[*out_specs, *[HBM] * p_out],
        out_shape=[*out_shape, *plan.out_shape], scratch_shapes=[*scratch_shapes, *plan.scratch],
        compiler_params=_params(1))(*args, *plan.inputs)


def _add_chunks(gslab, recv, gid, rid, name):
    _, rows, cols = gslab.shape

    def body(gid_ref, rid_ref, a_ref, b_ref, own_ref, part_ref):
        s = a_ref[...] + b_ref[...]

        @pl.when(pl.program_id(0) == 0)
        def _():
            own_ref[...] = s

        part_ref[...] = s.astype(BF16)

    blk = (1, rows, cols)
    grid_spec = pltpu.PrefetchScalarGridSpec(
        num_scalar_prefetch=2, grid=(4,),
        in_specs=[pl.BlockSpec(blk, lambda k, g, r: (g[k], 0, 0)), pl.BlockSpec(blk, lambda k, g, r: (r[k], 0, 0))],
        out_specs=[pl.BlockSpec(blk, lambda k, g, r: (0, 0, 0)),
                   pl.BlockSpec(blk, lambda k, g, r: (jnp.maximum(k - 1, 0), 0, 0))])
    return pl.pallas_call(
        body, name=name, grid_spec=grid_spec,
        out_shape=[jax.ShapeDtypeStruct(blk, F32), jax.ShapeDtypeStruct((3, rows, cols), BF16)],
        compiler_params=_params(1),
    )(gid, rid, gslab, recv)


def _sum_partials(own, recv, name):
    _, rows, cols = own.shape
    tr = _row_tile(rows, 1024, 16)

    def body(o_ref, r_ref, out_ref):
        acc = o_ref[0]
        for k in range(3):
            acc = acc + r_ref[k].astype(F32)
        out_ref[...] = acc

    return pl.pallas_call(
        body, name=name, grid=(rows // tr,),
        in_specs=[pl.BlockSpec((1, tr, cols), lambda i: (0, i, 0)), pl.BlockSpec((3, tr, cols), lambda i: (0, i, 0))],
        out_specs=pl.BlockSpec((tr, cols), lambda i: (i, 0)),
        out_shape=jax.ShapeDtypeStruct((rows, cols), F32),
        compiler_params=_params(1),
    )(own, recv)


def _chunk_ids():
    x, y, c = _mesh_pos()
    chips = [(x, y), (1 - x, y), (x, 1 - y), (1 - x, 1 - y)]
    gid = jnp.stack([4 * px + 2 * py + c for px, py in chips]).astype(jnp.int32)
    rid = jnp.stack([2 * px + py for px, py in chips]).astype(jnp.int32)
    return gid, rid


def _sibling_sums(slab, recv, tag):
    gid, rid = _chunk_ids()
    return _add_chunks(slab, recv, gid, rid, "rs_add_" + tag)


def _load_weights(slab_ref, offs, dsts, sems):
    cps = []
    for i, (off, dst) in enumerate(zip(offs, dsts)):
        f8 = dst.shape[0] // N_DEV
        cps += [pltpu.make_async_copy(slab_ref.at[j, pl.ds(off, f8), :], dst.at[pl.ds(j * f8, f8), :],
                                      sems.at[i * N_DEV + j]) for j in range(N_DEV)]
    for cp in cps:
        cp.start()
    for cp in cps:
        cp.wait()


def _chunk_rows(c):
    return pl.ds(pl.multiple_of(c * FF_CHUNK, FF_CHUNK), FF_CHUNK)


def _rms(xv):
    return lax.rsqrt(jnp.mean(xv * xv, axis=-1, keepdims=True) + RMS_EPS)


def _loss_terms(xv, tgt, gain):
    r = _rms(xv)
    xh = xv * r
    err = xh * gain - tgt
    loss = 0.5 * jnp.sum(jnp.mean(err * err, axis=-1, keepdims=True))
    dy = err * (1.0 / xv.shape[-1])
    dxh = dy * gain
    return loss, r * (dxh - xh * jnp.mean(dxh * xh, axis=-1, keepdims=True)), _rowsum8(dy * xh)


def _ffn_fwd(x, gain, slab, offs, f, name, plan=None, head=None):
    t, d = x.shape
    nc, tm = f // FF_CHUNK, TOKEN_TILE
    nt = t // tm
    n_head = 0 if head is None else 2

    def body(*refs):
        x_ref, gain_ref, slab_ref = refs[:3]
        xo_ref, g_ref, u_ref, n_ref = refs[3 + n_head:7 + n_head]
        wg_v, wu_v, wd_v, acc_ref, sems = refs[7 + 2 * n_head:]
        i = pl.program_id(0)

        @pl.when(i == 0)
        def _():
            _load_weights(slab_ref, offs, (wg_v, wu_v, wd_v), sems)

        xv = x_ref[...]
        n_ref[...] = ((xv * _rms(xv)) * gain_ref[...]).astype(BF16)
        acc_ref[...] = jnp.zeros_like(acc_ref)

        def chunk(c, carry):
            rows = _chunk_rows(c)
            nb = n_ref[...]
            g = _dot_nt(nb, wg_v[rows, :])
            u = _dot_nt(nb, wu_v[rows, :])
            g_ref[c] = g.astype(BF16)
            u_ref[c] = u.astype(BF16)
            h = (g * jax.nn.sigmoid(g)) * u
            acc_ref[...] += _dot_nn(h.astype(BF16), wd_v[rows, :])
            return carry

        lax.fori_loop(0, nc, chunk, 0, unroll=True)
        out = xv + FFN_RES_WEIGHT * acc_ref[...]
        if head is None:
            xo_ref[...] = out
            return
        tgt_ref, fgain_ref = refs[3:5]
        dgain_ref, loss_ref = refs[7 + n_head:7 + 2 * n_head]

        @pl.when(i == 0)
        def _():
            dgain_ref[...] = jnp.zeros_like(dgain_ref)
            loss_ref[...] = jnp.zeros_like(loss_ref)

        loss, dx, dgain = _loss_terms(out, tgt_ref[...], fgain_ref[...])
        xo_ref[...] = dx
        loss_ref[...] += loss
        dgain_ref[...] += dgain

        @pl.when(i == nt - 1)
        def _():
            _fold8(dgain_ref)

    tile = pl.BlockSpec((tm, d), lambda i: (i, 0))
    act = pl.BlockSpec((nc, tm, FF_CHUNK), lambda i: (0, i, 0))
    head_in = [] if head is None else [tile, _full((1, d))]
    head_out = [] if head is None else [_full((SUBLANES, d)), _full((SUBLANES, 128))]
    head_shape = [] if head is None else [jax.ShapeDtypeStruct((SUBLANES, d), F32), jax.ShapeDtypeStruct((SUBLANES, 128), F32)]
    return _grid_call(
        body, name=name, nt=nt, plan=plan,
        in_specs=[tile, _full((1, d)), HBM, *head_in],
        out_specs=[tile, act, act, tile, *head_out],
        out_shape=[jax.ShapeDtypeStruct((t, d), F32), jax.ShapeDtypeStruct((nc, t, FF_CHUNK), BF16),
                   jax.ShapeDtypeStruct((nc, t, FF_CHUNK), BF16), jax.ShapeDtypeStruct((t, d), BF16), *head_shape],
        scratch_shapes=[pltpu.VMEM((f, d), BF16), pltpu.VMEM((f, d), BF16), pltpu.VMEM((f, d), BF16),
                        pltpu.VMEM((tm, d), F32), pltpu.SemaphoreType.DMA((3 * N_DEV,))],
        args=(x, gain, slab, *([] if head is None else head)))


def _ffn_bwd(dxo, x, gain, gs, us, slab, offs, f, name, plan=None):
    t, d = x.shape
    nc, tm = f // FF_CHUNK, FFN_BWD_TILE
    nt = t // tm

    def body(dxo_ref, x_ref, gain_ref, g_ref, u_ref, slab_ref,
             dx_ref, h_ref, dg_ref, du_ref, df_ref, dgain_ref, wg_v, wu_v, wd_v, sems):
        i = pl.program_id(0)

        @pl.when(i == 0)
        def _():
            _load_weights(slab_ref, offs, (wg_v, wu_v, wd_v), sems)
            dgain_ref[...] = jnp.zeros_like(dgain_ref)

        df_ref[...] = (FFN_RES_WEIGHT * dxo_ref[...]).astype(BF16)

        def chunk(c):
            rows = pl.ds(c * FF_CHUNK, FF_CHUNK)
            g = g_ref[c].astype(F32)
            u = u_ref[c].astype(F32)
            sg = jax.nn.sigmoid(g)
            sil = g * sg
            dh = _dot_nt(df_ref[...], wd_v[rows, :])
            h_ref[c] = (sil * u).astype(BF16)
            du_ref[c] = (dh * sil).astype(BF16)
            dg_ref[c] = (dh * u * (sg * (1.0 + g * (1.0 - sg)))).astype(BF16)

        def back(c, dn):
            rows = pl.ds(c * FF_CHUNK, FF_CHUNK)
            return dn + _dot_nn(dg_ref[c], wg_v[rows, :]) + _dot_nn(du_ref[c], wu_v[rows, :])

        dn = jnp.zeros((tm, d), F32)
        for c in range(nc):
            chunk(c)
            if c:
                dn = back(c - 1, dn)
        dn = back(nc - 1, dn)
        xv = x_ref[...]
        r = _rms(xv)
        xh = xv * r
        dgain_ref[...] += _rowsum8(dn * xh)
        dxh = dn * gain_ref[...]
        dx_ref[...] = dxo_ref[...] + r * (dxh - xh * jnp.mean(dxh * xh, axis=-1, keepdims=True))

        @pl.when(i == nt - 1)
        def _():
            _fold8(dgain_ref)

    tile = pl.BlockSpec((tm, d), lambda i: (i, 0))
    act = pl.BlockSpec((nc, tm, FF_CHUNK), lambda i: (0, i, 0))
    act_shape = jax.ShapeDtypeStruct((nc, t, FF_CHUNK), BF16)
    return _grid_call(
        body, name=name, nt=nt, plan=plan,
        in_specs=[tile, tile, _full((1, d)), act, act, HBM],
        out_specs=[tile, act, act, act, tile, _full((SUBLANES, d))],
        out_shape=[jax.ShapeDtypeStruct((t, d), F32), act_shape, act_shape, act_shape,
                   jax.ShapeDtypeStruct((t, d), BF16), jax.ShapeDtypeStruct((SUBLANES, d), F32)],
        scratch_shapes=[pltpu.VMEM((f, d), BF16), pltpu.VMEM((f, d), BF16), pltpu.VMEM((f, d), BF16),
                        pltpu.SemaphoreType.DMA((3 * N_DEV,))],
        args=(dxo, x, gain, gs, us, slab))


def _tn_chunked(a, b, name, plan=None):
    nc, t, _ = a.shape
    n = b.shape[1]
    tb = _row_tile(t, 1024, TOKEN_TILE)

    def body(a_ref, b_ref, o_ref):
        @pl.when(pl.program_id(0) == 0)
        def _():
            o_ref[...] = jnp.zeros_like(o_ref)

        def chunk(c, carry):
            rows = _chunk_rows(c)
            o_ref[rows, :] += _dot_tn(a_ref[c], b_ref[...])
            return carry

        lax.fori_loop(0, nc, chunk, 0, unroll=True)

    return _grid_call(
        body, name=name, nt=t // tb, plan=plan,
        in_specs=[pl.BlockSpec((nc, tb, FF_CHUNK), lambda i: (0, i, 0)), pl.BlockSpec((tb, n), lambda i: (i, 0))],
        out_specs=[_full((nc * FF_CHUNK, n))],
        out_shape=[jax.ShapeDtypeStruct((nc * FF_CHUNK, n), F32)],
        scratch_shapes=[], args=(a, b))


def _tn(a, b, name, plan=None):
    t, k = a.shape
    n = b.shape[1]
    tb = _row_tile(t, 1024, TOKEN_TILE)

    def body(a_ref, b_ref, o_ref):
        @pl.when(pl.program_id(0) == 0)
        def _():
            o_ref[...] = jnp.zeros_like(o_ref)

        o_ref[...] += _dot_tn(a_ref[...].astype(BF16), b_ref[...].astype(BF16))

    return _grid_call(
        body, name=name, nt=t // tb, plan=plan,
        in_specs=[pl.BlockSpec((tb, k), lambda i: (i, 0)), pl.BlockSpec((tb, n), lambda i: (i, 0))],
        out_specs=[_full((k, n))],
        out_shape=[jax.ShapeDtypeStruct((k, n), F32)],
        scratch_shapes=[], args=(a, b))


def _layernorm_stats(u1):
    mu = jnp.mean(u1, axis=-1, keepdims=True)
    xc = u1 - mu
    rstd = lax.rsqrt(jnp.mean(xc * xc, axis=-1, keepdims=True) + LN_EPS)
    return xc * rstd, rstd


def _positions(tile_index, tm):
    return (tile_index * tm + lax.broadcasted_iota(jnp.int32, (tm, 1), 0)).astype(F32)


def _shifted_taps(src_ref, sh_ref, tm, offset_of):
    groups = {}
    for k in range(CONV_WIDTH):
        groups.setdefault(offset_of(k) % SUBLANES, []).append(k)
    span = tm + HALO - SUBLANES
    for rem, taps in sorted(groups.items()):
        if rem:
            sh_ref[0:span, :] = src_ref[rem:rem + span, :]
        ref = sh_ref if rem else src_ref
        for k in taps:
            base = offset_of(k) - rem
            yield k, ref[base:base + tm, :]


def _mix_fwd(x, gm, win_t, cdw, cb, lg, lb, pw, poolw, ps, wout):
    t, d = x.shape
    tm = TOKEN_TILE

    def body(x_ref, gm_ref, win_ref, cdw_ref, cb_ref, lg_ref, lb_ref, pw_ref, poolw_ref, ps_ref, wout_ref,
             xo_ref, h_ref, ag_ref, u0_ref, u1_ref, u2_ref, mixed_ref, cat_ref, eu_ref, ep_ref, sh_ref):
        i = pl.program_id(0)

        @pl.when(i == 0)
        def _():
            eu_ref[0:HALO, :] = jnp.zeros((HALO, D_CONV), F32)
            ep_ref[0:HALO, :] = jnp.zeros((HALO, D_POOL), F32)

        @pl.when(i > 0)
        def _():
            eu_ref[0:HALO, :] = eu_ref[tm:tm + HALO, :]
            ep_ref[0:HALO, :] = ep_ref[tm:tm + HALO, :]

        xv = x_ref[...]
        hb = ((xv * _rms(xv)) * gm_ref[...]).astype(BF16)
        h_ref[...] = hb
        proj = _dot_nt(hb, win_ref[...])
        a = proj[:, :D_CONV]
        g = proj[:, D_CONV:2 * D_CONV]
        ag_ref[...] = proj[:, :2 * D_CONV]
        u0 = a * jax.nn.sigmoid(g)
        u0_ref[...] = u0
        eu_ref[HALO:HALO + tm, :] = u0
        ep_ref[HALO:HALO + tm, :] = proj[:, 2 * D_CONV:]

        u1 = jnp.broadcast_to(cb_ref[...], (tm, D_CONV))
        for k, rows in _shifted_taps(eu_ref, sh_ref, tm, lambda k: HALO - (CONV_WIDTH - 1) + k):
            u1 = u1 + cdw_ref[k:k + 1, :] * rows
        u1_ref[...] = u1
        lnh, _ = _layernorm_stats(u1)
        ln = lnh * lg_ref[...] + lb_ref[...]
        u2 = (ln * jax.nn.sigmoid(ln)).astype(BF16)
        u2_ref[...] = u2
        conv_out = _dot_nn(u2, pw_ref[...])

        pos = _positions(i, tm)
        outs = []
        for gi, w in enumerate(POOL_WINDOWS):
            lo = gi * POOL_GROUP
            p = ep_ref[HALO:HALO + tm, lo:lo + POOL_GROUP]
            s = p
            for j in range(1, w):
                s = s + ep_ref[HALO - j:HALO - j + tm, lo:lo + POOL_GROUP]
            mixed = (s / jnp.minimum(pos + 1.0, float(w)) - p).astype(BF16)
            mixed_ref[:, lo:lo + POOL_GROUP] = mixed
            outs.append(_dot_nn(mixed, poolw_ref[gi]))
        pool_out = jnp.concatenate(outs, axis=-1) * ps_ref[...]
        cat = jnp.concatenate([conv_out, pool_out], axis=-1).astype(BF16)
        cat_ref[...] = cat
        xo_ref[...] = xv + _dot_nn(cat, wout_ref[...])

    def tile(c):
        return pl.BlockSpec((tm, c), lambda i: (i, 0))

    def out(c, dt):
        return jax.ShapeDtypeStruct((t, c), dt)

    return pl.pallas_call(
        body, name="mix_fwd", grid=(t // tm,),
        in_specs=[tile(d), _full((1, d)), _full((D_IN, d)), _full((HALO, D_CONV)), _full((1, D_CONV)),
                  _full((1, D_CONV)), _full((1, D_CONV)), _full((D_CONV, D_CONV)),
                  _full((len(POOL_WINDOWS), POOL_GROUP, POOL_GROUP)), _full((1, D_POOL)), _full((d, d))],
        out_specs=[tile(d), tile(d), tile(2 * D_CONV), tile(D_CONV), tile(D_CONV), tile(D_CONV), tile(D_POOL), tile(d)],
        out_shape=[out(d, F32), out(d, BF16), out(2 * D_CONV, F32), out(D_CONV, F32), out(D_CONV, F32),
                   out(D_CONV, BF16), out(D_POOL, BF16), out(d, BF16)],
        scratch_shapes=[pltpu.VMEM((HALO + tm, D_CONV), F32), pltpu.VMEM((HALO + tm, D_POOL), F32),
                        pltpu.VMEM((HALO + tm, D_CONV), F32)],
        compiler_params=_params(1),
    )(x, gm, win_t, cdw, cb, lg, lb, pw, poolw, ps, wout)


def _mix_bwd(dxo, x, gm, ag, u0, u1, mixed, win_t, cdw, lg, lb, pw, poolw, ps, wout, plan=None):
    t, d = x.shape
    tm = TOKEN_TILE
    nt = t // tm
    halo_blocks = tm // HALO

    def body(dxo_ref, x_ref, gm_ref, ag_ref, u0_ref, u0h_ref, u1_ref, mixed_ref,
             win_ref, cdw_ref, lg_ref, lb_ref, pw_ref, poolw_ref, ps_ref, wout_ref,
             dx_ref, dproj_ref, dco_ref, dgm_ref, dcdw_ref, dcb_ref, dlg_ref, dlb_ref, dpoolw_ref, dps_ref,
             eu_ref, ed_ref, eq_ref, sh_ref):
        i = pl.program_id(0)
        ti = nt - 1 - i

        @pl.when(i == 0)
        def _():
            for ref in (dgm_ref, dcdw_ref, dcb_ref, dlg_ref, dlb_ref, dpoolw_ref, dps_ref):
                ref[...] = jnp.zeros_like(ref)
            ed_ref[tm:tm + HALO, :] = jnp.zeros((HALO, D_CONV), F32)
            eq_ref[tm:tm + HALO, :] = jnp.zeros((HALO, D_POOL), F32)

        @pl.when(i > 0)
        def _():
            ed_ref[tm:tm + HALO, :] = ed_ref[0:HALO, :]
            eq_ref[tm:tm + HALO, :] = eq_ref[0:HALO, :]

        @pl.when(ti == 0)
        def _():
            eu_ref[0:HALO, :] = jnp.zeros((HALO, D_CONV), F32)

        @pl.when(ti > 0)
        def _():
            eu_ref[0:HALO, :] = u0h_ref[...]

        eu_ref[HALO:HALO + tm, :] = u0_ref[...]

        dxo = dxo_ref[...]
        dcat = _dot_nt(dxo.astype(BF16), wout_ref[...])
        dco = dcat[:, :D_CONV].astype(BF16)
        dco_ref[...] = dco
        dpo = dcat[:, D_CONV:]

        lnh, rstd = _layernorm_stats(u1_ref[...])
        ln = lnh * lg_ref[...] + lb_ref[...]
        sl = jax.nn.sigmoid(ln)
        dln = _dot_nt(dco, pw_ref[...]) * (sl * (1.0 + ln * (1.0 - sl)))
        dlg_ref[...] += _rowsum8(dln * lnh)
        dlb_ref[...] += _rowsum8(dln)
        dlnh = dln * lg_ref[...]
        du1 = rstd * (dlnh - jnp.mean(dlnh, axis=-1, keepdims=True)
                      - lnh * jnp.mean(dlnh * lnh, axis=-1, keepdims=True))
        dcb_ref[...] += _rowsum8(du1)
        ed_ref[0:tm, :] = du1

        du0 = jnp.zeros((tm, D_CONV), F32)
        for k, rows in _shifted_taps(ed_ref, sh_ref, tm, lambda k: CONV_WIDTH - 1 - k):
            du0 = du0 + cdw_ref[k:k + 1, :] * rows
        for k, rows in _shifted_taps(eu_ref, sh_ref, tm, lambda k: HALO - (CONV_WIDTH - 1) + k):
            dcdw_ref[SUBLANES * k:SUBLANES * (k + 1), :] += _rowsum8(du1 * rows)
        a = ag_ref[:, :D_CONV]
        sg = jax.nn.sigmoid(ag_ref[:, D_CONV:])
        pieces = [du0 * sg, du0 * a * (sg * (1.0 - sg))]

        pos = _positions(ti, tm)
        for gi, w in enumerate(POOL_WINDOWS):
            lo = gi * POOL_GROUP
            mg = mixed_ref[:, lo:lo + POOL_GROUP]
            dpo_g = dpo[:, lo:lo + POOL_GROUP]
            dps_ref[:, lo:lo + POOL_GROUP] += _rowsum8(dpo_g * _dot_nn(mg, poolw_ref[gi]))
            dout = (dpo_g * ps_ref[:, lo:lo + POOL_GROUP]).astype(BF16)
            dpoolw_ref[gi] += _dot_tn(mg, dout)
            dmx = _dot_nt(dout, poolw_ref[gi])
            q = dmx / jnp.minimum(pos + 1.0, float(w))
            eq_ref[0:tm, lo:lo + POOL_GROUP] = q
            s = q
            for j in range(1, w):
                s = s + eq_ref[j:j + tm, lo:lo + POOL_GROUP]
            pieces.append(s - dmx)
        dproj = jnp.concatenate(pieces, axis=-1).astype(BF16)
        dproj_ref[...] = dproj

        dh = _dot_nn(dproj, win_ref[...])
        xv = x_ref[...]
        r = _rms(xv)
        xh = xv * r
        dgm_ref[...] += _rowsum8(dh * xh)
        dxh = dh * gm_ref[...]
        dx_ref[...] = dxo + r * (dxh - xh * jnp.mean(dxh * xh, axis=-1, keepdims=True))

        @pl.when(i == nt - 1)
        def _():
            for ref in (dgm_ref, dcb_ref, dlg_ref, dlb_ref, dps_ref):
                _fold8(ref)
            for k in range(CONV_WIDTH):
                dcdw_ref[SUBLANES * k:SUBLANES * k + 1, :] = jnp.sum(
                    dcdw_ref[SUBLANES * k:SUBLANES * (k + 1), :], axis=0, keepdims=True)

    def tile(c):
        return pl.BlockSpec((tm, c), lambda i: (nt - 1 - i, 0))

    halo = pl.BlockSpec((HALO, D_CONV), lambda i: (jnp.maximum((nt - 1 - i) * halo_blocks - 1, 0), 0))
    n_groups = len(POOL_WINDOWS)
    return _grid_call(
        body, name="mix_bwd", nt=nt, plan=plan,
        in_specs=[tile(d), tile(d), _full((1, d)), tile(2 * D_CONV), tile(D_CONV), halo, tile(D_CONV), tile(D_POOL),
                  _full((D_IN, d)), _full((HALO, D_CONV)), _full((1, D_CONV)), _full((1, D_CONV)),
                  _full((D_CONV, D_CONV)), _full((n_groups, POOL_GROUP, POOL_GROUP)), _full((1, D_POOL)), _full((d, d))],
        out_specs=[tile(d), tile(D_IN), tile(D_CONV), _full((SUBLANES, d)), _full((HALO * SUBLANES, D_CONV)),
                   _full((SUBLANES, D_CONV)), _full((SUBLANES, D_CONV)), _full((SUBLANES, D_CONV)),
                   _full((n_groups, POOL_GROUP, POOL_GROUP)), _full((SUBLANES, D_POOL))],
        out_shape=[jax.ShapeDtypeStruct((t, d), F32), jax.ShapeDtypeStruct((t, D_IN), BF16),
                   jax.ShapeDtypeStruct((t, D_CONV), BF16), jax.ShapeDtypeStruct((SUBLANES, d), F32),
                   jax.ShapeDtypeStruct((HALO * SUBLANES, D_CONV), F32), jax.ShapeDtypeStruct((SUBLANES, D_CONV), F32),
                   jax.ShapeDtypeStruct((SUBLANES, D_CONV), F32), jax.ShapeDtypeStruct((SUBLANES, D_CONV), F32),
                   jax.ShapeDtypeStruct((n_groups, POOL_GROUP, POOL_GROUP), F32),
                   jax.ShapeDtypeStruct((SUBLANES, D_POOL), F32)],
        scratch_shapes=[pltpu.VMEM((HALO + tm, D_CONV), F32), pltpu.VMEM((tm + HALO, D_CONV), F32),
                        pltpu.VMEM((tm + HALO, D_POOL), F32), pltpu.VMEM((tm + HALO, D_CONV), F32)],
        args=(dxo, x, gm, ag, u0, u0, u1, mixed, win_t, cdw, lg, lb, pw, poolw, ps, wout))


def _adam_step(gv, w_ref, m_ref, v_ref, d_ref, nm_ref, nv_ref):
    nm = ADAM_B1 * m_ref[...] + (1.0 - ADAM_B1) * gv
    nv = ADAM_B2 * v_ref[...] + (1.0 - ADAM_B2) * (gv * gv)
    m_hat = nm / (1.0 - ADAM_B1 ** ADAM_STEP)
    v_hat = nv / (1.0 - ADAM_B2 ** ADAM_STEP)
    d_ref[...] = -ADAM_LR * (m_hat / (jnp.sqrt(v_hat) + ADAM_EPS) + ADAM_WD * w_ref[...])
    nm_ref[...] = nm
    nv_ref[...] = nv


def _adam_tile(rows, cols):
    return _row_tile(rows, max(SUBLANES, (256 * 1024) // cols // SUBLANES * SUBLANES), SUBLANES)


def _adamw_many(items, name):
    n = len(items)

    def body(*refs):
        ins, outs = refs[:4 * n], refs[4 * n:]
        for k in range(n):
            w_ref, g_ref, m_ref, v_ref = ins[4 * k:4 * k + 4]
            _adam_step(g_ref[...], w_ref, m_ref, v_ref, *outs[3 * k:3 * k + 3])

    return pl.pallas_call(
        body, name=name, out_shape=[jax.ShapeDtypeStruct(it[0].shape, F32) for it in items for _ in range(3)],
        compiler_params=_params(0))(*[a for it in items for a in it])


def _adamw_reduced(w, own, recv, m, v, name):
    rows, cols = w.shape
    tr = _adam_tile(rows, cols)

    def body(w_ref, o_ref, r_ref, m_ref, v_ref, g_ref, d_ref, nm_ref, nv_ref):
        gv = o_ref[0]
        for k in range(3):
            gv = gv + r_ref[k].astype(F32)
        g_ref[...] = gv
        _adam_step(gv, w_ref, m_ref, v_ref, d_ref, nm_ref, nv_ref)

    blk = pl.BlockSpec((tr, cols), lambda i: (i, 0))
    shape = jax.ShapeDtypeStruct((rows, cols), F32)
    return pl.pallas_call(
        body, name=name, grid=(rows // tr,),
        in_specs=[blk, pl.BlockSpec((1, tr, cols), lambda i: (0, i, 0)), pl.BlockSpec((3, tr, cols), lambda i: (0, i, 0)),
                  blk, blk],
        out_specs=[blk] * 4, out_shape=[shape] * 4,
        compiler_params=_params(1),
    )(w, own, recv, m, v)


def _as_2d(a):
    if a.ndim == 1:
        return a.reshape(a.shape[0] // 128, 128)
    if a.ndim == 3:
        return a.reshape(a.shape[0] * a.shape[1], a.shape[2])
    return a


def _pack_weight_slabs(p):
    def bf(parts):
        return [a.astype(BF16) for a in parts]

    cdw_bits = lax.bitcast_convert_type(p["conv_dw"], BF16).reshape(CONV_WIDTH, 2 * D_CONV // N_DEV)
    cdw_bits = jnp.pad(cdw_bits, ((0, 1), (0, 0))).reshape(4, D_MODEL)
    cdw_bits = jnp.pad(cdw_bits, ((0, CDW_ROWS - 4), (0, 0)))
    first = bf([p["ffn1_w_gate"].T, p["ffn1_w_up"].T, p["ffn1_w_down"]])
    rest = bf([p["ffn2_w_gate"].T, p["ffn2_w_up"].T, p["ffn2_w_down"], p["w_in"].T, p["w_out"],
               p["conv_pw"].reshape(D_CONV // N_DEV // 2, D_MODEL)]) + [cdw_bits]
    return jnp.concatenate(first, axis=0), jnp.concatenate(rest, axis=0)


def _unpack_rows(slab, offs, names):
    out = {}
    for name in names:
        o, n = offs[name]
        out[name] = slab[:, o:o + n, :].reshape(N_DEV * n, D_MODEL)
    return out


def _unpack_conv_taps(slab, offs):
    o, _ = offs["cdw"]
    bits = slab[:, o:o + 4, :].reshape(N_DEV, CONV_WIDTH + 1, D_CONV // N_DEV, 2)[:, :CONV_WIDTH]
    cdw = lax.bitcast_convert_type(bits, F32)
    return jnp.transpose(cdw, (1, 0, 2)).reshape(CONV_WIDTH, D_CONV)


def kernel(x, ffn1_norm, ffn1_w_gate, ffn1_w_up, ffn1_w_down, mix_norm, w_in, conv_dw, conv_dw_b, conv_ln_g, conv_ln_b, conv_pw, pool_w, pool_scale, w_out, ffn2_norm, ffn2_w_gate, ffn2_w_up, ffn2_w_down, final_norm, loss_target, m_ffn1_norm, m_ffn1_w_gate, m_ffn1_w_up, m_ffn1_w_down, m_mix_norm, m_w_in, m_conv_dw, m_conv_dw_b, m_conv_ln_g, m_conv_ln_b, m_conv_pw, m_pool_w, m_pool_scale, m_w_out, m_ffn2_norm, m_ffn2_w_gate, m_ffn2_w_up, m_ffn2_w_down, m_final_norm, v_ffn1_norm, v_ffn1_w_gate, v_ffn1_w_up, v_ffn1_w_down, v_mix_norm, v_w_in, v_conv_dw, v_conv_dw_b, v_conv_ln_g, v_conv_ln_b, v_conv_pw, v_pool_w, v_pool_scale, v_w_out, v_ffn2_norm, v_ffn2_w_gate, v_ffn2_w_up, v_ffn2_w_down, v_final_norm):
    given = dict(locals())
    p = {n: given[n] for n in WEIGHTS}
    f8 = ffn1_w_gate.shape[1]
    f = N_DEV * f8
    ffn_rows = (0, f8, 2 * f8)
    small = (("win", D_IN // N_DEV), ("wout", D_MODEL // N_DEV), ("pw", D_CONV // N_DEV // 2), ("cdw", CDW_ROWS))
    w_offs, _ = _layout((("g2", f8), ("u2", f8), ("d2", f8)) + small)
    s_offs, _ = _layout(small + (("rep", REP_ROWS),))
    x0 = x[0]
    target = loss_target[0]

    def row(vec):
        return vec.reshape(1, vec.shape[0])

    slab_first, slab_rest = _pack_weight_slabs(p)
    w_first = _run_comm(_GatherRelay(slab_first), "gather_ffn1")[0]
    x1, g1s, u1s, n1, w_rest = _ffn_fwd(x0, row(ffn1_norm), w_first, ffn_rows, f, "ffn1_fwd", _GatherRelay(slab_rest))
    w = _unpack_rows(w_rest, w_offs, ("win", "wout", "pw"))
    w["pw"] = w["pw"].reshape(D_CONV, D_CONV)
    cdw = jnp.pad(_unpack_conv_taps(w_rest, w_offs), ((0, HALO - CONV_WIDTH), (0, 0)))
    poolw = pool_w.astype(BF16)

    x2, h, ag, u0, u1, u2, mixed, cat = _mix_fwd(
        x1, row(mix_norm), w["win"], cdw, row(conv_dw_b), row(conv_ln_g), row(conv_ln_b), w["pw"], poolw,
        row(pool_scale), w["wout"])
    dx3, g2s, u2s, n2, d_final_norm, loss_part = _ffn_fwd(
        x2, row(ffn2_norm), w_rest, ffn_rows, f, "ffn2_fwd", head=(target, row(final_norm)))

    pending, reduced = {}, {}

    def chunks(a):
        return a.reshape(N_DEV, -1, D_MODEL)

    def after_sibling(name, slab, recv):
        pending[name], part = _sibling_sums(slab, recv, name)
        return _ChipsExchange(part)

    def after_chips(name, recv):
        reduced[name] = (pending.pop(name), recv)

    dx2, h2, dg2, du2, df2, d_ffn2_norm = _ffn_bwd(dx3, x2, row(ffn2_norm), g2s, u2s, w_rest, ffn_rows, f, "ffn2_bwd")
    s_g2 = chunks(_tn_chunked(dg2, n2, "ffn2_dgate")[0])
    s_u2, r = _tn_chunked(du2, n2, "ffn2_dup", _SiblingExchange(s_g2))
    s_u2 = chunks(s_u2)
    to_chips = after_sibling("g2", s_g2, r)
    s_d2, rc, r = _tn_chunked(h2, df2, "ffn2_ddown", _Together(to_chips, _SiblingExchange(s_u2)))
    s_d2 = chunks(s_d2)
    after_chips("g2", rc)
    to_chips = after_sibling("u2", s_u2, r)
    dx1, dproj, dco, d_mix_norm, d_cdw, d_cb, d_lg, d_lb, d_poolw, d_ps, rc, r = _mix_bwd(
        dx2, x1, row(mix_norm), ag, u0, u1, mixed, w["win"], cdw, row(conv_ln_g), row(conv_ln_b), w["pw"], poolw,
        row(pool_scale), w["wout"], _Together(to_chips, _SiblingExchange(s_d2)))
    after_chips("u2", rc)
    to_chips = after_sibling("d2", s_d2, r)
    d_win, rc = _tn(dproj, h, "mix_dwin", to_chips)
    after_chips("d2", rc)
    d_wout = _tn(cat, dx2, "mix_dwout")[0]
    d_pw = _tn(u2, dco, "mix_dpw")[0]
    dx0, h1, dg1, du1, df1, d_ffn1_norm = _ffn_bwd(dx1, x0, row(ffn1_norm), g1s, u1s, w_first, ffn_rows, f, "ffn1_bwd")

    d_cdw = d_cdw.reshape(HALO, SUBLANES, D_CONV)[:CONV_WIDTH, 0]
    d_cdw = jnp.transpose(d_cdw.reshape(CONV_WIDTH, N_DEV, D_CONV // N_DEV), (1, 0, 2)).reshape(N_DEV, -1)
    d_cdw = jnp.pad(d_cdw, ((0, 0), (0, CDW_ROWS * D_MODEL - d_cdw.shape[1]))).reshape(N_DEV, CDW_ROWS, D_MODEL)
    rep = jnp.concatenate([
        d_ffn1_norm[0:1], d_mix_norm[0:1], d_ffn2_norm[0:1], d_final_norm[0:1],
        jnp.concatenate([d_cb[0:1], d_lg[0:1]], axis=1), jnp.concatenate([d_lb[0:1], d_ps[0:1]], axis=1),
        jnp.zeros((2, D_MODEL), F32), d_poolw.reshape(-1, D_MODEL)], axis=0)
    rep = jnp.pad(rep, ((0, N_DEV * REP_ROWS - rep.shape[0]), (0, 0))).reshape(N_DEV, REP_ROWS, D_MODEL)
    s_small = jnp.concatenate([chunks(d_win), chunks(d_wout), chunks(d_pw), d_cdw, rep], axis=1)

    s_g1, r = _tn_chunked(dg1, n1, "ffn1_dgate", _SiblingExchange(s_small))
    s_g1 = chunks(s_g1)
    to_chips = after_sibling("small", s_small, r)
    s_u1, rc, r = _tn_chunked(du1, n1, "ffn1_dup", _Together(to_chips, _SiblingExchange(s_g1)))
    s_u1 = chunks(s_u1)
    mine_small = _sum_partials(pending.pop("small"), rc, "rs_sum_small")
    to_chips = after_sibling("g1", s_g1, r)
    o_rep, _ = s_offs["rep"]
    loss_rows = jnp.pad(loss_part, ((0, 0), (0, D_MODEL - loss_part.shape[1])))
    share = _Gather(jnp.concatenate([mine_small[o_rep:o_rep + REP_ROWS], loss_rows], axis=0))
    s_d1, rc, r, shared = _tn_chunked(h1, df1, "ffn1_ddown", _Together(to_chips, _SiblingExchange(s_u1), share))
    s_d1 = chunks(s_d1)
    after_chips("g1", rc)
    to_chips = after_sibling("u1", s_u1, r)
    rc, r = _run_comm(_Together(to_chips, _SiblingExchange(s_d1)), "rs_tail_up")
    after_chips("u1", rc)
    rc, = _run_comm(after_sibling("d1", s_d1, r), "rs_tail_down")
    after_chips("d1", rc)

    rep_all = shared[:, :REP_ROWS].reshape(N_DEV * REP_ROWS, D_MODEL)
    loss = jnp.sum(shared[:, REP_ROWS, 0])

    def small_rows(name):
        o, n = s_offs[name]
        return mine_small[o:o + n]

    g = {
        "ffn1_norm": rep_all[0], "mix_norm": rep_all[1], "ffn2_norm": rep_all[2], "final_norm": rep_all[3],
        "conv_dw_b": rep_all[4, :D_CONV], "conv_ln_g": rep_all[4, D_CONV:],
        "conv_ln_b": rep_all[5, :D_CONV], "pool_scale": rep_all[5, D_CONV:],
        "pool_w": rep_all[8:8 + pool_w.size // D_MODEL].reshape(pool_w.shape),
        "w_out": small_rows("wout"), "conv_pw": small_rows("pw").reshape(conv_pw.shape),
        "conv_dw": small_rows("cdw").reshape(-1)[:conv_dw.size].reshape(conv_dw.shape),
    }

    slab_of = {"ffn1_w_gate": "g1", "ffn1_w_up": "u1", "ffn1_w_down": "d1",
               "ffn2_w_gate": "g2", "ffn2_w_up": "u2", "ffn2_w_down": "d2"}
    transposed = ("ffn1_w_gate", "ffn1_w_up", "ffn2_w_gate", "ffn2_w_up", "w_in")
    g["w_in"] = small_rows("win")
    delta, new_m, new_v = {}, {}, {}

    def operands(n):
        wmv = [given[k] for k in (n, "m_" + n, "v_" + n)]
        return [a.T for a in wmv] if n in transposed else wmv

    def restore(n, a):
        return a.T if n in transposed else a.reshape(p[n].shape)

    others = [n for n in WEIGHTS if n not in slab_of]
    flat = _adamw_many([[_as_2d(a) for a in (wn, g[n], mn, vn)] for n in others for wn, mn, vn in [operands(n)]],
                       "adamw_small")
    for k, n in enumerate(others):
        delta[n], new_m[n], new_v[n] = (restore(n, a) for a in flat[3 * k:3 * k + 3])
    g["w_in"] = g["w_in"].T
    for n, slab in slab_of.items():
        wn, mn, vn = operands(n)
        outs = _adamw_reduced(wn, *reduced[slab], mn, vn, "adamw_" + n)
        g[n], delta[n], new_m[n], new_v[n] = (restore(n, a) for a in outs)

    return (loss, dx0[None], *[g[n] for n in WEIGHTS], *[delta[n] for n in WEIGHTS],
            *[new_m[n] for n in WEIGHTS], *[new_v[n] for n in WEIGHTS])
```
